```python
import math
import jax, jax.numpy as jnp
from jax import lax
import numpy as np

D_MODEL = 2048
BATCH = 4
SEQ = 2048
DEPTH = 2

CHUNK = 64
EPS = 1e-6
NEG_INF = -1e30

S5_WIDTH = D_MODEL // 4
S5_GROUP = 16
S5_GROUPS = S5_WIDTH // S5_GROUP
S5_STATE = 64

SSD_WIDTH = D_MODEL // 2
SSD_HEAD_DIM = 64
SSD_HEADS = SSD_WIDTH // SSD_HEAD_DIM
SSD_STATE = 128
SSD_GROUPS = 4
SSD_CONV = 4
SSD_CONV_DIM = SSD_WIDTH + 2 * SSD_GROUPS * SSD_STATE

ATT_WIDTH = D_MODEL // 4
ATT_HEAD_DIM = 64
ATT_HEADS = ATT_WIDTH // ATT_HEAD_DIM
IDX_HEADS = 8
IDX_DIM = 64
TOPK = 256
Q_BLOCK = 128
N_BUCKETS = 32
MAX_DISTANCE = 128
ATT_SCALE = ATT_HEAD_DIM ** -0.5
IDX_W_SCALE = (IDX_HEADS * IDX_DIM) ** -0.5

D_MIX = S5_WIDTH + SSD_WIDTH + ATT_WIDTH
IN_SPLITS = (S5_WIDTH, S5_WIDTH,
             SSD_WIDTH, SSD_CONV_DIM, SSD_HEADS,
             ATT_WIDTH, ATT_HEAD_DIM, ATT_HEAD_DIM,
             IDX_HEADS * IDX_DIM, IDX_DIM, IDX_HEADS,
             ATT_WIDTH)
D_IN = sum(IN_SPLITS)

kernel_name = 'hybrid_s5_ssd_dsa_parallel_heads'


def rmsnorm(x, w):
    xf = x.astype(jnp.float32)
    y = xf * lax.rsqrt(jnp.mean(xf * xf, axis=-1, keepdims=True) + EPS)
    return (y * w.astype(jnp.float32)).astype(x.dtype)


def s5_combine(e1, e2):
    a1r, a1i, b1r, b1i = e1
    a2r, a2i, b2r, b2i = e2
    ar = a2r * a1r - a2i * a1i
    ai = a2r * a1i + a2i * a1r
    br = a2r * b1r - a2i * b1i + b2r
    bi = a2r * b1i + a2i * b1r + b2i
    return ar, ai, br, bi


def s5_mixer(u, A_re, A_im, log_dt, B_re, B_im, C_re, C_im, D, glu_w, glu_b):
    b, L, _ = u.shape
    ug = u.reshape(b, L, S5_GROUPS, S5_GROUP)
    dt = jnp.exp(log_dt)[:, None]
    lre = jnp.minimum(A_re, -1e-4)
    lim = A_im
    mag = jnp.exp(lre * dt)
    lbr = mag * jnp.cos(lim * dt)
    lbi = mag * jnp.sin(lim * dt)
    nr, ni = lbr - 1.0, lbi
    den = lre * lre + lim * lim
    fr = (nr * lre + ni * lim) / den
    fi = (ni * lre - nr * lim) / den
    bbr = fr[..., None] * B_re - fi[..., None] * B_im
    bbi = fr[..., None] * B_im + fi[..., None] * B_re
    bu_r = jnp.einsum('blgc,gpc->blgp', ug, bbr)
    bu_i = jnp.einsum('blgc,gpc->blgp', ug, bbi)
    a_r = jnp.broadcast_to(lbr[None, None], bu_r.shape)
    a_i = jnp.broadcast_to(lbi[None, None], bu_i.shape)
    _, _, xr, xi = lax.associative_scan(s5_combine, (a_r, a_i, bu_r, bu_i), axis=1)
    y = (jnp.einsum('blgp,gcp->blgc', xr, C_re) - jnp.einsum('blgp,gcp->blgc', xi, C_im)
         + D * ug)
    y = jax.nn.gelu(y.reshape(b, L, S5_WIDTH))
    return y * jax.nn.sigmoid(jnp.einsum('ble,ef->blf', y, glu_w) + glu_b)


def causal_depthwise_conv(x, w, bias):
    k = w.shape[0]
    out = lax.conv_general_dilated(x, w[:, None, :], window_strides=(1,), padding=[(k - 1, 0)],
                                   dimension_numbers=('NWC', 'WIO', 'NWC'),
                                   feature_group_count=x.shape[-1])
    return out + bias


def segsum(a):
    t = a.shape[-1]
    ar = jnp.broadcast_to(a[..., :, None], a.shape + (t,))
    strict = jnp.tril(jnp.ones((t, t), dtype=bool), -1)
    ss = jnp.cumsum(jnp.where(strict, ar, 0), axis=-2)
    return jnp.where(jnp.tril(jnp.ones((t, t), dtype=bool)), ss, -jnp.inf)


def ssd_chunked(xd, a, Bm, Cm):
    b, L, h, p = xd.shape
    n = Bm.shape[-1]
    nc = L // CHUNK
    xc = xd.reshape(b, nc, CHUNK, h, p)
    Bc = Bm.reshape(b, nc, CHUNK, h, n)
    Cc = Cm.reshape(b, nc, CHUNK, h, n)
    ac = jnp.moveaxis(a.reshape(b, nc, CHUNK, h), 3, 1)
    a_cs = jnp.cumsum(ac, axis=-1)
    scores = jnp.einsum('bclhn,bcshn->bhcls', Cc, Bc) * jnp.exp(segsum(ac))
    y_diag = jnp.einsum('bhcls,bcshp->bclhp', scores, xc)
    decay_to_end = jnp.exp(a_cs[..., -1:] - a_cs)
    chunk_states = jnp.einsum('bclhn,bhcl,bclhp->bchpn', Bc, decay_to_end, xc)
    chunk_decay = jnp.exp(a_cs[..., -1])

    def carry_state(state, inp):
        st, dec = inp
        return (state * dec[..., None, None] + st).astype(state.dtype), state

    init = jnp.zeros((b, h, p, n), xd.dtype)
    _, prev = lax.scan(carry_state, init,
                       (jnp.moveaxis(chunk_states, 1, 0), jnp.moveaxis(chunk_decay, 2, 0)))
    prev = jnp.moveaxis(prev, 0, 1)
    y_off = jnp.einsum('bclhn,bchpn,bhcl->bclhp', Cc, prev, jnp.exp(a_cs))
    return (y_diag + y_off).reshape(b, L, h, p)


def ssd_mixer(z, xbc, dt_raw, conv_w, conv_b, dt_bias, A_log, D, norm_w):
    b, L, _ = z.shape
    xbc = jax.nn.silu(causal_depthwise_conv(xbc, conv_w, conv_b))
    xs, Bm, Cm = jnp.split(xbc, [SSD_WIDTH, SSD_WIDTH + SSD_GROUPS * SSD_STATE], axis=-1)
    xs = xs.reshape(b, L, SSD_HEADS, SSD_HEAD_DIM)
    rep = SSD_HEADS // SSD_GROUPS
    Bm = jnp.repeat(Bm.reshape(b, L, SSD_GROUPS, SSD_STATE), rep, axis=2)
    Cm = jnp.repeat(Cm.reshape(b, L, SSD_GROUPS, SSD_STATE), rep, axis=2)
    dt = jax.nn.softplus(dt_raw + dt_bias)
    a = dt * (-jnp.exp(A_log))
    y = ssd_chunked(xs * dt[..., None], a, Bm, Cm) + xs * D[:, None]
    y = y.reshape(b, L, SSD_WIDTH)
    return rmsnorm(y * jax.nn.silu(z), norm_w)


def t5_bucket(rel):
    nb = N_BUCKETS // 2
    max_exact = nb // 2
    ret = jnp.where(rel > 0, nb, 0)
    n = jnp.abs(rel)
    nf = jnp.maximum(n, 1).astype(jnp.float32)
    large = max_exact + (jnp.log(nf / max_exact) / math.log(MAX_DISTANCE / max_exact)
                         * (nb - max_exact)).astype(jnp.int32)
    large = jnp.minimum(large, nb - 1)
    return ret + jnp.where(n < max_exact, n, large)


def gather_rows(src, idx):
    return jax.vmap(lambda s, i: s[i])(src, idx)


def dsa_attention(q, k, v, qi, ki, wi, rel_bias):
    b, L, h, dh = q.shape
    n_sel = min(TOPK, L // 4)
    nblk = L // Q_BLOCK
    key_chunk = jnp.arange(L) // CHUNK

    def to_blocks(arr):
        return jnp.moveaxis(arr.reshape((b, nblk, Q_BLOCK) + arr.shape[2:]), 1, 0)

    def one_block(args):
        qb, qib, wb, blk = args
        t = blk * Q_BLOCK + jnp.arange(Q_BLOCK)
        qc = t // CHUNK
        idx_logits = jnp.einsum('bqhd,bsd->bqhs', qib, ki).astype(jnp.float32)
        score = jnp.einsum('bqhs,bqh->bqs', jax.nn.relu(idx_logits), wb.astype(jnp.float32))
        admissible = key_chunk[None, :] <= qc[:, None]
        score = jnp.where(admissible[None], score, NEG_INF)
        _, sel = lax.top_k(score, n_sel)
        valid = (sel // CHUNK) <= qc[None, :, None]
        k_sel = gather_rows(k, sel)
        v_sel = gather_rows(v, sel)
        logits = jnp.einsum('bqhd,bqkd->bqhk', qb, k_sel).astype(jnp.float32) * ATT_SCALE
        bias = rel_bias[t5_bucket(sel - t[None, :, None])]
        logits = logits + jnp.moveaxis(bias, -1, 2).astype(jnp.float32)
        logits = jnp.where(valid[:, :, None, :], logits, NEG_INF)
        p = jax.nn.softmax(logits, axis=-1).astype(v.dtype)
        return jnp.einsum('bqhk,bqkd->bqhd', p, v_sel)

    out = lax.map(one_block, (to_blocks(q), to_blocks(qi), to_blocks(wi * IDX_W_SCALE),
                              jnp.arange(nblk)))
    return jnp.moveaxis(out, 0, 1).reshape(b, L, h * dh)


def setup_inputs(seed: int = 0) -> dict:
    key = jax.random.key(seed)
    ks = jax.random.split(key, 24)
    f32 = jnp.float32

    def nrm(k, shape, s):
        return s * jax.random.normal(k, shape, f32)

    x = jax.random.normal(ks[0], (BATCH, SEQ, D_MODEL), f32)
    norm_w = 1.0 + nrm(ks[1], (DEPTH, D_MODEL), 0.02)
    w_in = nrm(ks[2], (DEPTH, D_MODEL, D_IN), D_MODEL ** -0.5)
    n_idx = jnp.arange(S5_STATE, dtype=f32)
    s5_A_re = -0.5 + nrm(ks[3], (DEPTH, S5_GROUPS, S5_STATE), 0.01)
    s5_A_im = math.pi * n_idx + nrm(ks[4], (DEPTH, S5_GROUPS, S5_STATE), 0.01)
    s5_log_dt = jax.random.uniform(ks[5], (DEPTH, S5_GROUPS), f32, math.log(1e-3), math.log(1e-1))
    s5_B_re = nrm(ks[6], (DEPTH, S5_GROUPS, S5_STATE, S5_GROUP), (2 * S5_GROUP) ** -0.5)
    s5_B_im = nrm(ks[7], (DEPTH, S5_GROUPS, S5_STATE, S5_GROUP), (2 * S5_GROUP) ** -0.5)
    s5_C_re = nrm(ks[8], (DEPTH, S5_GROUPS, S5_GROUP, S5_STATE), S5_STATE ** -0.5)
    s5_C_im = nrm(ks[9], (DEPTH, S5_GROUPS, S5_GROUP, S5_STATE), S5_STATE ** -0.5)
    s5_D = nrm(ks[10], (DEPTH, S5_GROUPS, S5_GROUP), 0.5)
    s5_glu_w = nrm(ks[11], (DEPTH, S5_WIDTH, S5_WIDTH), S5_WIDTH ** -0.5)
    s5_glu_b = nrm(ks[12], (DEPTH, S5_WIDTH), 0.01)
    ssd_conv_w = nrm(ks[13], (DEPTH, SSD_CONV, SSD_CONV_DIM), SSD_CONV ** -0.5)
    ssd_conv_b = nrm(ks[14], (DEPTH, SSD_CONV_DIM), 0.01)
    dt0 = jnp.exp(jax.random.uniform(ks[15], (DEPTH, SSD_HEADS), f32, math.log(1e-3), math.log(1e-1)))
    ssd_dt_bias = dt0 + jnp.log(-jnp.expm1(-dt0))
    ssd_A_log = jnp.log(jax.random.uniform(ks[16], (DEPTH, SSD_HEADS), f32, 1.0, 16.0))
    ssd_D = 1.0 + nrm(ks[17], (DEPTH, SSD_HEADS), 0.1)
    ssd_norm_w = 1.0 + nrm(ks[18], (DEPTH, SSD_WIDTH), 0.02)
    rel_bias = nrm(ks[19], (N_BUCKETS, ATT_HEADS), 0.5)
    w_out = nrm(ks[20], (DEPTH, D_MIX, D_MODEL), D_MIX ** -0.5)
    final_norm_w = 1.0 + nrm(ks[21], (D_MODEL,), 0.02)
    return {'x': x, 'norm_w': norm_w, 'w_in': w_in,
            's5_A_re': s5_A_re, 's5_A_im': s5_A_im, 's5_log_dt': s5_log_dt,
            's5_B_re': s5_B_re, 's5_B_im': s5_B_im, 's5_C_re': s5_C_re, 's5_C_im': s5_C_im,
            's5_D': s5_D, 's5_glu_w': s5_glu_w, 's5_glu_b': s5_glu_b,
            'ssd_conv_w': ssd_conv_w, 'ssd_conv_b': ssd_conv_b, 'ssd_dt_bias': ssd_dt_bias,
            'ssd_A_log': ssd_A_log, 'ssd_D': ssd_D, 'ssd_norm_w': ssd_norm_w,
            'rel_bias': rel_bias, 'w_out': w_out, 'final_norm_w': final_norm_w}


def reference(x, norm_w, w_in, s5_A_re, s5_A_im, s5_log_dt, s5_B_re, s5_B_im, s5_C_re, s5_C_im,
              s5_D, s5_glu_w, s5_glu_b, ssd_conv_w, ssd_conv_b, ssd_dt_bias, ssd_A_log, ssd_D,
              ssd_norm_w, rel_bias, w_out, final_norm_w):
    b, L, _ = x.shape
    offsets = [int(o) for o in np.cumsum(IN_SPLITS)[:-1]]
    for l in range(DEPTH):
        h = rmsnorm(x, norm_w[l])
        proj = jnp.einsum('bld,de->ble', h, w_in[l])
        (s5_u, s5_z, ssd_z, ssd_xbc, ssd_dt, att_q, att_k, att_v,
         idx_q, idx_k, idx_w, att_z) = jnp.split(proj, offsets, axis=-1)
        y_s5 = s5_mixer(s5_u, s5_A_re[l], s5_A_im[l], s5_log_dt[l], s5_B_re[l], s5_B_im[l],
                        s5_C_re[l], s5_C_im[l], s5_D[l], s5_glu_w[l], s5_glu_b[l]) * jax.nn.silu(s5_z)
        y_ssd = ssd_mixer(ssd_z, ssd_xbc, ssd_dt, ssd_conv_w[l], ssd_conv_b[l], ssd_dt_bias[l],
                          ssd_A_log[l], ssd_D[l], ssd_norm_w[l])
        y_att = dsa_attention(att_q.reshape(b, L, ATT_HEADS, ATT_HEAD_DIM), att_k, att_v,
                              idx_q.reshape(b, L, IDX_HEADS, IDX_DIM), idx_k, idx_w,
                              rel_bias) * jax.nn.silu(att_z)
        mix = jnp.concatenate([y_s5, y_ssd, y_att], axis=-1)
        x = x + jnp.einsum('ble,ed->bld', mix, w_out[l])
    return rmsnorm(x, final_norm_w)
```

```python
import functools
import math

import numpy as np
import jax
import jax.numpy as jnp
from jax import lax
from jax.experimental import pallas as pl
from jax.experimental.pallas import tpu as pltpu

F32 = jnp.float32
BF16 = jnp.bfloat16

EPS = 1e-6
CHUNK = 64

S5_GROUP = 16
S5_STATE = 64
SSD_HEAD_DIM = 64
SSD_STATE = 128
SSD_GROUPS = 4
SSD_CONV = 4
ATT_HEAD_DIM = 64
IDX_DIM = 64
TOPK = 256
N_BUCKETS = 32
MAX_DISTANCE = 128

LANE = 128
SUBLANE = 8
VMEM_LIMIT = 56 * 1024 * 1024

INT_MIN = -2 ** 31


def _cparams(sem):
    return pltpu.CompilerParams(dimension_semantics=sem, vmem_limit_bytes=VMEM_LIMIT)


def _inproj_body(x_ref, nw_ref, w_ref, o_ref, h_ref):
    @pl.when(pl.program_id(1) == 0)
    def _():
        xf = x_ref[...]
        ms = jnp.mean(xf * xf, axis=-1, keepdims=True)
        h_ref[...] = (xf * lax.rsqrt(ms + EPS) * nw_ref[...]).astype(BF16)

    o_ref[...] = jnp.dot(h_ref[...], w_ref[...], preferred_element_type=F32)


def _inproj(x2, nw, w, tm, tn):
    m, d = x2.shape
    n = w.shape[1]
    return pl.pallas_call(
        _inproj_body,
        grid=(m // tm, n // tn),
        in_specs=[pl.BlockSpec((tm, d), lambda i, j: (i, 0)),
                  pl.BlockSpec((1, d), lambda i, j: (0, 0)),
                  pl.BlockSpec((d, tn), lambda i, j: (0, j))],
        out_specs=pl.BlockSpec((tm, tn), lambda i, j: (i, j)),
        out_shape=jax.ShapeDtypeStruct((m, n), F32),
        scratch_shapes=[pltpu.VMEM((tm, d), BF16)],
        compiler_params=_cparams(("parallel", "arbitrary")),
        name="inproj",
    )(x2, nw, w)


def _outproj_body(ys5_ref, yssd_ref, yatt_ref, w_ref, x_ref, fnw_ref, o_ref, *, w5, wssd, final):
    acc = x_ref[...]
    acc += jnp.dot(ys5_ref[...], w_ref[0:w5, :], preferred_element_type=F32)
    acc += jnp.dot(yssd_ref[...], w_ref[w5:w5 + wssd, :], preferred_element_type=F32)
    acc += jnp.dot(yatt_ref[...], w_ref[w5 + wssd:, :], preferred_element_type=F32)
    if final:
        ms = jnp.mean(acc * acc, axis=-1, keepdims=True)
        acc = acc * lax.rsqrt(ms + EPS) * fnw_ref[...]
    o_ref[...] = acc


def _outproj(ys5, yssd, yatt, w, x2, fnw, tm, final):
    m, d = x2.shape
    w5, wssd, watt = ys5.shape[1], yssd.shape[1], yatt.shape[1]
    body = functools.partial(_outproj_body, w5=w5, wssd=wssd, final=final)
    return pl.pallas_call(
        body,
        grid=(m // tm,),
        in_specs=[pl.BlockSpec((tm, w5), lambda i: (i, 0)),
                  pl.BlockSpec((tm, wssd), lambda i: (i, 0)),
                  pl.BlockSpec((tm, watt), lambda i: (i, 0)),
                  pl.BlockSpec(w.shape, lambda i: (0, 0)),
                  pl.BlockSpec((tm, d), lambda i: (i, 0)),
                  pl.BlockSpec((1, d), lambda i: (0, 0))],
        out_specs=pl.BlockSpec((tm, d), lambda i: (i, 0)),
        out_shape=jax.ShapeDtypeStruct((m, d), F32),
        compiler_params=_cparams(("parallel",)),
        name="outproj",
    )(ys5, yssd, yatt, w, x2, fnw)


def _shift_down_one(x):
    rolled = pltpu.roll(x, 1, axis=0)
    row = lax.broadcasted_iota(jnp.int32, x.shape, 0)
    return jnp.where(row == 0, 0.0, rolled)


def _s5_body(u_ref, z_ref, bblk_ref, cblk_ref, lam_ref, lamseg_ref, d_ref, gw_ref, gb_ref,
             o_ref, uperm_ref, xs_ref, yperm_ref, *, nchunk, cw, sw, rt):
    L = u_ref.shape[0]
    seg = L // SUBLANE

    for c in range(nchunk):
        for j in range(SUBLANE):
            for k in range(cw // LANE):
                lo = c * cw + k * LANE
                uperm_ref[k, pl.ds(j, seg, stride=SUBLANE), :] = u_ref[pl.ds(j * seg, seg), lo:lo + LANE]
        for r0 in range(0, L, rt):
            up = jnp.concatenate([uperm_ref[k, r0:r0 + rt, :] for k in range(cw // LANE)], axis=1)
            xs_ref[r0:r0 + rt, :] = jnp.dot(up.astype(BF16), bblk_ref[c], preferred_element_type=F32)
        lr = lam_ref[c, 0]
        li = lam_ref[c, 1]

        def scan_step(tau, carry):
            xr, xi = carry
            row = pl.multiple_of(tau * SUBLANE, SUBLANE)
            nxr = lr * xr - li * xi + xs_ref[pl.ds(row, SUBLANE), 0:sw]
            nxi = lr * xi + li * xr + xs_ref[pl.ds(row, SUBLANE), sw:2 * sw]
            xs_ref[pl.ds(row, SUBLANE), 0:sw] = nxr
            xs_ref[pl.ds(row, SUBLANE), sw:2 * sw] = nxi
            return nxr, nxi

        zero = jnp.zeros((SUBLANE, sw), F32)
        er, ei = lax.fori_loop(0, seg, scan_step, (zero, zero))

        sr = lamseg_ref[c, 0]
        si = lamseg_ref[c, 1]
        cr, ci = zero, zero
        for _ in range(SUBLANE - 1):
            tr = er + (sr * cr - si * ci)
            ti = ei + (sr * ci + si * cr)
            cr, ci = _shift_down_one(tr), _shift_down_one(ti)

        def fix_step(tau, carry):
            fr, fi = carry
            nfr = lr * fr - li * fi
            nfi = lr * fi + li * fr
            row = pl.multiple_of(tau * SUBLANE, SUBLANE)
            xs_ref[pl.ds(row, SUBLANE), 0:sw] += nfr
            xs_ref[pl.ds(row, SUBLANE), sw:2 * sw] += nfi
            return nfr, nfi

        lax.fori_loop(0, seg, fix_step, (cr, ci))

        for r0 in range(0, L, rt):
            yc = jnp.dot(xs_ref[r0:r0 + rt, :].astype(BF16), cblk_ref[c], preferred_element_type=F32)
            for k in range(cw // LANE):
                yperm_ref[c * (cw // LANE) + k, r0:r0 + rt, :] = yc[:, k * LANE:(k + 1) * LANE]

    for j in range(SUBLANE):
        y = jnp.concatenate([yperm_ref[k, pl.ds(j, seg, stride=SUBLANE), :]
                             for k in range(yperm_ref.shape[0])], axis=1)
        y = y + d_ref[...] * u_ref[j * seg:(j + 1) * seg, :]
        y = jax.nn.gelu(y)
        g = jnp.dot(y.astype(BF16), gw_ref[...], preferred_element_type=F32) + gb_ref[...]
        y = y * jax.nn.sigmoid(g)
        o_ref[j * seg:(j + 1) * seg, :] = (y * jax.nn.silu(z_ref[j * seg:(j + 1) * seg, :])).astype(BF16)


def _s5_mixer(proj, B, L, u_blk, z_blk, width, prm):
    bblk, cblk, lam, lamseg, dvec, gw, gb = prm
    nchunk, cw, sw2 = bblk.shape
    sw = sw2 // 2
    rt = min(512, L)
    body = functools.partial(_s5_body, nchunk=nchunk, cw=cw, sw=sw, rt=rt)
    full = lambda a: pl.BlockSpec(a.shape, lambda b: (0,) * a.ndim)
    return pl.pallas_call(
        body,
        grid=(B,),
        in_specs=[pl.BlockSpec((L, width), lambda b: (b, u_blk)),
                  pl.BlockSpec((L, width), lambda b: (b, z_blk)),
                  full(bblk), full(cblk), full(lam), full(lamseg), full(dvec), full(gw), full(gb)],
        out_specs=pl.BlockSpec((L, width), lambda b: (b, 0)),
        out_shape=jax.ShapeDtypeStruct((B * L, width), BF16),
        scratch_shapes=[pltpu.VMEM((cw // LANE, L, LANE), F32),
                        pltpu.VMEM((L, 2 * sw), F32),
                        pltpu.VMEM((width // LANE, L, LANE), F32)],
        compiler_params=_cparams(("parallel",)),
        name="s5_mixer",
    )(proj, proj, bblk, cblk, lam, lamseg, dvec, gw, gb)


def _cpow(re, im, n):
    rr, ri = jnp.ones_like(re), jnp.zeros_like(im)
    br, bi = re, im
    while n:
        if n & 1:
            rr, ri = rr * br - ri * bi, rr * bi + ri * br
        br, bi = br * br - bi * bi, 2.0 * br * bi
        n >>= 1
    return rr, ri


def _s5_params(A_re, A_im, log_dt, B_re, B_im, C_re, C_im, D, glu_w, glu_b, seg, nchunk):
    G, P, C = B_re.shape
    dt = jnp.exp(log_dt)[:, None]
    lre = jnp.minimum(A_re, -1e-4)
    lim = A_im
    mag = jnp.exp(lre * dt)
    lbr = mag * jnp.cos(lim * dt)
    lbi = mag * jnp.sin(lim * dt)
    nr, ni = lbr - 1.0, lbi
    den = lre * lre + lim * lim
    fr = (nr * lre + ni * lim) / den
    fi = (ni * lre - nr * lim) / den
    bbr = fr[..., None] * B_re - fi[..., None] * B_im
    bbi = fr[..., None] * B_im + fi[..., None] * B_re
    gc = G // nchunk
    eye = jnp.eye(gc, dtype=F32)

    def blockdiag_in(bb):
        t = jnp.transpose(bb, (0, 2, 1)).reshape(nchunk, gc, C, P)
        return jnp.einsum('ngcp,gh->ngchp', t, eye).reshape(nchunk, gc * C, gc * P)

    def blockdiag_out(cc):
        t = jnp.transpose(cc, (0, 2, 1)).reshape(nchunk, gc, P, C)
        return jnp.einsum('ngpc,gh->ngphc', t, eye).reshape(nchunk, gc * P, gc * C)

    bblk = jnp.concatenate([blockdiag_in(bbr), blockdiag_in(bbi)], axis=-1).astype(BF16)
    cblk = jnp.concatenate([blockdiag_out(C_re), blockdiag_out(-C_im)], axis=1).astype(BF16)

    def rows(v):
        return jnp.broadcast_to(v.reshape(nchunk, 1, gc * P), (nchunk, SUBLANE, gc * P))

    lam = jnp.stack([rows(lbr), rows(lbi)], axis=1)
    pr, pi = _cpow(lbr, lbi, seg)
    lamseg = jnp.stack([rows(pr), rows(pi)], axis=1)
    return (bblk, cblk, lam, lamseg, D.reshape(1, G * C), glu_w.astype(BF16), glu_b.reshape(1, -1))


def _split3(x):
    h = x.astype(BF16)
    r = x - h.astype(F32)
    m = r.astype(BF16)
    l = (r - m.astype(F32)).astype(BF16)
    return h, m, l


def _dot_f32(a, b_bf16):
    h, m, l = _split3(a)
    dot = lambda p: jnp.dot(p, b_bf16, preferred_element_type=F32)
    return dot(h) + dot(m) + dot(l)


def _ssd_body(z_ref, xbc_ref, dt_ref, cw_ref, cb_ref, dtb_ref, a_ref, dx_ref, nw_ref, e_ref,
              o_ref, state_ref, xpad_ref, *, nheads, hd, ns, ngroups):
    Q = z_ref.shape[0]
    width = nheads * hd
    gw = width // ngroups
    halo = SUBLANE

    @pl.when(pl.program_id(1) == 0)
    def _():
        state_ref[...] = jnp.zeros_like(state_ref)
        xpad_ref[0:halo, :] = jnp.zeros((halo, xpad_ref.shape[1]), F32)

    xpad_ref[halo:halo + Q, :] = xbc_ref[...]
    acc = cb_ref[...] + cw_ref[0:1, :] * xpad_ref[halo - 3:halo - 3 + Q, :]
    for k in range(1, SSD_CONV):
        acc += cw_ref[k:k + 1, :] * xpad_ref[halo - 3 + k:halo - 3 + k + Q, :]
    xpad_ref[0:halo, :] = xpad_ref[Q:Q + halo, :]
    xc = jax.nn.silu(acc)
    xs = xc[:, 0:width]
    bm = xc[:, width:width + ngroups * ns].astype(BF16)
    cm = xc[:, width + ngroups * ns:].astype(BF16)

    lane = lax.broadcasted_iota(jnp.int32, (Q, LANE), 1)
    dt = jnp.where(lane < nheads, jax.nn.softplus(dt_ref[...] + dtb_ref[...]), 0.0)
    a = dt * a_ref[...]
    rowi = lax.broadcasted_iota(jnp.int32, (Q, Q), 0)
    coli = lax.broadcasted_iota(jnp.int32, (Q, Q), 1)
    tril = coli <= rowi
    acs = _dot_f32_lhs(tril.astype(BF16), a)
    acs_t = acs.T
    last = acs[Q - 1:Q, :]
    emat = e_ref[...]
    dt_x = _expand(dt, emat)
    dec_x = _expand(jnp.exp(acs), emat)
    dte_x = _expand(jnp.exp(last - acs), emat)
    dlast_x = dec_x[Q - 1:Q, :]

    xd = xs * dt_x
    xd_b = xd.astype(BF16)
    xw_b = (xd * dte_x).astype(BF16)
    st = state_ref[...]
    st_b = st.astype(BF16)

    y_parts = []
    new_state = []
    hlane = lax.broadcasted_iota(jnp.int32, (Q, LANE), 1) < hd
    heads_per_group = nheads // ngroups
    for g in range(ngroups):
        cg = cm[:, g * ns:(g + 1) * ns]
        bg = bm[:, g * ns:(g + 1) * ns]
        cb = lax.dot_general(cg, bg, (((1,), (1,)), ((), ())), preferred_element_type=F32)
        y_off = jnp.dot(cg, st_b[:, g * gw:(g + 1) * gw], preferred_element_type=F32)
        pair_out = []
        for pr in range(gw // LANE):
            xpair = xd_b[:, g * gw + pr * LANE: g * gw + (pr + 1) * LANE]
            res = []
            for hh in range(LANE // hd):
                h = g * heads_per_group + pr * (LANE // hd) + hh
                diff = acs[:, h:h + 1] - acs_t[h:h + 1, :]
                s_h = jnp.where(tril, cb * jnp.exp(jnp.minimum(diff, 0.0)), 0.0).astype(BF16)
                res.append(jnp.dot(s_h, xpair, preferred_element_type=F32))
            pair_out.append(jnp.where(hlane, res[0], res[1]))
        y_diag = jnp.concatenate(pair_out, axis=1)
        y_parts.append(y_diag + y_off * dec_x[:, g * gw:(g + 1) * gw])
        upd = lax.dot_general(bg, xw_b[:, g * gw:(g + 1) * gw], (((0,), (0,)), ((), ())),
                              preferred_element_type=F32)
        new_state.append(st[:, g * gw:(g + 1) * gw] * dlast_x[:, g * gw:(g + 1) * gw] + upd)
    state_ref[...] = jnp.concatenate(new_state, axis=1)

    y = jnp.concatenate(y_parts, axis=1) + xs * dx_ref[...]
    gt = y * jax.nn.silu(z_ref[...])
    ms = jnp.mean(gt * gt, axis=-1, keepdims=True)
    o_ref[...] = (gt * lax.rsqrt(ms + EPS) * nw_ref[...]).astype(BF16)


def _dot_f32_lhs(a_bf16, b):
    h, m, l = _split3(b)
    dot = lambda p: jnp.dot(a_bf16, p, preferred_element_type=F32)
    return dot(h) + dot(m) + dot(l)


def _expand(v, emat):
    return _dot_f32(v, emat)


def _ssd_mixer(proj, B, L, z_blk, xbc_blk, dt_blk, prm, Q):
    cw, cb, dtb, avec, dx, nw, emat = prm
    width = nw.shape[1]
    cdim = cw.shape[1]
    nheads = width // SSD_HEAD_DIM
    nt = L // Q
    body = functools.partial(_ssd_body, nheads=nheads, hd=SSD_HEAD_DIM, ns=SSD_STATE, ngroups=SSD_GROUPS)
    full = lambda a: pl.BlockSpec(a.shape, lambda b, t: (0,) * a.ndim)
    return pl.pallas_call(
        body,
        grid=(B, nt),
        in_specs=[pl.BlockSpec((Q, width), lambda b, t: (b * nt + t, z_blk)),
                  pl.BlockSpec((Q, cdim), lambda b, t: (b * nt + t, xbc_blk)),
                  pl.BlockSpec((Q, LANE), lambda b, t: (b * nt + t, dt_blk)),
                  full(cw), full(cb), full(dtb), full(avec), full(dx), full(nw), full(emat)],
        out_specs=pl.BlockSpec((Q, width), lambda b, t: (b * nt + t, 0)),
        out_shape=jax.ShapeDtypeStruct((B * L, width), BF16),
        scratch_shapes=[pltpu.VMEM((SSD_STATE, width), F32),
                        pltpu.VMEM((Q + SUBLANE, cdim), F32)],
        compiler_params=_cparams(("parallel", "arbitrary")),
        name="ssd_mixer",
    )(proj, proj, proj, cw, cb, dtb, avec, dx, nw, emat)


def _ssd_params(conv_w, conv_b, dt_bias, A_log, D, norm_w):
    nheads = dt_bias.shape[0]
    width = nheads * SSD_HEAD_DIM
    pad = lambda v: jnp.zeros((1, LANE), F32).at[0, :nheads].set(v)
    emat = jnp.zeros((LANE, width), F32).at[
        jnp.repeat(jnp.arange(nheads), SSD_HEAD_DIM), jnp.arange(width)].set(1.0).astype(BF16)
    return (conv_w, conv_b.reshape(1, -1), pad(dt_bias), pad(-jnp.exp(A_log)),
            jnp.repeat(D, SSD_HEAD_DIM).reshape(1, width), norm_w.reshape(1, width), emat)


def _float_key(x):
    b = lax.bitcast_convert_type(x, jnp.int32)
    return b ^ ((b >> 31) & 0x7FFFFFFF)


def _colsum(x):
    return jnp.sum(x, axis=0, keepdims=True)


def _dsa_body(q_ref, qi_ref, z_ref, wq_ref, kv_ref, band_ref, far_ref, o_ref,
              kpad_ref, kipad_ref, vt_ref, key_ref, m_ref, l_ref, acc_ref, *, nheads, hd, topk):
    i = pl.program_id(1)
    TQ = q_ref.shape[0]
    KB = TQ
    L = kv_ref.shape[0]
    nkb_total = L // KB
    npair = nheads // 2
    NT = (((1,), (1,)), ((), ()))

    @pl.when(i == 0)
    def _():
        lane = lax.broadcasted_iota(jnp.int32, (KB, LANE), 1)
        for kb in range(nkb_total):
            blk = kv_ref[kb * KB:(kb + 1) * KB, 0:LANE]
            kk = jnp.where(lane < hd, blk, pltpu.roll(blk, hd, axis=1))
            kpad_ref[0, kb * KB:(kb + 1) * KB, :] = jnp.where(lane < hd, kk, 0.0).astype(BF16)
            kpad_ref[1, kb * KB:(kb + 1) * KB, :] = jnp.where(lane < hd, 0.0, kk).astype(BF16)
            blk2 = kv_ref[kb * KB:(kb + 1) * KB, LANE:2 * LANE]
            ki2 = jnp.where(lane < hd, blk2, pltpu.roll(blk2, hd, axis=1))
            kipad_ref[0, kb * KB:(kb + 1) * KB, :] = jnp.where(lane < hd, ki2, 0.0).astype(BF16)
            kipad_ref[1, kb * KB:(kb + 1) * KB, :] = jnp.where(lane < hd, 0.0, ki2).astype(BF16)
            vt_ref[:, kb * KB:(kb + 1) * KB] = blk.T[hd:2 * hd, :].astype(BF16)

    qs = [(q_ref[:, p * LANE:(p + 1) * LANE] * (hd ** -0.5)).astype(BF16) for p in range(npair)]
    qis = [qi_ref[:, p * LANE:(p + 1) * LANE].astype(BF16) for p in range(npair)]
    wt = wq_ref[...].T * ((nheads * IDX_DIM) ** -0.5)
    w_rows = [wt[16 + h:17 + h, :] for h in range(nheads)]

    def idx_scores(kb):
        s = jnp.zeros((KB, TQ), F32)
        for h in range(nheads):
            kside = kipad_ref[h % 2, pl.ds(pl.multiple_of(kb * KB, KB), KB), :]
            lg = lax.dot_general(kside, qis[h // 2], NT, preferred_element_type=F32)
            s += jnp.maximum(lg, 0.0) * w_rows[h]
        return s

    def far_keys(kb, _):
        key_ref[pl.ds(pl.multiple_of(kb * KB, KB), KB), :] = _float_key(idx_scores(kb))
        return 0

    lax.fori_loop(0, i, far_keys, 0)
    krow = lax.broadcasted_iota(jnp.int32, (KB, TQ), 0)
    qcol = lax.broadcasted_iota(jnp.int32, (KB, TQ), 1)
    adm = (krow // CHUNK) <= (qcol // CHUNK)
    key_ref[pl.ds(pl.multiple_of(i * KB, KB), KB), :] = jnp.where(adm, _float_key(idx_scores(i)), INT_MIN)

    def count(pred):
        def body(kb, c):
            blk = key_ref[pl.ds(pl.multiple_of(kb * KB, KB), KB), :]
            return c + _colsum(jnp.where(pred(blk, kb), 1.0, 0.0))
        return lax.fori_loop(0, i + 1, body, jnp.zeros((1, TQ), F32))

    def bisect(step, tu):
        bit = jnp.left_shift(jnp.int32(1), 31 - step)
        cand_u = tu | bit
        cand = cand_u ^ INT_MIN
        cnt = count(lambda blk, kb: blk >= cand)
        return jnp.where(cnt >= topk, cand_u, tu)

    thr = lax.fori_loop(0, 32, bisect, jnp.zeros((1, TQ), jnp.int32)) ^ INT_MIN

    need = topk - count(lambda blk, kb: blk > thr)

    def count_eq_below(bound):
        return count(lambda blk, kb: (blk == thr) & (krow + kb * KB < bound))

    nbits = int(L).bit_length()

    def bisect_idx(step, bnd):
        bit = jnp.left_shift(jnp.int32(1), nbits - 1 - step)
        cand = bnd | bit
        return jnp.where(count_eq_below(cand) <= need, cand, bnd)

    bound = lax.fori_loop(0, nbits, bisect_idx, jnp.zeros((1, TQ), jnp.int32))

    m_ref[...] = jnp.full(m_ref.shape, -1e30, F32)
    l_ref[...] = jnp.zeros(l_ref.shape, F32)
    acc_ref[...] = jnp.zeros(acc_ref.shape, F32)

    def attend(kb, bias_of_head):
        start = pl.multiple_of(kb * KB, KB)
        blk = key_ref[pl.ds(start, KB), :]
        pos = krow + kb * KB
        keep = ((blk > thr) | ((blk == thr) & (pos < bound))) & (blk != INT_MIN)
        vt = vt_ref[:, pl.ds(start, KB)]
        for h in range(nheads):
            kside = kpad_ref[h % 2, pl.ds(start, KB), :]
            lg = lax.dot_general(kside, qs[h // 2], NT, preferred_element_type=F32) + bias_of_head(h)
            lg = jnp.where(keep, lg, -1e30)
            m_old = m_ref[h]
            m_new = jnp.maximum(m_old, jnp.max(lg, axis=0, keepdims=True))
            alpha = jnp.exp(m_old - m_new)
            p = jnp.where(keep, jnp.exp(lg - m_new), 0.0)
            l_ref[h] = alpha * l_ref[h] + _colsum(p)
            acc_ref[h] = alpha * acc_ref[h] + jnp.dot(vt, p.astype(BF16), preferred_element_type=F32)
            m_ref[h] = m_new

    def far_attend(kb, _):
        attend(kb, lambda h: far_ref[h])
        return 0

    lax.fori_loop(0, i - 1, far_attend, 0)

    @pl.when(i >= 1)
    def _():
        attend(i - 1, lambda h: band_ref[h, 0:KB, :])

    attend(i, lambda h: band_ref[h, KB:2 * KB, :])

    outs = [acc_ref[h] / l_ref[h] for h in range(nheads)]
    out = jnp.concatenate(outs, axis=0).T
    o_ref[...] = (out * jax.nn.silu(z_ref[...])).astype(BF16)


def _dsa_mixer(proj, B, L, q_blk, z_blk, qi_blk, small_blk, wq_blk, band, far, nheads, TQ):
    width = nheads * ATT_HEAD_DIM
    nq = L // TQ
    small_w = 4 * LANE
    body = functools.partial(_dsa_body, nheads=nheads, hd=ATT_HEAD_DIM, topk=min(TOPK, L // 4))
    full = lambda a: pl.BlockSpec(a.shape, lambda b, i: (0,) * a.ndim)
    return pl.pallas_call(
        body,
        grid=(B, nq),
        in_specs=[pl.BlockSpec((TQ, width), lambda b, i: (b * nq + i, q_blk)),
                  pl.BlockSpec((TQ, width), lambda b, i: (b * nq + i, qi_blk)),
                  pl.BlockSpec((TQ, width), lambda b, i: (b * nq + i, z_blk)),
                  pl.BlockSpec((TQ, LANE), lambda b, i: (b * nq + i, wq_blk)),
                  pl.BlockSpec((L, small_w), lambda b, i: (b, small_blk)),
                  full(band), full(far)],
        out_specs=pl.BlockSpec((TQ, width), lambda b, i: (b * nq + i, 0)),
        out_shape=jax.ShapeDtypeStruct((B * L, width), BF16),
        scratch_shapes=[pltpu.VMEM((2, L, LANE), BF16),
                        pltpu.VMEM((2, L, LANE), BF16),
                        pltpu.VMEM((ATT_HEAD_DIM, L), BF16),
                        pltpu.VMEM((L, TQ), jnp.int32),
                        pltpu.VMEM((nheads, 1, TQ), F32),
                        pltpu.VMEM((nheads, 1, TQ), F32),
                        pltpu.VMEM((nheads, ATT_HEAD_DIM, TQ), F32)],
        compiler_params=_cparams(("parallel", "arbitrary")),
        name="dsa_mixer",
    )(proj, proj, proj, proj, proj, band, far)


def _t5_bucket_static(rel):
    nb = N_BUCKETS // 2
    max_exact = nb // 2
    ret = np.where(rel > 0, nb, 0)
    n = np.abs(rel)
    nf = np.maximum(n, 1).astype(np.float64)
    large = max_exact + (np.log(nf / max_exact) / math.log(MAX_DISTANCE / max_exact)
                         * (nb - max_exact)).astype(np.int32)
    large = np.minimum(large, nb - 1)
    return ret + np.where(n < max_exact, n, large)


def _dsa_bias_tables(rel_bias, TQ):
    a = np.arange(2 * TQ)[:, None]
    j = np.arange(TQ)[None, :]
    band_idx = _t5_bucket_static(a - TQ - j)
    band = jnp.transpose(rel_bias[band_idx], (2, 0, 1))
    far_bucket = int(_t5_bucket_static(np.array([-(TQ + 1)]))[0])
    assert far_bucket == int(_t5_bucket_static(np.array([-(10 ** 6)]))[0])
    far = jnp.broadcast_to(rel_bias[far_bucket][:, None, None], (rel_bias.shape[1], 1, TQ))
    return band, far


def _layout(d_model):
    s5w = d_model // 4
    ssdw = d_model // 2
    attw = d_model // 4
    cdim = ssdw + 2 * SSD_GROUPS * SSD_STATE
    nh_ssd = ssdw // SSD_HEAD_DIM
    nh_att = attw // ATT_HEAD_DIM
    splits = (s5w, s5w, ssdw, cdim, nh_ssd, attw, ATT_HEAD_DIM, ATT_HEAD_DIM,
              nh_att * IDX_DIM, IDX_DIM, nh_att, attw)
    names = ('s5_u', 's5_z', 'ssd_z', 'ssd_xbc', 'ssd_dt', 'att_q', 'att_k', 'att_v',
             'idx_q', 'idx_k', 'idx_w', 'att_z')
    src, o = {}, 0
    for nme, s in zip(names, splits):
        src[nme] = (o, s)
        o += s
    dst = {'s5_u': 0, 's5_z': s5w, 'ssd_z': 2 * s5w, 'ssd_xbc': 2 * s5w + ssdw}
    o = 2 * s5w + ssdw + cdim
    dst['att_q'] = o
    dst['att_z'] = o + attw
    dst['idx_q'] = o + 2 * attw
    small = o + 3 * attw
    dst['att_k'] = small
    dst['att_v'] = small + ATT_HEAD_DIM
    dst['idx_k'] = small + LANE
    dst['ssd_dt'] = small + 2 * LANE
    dst['idx_w'] = small + 2 * LANE + 16
    total = small + 4 * LANE
    return src, dst, total, small


def kernel(x, norm_w, w_in, s5_A_re, s5_A_im, s5_log_dt, s5_B_re, s5_B_im, s5_C_re, s5_C_im, s5_D, s5_glu_w, s5_glu_b, ssd_conv_w, ssd_conv_b, ssd_dt_bias, ssd_A_log, ssd_D, ssd_norm_w, rel_bias, w_out, final_norm_w):
    B, L, d = x.shape
    depth = w_in.shape[0]
    src, dst, total, small = _layout(d)
    s5w, ssdw, attw = d // 4, d // 2, d // 4
    cdim = ssdw + 2 * SSD_GROUPS * SSD_STATE
    nh_att = attw // ATT_HEAD_DIM
    assert ssdw // SSD_HEAD_DIM <= 16 and nh_att <= LANE - 16

    tn = 1024
    total_pad = -(-total // tn) * tn
    perm = np.full((total_pad,), w_in.shape[2], np.int64)
    for nme, (so, sz) in src.items():
        perm[dst[nme]:dst[nme] + sz] = np.arange(so, so + sz)
    w_in_p = jnp.concatenate([w_in, jnp.zeros((depth, d, 1), w_in.dtype)], axis=2)[:, :, perm].astype(BF16)
    w_out_b = w_out.astype(BF16)

    TQ = 128
    Q = 128
    band, far = _dsa_bias_tables(rel_bias, TQ)
    tm_in = min(512, B * L)
    tm_out = min(256, B * L)
    nchunk = 2
    seg = L // SUBLANE

    x2 = x.reshape(B * L, d)
    for l in range(depth):
        proj = _inproj(x2, norm_w[l].reshape(1, d), w_in_p[l], tm_in, tn)
        s5p = _s5_params(s5_A_re[l], s5_A_im[l], s5_log_dt[l], s5_B_re[l], s5_B_im[l], s5_C_re[l],
                         s5_C_im[l], s5_D[l], s5_glu_w[l], s5_glu_b[l], seg, nchunk)
        y_s5 = _s5_mixer(proj, B, L, dst['s5_u'] // s5w, dst['s5_z'] // s5w, s5w, s5p)
        ssdp = _ssd_params(ssd_conv_w[l], ssd_conv_b[l], ssd_dt_bias[l], ssd_A_log[l], ssd_D[l], ssd_norm_w[l])
        y_ssd = _ssd_mixer(proj, B, L, dst['ssd_z'] // ssdw, dst['ssd_xbc'] // cdim, dst['ssd_dt'] // LANE, ssdp, Q)
        y_att = _dsa_mixer(proj, B, L, dst['att_q'] // attw, dst['att_z'] // attw, dst['idx_q'] // attw,
                           small // (4 * LANE), dst['ssd_dt'] // LANE, band, far, nh_att, TQ)
        x2 = _outproj(y_s5, y_ssd, y_att, w_out_b[l], x2, final_norm_w.reshape(1, d), tm_out,
                      final=(l == depth - 1))
    return x2.reshape(B, L, d)
```

```python
import functools
import math

import numpy as np
import jax
import jax.numpy as jnp
from jax import lax
from jax.experimental import pallas as pl
from jax.experimental.pallas import tpu as pltpu

F32 = jnp.float32
BF16 = jnp.bfloat16

EPS = 1e-6
CHUNK = 64

S5_GROUP = 16
S5_STATE = 64
SSD_HEAD_DIM = 64
SSD_STATE = 128
SSD_GROUPS = 4
SSD_CONV = 4
ATT_HEAD_DIM = 64
IDX_DIM = 64
TOPK = 256
N_BUCKETS = 32
MAX_DISTANCE = 128

LANE = 128
SUBLANE = 8
VMEM_LIMIT = 56 * 1024 * 1024

INT_MIN = -2 ** 31
NEG = -1e30
IDX_W_LANE = 16


def _cparams(sem):
    return pltpu.CompilerParams(dimension_semantics=sem, vmem_limit_bytes=VMEM_LIMIT)


def _inproj_body(x_ref, nw_ref, w_ref, o_ref, h_ref):
    @pl.when(pl.program_id(1) == 0)
    def _():
        xf = x_ref[...]
        ms = jnp.mean(xf * xf, axis=-1, keepdims=True)
        h_ref[...] = (xf * lax.rsqrt(ms + EPS) * nw_ref[...]).astype(BF16)

    o_ref[...] = jnp.dot(h_ref[...], w_ref[...], preferred_element_type=F32)


def _inproj(x2, nw, w, tm, tn):
    m, d = x2.shape
    n = w.shape[1]
    return pl.pallas_call(
        _inproj_body,
        grid=(m // tm, n // tn),
        in_specs=[pl.BlockSpec((tm, d), lambda i, j: (i, 0)),
                  pl.BlockSpec((1, d), lambda i, j: (0, 0)),
                  pl.BlockSpec((d, tn), lambda i, j: (0, j))],
        out_specs=pl.BlockSpec((tm, tn), lambda i, j: (i, j)),
        out_shape=jax.ShapeDtypeStruct((m, n), F32),
        scratch_shapes=[pltpu.VMEM((tm, d), BF16)],
        compiler_params=_cparams(("parallel", "arbitrary")),
        name="inproj",
    )(x2, nw, w)


def _outproj_body(ys5_ref, yssd_ref, yatt_ref, w_ref, x_ref, fnw_ref, o_ref, *, w5, wssd, final):
    acc = x_ref[...]
    acc += jnp.dot(ys5_ref[...], w_ref[0:w5, :], preferred_element_type=F32)
    acc += jnp.dot(yssd_ref[...], w_ref[w5:w5 + wssd, :], preferred_element_type=F32)
    acc += jnp.dot(yatt_ref[...], w_ref[w5 + wssd:, :], preferred_element_type=F32)
    if final:
        ms = jnp.mean(acc * acc, axis=-1, keepdims=True)
        acc = acc * lax.rsqrt(ms + EPS) * fnw_ref[...]
    o_ref[...] = acc


def _outproj(ys5, yssd, yatt, w, x2, fnw, tm, final):
    m, d = x2.shape
    w5, wssd, watt = ys5.shape[1], yssd.shape[1], yatt.shape[1]
    body = functools.partial(_outproj_body, w5=w5, wssd=wssd, final=final)
    return pl.pallas_call(
        body,
        grid=(m // tm,),
        in_specs=[pl.BlockSpec((tm, w5), lambda i: (i, 0)),
                  pl.BlockSpec((tm, wssd), lambda i: (i, 0)),
                  pl.BlockSpec((tm, watt), lambda i: (i, 0)),
                  pl.BlockSpec(w.shape, lambda i: (0, 0)),
                  pl.BlockSpec((tm, d), lambda i: (i, 0)),
                  pl.BlockSpec((1, d), lambda i: (0, 0))],
        out_specs=pl.BlockSpec((tm, d), lambda i: (i, 0)),
        out_shape=jax.ShapeDtypeStruct((m, d), F32),
        compiler_params=_cparams(("parallel",)),
        name="outproj",
    )(ys5, yssd, yatt, w, x2, fnw)


def _shift_down_one(x):
    rolled = pltpu.roll(x, 1, axis=0)
    row = lax.broadcasted_iota(jnp.int32, x.shape, 0)
    return jnp.where(row == 0, 0.0, rolled)


def _s5_body(u_ref, z_ref, bblk_ref, cblk_ref, lam_ref, lamseg_ref, d_ref, gw_ref, gb_ref,
             o_ref, uperm_ref, xs_ref, yperm_ref, *, nchunk, cw, sw, rt):
    L = u_ref.shape[0]
    seg = L // SUBLANE

    for c in range(nchunk):
        for j in range(SUBLANE):
            for k in range(cw // LANE):
                lo = c * cw + k * LANE
                uperm_ref[k, pl.ds(j, seg, stride=SUBLANE), :] = u_ref[pl.ds(j * seg, seg), lo:lo + LANE]
        for r0 in range(0, L, rt):
            up = jnp.concatenate([uperm_ref[k, r0:r0 + rt, :] for k in range(cw // LANE)], axis=1)
            xs_ref[r0:r0 + rt, :] = jnp.dot(up.astype(BF16), bblk_ref[c], preferred_element_type=F32)
        lr = lam_ref[c, 0]
        li = lam_ref[c, 1]

        def scan_step(tau, carry):
            xr, xi = carry
            row = pl.multiple_of(tau * SUBLANE, SUBLANE)
            nxr = lr * xr - li * xi + xs_ref[pl.ds(row, SUBLANE), 0:sw]
            nxi = lr * xi + li * xr + xs_ref[pl.ds(row, SUBLANE), sw:2 * sw]
            xs_ref[pl.ds(row, SUBLANE), 0:sw] = nxr
            xs_ref[pl.ds(row, SUBLANE), sw:2 * sw] = nxi
            return nxr, nxi

        zero = jnp.zeros((SUBLANE, sw), F32)
        er, ei = lax.fori_loop(0, seg, scan_step, (zero, zero))

        sr = lamseg_ref[c, 0]
        si = lamseg_ref[c, 1]
        cr, ci = zero, zero
        for _ in range(SUBLANE - 1):
            tr = er + (sr * cr - si * ci)
            ti = ei + (sr * ci + si * cr)
            cr, ci = _shift_down_one(tr), _shift_down_one(ti)

        def fix_step(tau, carry):
            fr, fi = carry
            nfr = lr * fr - li * fi
            nfi = lr * fi + li * fr
            row = pl.multiple_of(tau * SUBLANE, SUBLANE)
            xs_ref[pl.ds(row, SUBLANE), 0:sw] += nfr
            xs_ref[pl.ds(row, SUBLANE), sw:2 * sw] += nfi
            return nfr, nfi

        lax.fori_loop(0, seg, fix_step, (cr, ci))

        for r0 in range(0, L, rt):
            yc = jnp.dot(xs_ref[r0:r0 + rt, :].astype(BF16), cblk_ref[c], preferred_element_type=F32)
            for k in range(cw // LANE):
                yperm_ref[c * (cw // LANE) + k, r0:r0 + rt, :] = yc[:, k * LANE:(k + 1) * LANE]

    for j in range(SUBLANE):
        y = jnp.concatenate([yperm_ref[k, pl.ds(j, seg, stride=SUBLANE), :]
                             for k in range(yperm_ref.shape[0])], axis=1)
        y = y + d_ref[...] * u_ref[j * seg:(j + 1) * seg, :]
        y = jax.nn.gelu(y)
        g = jnp.dot(y.astype(BF16), gw_ref[...], preferred_element_type=F32) + gb_ref[...]
        y = y * jax.nn.sigmoid(g)
        o_ref[j * seg:(j + 1) * seg, :] = (y * jax.nn.silu(z_ref[j * seg:(j + 1) * seg, :])).astype(BF16)


def _s5_mixer(proj, B, L, u_blk, z_blk, width, prm):
    bblk, cblk, lam, lamseg, dvec, gw, gb = prm
    nchunk, cw, sw2 = bblk.shape
    sw = sw2 // 2
    rt = min(512, L)
    body = functools.partial(_s5_body, nchunk=nchunk, cw=cw, sw=sw, rt=rt)
    full = lambda a: pl.BlockSpec(a.shape, lambda b: (0,) * a.ndim)
    return pl.pallas_call(
        body,
        grid=(B,),
        in_specs=[pl.BlockSpec((L, width), lambda b: (b, u_blk)),
                  pl.BlockSpec((L, width), lambda b: (b, z_blk)),
                  full(bblk), full(cblk), full(lam), full(lamseg), full(dvec), full(gw), full(gb)],
        out_specs=pl.BlockSpec((L, width), lambda b: (b, 0)),
        out_shape=jax.ShapeDtypeStruct((B * L, width), BF16),
        scratch_shapes=[pltpu.VMEM((cw // LANE, L, LANE), F32),
                        pltpu.VMEM((L, 2 * sw), F32),
                        pltpu.VMEM((width // LANE, L, LANE), F32)],
        compiler_params=_cparams(("parallel",)),
        name="s5_mixer",
    )(proj, proj, bblk, cblk, lam, lamseg, dvec, gw, gb)


def _cpow(re, im, n):
    rr, ri = jnp.ones_like(re), jnp.zeros_like(im)
    br, bi = re, im
    while n:
        if n & 1:
            rr, ri = rr * br - ri * bi, rr * bi + ri * br
        br, bi = br * br - bi * bi, 2.0 * br * bi
        n >>= 1
    return rr, ri


def _s5_params(A_re, A_im, log_dt, B_re, B_im, C_re, C_im, D, glu_w, glu_b, seg, nchunk):
    G, P, C = B_re.shape
    dt = jnp.exp(log_dt)[:, None]
    lre = jnp.minimum(A_re, -1e-4)
    lim = A_im
    mag = jnp.exp(lre * dt)
    lbr = mag * jnp.cos(lim * dt)
    lbi = mag * jnp.sin(lim * dt)
    nr, ni = lbr - 1.0, lbi
    den = lre * lre + lim * lim
    fr = (nr * lre + ni * lim) / den
    fi = (ni * lre - nr * lim) / den
    bbr = fr[..., None] * B_re - fi[..., None] * B_im
    bbi = fr[..., None] * B_im + fi[..., None] * B_re
    gc = G // nchunk
    eye = jnp.eye(gc, dtype=F32)

    def blockdiag_in(bb):
        t = jnp.transpose(bb, (0, 2, 1)).reshape(nchunk, gc, C, P)
        return jnp.einsum('ngcp,gh->ngchp', t, eye).reshape(nchunk, gc * C, gc * P)

    def blockdiag_out(cc):
        t = jnp.transpose(cc, (0, 2, 1)).reshape(nchunk, gc, P, C)
        return jnp.einsum('ngpc,gh->ngphc', t, eye).reshape(nchunk, gc * P, gc * C)

    bblk = jnp.concatenate([blockdiag_in(bbr), blockdiag_in(bbi)], axis=-1).astype(BF16)
    cblk = jnp.concatenate([blockdiag_out(C_re), blockdiag_out(-C_im)], axis=1).astype(BF16)

    def rows(v):
        return jnp.broadcast_to(v.reshape(nchunk, 1, gc * P), (nchunk, SUBLANE, gc * P))

    lam = jnp.stack([rows(lbr), rows(lbi)], axis=1)
    pr, pi = _cpow(lbr, lbi, seg)
    lamseg = jnp.stack([rows(pr), rows(pi)], axis=1)
    return (bblk, cblk, lam, lamseg, D.reshape(1, G * C), glu_w.astype(BF16), glu_b.reshape(1, -1))


def _split3(x):
    h = x.astype(BF16)
    r = x - h.astype(F32)
    m = r.astype(BF16)
    l = (r - m.astype(F32)).astype(BF16)
    return h, m, l


def _dot_f32(a, b_bf16):
    h, m, l = _split3(a)
    dot = lambda p: jnp.dot(p, b_bf16, preferred_element_type=F32)
    return dot(h) + dot(m) + dot(l)


def _ssd_body(z_ref, xbc_ref, dt_ref, cw_ref, cb_ref, dtb_ref, a_ref, dx_ref, nw_ref, e_ref,
              o_ref, state_ref, xpad_ref, *, nheads, hd, ns, ngroups):
    Q = z_ref.shape[0]
    width = nheads * hd
    gw = width // ngroups
    halo = SUBLANE

    @pl.when(pl.program_id(1) == 0)
    def _():
        state_ref[...] = jnp.zeros_like(state_ref)
        xpad_ref[0:halo, :] = jnp.zeros((halo, xpad_ref.shape[1]), F32)

    xpad_ref[halo:halo + Q, :] = xbc_ref[...]
    acc = cb_ref[...] + cw_ref[0:1, :] * xpad_ref[halo - 3:halo - 3 + Q, :]
    for k in range(1, SSD_CONV):
        acc += cw_ref[k:k + 1, :] * xpad_ref[halo - 3 + k:halo - 3 + k + Q, :]
    xpad_ref[0:halo, :] = xpad_ref[Q:Q + halo, :]
    xc = jax.nn.silu(acc)
    xs = xc[:, 0:width]
    bm = xc[:, width:width + ngroups * ns].astype(BF16)
    cm = xc[:, width + ngroups * ns:].astype(BF16)

    lane = lax.broadcasted_iota(jnp.int32, (Q, LANE), 1)
    dt = jnp.where(lane < nheads, jax.nn.softplus(dt_ref[...] + dtb_ref[...]), 0.0)
    a = dt * a_ref[...]
    rowi = lax.broadcasted_iota(jnp.int32, (Q, Q), 0)
    coli = lax.broadcasted_iota(jnp.int32, (Q, Q), 1)
    tril = coli <= rowi
    acs = _dot_f32_lhs(tril.astype(BF16), a)
    acs_t = acs.T
    last = acs[Q - 1:Q, :]
    emat = e_ref[...]
    dt_x = _expand(dt, emat)
    dec_x = _expand(jnp.exp(acs), emat)
    dte_x = _expand(jnp.exp(last - acs), emat)
    dlast_x = dec_x[Q - 1:Q, :]

    xd = xs * dt_x
    xd_b = xd.astype(BF16)
    xw_b = (xd * dte_x).astype(BF16)
    st = state_ref[...]
    st_b = st.astype(BF16)

    y_parts = []
    new_state = []
    hlane = lax.broadcasted_iota(jnp.int32, (Q, LANE), 1) < hd
    heads_per_group = nheads // ngroups
    for g in range(ngroups):
        cg = cm[:, g * ns:(g + 1) * ns]
        bg = bm[:, g * ns:(g + 1) * ns]
        cb = lax.dot_general(cg, bg, (((1,), (1,)), ((), ())), preferred_element_type=F32)
        y_off = jnp.dot(cg, st_b[:, g * gw:(g + 1) * gw], preferred_element_type=F32)
        pair_out = []
        for pr in range(gw // LANE):
            xpair = xd_b[:, g * gw + pr * LANE: g * gw + (pr + 1) * LANE]
            res = []
            for hh in range(LANE // hd):
                h = g * heads_per_group + pr * (LANE // hd) + hh
                diff = acs[:, h:h + 1] - acs_t[h:h + 1, :]
                s_h = jnp.where(tril, cb * jnp.exp(jnp.minimum(diff, 0.0)), 0.0).astype(BF16)
                res.append(jnp.dot(s_h, xpair, preferred_element_type=F32))
            pair_out.append(jnp.where(hlane, res[0], res[1]))
        y_diag = jnp.concatenate(pair_out, axis=1)
        y_parts.append(y_diag + y_off * dec_x[:, g * gw:(g + 1) * gw])
        upd = lax.dot_general(bg, xw_b[:, g * gw:(g + 1) * gw], (((0,), (0,)), ((), ())),
                              preferred_element_type=F32)
        new_state.append(st[:, g * gw:(g + 1) * gw] * dlast_x[:, g * gw:(g + 1) * gw] + upd)
    state_ref[...] = jnp.concatenate(new_state, axis=1)

    y = jnp.concatenate(y_parts, axis=1) + xs * dx_ref[...]
    gt = y * jax.nn.silu(z_ref[...])
    ms = jnp.mean(gt * gt, axis=-1, keepdims=True)
    o_ref[...] = (gt * lax.rsqrt(ms + EPS) * nw_ref[...]).astype(BF16)


def _dot_f32_lhs(a_bf16, b):
    h, m, l = _split3(b)
    dot = lambda p: jnp.dot(a_bf16, p, preferred_element_type=F32)
    return dot(h) + dot(m) + dot(l)


def _expand(v, emat):
    return _dot_f32(v, emat)


def _ssd_mixer(proj, B, L, z_blk, xbc_blk, dt_blk, prm, Q):
    cw, cb, dtb, avec, dx, nw, emat = prm
    width = nw.shape[1]
    cdim = cw.shape[1]
    nheads = width // SSD_HEAD_DIM
    nt = L // Q
    body = functools.partial(_ssd_body, nheads=nheads, hd=SSD_HEAD_DIM, ns=SSD_STATE, ngroups=SSD_GROUPS)
    full = lambda a: pl.BlockSpec(a.shape, lambda b, t: (0,) * a.ndim)
    return pl.pallas_call(
        body,
        grid=(B, nt),
        in_specs=[pl.BlockSpec((Q, width), lambda b, t: (b * nt + t, z_blk)),
                  pl.BlockSpec((Q, cdim), lambda b, t: (b * nt + t, xbc_blk)),
                  pl.BlockSpec((Q, LANE), lambda b, t: (b * nt + t, dt_blk)),
                  full(cw), full(cb), full(dtb), full(avec), full(dx), full(nw), full(emat)],
        out_specs=pl.BlockSpec((Q, width), lambda b, t: (b * nt + t, 0)),
        out_shape=jax.ShapeDtypeStruct((B * L, width), BF16),
        scratch_shapes=[pltpu.VMEM((SSD_STATE, width), F32),
                        pltpu.VMEM((Q + SUBLANE, cdim), F32)],
        compiler_params=_cparams(("parallel", "arbitrary")),
        name="ssd_mixer",
    )(proj, proj, proj, cw, cb, dtb, avec, dx, nw, emat)


def _ssd_params(conv_w, conv_b, dt_bias, A_log, D, norm_w):
    nheads = dt_bias.shape[0]
    width = nheads * SSD_HEAD_DIM
    pad = lambda v: jnp.zeros((1, LANE), F32).at[0, :nheads].set(v)
    emat = jnp.zeros((LANE, width), F32).at[
        jnp.repeat(jnp.arange(nheads), SSD_HEAD_DIM), jnp.arange(width)].set(1.0).astype(BF16)
    return (conv_w, conv_b.reshape(1, -1), pad(dt_bias), pad(-jnp.exp(A_log)),
            jnp.repeat(D, SSD_HEAD_DIM).reshape(1, width), norm_w.reshape(1, width), emat)


def _float_key(x):
    b = lax.bitcast_convert_type(x, jnp.int32)
    return b ^ ((b >> 31) & 0x7FFFFFFF)


def _dsa_body(q_ref, qi_ref, z_ref, wq_ref, kv_ref, band_ref, far_ref, o_ref,
              kpad_ref, kipad_ref, vt_ref, key_ref, lg_ref, bound_ref, m_ref, acc_ref, *, nheads, hd, topk):
    i = pl.program_id(1)
    TQ = q_ref.shape[0]
    KB = TQ
    L = kv_ref.shape[0]
    nkb_total = L // KB
    npair = nheads // 2
    NT = (((1,), (1,)), ((), ()))

    @pl.when(i == 0)
    def _():
        lane = lax.broadcasted_iota(jnp.int32, (KB, LANE), 1)
        for kb in range(nkb_total):
            blk = kv_ref[kb * KB:(kb + 1) * KB, 0:LANE]
            kk = jnp.where(lane < hd, blk, pltpu.roll(blk, hd, axis=1))
            kpad_ref[0, kb * KB:(kb + 1) * KB, :] = jnp.where(lane < hd, kk, 0.0).astype(BF16)
            kpad_ref[1, kb * KB:(kb + 1) * KB, :] = jnp.where(lane < hd, 0.0, kk).astype(BF16)
            blk2 = kv_ref[kb * KB:(kb + 1) * KB, LANE:2 * LANE]
            ki2 = jnp.where(lane < hd, blk2, pltpu.roll(blk2, hd, axis=1))
            kipad_ref[0, kb * KB:(kb + 1) * KB, :] = jnp.where(lane < hd, ki2, 0.0).astype(BF16)
            kipad_ref[1, kb * KB:(kb + 1) * KB, :] = jnp.where(lane < hd, 0.0, ki2).astype(BF16)
            vt_ref[:, kb * KB:(kb + 1) * KB] = blk.T[hd:2 * hd, :].astype(BF16)

    qs = [(q_ref[:, p * LANE:(p + 1) * LANE] * (hd ** -0.5)).astype(BF16) for p in range(npair)]
    qis = [qi_ref[:, p * LANE:(p + 1) * LANE].astype(BF16) for p in range(npair)]
    wt = wq_ref[...].T * ((nheads * IDX_DIM) ** -0.5)
    w_rows = [wt[IDX_W_LANE + h:IDX_W_LANE + h + 1, :] for h in range(nheads)]

    RB = LANE

    def rows(kb, r=0, n=KB):
        return pl.ds(pl.multiple_of(kb * KB + r, SUBLANE), n)

    def fold_sum(x):
        return jnp.sum(x.reshape(x.shape[0] // SUBLANE, SUBLANE, TQ), axis=0)

    def fold_max(x):
        return jnp.max(x.reshape(x.shape[0] // SUBLANE, SUBLANE, TQ), axis=0)

    def idx_keys(kb, diagonal):
        for r in range(0, KB, RB):
            s = jnp.zeros((RB, TQ), F32)
            for h in range(nheads):
                lg = lax.dot_general(kipad_ref[h % 2, rows(kb, r, RB), :], qis[h // 2], NT,
                                     preferred_element_type=F32)
                s += jnp.maximum(lg, 0.0) * w_rows[h]
            key = _float_key(s)
            if diagonal:
                krow = lax.broadcasted_iota(jnp.int32, (RB, TQ), 0) + r
                qcol = lax.broadcasted_iota(jnp.int32, (RB, TQ), 1)
                key = jnp.where((krow // CHUNK) <= (qcol // CHUNK), key, INT_MIN)
            key_ref[rows(kb, r, RB), :] = key

    def far_keys(kb, _):
        idx_keys(kb, False)
        return 0

    lax.fori_loop(0, i, far_keys, 0)
    idx_keys(i, True)

    def count(preds):
        def body(kb, cs):
            blk = key_ref[rows(kb), :]
            return tuple(c + fold_sum(jnp.where(p(blk, kb), 1.0, 0.0)) for c, p in zip(cs, preds))
        z = jnp.zeros((SUBLANE, TQ), F32)
        cs = lax.fori_loop(0, i + 1, body, (z,) * len(preds))
        return [jnp.sum(c, axis=0, keepdims=True) for c in cs]

    def bisect(step, tu):
        bit = jnp.left_shift(jnp.int32(1), 31 - step)
        cand_u = tu | bit
        cand = cand_u ^ INT_MIN
        cnt, = count([lambda blk, kb: blk >= cand])
        return jnp.where(cnt >= topk, cand_u, tu)

    thr = lax.fori_loop(0, 32, bisect, jnp.zeros((1, TQ), jnp.int32)) ^ INT_MIN

    cgt, ceq = count([lambda blk, kb: blk > thr, lambda blk, kb: blk == thr])
    need = topk - cgt
    nbits = int(L).bit_length()
    bound_ref[...] = jnp.full((1, TQ), 2 ** nbits - 1, jnp.int32)

    @pl.when(jnp.max(ceq - need) > 0.0)
    def _():
        krow = lax.broadcasted_iota(jnp.int32, (KB, TQ), 0)

        def bisect_idx(step, bnd):
            bit = jnp.left_shift(jnp.int32(1), nbits - 1 - step)
            cand = bnd | bit
            cnt, = count([lambda blk, kb: (blk == thr) & (krow + kb * KB < cand)])
            return jnp.where(cnt <= need, cand, bnd)

        bound_ref[...] = lax.fori_loop(0, nbits, bisect_idx, jnp.zeros((1, TQ), jnp.int32))

    bound = bound_ref[...]

    m_ref[...] = jnp.full(m_ref.shape, NEG, F32)

    def store_logits(kb, bias_of_head):
        for r in range(0, KB, RB):
            blk = key_ref[rows(kb, r, RB), :]
            pos = lax.broadcasted_iota(jnp.int32, (RB, TQ), 0) + (kb * KB + r)
            keep = ((blk > thr) | ((blk == thr) & (pos < bound))) & (blk != INT_MIN)
            mask = jnp.where(keep, 0.0, NEG)
            for h in range(nheads):
                lg = lax.dot_general(kpad_ref[h % 2, rows(kb, r, RB), :], qs[h // 2], NT,
                                     preferred_element_type=F32)
                lg = lg + bias_of_head(h, r) + mask
                lg_ref[h, rows(kb, r, RB), :] = lg
                m_ref[h] = jnp.maximum(m_ref[h], fold_max(lg))

    def far_logits(kb, _):
        store_logits(kb, lambda h, r: far_ref[h])
        return 0

    lax.fori_loop(0, i - 1, far_logits, 0)

    @pl.when(i >= 1)
    def _():
        store_logits(i - 1, lambda h, r: band_ref[h, r:r + RB, :])

    store_logits(i, lambda h, r: band_ref[h, KB + r:KB + r + RB, :])

    m_fin = [jnp.max(m_ref[h], axis=0, keepdims=True) for h in range(nheads)]
    acc_ref[...] = jnp.zeros(acc_ref.shape, F32)

    def accumulate(kb, ls):
        vt = vt_ref[:, rows(kb)]
        out = []
        for h in range(nheads):
            p = jnp.exp(lg_ref[h, rows(kb), :] - m_fin[h])
            out.append(ls[h] + fold_sum(p))
            acc_ref[h] += jnp.dot(vt, p.astype(BF16), preferred_element_type=F32)
        return tuple(out)

    ls = lax.fori_loop(0, i + 1, accumulate, (jnp.zeros((SUBLANE, TQ), F32),) * nheads)

    outs = [acc_ref[h] / jnp.sum(ls[h], axis=0, keepdims=True) for h in range(nheads)]
    out = jnp.concatenate(outs, axis=0).T
    o_ref[...] = (out * jax.nn.silu(z_ref[...])).astype(BF16)


def _dsa_mixer(proj, B, L, q_blk, z_blk, qi_blk, small_blk, wq_blk, band, far, nheads, TQ):
    width = nheads * ATT_HEAD_DIM
    nq = L // TQ
    small_w = 4 * LANE
    body = functools.partial(_dsa_body, nheads=nheads, hd=ATT_HEAD_DIM, topk=min(TOPK, L // 4))
    full = lambda a: pl.BlockSpec(a.shape, lambda b, i: (0,) * a.ndim)
    return pl.pallas_call(
        body,
        grid=(B, nq),
        in_specs=[pl.BlockSpec((TQ, width), lambda b, i: (b * nq + i, q_blk)),
                  pl.BlockSpec((TQ, width), lambda b, i: (b * nq + i, qi_blk)),
                  pl.BlockSpec((TQ, width), lambda b, i: (b * nq + i, z_blk)),
                  pl.BlockSpec((TQ, LANE), lambda b, i: (b * nq + i, wq_blk)),
                  pl.BlockSpec((L, small_w), lambda b, i: (b, small_blk)),
                  full(band), full(far)],
        out_specs=pl.BlockSpec((TQ, width), lambda b, i: (b * nq + i, 0)),
        out_shape=jax.ShapeDtypeStruct((B * L, width), BF16),
        scratch_shapes=[pltpu.VMEM((2, L, LANE), BF16),
                        pltpu.VMEM((2, L, LANE), BF16),
                        pltpu.VMEM((ATT_HEAD_DIM, L), BF16),
                        pltpu.VMEM((L, TQ), jnp.int32),
                        pltpu.VMEM((nheads, L, TQ), F32),
                        pltpu.VMEM((1, TQ), jnp.int32),
                        pltpu.VMEM((nheads, SUBLANE, TQ), F32),
                        pltpu.VMEM((nheads, ATT_HEAD_DIM, TQ), F32)],
        compiler_params=_cparams(("parallel", "arbitrary")),
        name="dsa_mixer",
    )(proj, proj, proj, proj, proj, band, far)


def _t5_bucket_static(rel):
    nb = N_BUCKETS // 2
    max_exact = nb // 2
    ret = np.where(rel > 0, nb, 0)
    n = np.abs(rel)
    nf = np.maximum(n, 1).astype(np.float64)
    large = max_exact + (np.log(nf / max_exact) / math.log(MAX_DISTANCE / max_exact)
                         * (nb - max_exact)).astype(np.int32)
    large = np.minimum(large, nb - 1)
    return ret + np.where(n < max_exact, n, large)


def _dsa_bias_tables(rel_bias, TQ):
    a = np.arange(2 * TQ)[:, None]
    j = np.arange(TQ)[None, :]
    band_idx = _t5_bucket_static(a - TQ - j)
    band = jnp.transpose(rel_bias[band_idx], (2, 0, 1))
    far_bucket = int(_t5_bucket_static(np.array([-(TQ + 1)]))[0])
    assert far_bucket == int(_t5_bucket_static(np.array([-(10 ** 6)]))[0])
    far = jnp.broadcast_to(rel_bias[far_bucket][:, None, None], (rel_bias.shape[1], 1, TQ))
    return band, far


def _layout(d_model):
    s5w = d_model // 4
    ssdw = d_model // 2
    attw = d_model // 4
    cdim = ssdw + 2 * SSD_GROUPS * SSD_STATE
    nh_ssd = ssdw // SSD_HEAD_DIM
    nh_att = attw // ATT_HEAD_DIM
    splits = (s5w, s5w, ssdw, cdim, nh_ssd, attw, ATT_HEAD_DIM, ATT_HEAD_DIM,
              nh_att * IDX_DIM, IDX_DIM, nh_att, attw)
    names = ('s5_u', 's5_z', 'ssd_z', 'ssd_xbc', 'ssd_dt', 'att_q', 'att_k', 'att_v',
             'idx_q', 'idx_k', 'idx_w', 'att_z')
    src, o = {}, 0
    for nme, s in zip(names, splits):
        src[nme] = (o, s)
        o += s
    dst = {'s5_u': 0, 's5_z': s5w, 'ssd_z': 2 * s5w, 'ssd_xbc': 2 * s5w + ssdw}
    o = 2 * s5w + ssdw + cdim
    dst['att_q'] = o
    dst['att_z'] = o + attw
    dst['idx_q'] = o + 2 * attw
    small = o + 3 * attw
    dst['att_k'] = small
    dst['att_v'] = small + ATT_HEAD_DIM
    dst['idx_k'] = small + LANE
    dst['ssd_dt'] = small + 2 * LANE
    dst['idx_w'] = small + 2 * LANE + IDX_W_LANE
    total = small + 4 * LANE
    return src, dst, total, small


def kernel(x, norm_w, w_in, s5_A_re, s5_A_im, s5_log_dt, s5_B_re, s5_B_im, s5_C_re, s5_C_im, s5_D, s5_glu_w, s5_glu_b, ssd_conv_w, ssd_conv_b, ssd_dt_bias, ssd_A_log, ssd_D, ssd_norm_w, rel_bias, w_out, final_norm_w):
    B, L, d = x.shape
    depth = w_in.shape[0]
    src, dst, total, small = _layout(d)
    s5w, ssdw, attw = d // 4, d // 2, d // 4
    cdim = ssdw + 2 * SSD_GROUPS * SSD_STATE
    nh_att = attw // ATT_HEAD_DIM
    assert ssdw // SSD_HEAD_DIM <= IDX_W_LANE and nh_att <= LANE - IDX_W_LANE

    tn = 1024
    total_pad = -(-total // tn) * tn
    w_in_b = w_in.astype(BF16)
    pieces, o = [], 0
    for nme in sorted(dst, key=dst.get):
        if dst[nme] > o:
            pieces.append(jnp.zeros((depth, d, dst[nme] - o), BF16))
        so, sz = src[nme]
        pieces.append(w_in_b[:, :, so:so + sz])
        o = dst[nme] + sz
    pieces.append(jnp.zeros((depth, d, total_pad - o), BF16))
    w_in_p = jnp.concatenate(pieces, axis=2)
    w_out_b = w_out.astype(BF16)

    TQ = 256
    Q = 128
    band, far = _dsa_bias_tables(rel_bias, TQ)
    tm_in = min(512, B * L)
    tm_out = min(256, B * L)
    nchunk = 2
    seg = L // SUBLANE

    x2 = x.reshape(B * L, d)
    for l in range(depth):
        proj = _inproj(x2, norm_w[l].reshape(1, d), w_in_p[l], tm_in, tn)
        s5p = _s5_params(s5_A_re[l], s5_A_im[l], s5_log_dt[l], s5_B_re[l], s5_B_im[l], s5_C_re[l],
                         s5_C_im[l], s5_D[l], s5_glu_w[l], s5_glu_b[l], seg, nchunk)
        y_s5 = _s5_mixer(proj, B, L, dst['s5_u'] // s5w, dst['s5_z'] // s5w, s5w, s5p)
        ssdp = _ssd_params(ssd_conv_w[l], ssd_conv_b[l], ssd_dt_bias[l], ssd_A_log[l], ssd_D[l], ssd_norm_w[l])
        y_ssd = _ssd_mixer(proj, B, L, dst['ssd_z'] // ssdw, dst['ssd_xbc'] // cdim, dst['ssd_dt'] // LANE, ssdp, Q)
        y_att = _dsa_mixer(proj, B, L, dst['att_q'] // attw, dst['att_z'] // attw, dst['idx_q'] // attw,
                           small // (4 * LANE), dst['ssd_dt'] // LANE, band, far, nh_att, TQ)
        x2 = _outproj(y_s5, y_ssd, y_att, w_out_b[l], x2, final_norm_w.reshape(1, d), tm_out,
                      final=(l == depth - 1))
    return x2.reshape(B, L, d)
```

```python
import functools
import math

import numpy as np
import jax
import jax.numpy as jnp
from jax import lax
from jax.experimental import pallas as pl
from jax.experimental.pallas import tpu as pltpu

F32 = jnp.float32
BF16 = jnp.bfloat16

EPS = 1e-6
CHUNK = 64

S5_GROUP = 16
S5_STATE = 64
SSD_HEAD_DIM = 64
SSD_STATE = 128
SSD_GROUPS = 4
SSD_CONV = 4
ATT_HEAD_DIM = 64
IDX_DIM = 64
TOPK = 256
N_BUCKETS = 32
MAX_DISTANCE = 128

LANE = 128
SUBLANE = 8
VMEM_LIMIT = 56 * 1024 * 1024

INT_MIN = -2 ** 31
NEG = -1e30
IDX_W_LANE = 16


def _cparams(sem):
    return pltpu.CompilerParams(dimension_semantics=sem, vmem_limit_bytes=VMEM_LIMIT)


def _inproj_body(x_ref, nw_ref, w_ref, o_ref, h_ref):
    @pl.when(pl.program_id(1) == 0)
    def _():
        xf = x_ref[...]
        ms = jnp.mean(xf * xf, axis=-1, keepdims=True)
        h_ref[...] = (xf * lax.rsqrt(ms + EPS) * nw_ref[...]).astype(BF16)

    o_ref[...] = jnp.dot(h_ref[...], w_ref[...], preferred_element_type=F32)


def _inproj(x2, nw, w, layer, tm, tn):
    m, d = x2.shape
    n = w.shape[2]
    return pl.pallas_call(
        _inproj_body,
        grid=(m // tm, n // tn),
        in_specs=[pl.BlockSpec((tm, d), lambda i, j: (i, 0)),
                  pl.BlockSpec((1, d), lambda i, j: (0, 0)),
                  pl.BlockSpec((None, d, tn), lambda i, j: (layer, 0, j))],
        out_specs=pl.BlockSpec((tm, tn), lambda i, j: (i, j)),
        out_shape=jax.ShapeDtypeStruct((m, n), F32),
        scratch_shapes=[pltpu.VMEM((tm, d), BF16)],
        compiler_params=_cparams(("parallel", "arbitrary")),
        name="inproj",
    )(x2, nw, w)


def _outproj_body(ys5_ref, yssd_ref, yatt_ref, w_ref, x_ref, fnw_ref, o_ref, *, w5, wssd, final):
    acc = x_ref[...]
    acc += jnp.dot(ys5_ref[...], w_ref[0:w5, :], preferred_element_type=F32)
    acc += jnp.dot(yssd_ref[...], w_ref[w5:w5 + wssd, :], preferred_element_type=F32)
    acc += jnp.dot(yatt_ref[...], w_ref[w5 + wssd:, :], preferred_element_type=F32)
    if final:
        ms = jnp.mean(acc * acc, axis=-1, keepdims=True)
        acc = acc * lax.rsqrt(ms + EPS) * fnw_ref[...]
    o_ref[...] = acc


def _outproj(ys5, yssd, yatt, w, layer, x2, fnw, tm, final):
    m, d = x2.shape
    w5, wssd, watt = ys5.shape[1], yssd.shape[1], yatt.shape[1]
    body = functools.partial(_outproj_body, w5=w5, wssd=wssd, final=final)
    return pl.pallas_call(
        body,
        grid=(m // tm,),
        in_specs=[pl.BlockSpec((tm, w5), lambda i: (i, 0)),
                  pl.BlockSpec((tm, wssd), lambda i: (i, 0)),
                  pl.BlockSpec((tm, watt), lambda i: (i, 0)),
                  pl.BlockSpec((None,) + w.shape[1:], lambda i: (layer, 0, 0)),
                  pl.BlockSpec((tm, d), lambda i: (i, 0)),
                  pl.BlockSpec((1, d), lambda i: (0, 0))],
        out_specs=pl.BlockSpec((tm, d), lambda i: (i, 0)),
        out_shape=jax.ShapeDtypeStruct((m, d), F32),
        compiler_params=_cparams(("parallel",)),
        name="outproj",
    )(ys5, yssd, yatt, w, x2, fnw)


def _shift_down_one(x):
    rolled = pltpu.roll(x, 1, axis=0)
    row = lax.broadcasted_iota(jnp.int32, x.shape, 0)
    return jnp.where(row == 0, 0.0, rolled)


def _s5_body(u_ref, z_ref, bblk_ref, cblk_ref, lam_ref, lamseg_ref, d_ref, gw_ref, gb_ref,
             o_ref, uperm_ref, xs_ref, yperm_ref, *, nchunk, cw, sw, rt):
    L = u_ref.shape[0]
    seg = L // SUBLANE

    for c in range(nchunk):
        for j in range(SUBLANE):
            for k in range(cw // LANE):
                lo = c * cw + k * LANE
                uperm_ref[k, pl.ds(j, seg, stride=SUBLANE), :] = u_ref[pl.ds(j * seg, seg), lo:lo + LANE]
        for r0 in range(0, L, rt):
            up = jnp.concatenate([uperm_ref[k, r0:r0 + rt, :] for k in range(cw // LANE)], axis=1)
            xs_ref[r0:r0 + rt, :] = jnp.dot(up.astype(BF16), bblk_ref[c], preferred_element_type=F32)
        lr = lam_ref[c, 0]
        li = lam_ref[c, 1]

        def scan_step(tau, carry):
            xr, xi = carry
            row = pl.multiple_of(tau * SUBLANE, SUBLANE)
            nxr = lr * xr - li * xi + xs_ref[pl.ds(row, SUBLANE), 0:sw]
            nxi = lr * xi + li * xr + xs_ref[pl.ds(row, SUBLANE), sw:2 * sw]
            xs_ref[pl.ds(row, SUBLANE), 0:sw] = nxr
            xs_ref[pl.ds(row, SUBLANE), sw:2 * sw] = nxi
            return nxr, nxi

        zero = jnp.zeros((SUBLANE, sw), F32)
        er, ei = lax.fori_loop(0, seg, scan_step, (zero, zero))

        sr = lamseg_ref[c, 0]
        si = lamseg_ref[c, 1]
        cr, ci = zero, zero
        for _ in range(SUBLANE - 1):
            tr = er + (sr * cr - si * ci)
            ti = ei + (sr * ci + si * cr)
            cr, ci = _shift_down_one(tr), _shift_down_one(ti)

        def fix_step(tau, carry):
            fr, fi = carry
            nfr = lr * fr - li * fi
            nfi = lr * fi + li * fr
            row = pl.multiple_of(tau * SUBLANE, SUBLANE)
            xs_ref[pl.ds(row, SUBLANE), 0:sw] += nfr
            xs_ref[pl.ds(row, SUBLANE), sw:2 * sw] += nfi
            return nfr, nfi

        lax.fori_loop(0, seg, fix_step, (cr, ci))

        for r0 in range(0, L, rt):
            yc = jnp.dot(xs_ref[r0:r0 + rt, :].astype(BF16), cblk_ref[c], preferred_element_type=F32)
            for k in range(cw // LANE):
                yperm_ref[c * (cw // LANE) + k, r0:r0 + rt, :] = yc[:, k * LANE:(k + 1) * LANE]

    for j in range(SUBLANE):
        y = jnp.concatenate([yperm_ref[k, pl.ds(j, seg, stride=SUBLANE), :]
                             for k in range(yperm_ref.shape[0])], axis=1)
        y = y + d_ref[...] * u_ref[j * seg:(j + 1) * seg, :]
        y = jax.nn.gelu(y)
        g = jnp.dot(y.astype(BF16), gw_ref[...], preferred_element_type=F32) + gb_ref[...]
        y = y * jax.nn.sigmoid(g)
        o_ref[j * seg:(j + 1) * seg, :] = (y * jax.nn.silu(z_ref[j * seg:(j + 1) * seg, :])).astype(BF16)


def _s5_mixer(proj, B, L, u_blk, z_blk, width, prm):
    bblk, cblk, lam, lamseg, dvec, gw, gb = prm
    nchunk, cw, sw2 = bblk.shape
    sw = sw2 // 2
    rt = min(512, L)
    body = functools.partial(_s5_body, nchunk=nchunk, cw=cw, sw=sw, rt=rt)
    full = lambda a: pl.BlockSpec(a.shape, lambda b: (0,) * a.ndim)
    return pl.pallas_call(
        body,
        grid=(B,),
        in_specs=[pl.BlockSpec((L, width), lambda b: (b, u_blk)),
                  pl.BlockSpec((L, width), lambda b: (b, z_blk)),
                  full(bblk), full(cblk), full(lam), full(lamseg), full(dvec), full(gw), full(gb)],
        out_specs=pl.BlockSpec((L, width), lambda b: (b, 0)),
        out_shape=jax.ShapeDtypeStruct((B * L, width), BF16),
        scratch_shapes=[pltpu.VMEM((cw // LANE, L, LANE), F32),
                        pltpu.VMEM((L, 2 * sw), F32),
                        pltpu.VMEM((width // LANE, L, LANE), F32)],
        compiler_params=_cparams(("parallel",)),
        name="s5_mixer",
    )(proj, proj, bblk, cblk, lam, lamseg, dvec, gw, gb)


def _cpow(re, im, n):
    rr, ri = jnp.ones_like(re), jnp.zeros_like(im)
    br, bi = re, im
    while n:
        if n & 1:
            rr, ri = rr * br - ri * bi, rr * bi + ri * br
        br, bi = br * br - bi * bi, 2.0 * br * bi
        n >>= 1
    return rr, ri


def _s5_params(A_re, A_im, log_dt, B_re, B_im, C_re, C_im, D, glu_w, glu_b, seg, nchunk):
    G, P, C = B_re.shape
    dt = jnp.exp(log_dt)[:, None]
    lre = jnp.minimum(A_re, -1e-4)
    lim = A_im
    mag = jnp.exp(lre * dt)
    lbr = mag * jnp.cos(lim * dt)
    lbi = mag * jnp.sin(lim * dt)
    nr, ni = lbr - 1.0, lbi
    den = lre * lre + lim * lim
    fr = (nr * lre + ni * lim) / den
    fi = (ni * lre - nr * lim) / den
    bbr = fr[..., None] * B_re - fi[..., None] * B_im
    bbi = fr[..., None] * B_im + fi[..., None] * B_re
    gc = G // nchunk
    eye = jnp.eye(gc, dtype=F32)

    def blockdiag_in(bb):
        t = jnp.transpose(bb, (0, 2, 1)).reshape(nchunk, gc, C, P)
        return jnp.einsum('ngcp,gh->ngchp', t, eye).reshape(nchunk, gc * C, gc * P)

    def blockdiag_out(cc):
        t = jnp.transpose(cc, (0, 2, 1)).reshape(nchunk, gc, P, C)
        return jnp.einsum('ngpc,gh->ngphc', t, eye).reshape(nchunk, gc * P, gc * C)

    bblk = jnp.concatenate([blockdiag_in(bbr), blockdiag_in(bbi)], axis=-1).astype(BF16)
    cblk = jnp.concatenate([blockdiag_out(C_re), blockdiag_out(-C_im)], axis=1).astype(BF16)

    def rows(v):
        return jnp.broadcast_to(v.reshape(nchunk, 1, gc * P), (nchunk, SUBLANE, gc * P))

    lam = jnp.stack([rows(lbr), rows(lbi)], axis=1)
    pr, pi = _cpow(lbr, lbi, seg)
    lamseg = jnp.stack([rows(pr), rows(pi)], axis=1)
    return (bblk, cblk, lam, lamseg, D.reshape(1, G * C), glu_w.astype(BF16), glu_b.reshape(1, -1))


def _split3(x):
    h = x.astype(BF16)
    r = x - h.astype(F32)
    m = r.astype(BF16)
    l = (r - m.astype(F32)).astype(BF16)
    return h, m, l


def _dot_f32(a, b_bf16):
    h, m, l = _split3(a)
    dot = lambda p: jnp.dot(p, b_bf16, preferred_element_type=F32)
    return dot(h) + dot(m) + dot(l)


def _ssd_body(z_ref, xbc_ref, dt_ref, cw_ref, cb_ref, dtb_ref, a_ref, dx_ref, nw_ref, e_ref,
              o_ref, state_ref, xpad_ref, *, nheads, hd, ns, ngroups):
    Q = z_ref.shape[0]
    width = nheads * hd
    gw = width // ngroups
    halo = SUBLANE

    @pl.when(pl.program_id(1) == 0)
    def _():
        state_ref[...] = jnp.zeros_like(state_ref)
        xpad_ref[0:halo, :] = jnp.zeros((halo, xpad_ref.shape[1]), F32)

    xpad_ref[halo:halo + Q, :] = xbc_ref[...]
    acc = cb_ref[...] + cw_ref[0:1, :] * xpad_ref[halo - 3:halo - 3 + Q, :]
    for k in range(1, SSD_CONV):
        acc += cw_ref[k:k + 1, :] * xpad_ref[halo - 3 + k:halo - 3 + k + Q, :]
    xpad_ref[0:halo, :] = xpad_ref[Q:Q + halo, :]
    xc = jax.nn.silu(acc)
    xs = xc[:, 0:width]
    bm = xc[:, width:width + ngroups * ns].astype(BF16)
    cm = xc[:, width + ngroups * ns:].astype(BF16)

    lane = lax.broadcasted_iota(jnp.int32, (Q, LANE), 1)
    dt = jnp.where(lane < nheads, jax.nn.softplus(dt_ref[...] + dtb_ref[...]), 0.0)
    a = dt * a_ref[...]
    rowi = lax.broadcasted_iota(jnp.int32, (Q, Q), 0)
    coli = lax.broadcasted_iota(jnp.int32, (Q, Q), 1)
    tril = coli <= rowi
    acs = _dot_f32_lhs(tril.astype(BF16), a)
    acs_t = acs.T
    last = acs[Q - 1:Q, :]
    emat = e_ref[...]
    dt_x = _expand(dt, emat)
    dec_x = _expand(jnp.exp(acs), emat)
    dte_x = _expand(jnp.exp(last - acs), emat)
    dlast_x = dec_x[Q - 1:Q, :]

    xd = xs * dt_x
    xd_b = xd.astype(BF16)
    xw_b = (xd * dte_x).astype(BF16)
    st = state_ref[...]
    st_b = st.astype(BF16)

    y_parts = []
    new_state = []
    hlane = lax.broadcasted_iota(jnp.int32, (Q, LANE), 1) < hd
    heads_per_group = nheads // ngroups
    for g in range(ngroups):
        cg = cm[:, g * ns:(g + 1) * ns]
        bg = bm[:, g * ns:(g + 1) * ns]
        cb = lax.dot_general(cg, bg, (((1,), (1,)), ((), ())), preferred_element_type=F32)
        y_off = jnp.dot(cg, st_b[:, g * gw:(g + 1) * gw], preferred_element_type=F32)
        pair_out = []
        for pr in range(gw // LANE):
            xpair = xd_b[:, g * gw + pr * LANE: g * gw + (pr + 1) * LANE]
            res = []
            for hh in range(LANE // hd):
                h = g * heads_per_group + pr * (LANE // hd) + hh
                diff = acs[:, h:h + 1] - acs_t[h:h + 1, :]
                s_h = jnp.where(tril, cb * jnp.exp(jnp.minimum(diff, 0.0)), 0.0).astype(BF16)
                res.append(jnp.dot(s_h, xpair, preferred_element_type=F32))
            pair_out.append(jnp.where(hlane, res[0], res[1]))
        y_diag = jnp.concatenate(pair_out, axis=1)
        y_parts.append(y_diag + y_off * dec_x[:, g * gw:(g + 1) * gw])
        upd = lax.dot_general(bg, xw_b[:, g * gw:(g + 1) * gw], (((0,), (0,)), ((), ())),
                              preferred_element_type=F32)
        new_state.append(st[:, g * gw:(g + 1) * gw] * dlast_x[:, g * gw:(g + 1) * gw] + upd)
    state_ref[...] = jnp.concatenate(new_state, axis=1)

    y = jnp.concatenate(y_parts, axis=1) + xs * dx_ref[...]
    gt = y * jax.nn.silu(z_ref[...])
    ms = jnp.mean(gt * gt, axis=-1, keepdims=True)
    o_ref[...] = (gt * lax.rsqrt(ms + EPS) * nw_ref[...]).astype(BF16)


def _dot_f32_lhs(a_bf16, b):
    h, m, l = _split3(b)
    dot = lambda p: jnp.dot(a_bf16, p, preferred_element_type=F32)
    return dot(h) + dot(m) + dot(l)


def _expand(v, emat):
    return _dot_f32(v, emat)


def _ssd_mixer(proj, B, L, z_blk, xbc_blk, dt_blk, prm, Q):
    cw, cb, dtb, avec, dx, nw, emat = prm
    width = nw.shape[1]
    cdim = cw.shape[1]
    nheads = width // SSD_HEAD_DIM
    nt = L // Q
    body = functools.partial(_ssd_body, nheads=nheads, hd=SSD_HEAD_DIM, ns=SSD_STATE, ngroups=SSD_GROUPS)
    full = lambda a: pl.BlockSpec(a.shape, lambda b, t: (0,) * a.ndim)
    return pl.pallas_call(
        body,
        grid=(B, nt),
        in_specs=[pl.BlockSpec((Q, width), lambda b, t: (b * nt + t, z_blk)),
                  pl.BlockSpec((Q, cdim), lambda b, t: (b * nt + t, xbc_blk)),
                  pl.BlockSpec((Q, LANE), lambda b, t: (b * nt + t, dt_blk)),
                  full(cw), full(cb), full(dtb), full(avec), full(dx), full(nw), full(emat)],
        out_specs=pl.BlockSpec((Q, width), lambda b, t: (b * nt + t, 0)),
        out_shape=jax.ShapeDtypeStruct((B * L, width), BF16),
        scratch_shapes=[pltpu.VMEM((SSD_STATE, width), F32),
                        pltpu.VMEM((Q + SUBLANE, cdim), F32)],
        compiler_params=_cparams(("parallel", "arbitrary")),
        name="ssd_mixer",
    )(proj, proj, proj, cw, cb, dtb, avec, dx, nw, emat)


def _ssd_params(conv_w, conv_b, dt_bias, A_log, D, norm_w):
    nheads = dt_bias.shape[0]
    width = nheads * SSD_HEAD_DIM
    pad = lambda v: jnp.zeros((1, LANE), F32).at[0, :nheads].set(v)
    emat = jnp.zeros((LANE, width), F32).at[
        jnp.repeat(jnp.arange(nheads), SSD_HEAD_DIM), jnp.arange(width)].set(1.0).astype(BF16)
    return (conv_w, conv_b.reshape(1, -1), pad(dt_bias), pad(-jnp.exp(A_log)),
            jnp.repeat(D, SSD_HEAD_DIM).reshape(1, width), norm_w.reshape(1, width), emat)


def _float_key(x):
    b = lax.bitcast_convert_type(x, jnp.int32)
    return b ^ ((b >> 31) & 0x7FFFFFFF)


def _dsa_body(q_ref, qi_ref, z_ref, wq_ref, kv_ref, band_ref, far_ref, o_ref,
              kpad_ref, kipad_ref, vt_ref, key_ref, lg_ref, bound_ref, m_ref, acc_ref, *, nheads, hd, topk):
    i = pl.program_id(1)
    TQ = q_ref.shape[0]
    KB = TQ
    L = kv_ref.shape[0]
    nkb_total = L // KB
    npair = nheads // 2
    NT = (((1,), (1,)), ((), ()))

    @pl.when(i == 0)
    def _():
        lane = lax.broadcasted_iota(jnp.int32, (KB, LANE), 1)
        for kb in range(nkb_total):
            blk = kv_ref[kb * KB:(kb + 1) * KB, 0:LANE]
            kk = jnp.where(lane < hd, blk, pltpu.roll(blk, hd, axis=1))
            kpad_ref[0, kb * KB:(kb + 1) * KB, :] = jnp.where(lane < hd, kk, 0.0).astype(BF16)
            kpad_ref[1, kb * KB:(kb + 1) * KB, :] = jnp.where(lane < hd, 0.0, kk).astype(BF16)
            blk2 = kv_ref[kb * KB:(kb + 1) * KB, LANE:2 * LANE]
            ki2 = jnp.where(lane < hd, blk2, pltpu.roll(blk2, hd, axis=1))
            kipad_ref[0, kb * KB:(kb + 1) * KB, :] = jnp.where(lane < hd, ki2, 0.0).astype(BF16)
            kipad_ref[1, kb * KB:(kb + 1) * KB, :] = jnp.where(lane < hd, 0.0, ki2).astype(BF16)
            vt_ref[:, kb * KB:(kb + 1) * KB] = blk.T[hd:2 * hd, :].astype(BF16)

    qs = [(q_ref[:, p * LANE:(p + 1) * LANE] * (hd ** -0.5)).astype(BF16) for p in range(npair)]
    qis = [qi_ref[:, p * LANE:(p + 1) * LANE].astype(BF16) for p in range(npair)]
    wt = wq_ref[...].T * ((nheads * IDX_DIM) ** -0.5)
    w_rows = [wt[IDX_W_LANE + h:IDX_W_LANE + h + 1, :] for h in range(nheads)]

    RB = LANE

    def rows(kb, r=0, n=KB):
        return pl.ds(pl.multiple_of(kb * KB + r, SUBLANE), n)

    def fold_sum(x):
        return jnp.sum(x.reshape(x.shape[0] // SUBLANE, SUBLANE, TQ), axis=0)

    def fold_max(x):
        return jnp.max(x.reshape(x.shape[0] // SUBLANE, SUBLANE, TQ), axis=0)

    def idx_keys(kb, diagonal):
        for r in range(0, KB, RB):
            s = jnp.zeros((RB, TQ), F32)
            for h in range(nheads):
                lg = lax.dot_general(kipad_ref[h % 2, rows(kb, r, RB), :], qis[h // 2], NT,
                                     preferred_element_type=F32)
                s += jnp.maximum(lg, 0.0) * w_rows[h]
            key = _float_key(s)
            if diagonal:
                krow = lax.broadcasted_iota(jnp.int32, (RB, TQ), 0) + r
                qcol = lax.broadcasted_iota(jnp.int32, (RB, TQ), 1)
                key = jnp.where((krow // CHUNK) <= (qcol // CHUNK), key, INT_MIN)
            key_ref[rows(kb, r, RB), :] = key

    def far_keys(kb, _):
        idx_keys(kb, False)
        return 0

    lax.fori_loop(0, i, far_keys, 0)
    idx_keys(i, True)

    def count(preds):
        def body(kb, cs):
            blk = key_ref[rows(kb), :]
            return tuple(c + fold_sum(jnp.where(p(blk, kb), 1.0, 0.0)) for c, p in zip(cs, preds))
        z = jnp.zeros((SUBLANE, TQ), F32)
        cs = lax.fori_loop(0, i + 1, body, (z,) * len(preds))
        return [jnp.sum(c, axis=0, keepdims=True) for c in cs]

    def bisect(step, tu):
        bit = jnp.left_shift(jnp.int32(1), 31 - step)
        cand_u = tu | bit
        cand = cand_u ^ INT_MIN
        cnt, = count([lambda blk, kb: blk >= cand])
        return jnp.where(cnt >= topk, cand_u, tu)

    thr = lax.fori_loop(0, 32, bisect, jnp.zeros((1, TQ), jnp.int32)) ^ INT_MIN

    cgt, ceq = count([lambda blk, kb: blk > thr, lambda blk, kb: blk == thr])
    need = topk - cgt
    nbits = int(L).bit_length()
    bound_ref[...] = jnp.full((1, TQ), 2 ** nbits - 1, jnp.int32)

    @pl.when(jnp.max(ceq - need) > 0.0)
    def _():
        krow = lax.broadcasted_iota(jnp.int32, (KB, TQ), 0)

        def bisect_idx(step, bnd):
            bit = jnp.left_shift(jnp.int32(1), nbits - 1 - step)
            cand = bnd | bit
            cnt, = count([lambda blk, kb: (blk == thr) & (krow + kb * KB < cand)])
            return jnp.where(cnt <= need, cand, bnd)

        bound_ref[...] = lax.fori_loop(0, nbits, bisect_idx, jnp.zeros((1, TQ), jnp.int32))

    bound = bound_ref[...]

    m_ref[...] = jnp.full(m_ref.shape, NEG, F32)

    def store_logits(kb, bias_of_head):
        for r in range(0, KB, RB):
            blk = key_ref[rows(kb, r, RB), :]
            pos = lax.broadcasted_iota(jnp.int32, (RB, TQ), 0) + (kb * KB + r)
            keep = ((blk > thr) | ((blk == thr) & (pos < bound))) & (blk != INT_MIN)
            mask = jnp.where(keep, 0.0, NEG)
            for h in range(nheads):
                lg = lax.dot_general(kpad_ref[h % 2, rows(kb, r, RB), :], qs[h // 2], NT,
                                     preferred_element_type=F32)
                lg = lg + bias_of_head(h, r) + mask
                lg_ref[h, rows(kb, r, RB), :] = lg
                m_ref[h] = jnp.maximum(m_ref[h], fold_max(lg))

    def far_logits(kb, _):
        store_logits(kb, lambda h, r: far_ref[h])
        return 0

    lax.fori_loop(0, i - 1, far_logits, 0)

    @pl.when(i >= 1)
    def _():
        store_logits(i - 1, lambda h, r: band_ref[h, r:r + RB, :])

    store_logits(i, lambda h, r: band_ref[h, KB + r:KB + r + RB, :])

    m_fin = [jnp.max(m_ref[h], axis=0, keepdims=True) for h in range(nheads)]
    acc_ref[...] = jnp.zeros(acc_ref.shape, F32)

    def accumulate(kb, ls):
        vt = vt_ref[:, rows(kb)]
        out = []
        for h in range(nheads):
            p = jnp.exp(lg_ref[h, rows(kb), :] - m_fin[h])
            out.append(ls[h] + fold_sum(p))
            acc_ref[h] += jnp.dot(vt, p.astype(BF16), preferred_element_type=F32)
        return tuple(out)

    ls = lax.fori_loop(0, i + 1, accumulate, (jnp.zeros((SUBLANE, TQ), F32),) * nheads)

    outs = [acc_ref[h] / jnp.sum(ls[h], axis=0, keepdims=True) for h in range(nheads)]
    out = jnp.concatenate(outs, axis=0).T
    o_ref[...] = (out * jax.nn.silu(z_ref[...])).astype(BF16)


def _dsa_mixer(proj, B, L, q_blk, z_blk, qi_blk, small_blk, wq_blk, band, far, nheads, TQ):
    width = nheads * ATT_HEAD_DIM
    nq = L // TQ
    small_w = 4 * LANE
    body = functools.partial(_dsa_body, nheads=nheads, hd=ATT_HEAD_DIM, topk=min(TOPK, L // 4))
    full = lambda a: pl.BlockSpec(a.shape, lambda b, i: (0,) * a.ndim)
    return pl.pallas_call(
        body,
        grid=(B, nq),
        in_specs=[pl.BlockSpec((TQ, width), lambda b, i: (b * nq + i, q_blk)),
                  pl.BlockSpec((TQ, width), lambda b, i: (b * nq + i, qi_blk)),
                  pl.BlockSpec((TQ, width), lambda b, i: (b * nq + i, z_blk)),
                  pl.BlockSpec((TQ, LANE), lambda b, i: (b * nq + i, wq_blk)),
                  pl.BlockSpec((L, small_w), lambda b, i: (b, small_blk)),
                  full(band), full(far)],
        out_specs=pl.BlockSpec((TQ, width), lambda b, i: (b * nq + i, 0)),
        out_shape=jax.ShapeDtypeStruct((B * L, width), BF16),
        scratch_shapes=[pltpu.VMEM((2, L, LANE), BF16),
                        pltpu.VMEM((2, L, LANE), BF16),
                        pltpu.VMEM((ATT_HEAD_DIM, L), BF16),
                        pltpu.VMEM((L, TQ), jnp.int32),
                        pltpu.VMEM((nheads, L, TQ), F32),
                        pltpu.VMEM((1, TQ), jnp.int32),
                        pltpu.VMEM((nheads, SUBLANE, TQ), F32),
                        pltpu.VMEM((nheads, ATT_HEAD_DIM, TQ), F32)],
        compiler_params=_cparams(("parallel", "arbitrary")),
        name="dsa_mixer",
    )(proj, proj, proj, proj, proj, band, far)


def _t5_bucket_static(rel):
    nb = N_BUCKETS // 2
    max_exact = nb // 2
    ret = np.where(rel > 0, nb, 0)
    n = np.abs(rel)
    nf = np.maximum(n, 1).astype(np.float64)
    large = max_exact + (np.log(nf / max_exact) / math.log(MAX_DISTANCE / max_exact)
                         * (nb - max_exact)).astype(np.int32)
    large = np.minimum(large, nb - 1)
    return ret + np.where(n < max_exact, n, large)


def _dsa_bias_tables(rel_bias, TQ):
    a = np.arange(2 * TQ)[:, None]
    j = np.arange(TQ)[None, :]
    band_idx = _t5_bucket_static(a - TQ - j)
    idx = jnp.asarray(band_idx, jnp.int32)[None]
    band = jnp.zeros((rel_bias.shape[1],) + band_idx.shape, F32)
    for bucket in np.unique(band_idx):
        band = jnp.where(idx == int(bucket), rel_bias[int(bucket)][:, None, None], band)
    far_bucket = int(_t5_bucket_static(np.array([-(TQ + 1)]))[0])
    assert far_bucket == int(_t5_bucket_static(np.array([-(10 ** 6)]))[0])
    far = jnp.broadcast_to(rel_bias[far_bucket][:, None, None], (rel_bias.shape[1], 1, TQ))
    return band, far


def _layout(d_model):
    s5w = d_model // 4
    ssdw = d_model // 2
    attw = d_model // 4
    cdim = ssdw + 2 * SSD_GROUPS * SSD_STATE
    nh_ssd = ssdw // SSD_HEAD_DIM
    nh_att = attw // ATT_HEAD_DIM
    splits = (s5w, s5w, ssdw, cdim, nh_ssd, attw, ATT_HEAD_DIM, ATT_HEAD_DIM,
              nh_att * IDX_DIM, IDX_DIM, nh_att, attw)
    names = ('s5_u', 's5_z', 'ssd_z', 'ssd_xbc', 'ssd_dt', 'att_q', 'att_k', 'att_v',
             'idx_q', 'idx_k', 'idx_w', 'att_z')
    src, o = {}, 0
    for nme, s in zip(names, splits):
        src[nme] = (o, s)
        o += s
    dst = {'s5_u': 0, 's5_z': s5w, 'ssd_z': 2 * s5w, 'ssd_xbc': 2 * s5w + ssdw}
    o = 2 * s5w + ssdw + cdim
    dst['att_q'] = o
    dst['att_z'] = o + attw
    dst['idx_q'] = o + 2 * attw
    small = o + 3 * attw
    dst['att_k'] = small
    dst['att_v'] = small + ATT_HEAD_DIM
    dst['idx_k'] = small + LANE
    dst['ssd_dt'] = small + 2 * LANE
    dst['idx_w'] = small + 2 * LANE + IDX_W_LANE
    total = small + 4 * LANE
    return src, dst, total, small


def kernel(x, norm_w, w_in, s5_A_re, s5_A_im, s5_log_dt, s5_B_re, s5_B_im, s5_C_re, s5_C_im, s5_D, s5_glu_w, s5_glu_b, ssd_conv_w, ssd_conv_b, ssd_dt_bias, ssd_A_log, ssd_D, ssd_norm_w, rel_bias, w_out, final_norm_w):
    B, L, d = x.shape
    depth = w_in.shape[0]
    src, dst, total, small = _layout(d)
    s5w, ssdw, attw = d // 4, d // 2, d // 4
    cdim = ssdw + 2 * SSD_GROUPS * SSD_STATE
    nh_att = attw // ATT_HEAD_DIM
    assert ssdw // SSD_HEAD_DIM <= IDX_W_LANE and nh_att <= LANE - IDX_W_LANE

    tn = 1024
    total_pad = -(-total // tn) * tn
    w_in_b = w_in.astype(BF16)
    pieces, o = [], 0
    for nme in sorted(dst, key=dst.get):
        if dst[nme] > o:
            pieces.append(jnp.zeros((depth, d, dst[nme] - o), BF16))
        so, sz = src[nme]
        pieces.append(w_in_b[:, :, so:so + sz])
        o = dst[nme] + sz
    pieces.append(jnp.zeros((depth, d, total_pad - o), BF16))
    w_in_p = jnp.concatenate(pieces, axis=2)
    w_out_b = w_out.astype(BF16)

    TQ = 256
    Q = 128
    band, far = _dsa_bias_tables(rel_bias, TQ)
    tm_in = min(512, B * L)
    tm_out = min(256, B * L)
    nchunk = 2
    seg = L // SUBLANE

    x2 = x.reshape(B * L, d)
    for l in range(depth):
        proj = _inproj(x2, norm_w[l].reshape(1, d), w_in_p, l, tm_in, tn)
        s5p = _s5_params(s5_A_re[l], s5_A_im[l], s5_log_dt[l], s5_B_re[l], s5_B_im[l], s5_C_re[l],
                         s5_C_im[l], s5_D[l], s5_glu_w[l], s5_glu_b[l], seg, nchunk)
        y_s5 = _s5_mixer(proj, B, L, dst['s5_u'] // s5w, dst['s5_z'] // s5w, s5w, s5p)
        ssdp = _ssd_params(ssd_conv_w[l], ssd_conv_b[l], ssd_dt_bias[l], ssd_A_log[l], ssd_D[l], ssd_norm_w[l])
        y_ssd = _ssd_mixer(proj, B, L, dst['ssd_z'] // ssdw, dst['ssd_xbc'] // cdim, dst['ssd_dt'] // LANE, ssdp, Q)
        y_att = _dsa_mixer(proj, B, L, dst['att_q'] // attw, dst['att_z'] // attw, dst['idx_q'] // attw,
                           small // (4 * LANE), dst['ssd_dt'] // LANE, band, far, nh_att, TQ)
        x2 = _outproj(y_s5, y_ssd, y_att, w_out_b, l, x2, final_norm_w.reshape(1, d), tm_out,
                      final=(l == depth - 1))
    return x2.reshape(B, L, d)
```

```python
import functools
import math

import numpy as np
import jax
import jax.numpy as jnp
from jax import lax
from jax.experimental import pallas as pl
from jax.experimental.pallas import tpu as pltpu

F32 = jnp.float32
BF16 = jnp.bfloat16

EPS = 1e-6
CHUNK = 64

S5_GROUP = 16
S5_STATE = 64
SSD_HEAD_DIM = 64
SSD_STATE = 128
SSD_GROUPS = 4
SSD_CONV = 4
ATT_HEAD_DIM = 64
IDX_DIM = 64
TOPK = 256
N_BUCKETS = 32
MAX_DISTANCE = 128

LANE = 128
SUBLANE = 8
VMEM_LIMIT = 56 * 1024 * 1024

INT_MIN = -2 ** 31
NEG = -1e30
IDX_W_LANE = 16


def _cparams(sem):
    return pltpu.CompilerParams(dimension_semantics=sem, vmem_limit_bytes=VMEM_LIMIT)


def _inproj_body(x_ref, nw_ref, w_ref, o_ref, h_ref):
    @pl.when(pl.program_id(1) == 0)
    def _():
        xf = x_ref[...]
        ms = jnp.mean(xf * xf, axis=-1, keepdims=True)
        h_ref[...] = (xf * lax.rsqrt(ms + EPS) * nw_ref[...]).astype(BF16)

    o_ref[...] = jnp.dot(h_ref[...], w_ref[...], preferred_element_type=F32)


def _inproj(x2, nw, w, layer, tm, tn):
    m, d = x2.shape
    n = w.shape[2]
    return pl.pallas_call(
        _inproj_body,
        grid=(m // tm, n // tn),
        in_specs=[pl.BlockSpec((tm, d), lambda i, j: (i, 0)),
                  pl.BlockSpec((1, d), lambda i, j: (0, 0)),
                  pl.BlockSpec((None, d, tn), lambda i, j: (layer, 0, j))],
        out_specs=pl.BlockSpec((tm, tn), lambda i, j: (i, j)),
        out_shape=jax.ShapeDtypeStruct((m, n), F32),
        scratch_shapes=[pltpu.VMEM((tm, d), BF16)],
        compiler_params=_cparams(("parallel", "arbitrary")),
        name="inproj",
    )(x2, nw, w)


def _outproj_body(ys5_ref, yssd_ref, yatt_ref, w_ref, x_ref, fnw_ref, o_ref, *, w5, wssd, final):
    acc = x_ref[...]
    acc += jnp.dot(ys5_ref[...], w_ref[0:w5, :], preferred_element_type=F32)
    acc += jnp.dot(yssd_ref[...], w_ref[w5:w5 + wssd, :], preferred_element_type=F32)
    acc += jnp.dot(yatt_ref[...], w_ref[w5 + wssd:, :], preferred_element_type=F32)
    if final:
        ms = jnp.mean(acc * acc, axis=-1, keepdims=True)
        acc = acc * lax.rsqrt(ms + EPS) * fnw_ref[...]
    o_ref[...] = acc


def _outproj(ys5, yssd, yatt, w, layer, x2, fnw, tm, final):
    m, d = x2.shape
    w5, wssd, watt = ys5.shape[1], yssd.shape[1], yatt.shape[1]
    body = functools.partial(_outproj_body, w5=w5, wssd=wssd, final=final)
    return pl.pallas_call(
        body,
        grid=(m // tm,),
        in_specs=[pl.BlockSpec((tm, w5), lambda i: (i, 0)),
                  pl.BlockSpec((tm, wssd), lambda i: (i, 0)),
                  pl.BlockSpec((tm, watt), lambda i: (i, 0)),
                  pl.BlockSpec((None,) + w.shape[1:], lambda i: (layer, 0, 0)),
                  pl.BlockSpec((tm, d), lambda i: (i, 0)),
                  pl.BlockSpec((1, d), lambda i: (0, 0))],
        out_specs=pl.BlockSpec((tm, d), lambda i: (i, 0)),
        out_shape=jax.ShapeDtypeStruct((m, d), F32),
        compiler_params=_cparams(("parallel",)),
        name="outproj",
    )(ys5, yssd, yatt, w, x2, fnw)


def _shift_down_one(x):
    rolled = pltpu.roll(x, 1, axis=0)
    row = lax.broadcasted_iota(jnp.int32, x.shape, 0)
    return jnp.where(row == 0, 0.0, rolled)


def _s5_body(u_ref, z_ref, bblk_ref, cblk_ref, lam_ref, lamseg_ref, d_ref, gw_ref, gb_ref,
             o_ref, uperm_ref, xs_ref, yperm_ref, *, nchunk, cw, sw, rt):
    L = u_ref.shape[0]
    seg = L // SUBLANE

    for c in range(nchunk):
        for j in range(SUBLANE):
            for k in range(cw // LANE):
                lo = c * cw + k * LANE
                uperm_ref[k, pl.ds(j, seg, stride=SUBLANE), :] = u_ref[pl.ds(j * seg, seg), lo:lo + LANE]
        for r0 in range(0, L, rt):
            up = jnp.concatenate([uperm_ref[k, r0:r0 + rt, :] for k in range(cw // LANE)], axis=1)
            xs_ref[r0:r0 + rt, :] = jnp.dot(up.astype(BF16), bblk_ref[c], preferred_element_type=F32)
        lr = lam_ref[c, 0]
        li = lam_ref[c, 1]

        def scan_step(tau, carry):
            xr, xi = carry
            row = pl.multiple_of(tau * SUBLANE, SUBLANE)
            nxr = lr * xr - li * xi + xs_ref[pl.ds(row, SUBLANE), 0:sw]
            nxi = lr * xi + li * xr + xs_ref[pl.ds(row, SUBLANE), sw:2 * sw]
            xs_ref[pl.ds(row, SUBLANE), 0:sw] = nxr
            xs_ref[pl.ds(row, SUBLANE), sw:2 * sw] = nxi
            return nxr, nxi

        zero = jnp.zeros((SUBLANE, sw), F32)
        er, ei = lax.fori_loop(0, seg, scan_step, (zero, zero))

        sr = lamseg_ref[c, 0]
        si = lamseg_ref[c, 1]
        cr, ci = zero, zero
        for _ in range(SUBLANE - 1):
            tr = er + (sr * cr - si * ci)
            ti = ei + (sr * ci + si * cr)
            cr, ci = _shift_down_one(tr), _shift_down_one(ti)

        def fix_step(tau, carry):
            fr, fi = carry
            nfr = lr * fr - li * fi
            nfi = lr * fi + li * fr
            row = pl.multiple_of(tau * SUBLANE, SUBLANE)
            xs_ref[pl.ds(row, SUBLANE), 0:sw] += nfr
            xs_ref[pl.ds(row, SUBLANE), sw:2 * sw] += nfi
            return nfr, nfi

        lax.fori_loop(0, seg, fix_step, (cr, ci))

        for r0 in range(0, L, rt):
            yc = jnp.dot(xs_ref[r0:r0 + rt, :].astype(BF16), cblk_ref[c], preferred_element_type=F32)
            for k in range(cw // LANE):
                yperm_ref[c * (cw // LANE) + k, r0:r0 + rt, :] = yc[:, k * LANE:(k + 1) * LANE]

    for j in range(SUBLANE):
        y = jnp.concatenate([yperm_ref[k, pl.ds(j, seg, stride=SUBLANE), :]
                             for k in range(yperm_ref.shape[0])], axis=1)
        y = y + d_ref[...] * u_ref[j * seg:(j + 1) * seg, :]
        y = jax.nn.gelu(y)
        g = jnp.dot(y.astype(BF16), gw_ref[...], preferred_element_type=F32) + gb_ref[...]
        y = y * jax.nn.sigmoid(g)
        o_ref[j * seg:(j + 1) * seg, :] = (y * jax.nn.silu(z_ref[j * seg:(j + 1) * seg, :])).astype(BF16)


def _s5_mixer(proj, B, L, u_blk, z_blk, width, prm):
    bblk, cblk, lam, lamseg, dvec, gw, gb = prm
    nchunk, cw, sw2 = bblk.shape
    sw = sw2 // 2
    rt = min(512, L)
    body = functools.partial(_s5_body, nchunk=nchunk, cw=cw, sw=sw, rt=rt)
    full = lambda a: pl.BlockSpec(a.shape, lambda b: (0,) * a.ndim)
    return pl.pallas_call(
        body,
        grid=(B,),
        in_specs=[pl.BlockSpec((L, width), lambda b: (b, u_blk)),
                  pl.BlockSpec((L, width), lambda b: (b, z_blk)),
                  full(bblk), full(cblk), full(lam), full(lamseg), full(dvec), full(gw), full(gb)],
        out_specs=pl.BlockSpec((L, width), lambda b: (b, 0)),
        out_shape=jax.ShapeDtypeStruct((B * L, width), BF16),
        scratch_shapes=[pltpu.VMEM((cw // LANE, L, LANE), F32),
                        pltpu.VMEM((L, 2 * sw), F32),
                        pltpu.VMEM((width // LANE, L, LANE), F32)],
        compiler_params=_cparams(("parallel",)),
        name="s5_mixer",
    )(proj, proj, bblk, cblk, lam, lamseg, dvec, gw, gb)


def _cpow(re, im, n):
    rr, ri = jnp.ones_like(re), jnp.zeros_like(im)
    br, bi = re, im
    while n:
        if n & 1:
            rr, ri = rr * br - ri * bi, rr * bi + ri * br
        br, bi = br * br - bi * bi, 2.0 * br * bi
        n >>= 1
    return rr, ri


def _s5_params(A_re, A_im, log_dt, B_re, B_im, C_re, C_im, D, glu_w, glu_b, seg, nchunk):
    G, P, C = B_re.shape
    dt = jnp.exp(log_dt)[:, None]
    lre = jnp.minimum(A_re, -1e-4)
    lim = A_im
    mag = jnp.exp(lre * dt)
    lbr = mag * jnp.cos(lim * dt)
    lbi = mag * jnp.sin(lim * dt)
    nr, ni = lbr - 1.0, lbi
    den = lre * lre + lim * lim
    fr = (nr * lre + ni * lim) / den
    fi = (ni * lre - nr * lim) / den
    bbr = fr[..., None] * B_re - fi[..., None] * B_im
    bbi = fr[..., None] * B_im + fi[..., None] * B_re
    gc = G // nchunk
    eye = jnp.eye(gc, dtype=F32)

    def blockdiag_in(bb):
        t = jnp.transpose(bb, (0, 2, 1)).reshape(nchunk, gc, C, P)
        return jnp.einsum('ngcp,gh->ngchp', t, eye).reshape(nchunk, gc * C, gc * P)

    def blockdiag_out(cc):
        t = jnp.transpose(cc, (0, 2, 1)).reshape(nchunk, gc, P, C)
        return jnp.einsum('ngpc,gh->ngphc', t, eye).reshape(nchunk, gc * P, gc * C)

    bblk = jnp.concatenate([blockdiag_in(bbr), blockdiag_in(bbi)], axis=-1).astype(BF16)
    cblk = jnp.concatenate([blockdiag_out(C_re), blockdiag_out(-C_im)], axis=1).astype(BF16)

    def rows(v):
        return jnp.broadcast_to(v.reshape(nchunk, 1, gc * P), (nchunk, SUBLANE, gc * P))

    lam = jnp.stack([rows(lbr), rows(lbi)], axis=1)
    pr, pi = _cpow(lbr, lbi, seg)
    lamseg = jnp.stack([rows(pr), rows(pi)], axis=1)
    return (bblk, cblk, lam, lamseg, D.reshape(1, G * C), glu_w.astype(BF16), glu_b.reshape(1, -1))


def _split3(x):
    h = x.astype(BF16)
    r = x - h.astype(F32)
    m = r.astype(BF16)
    l = (r - m.astype(F32)).astype(BF16)
    return h, m, l


def _dot_f32(a, b_bf16):
    h, m, l = _split3(a)
    dot = lambda p: jnp.dot(p, b_bf16, preferred_element_type=F32)
    return dot(h) + dot(m) + dot(l)


def _ssd_body(z_ref, xbc_ref, dt_ref, cw_ref, cb_ref, dtb_ref, a_ref, dx_ref, nw_ref, e_ref,
              o_ref, state_ref, xpad_ref, *, nheads, hd, ns, ngroups):
    Q = z_ref.shape[0]
    width = nheads * hd
    gw = width // ngroups
    halo = SUBLANE

    @pl.when(pl.program_id(1) == 0)
    def _():
        state_ref[...] = jnp.zeros_like(state_ref)
        xpad_ref[0:halo, :] = jnp.zeros((halo, xpad_ref.shape[1]), F32)

    xpad_ref[halo:halo + Q, :] = xbc_ref[...]
    acc = cb_ref[...] + cw_ref[0:1, :] * xpad_ref[halo - 3:halo - 3 + Q, :]
    for k in range(1, SSD_CONV):
        acc += cw_ref[k:k + 1, :] * xpad_ref[halo - 3 + k:halo - 3 + k + Q, :]
    xpad_ref[0:halo, :] = xpad_ref[Q:Q + halo, :]
    xc = jax.nn.silu(acc)
    xs = xc[:, 0:width]
    bm = xc[:, width:width + ngroups * ns].astype(BF16)
    cm = xc[:, width + ngroups * ns:].astype(BF16)

    lane = lax.broadcasted_iota(jnp.int32, (Q, LANE), 1)
    dt = jnp.where(lane < nheads, jax.nn.softplus(dt_ref[...] + dtb_ref[...]), 0.0)
    a = dt * a_ref[...]
    rowi = lax.broadcasted_iota(jnp.int32, (Q, Q), 0)
    coli = lax.broadcasted_iota(jnp.int32, (Q, Q), 1)
    tril = coli <= rowi
    acs = _dot_f32_lhs(tril.astype(BF16), a)
    acs_t = acs.T
    last = acs[Q - 1:Q, :]
    emat = e_ref[...]
    dt_x = _expand(dt, emat)
    dec_x = _expand(jnp.exp(acs), emat)
    dte_x = _expand(jnp.exp(last - acs), emat)
    dlast_x = dec_x[Q - 1:Q, :]

    xd = xs * dt_x
    xd_b = xd.astype(BF16)
    xw_b = (xd * dte_x).astype(BF16)
    st = state_ref[...]
    st_b = st.astype(BF16)

    y_parts = []
    new_state = []
    hlane = lax.broadcasted_iota(jnp.int32, (Q, LANE), 1) < hd
    heads_per_group = nheads // ngroups
    for g in range(ngroups):
        cg = cm[:, g * ns:(g + 1) * ns]
        bg = bm[:, g * ns:(g + 1) * ns]
        cb = lax.dot_general(cg, bg, (((1,), (1,)), ((), ())), preferred_element_type=F32)
        y_off = jnp.dot(cg, st_b[:, g * gw:(g + 1) * gw], preferred_element_type=F32)
        pair_out = []
        for pr in range(gw // LANE):
            xpair = xd_b[:, g * gw + pr * LANE: g * gw + (pr + 1) * LANE]
            res = []
            for hh in range(LANE // hd):
                h = g * heads_per_group + pr * (LANE // hd) + hh
                diff = acs[:, h:h + 1] - acs_t[h:h + 1, :]
                s_h = jnp.where(tril, cb * jnp.exp(jnp.minimum(diff, 0.0)), 0.0).astype(BF16)
                res.append(jnp.dot(s_h, xpair, preferred_element_type=F32))
            pair_out.append(jnp.where(hlane, res[0], res[1]))
        y_diag = jnp.concatenate(pair_out, axis=1)
        y_parts.append(y_diag + y_off * dec_x[:, g * gw:(g + 1) * gw])
        upd = lax.dot_general(bg, xw_b[:, g * gw:(g + 1) * gw], (((0,), (0,)), ((), ())),
                              preferred_element_type=F32)
        new_state.append(st[:, g * gw:(g + 1) * gw] * dlast_x[:, g * gw:(g + 1) * gw] + upd)
    state_ref[...] = jnp.concatenate(new_state, axis=1)

    y = jnp.concatenate(y_parts, axis=1) + xs * dx_ref[...]
    gt = y * jax.nn.silu(z_ref[...])
    ms = jnp.mean(gt * gt, axis=-1, keepdims=True)
    o_ref[...] = (gt * lax.rsqrt(ms + EPS) * nw_ref[...]).astype(BF16)


def _dot_f32_lhs(a_bf16, b):
    h, m, l = _split3(b)
    dot = lambda p: jnp.dot(a_bf16, p, preferred_element_type=F32)
    return dot(h) + dot(m) + dot(l)


def _expand(v, emat):
    return _dot_f32(v, emat)


def _ssd_mixer(proj, B, L, z_blk, xbc_blk, dt_blk, prm, Q):
    cw, cb, dtb, avec, dx, nw, emat = prm
    width = nw.shape[1]
    cdim = cw.shape[1]
    nheads = width // SSD_HEAD_DIM
    nt = L // Q
    body = functools.partial(_ssd_body, nheads=nheads, hd=SSD_HEAD_DIM, ns=SSD_STATE, ngroups=SSD_GROUPS)
    full = lambda a: pl.BlockSpec(a.shape, lambda b, t: (0,) * a.ndim)
    return pl.pallas_call(
        body,
        grid=(B, nt),
        in_specs=[pl.BlockSpec((Q, width), lambda b, t: (b * nt + t, z_blk)),
                  pl.BlockSpec((Q, cdim), lambda b, t: (b * nt + t, xbc_blk)),
                  pl.BlockSpec((Q, LANE), lambda b, t: (b * nt + t, dt_blk)),
                  full(cw), full(cb), full(dtb), full(avec), full(dx), full(nw), full(emat)],
        out_specs=pl.BlockSpec((Q, width), lambda b, t: (b * nt + t, 0)),
        out_shape=jax.ShapeDtypeStruct((B * L, width), BF16),
        scratch_shapes=[pltpu.VMEM((SSD_STATE, width), F32),
                        pltpu.VMEM((Q + SUBLANE, cdim), F32)],
        compiler_params=_cparams(("parallel", "arbitrary")),
        name="ssd_mixer",
    )(proj, proj, proj, cw, cb, dtb, avec, dx, nw, emat)


def _ssd_params(conv_w, conv_b, dt_bias, A_log, D, norm_w):
    nheads = dt_bias.shape[0]
    width = nheads * SSD_HEAD_DIM
    pad = lambda v: jnp.zeros((1, LANE), F32).at[0, :nheads].set(v)
    emat = jnp.zeros((LANE, width), F32).at[
        jnp.repeat(jnp.arange(nheads), SSD_HEAD_DIM), jnp.arange(width)].set(1.0).astype(BF16)
    return (conv_w, conv_b.reshape(1, -1), pad(dt_bias), pad(-jnp.exp(A_log)),
            jnp.repeat(D, SSD_HEAD_DIM).reshape(1, width), norm_w.reshape(1, width), emat)


def _float_key(x):
    b = lax.bitcast_convert_type(x, jnp.int32)
    return b ^ ((b >> 31) & 0x7FFFFFFF)


def _dsa_body(q_ref, qi_ref, z_ref, wq_ref, kv_ref, band_ref, far_ref, o_ref,
              kpad_ref, kipad_ref, vt_ref, key_ref, lg_ref, bound_ref, m_ref, l_ref, acc_ref, *, nheads, hd, topk):
    i = pl.program_id(1)
    TQ = q_ref.shape[0]
    KB = TQ
    L = kv_ref.shape[0]
    nkb_total = L // KB
    npair = nheads // 2
    NT = (((1,), (1,)), ((), ()))

    @pl.when(i == 0)
    def _():
        lane = lax.broadcasted_iota(jnp.int32, (KB, LANE), 1)
        for kb in range(nkb_total):
            blk = kv_ref[kb * KB:(kb + 1) * KB, 0:LANE]
            kk = jnp.where(lane < hd, blk, pltpu.roll(blk, hd, axis=1))
            kpad_ref[0, kb * KB:(kb + 1) * KB, :] = jnp.where(lane < hd, kk, 0.0).astype(BF16)
            kpad_ref[1, kb * KB:(kb + 1) * KB, :] = jnp.where(lane < hd, 0.0, kk).astype(BF16)
            blk2 = kv_ref[kb * KB:(kb + 1) * KB, LANE:2 * LANE]
            ki2 = jnp.where(lane < hd, blk2, pltpu.roll(blk2, hd, axis=1))
            kipad_ref[0, kb * KB:(kb + 1) * KB, :] = jnp.where(lane < hd, ki2, 0.0).astype(BF16)
            kipad_ref[1, kb * KB:(kb + 1) * KB, :] = jnp.where(lane < hd, 0.0, ki2).astype(BF16)
            vt_ref[:, kb * KB:(kb + 1) * KB] = blk.T[hd:2 * hd, :].astype(BF16)

    qs = [(q_ref[:, p * LANE:(p + 1) * LANE] * (hd ** -0.5)).astype(BF16) for p in range(npair)]
    qis = [qi_ref[:, p * LANE:(p + 1) * LANE].astype(BF16) for p in range(npair)]
    wt = wq_ref[...].T * ((nheads * IDX_DIM) ** -0.5)
    w_rows = [wt[IDX_W_LANE + h:IDX_W_LANE + h + 1, :] for h in range(nheads)]

    RB = LANE

    def rows(kb, r=0, n=KB):
        return pl.ds(pl.multiple_of(kb * KB + r, SUBLANE), n)

    PART = m_ref.shape[1]

    def fold_sum(x):
        return jnp.sum(x.reshape(x.shape[0] // PART, PART, TQ), axis=0)

    def fold_max(x):
        return jnp.max(x.reshape(x.shape[0] // PART, PART, TQ), axis=0)

    def idx_keys(kb, diagonal):
        for r in range(0, KB, RB):
            s = jnp.zeros((RB, TQ), F32)
            for h in range(nheads):
                lg = lax.dot_general(kipad_ref[h % 2, rows(kb, r, RB), :], qis[h // 2], NT,
                                     preferred_element_type=F32)
                s += jnp.maximum(lg, 0.0) * w_rows[h]
            key = _float_key(s)
            if diagonal:
                krow = lax.broadcasted_iota(jnp.int32, (RB, TQ), 0) + r
                qcol = lax.broadcasted_iota(jnp.int32, (RB, TQ), 1)
                key = jnp.where((krow // CHUNK) <= (qcol // CHUNK), key, INT_MIN)
            key_ref[rows(kb, r, RB), :] = key

    def far_keys(kb, _):
        idx_keys(kb, False)
        return 0

    lax.fori_loop(0, i, far_keys, 0)
    idx_keys(i, True)

    def count(preds):
        def body(kb, cs):
            blk = key_ref[rows(kb), :]
            return tuple(c + fold_sum(jnp.where(p(blk, kb), 1.0, 0.0)) for c, p in zip(cs, preds))
        z = jnp.zeros((PART, TQ), F32)
        cs = lax.fori_loop(0, i + 1, body, (z,) * len(preds))
        return [jnp.sum(c, axis=0, keepdims=True) for c in cs]

    def bisect(step, tu):
        bit = jnp.left_shift(jnp.int32(1), 31 - step)
        cand_u = tu | bit
        cand = cand_u ^ INT_MIN
        cnt, = count([lambda blk, kb: blk >= cand])
        return jnp.where(cnt >= topk, cand_u, tu)

    thr = lax.fori_loop(0, 32, bisect, jnp.zeros((1, TQ), jnp.int32)) ^ INT_MIN

    cgt, ceq = count([lambda blk, kb: blk > thr, lambda blk, kb: blk == thr])
    need = topk - cgt
    nbits = int(L).bit_length()
    bound_ref[...] = jnp.full((1, TQ), 2 ** nbits - 1, jnp.int32)

    @pl.when(jnp.max(ceq - need) > 0.0)
    def _():
        krow = lax.broadcasted_iota(jnp.int32, (KB, TQ), 0)

        def bisect_idx(step, bnd):
            bit = jnp.left_shift(jnp.int32(1), nbits - 1 - step)
            cand = bnd | bit
            cnt, = count([lambda blk, kb: (blk == thr) & (krow + kb * KB < cand)])
            return jnp.where(cnt <= need, cand, bnd)

        bound_ref[...] = lax.fori_loop(0, nbits, bisect_idx, jnp.zeros((1, TQ), jnp.int32))

    bound = bound_ref[...]

    m_ref[...] = jnp.full(m_ref.shape, NEG, F32)

    def store_logits(kb, bias_of_head):
        for r in range(0, KB, RB):
            blk = key_ref[rows(kb, r, RB), :]
            pos = lax.broadcasted_iota(jnp.int32, (RB, TQ), 0) + (kb * KB + r)
            keep = ((blk > thr) | ((blk == thr) & (pos < bound))) & (blk != INT_MIN)
            mask = jnp.where(keep, 0.0, NEG)
            for h in range(nheads):
                lg = lax.dot_general(kpad_ref[h % 2, rows(kb, r, RB), :], qs[h // 2], NT,
                                     preferred_element_type=F32)
                lg = lg + bias_of_head(h, r) + mask
                lg_ref[h, rows(kb, r, RB), :] = lg
                m_ref[h] = jnp.maximum(m_ref[h], fold_max(lg))

    def far_logits(kb, _):
        store_logits(kb, lambda h, r: far_ref[h])
        return 0

    lax.fori_loop(0, i - 1, far_logits, 0)

    @pl.when(i >= 1)
    def _():
        store_logits(i - 1, lambda h, r: band_ref[h, r:r + RB, :])

    store_logits(i, lambda h, r: band_ref[h, KB + r:KB + r + RB, :])

    m_fin = [jnp.max(m_ref[h], axis=0, keepdims=True) for h in range(nheads)]
    acc_ref[...] = jnp.zeros(acc_ref.shape, F32)
    l_ref[...] = jnp.zeros(l_ref.shape, F32)

    def accumulate(kb, _):
        vt = vt_ref[:, rows(kb)]
        for h in range(nheads):
            p = jnp.exp(lg_ref[h, rows(kb), :] - m_fin[h])
            l_ref[h] += fold_sum(p)
            acc_ref[h] += jnp.dot(vt, p.astype(BF16), preferred_element_type=F32)
        return 0

    lax.fori_loop(0, i + 1, accumulate, 0)

    outs = [acc_ref[h] / jnp.sum(l_ref[h], axis=0, keepdims=True) for h in range(nheads)]
    out = jnp.concatenate(outs, axis=0).T
    o_ref[...] = (out * jax.nn.silu(z_ref[...])).astype(BF16)


def _dsa_mixer(proj, B, L, q_blk, z_blk, qi_blk, small_blk, wq_blk, band, far, nheads, TQ):
    width = nheads * ATT_HEAD_DIM
    nq = L // TQ
    small_w = 4 * LANE
    body = functools.partial(_dsa_body, nheads=nheads, hd=ATT_HEAD_DIM, topk=min(TOPK, L // 4))
    full = lambda a: pl.BlockSpec(a.shape, lambda b, i: (0,) * a.ndim)
    return pl.pallas_call(
        body,
        grid=(B, nq),
        in_specs=[pl.BlockSpec((TQ, width), lambda b, i: (b * nq + i, q_blk)),
                  pl.BlockSpec((TQ, width), lambda b, i: (b * nq + i, qi_blk)),
                  pl.BlockSpec((TQ, width), lambda b, i: (b * nq + i, z_blk)),
                  pl.BlockSpec((TQ, LANE), lambda b, i: (b * nq + i, wq_blk)),
                  pl.BlockSpec((L, small_w), lambda b, i: (b, small_blk)),
                  full(band), full(far)],
        out_specs=pl.BlockSpec((TQ, width), lambda b, i: (b * nq + i, 0)),
        out_shape=jax.ShapeDtypeStruct((B * L, width), BF16),
        scratch_shapes=[pltpu.VMEM((2, L, LANE), BF16),
                        pltpu.VMEM((2, L, LANE), BF16),
                        pltpu.VMEM((ATT_HEAD_DIM, L), BF16),
                        pltpu.VMEM((L, TQ), jnp.int32),
                        pltpu.VMEM((nheads, L, TQ), F32),
                        pltpu.VMEM((1, TQ), jnp.int32),
                        pltpu.VMEM((nheads, 4 * SUBLANE, TQ), F32),
                        pltpu.VMEM((nheads, 4 * SUBLANE, TQ), F32),
                        pltpu.VMEM((nheads, ATT_HEAD_DIM, TQ), F32)],
        compiler_params=_cparams(("parallel", "arbitrary")),
        name="dsa_mixer",
    )(proj, proj, proj, proj, proj, band, far)


def _t5_bucket_static(rel):
    nb = N_BUCKETS // 2
    max_exact = nb // 2
    ret = np.where(rel > 0, nb, 0)
    n = np.abs(rel)
    nf = np.maximum(n, 1).astype(np.float64)
    large = max_exact + (np.log(nf / max_exact) / math.log(MAX_DISTANCE / max_exact)
                         * (nb - max_exact)).astype(np.int32)
    large = np.minimum(large, nb - 1)
    return ret + np.where(n < max_exact, n, large)


def _dsa_bias_tables(rel_bias, TQ):
    a = np.arange(2 * TQ)[:, None]
    j = np.arange(TQ)[None, :]
    band_idx = _t5_bucket_static(a - TQ - j)
    idx = jnp.asarray(band_idx, jnp.int32)[None]
    band = jnp.zeros((rel_bias.shape[1],) + band_idx.shape, F32)
    for bucket in np.unique(band_idx):
        band = jnp.where(idx == int(bucket), rel_bias[int(bucket)][:, None, None], band)
    far_bucket = int(_t5_bucket_static(np.array([-(TQ + 1)]))[0])
    assert far_bucket == int(_t5_bucket_static(np.array([-(10 ** 6)]))[0])
    far = jnp.broadcast_to(rel_bias[far_bucket][:, None, None], (rel_bias.shape[1], 1, TQ))
    return band, far


def _layout(d_model):
    s5w = d_model // 4
    ssdw = d_model // 2
    attw = d_model // 4
    cdim = ssdw + 2 * SSD_GROUPS * SSD_STATE
    nh_ssd = ssdw // SSD_HEAD_DIM
    nh_att = attw // ATT_HEAD_DIM
    splits = (s5w, s5w, ssdw, cdim, nh_ssd, attw, ATT_HEAD_DIM, ATT_HEAD_DIM,
              nh_att * IDX_DIM, IDX_DIM, nh_att, attw)
    names = ('s5_u', 's5_z', 'ssd_z', 'ssd_xbc', 'ssd_dt', 'att_q', 'att_k', 'att_v',
             'idx_q', 'idx_k', 'idx_w', 'att_z')
    src, o = {}, 0
    for nme, s in zip(names, splits):
        src[nme] = (o, s)
        o += s
    dst = {'s5_u': 0, 's5_z': s5w, 'ssd_z': 2 * s5w, 'ssd_xbc': 2 * s5w + ssdw}
    o = 2 * s5w + ssdw + cdim
    dst['att_q'] = o
    dst['att_z'] = o + attw
    dst['idx_q'] = o + 2 * attw
    small = o + 3 * attw
    dst['att_k'] = small
    dst['att_v'] = small + ATT_HEAD_DIM
    dst['idx_k'] = small + LANE
    dst['ssd_dt'] = small + 2 * LANE
    dst['idx_w'] = small + 2 * LANE + IDX_W_LANE
    total = small + 4 * LANE
    return src, dst, total, small


def kernel(x, norm_w, w_in, s5_A_re, s5_A_im, s5_log_dt, s5_B_re, s5_B_im, s5_C_re, s5_C_im, s5_D, s5_glu_w, s5_glu_b, ssd_conv_w, ssd_conv_b, ssd_dt_bias, ssd_A_log, ssd_D, ssd_norm_w, rel_bias, w_out, final_norm_w):
    B, L, d = x.shape
    depth = w_in.shape[0]
    src, dst, total, small = _layout(d)
    s5w, ssdw, attw = d // 4, d // 2, d // 4
    cdim = ssdw + 2 * SSD_GROUPS * SSD_STATE
    nh_att = attw // ATT_HEAD_DIM
    assert ssdw // SSD_HEAD_DIM <= IDX_W_LANE and nh_att <= LANE - IDX_W_LANE

    tn = 1024
    total_pad = -(-total // tn) * tn
    w_in_b = w_in.astype(BF16)
    pieces, o = [], 0
    for nme in sorted(dst, key=dst.get):
        if dst[nme] > o:
            pieces.append(jnp.zeros((depth, d, dst[nme] - o), BF16))
        so, sz = src[nme]
        pieces.append(w_in_b[:, :, so:so + sz])
        o = dst[nme] + sz
    pieces.append(jnp.zeros((depth, d, total_pad - o), BF16))
    w_in_p = jnp.concatenate(pieces, axis=2)
    w_out_b = w_out.astype(BF16)

    TQ = 256
    Q = 128
    band, far = _dsa_bias_tables(rel_bias, TQ)
    tm_in = min(1024, B * L)
    tm_out = min(256, B * L)
    nchunk = 2
    seg = L // SUBLANE

    x2 = x.reshape(B * L, d)
    for l in range(depth):
        proj = _inproj(x2, norm_w[l].reshape(1, d), w_in_p, l, tm_in, tn)
        s5p = _s5_params(s5_A_re[l], s5_A_im[l], s5_log_dt[l], s5_B_re[l], s5_B_im[l], s5_C_re[l],
                         s5_C_im[l], s5_D[l], s5_glu_w[l], s5_glu_b[l], seg, nchunk)
        y_s5 = _s5_mixer(proj, B, L, dst['s5_u'] // s5w, dst['s5_z'] // s5w, s5w, s5p)
        ssdp = _ssd_params(ssd_conv_w[l], ssd_conv_b[l], ssd_dt_bias[l], ssd_A_log[l], ssd_D[l], ssd_norm_w[l])
        y_ssd = _ssd_mixer(proj, B, L, dst['ssd_z'] // ssdw, dst['ssd_xbc'] // cdim, dst['ssd_dt'] // LANE, ssdp, Q)
        y_att = _dsa_mixer(proj, B, L, dst['att_q'] // attw, dst['att_z'] // attw, dst['idx_q'] // attw,
                           small // (4 * LANE), dst['ssd_dt'] // LANE, band, far, nh_att, TQ)
        x2 = _outproj(y_s5, y_ssd, y_att, w_out_b, l, x2, final_norm_w.reshape(1, d), tm_out,
                      final=(l == depth - 1))
    return x2.reshape(B, L, d)
```

```python
import functools
import math

import numpy as np
import jax
import jax.numpy as jnp
from jax import lax
from jax.experimental import pallas as pl
from jax.experimental.pallas import tpu as pltpu

F32 = jnp.float32
BF16 = jnp.bfloat16

EPS = 1e-6
CHUNK = 64

S5_GROUP = 16
S5_STATE = 64
SSD_HEAD_DIM = 64
SSD_STATE = 128
SSD_GROUPS = 4
SSD_CONV = 4
ATT_HEAD_DIM = 64
IDX_DIM = 64
TOPK = 256
N_BUCKETS = 32
MAX_DISTANCE = 128

LANE = 128
SUBLANE = 8
VMEM_LIMIT = 56 * 1024 * 1024

INT_MIN = -2 ** 31
NEG = -1e30


def _cparams(sem):
    return pltpu.CompilerParams(dimension_semantics=sem, vmem_limit_bytes=VMEM_LIMIT)


def _inproj_body(x_ref, nw_ref, w_ref, o_ref, h_ref):
    @pl.when(pl.program_id(1) == 0)
    def _():
        xf = x_ref[...]
        ms = jnp.mean(xf * xf, axis=-1, keepdims=True)
        h_ref[...] = (xf * lax.rsqrt(ms + EPS) * nw_ref[...]).astype(BF16)

    o_ref[...] = jnp.dot(h_ref[...], w_ref[...], preferred_element_type=F32)


def _inproj(x2, nw, w, layer, tm, tn):
    m, d = x2.shape
    n = w.shape[2]
    return pl.pallas_call(
        _inproj_body,
        grid=(m // tm, n // tn),
        in_specs=[pl.BlockSpec((tm, d), lambda i, j: (i, 0)),
                  pl.BlockSpec((1, d), lambda i, j: (0, 0)),
                  pl.BlockSpec((None, d, tn), lambda i, j: (layer, 0, j))],
        out_specs=pl.BlockSpec((tm, tn), lambda i, j: (i, j)),
        out_shape=jax.ShapeDtypeStruct((m, n), F32),
        scratch_shapes=[pltpu.VMEM((tm, d), BF16)],
        compiler_params=_cparams(("parallel", "arbitrary")),
        name="inproj",
    )(x2, nw, w)


def _outproj_body(ys5_ref, yssd_ref, yatt_ref, w_ref, x_ref, fnw_ref, o_ref, *, w5, wssd, final):
    acc = x_ref[...]
    acc += jnp.dot(ys5_ref[...], w_ref[0:w5, :], preferred_element_type=F32)
    acc += jnp.dot(yssd_ref[...], w_ref[w5:w5 + wssd, :], preferred_element_type=F32)
    acc += jnp.dot(yatt_ref[...], w_ref[w5 + wssd:, :], preferred_element_type=F32)
    if final:
        ms = jnp.mean(acc * acc, axis=-1, keepdims=True)
        acc = acc * lax.rsqrt(ms + EPS) * fnw_ref[...]
    o_ref[...] = acc


def _outproj(ys5, yssd, yatt, w, layer, x2, fnw, tm, final):
    m, d = x2.shape
    w5, wssd, watt = ys5.shape[1], yssd.shape[1], yatt.shape[1]
    body = functools.partial(_outproj_body, w5=w5, wssd=wssd, final=final)
    return pl.pallas_call(
        body,
        grid=(m // tm,),
        in_specs=[pl.BlockSpec((tm, w5), lambda i: (i, 0)),
                  pl.BlockSpec((tm, wssd), lambda i: (i, 0)),
                  pl.BlockSpec((tm, watt), lambda i: (i, 0)),
                  pl.BlockSpec((None,) + w.shape[1:], lambda i: (layer, 0, 0)),
                  pl.BlockSpec((tm, d), lambda i: (i, 0)),
                  pl.BlockSpec((1, d), lambda i: (0, 0))],
        out_specs=pl.BlockSpec((tm, d), lambda i: (i, 0)),
        out_shape=jax.ShapeDtypeStruct((m, d), F32),
        compiler_params=_cparams(("parallel",)),
        name="outproj",
    )(ys5, yssd, yatt, w, x2, fnw)


def _shift_down_one(x):
    rolled = pltpu.roll(x, 1, axis=0)
    row = lax.broadcasted_iota(jnp.int32, x.shape, 0)
    return jnp.where(row == 0, 0.0, rolled)


def _s5_body(u_ref, z_ref, bblk_ref, cblk_ref, lam_ref, lamseg_ref, d_ref, gw_ref, gb_ref,
             o_ref, uperm_ref, xs_ref, yperm_ref, *, nchunk, cw, sw, rt):
    L = u_ref.shape[0]
    seg = L // SUBLANE

    for c in range(nchunk):
        for j in range(SUBLANE):
            for k in range(cw // LANE):
                lo = c * cw + k * LANE
                uperm_ref[k, pl.ds(j, seg, stride=SUBLANE), :] = u_ref[pl.ds(j * seg, seg), lo:lo + LANE]
        for r0 in range(0, L, rt):
            up = jnp.concatenate([uperm_ref[k, r0:r0 + rt, :] for k in range(cw // LANE)], axis=1)
            xs_ref[r0:r0 + rt, :] = jnp.dot(up.astype(BF16), bblk_ref[c], preferred_element_type=F32)
        lr = lam_ref[c, 0]
        li = lam_ref[c, 1]

        def scan_step(tau, carry):
            xr, xi = carry
            row = pl.multiple_of(tau * SUBLANE, SUBLANE)
            nxr = lr * xr - li * xi + xs_ref[pl.ds(row, SUBLANE), 0:sw]
            nxi = lr * xi + li * xr + xs_ref[pl.ds(row, SUBLANE), sw:2 * sw]
            xs_ref[pl.ds(row, SUBLANE), 0:sw] = nxr
            xs_ref[pl.ds(row, SUBLANE), sw:2 * sw] = nxi
            return nxr, nxi

        zero = jnp.zeros((SUBLANE, sw), F32)
        er, ei = lax.fori_loop(0, seg, scan_step, (zero, zero))

        sr = lamseg_ref[c, 0]
        si = lamseg_ref[c, 1]
        cr, ci = zero, zero
        for _ in range(SUBLANE - 1):
            tr = er + (sr * cr - si * ci)
            ti = ei + (sr * ci + si * cr)
            cr, ci = _shift_down_one(tr), _shift_down_one(ti)

        def fix_step(tau, carry):
            fr, fi = carry
            nfr = lr * fr - li * fi
            nfi = lr * fi + li * fr
            row = pl.multiple_of(tau * SUBLANE, SUBLANE)
            xs_ref[pl.ds(row, SUBLANE), 0:sw] += nfr
            xs_ref[pl.ds(row, SUBLANE), sw:2 * sw] += nfi
            return nfr, nfi

        lax.fori_loop(0, seg, fix_step, (cr, ci))

        for r0 in range(0, L, rt):
            yc = jnp.dot(xs_ref[r0:r0 + rt, :].astype(BF16), cblk_ref[c], preferred_element_type=F32)
            for k in range(cw // LANE):
                yperm_ref[c * (cw // LANE) + k, r0:r0 + rt, :] = yc[:, k * LANE:(k + 1) * LANE]

    for j in range(SUBLANE):
        y = jnp.concatenate([yperm_ref[k, pl.ds(j, seg, stride=SUBLANE), :]
                             for k in range(yperm_ref.shape[0])], axis=1)
        y = y + d_ref[...] * u_ref[j * seg:(j + 1) * seg, :]
        y = jax.nn.gelu(y)
        g = jnp.dot(y.astype(BF16), gw_ref[...], preferred_element_type=F32) + gb_ref[...]
        y = y * jax.nn.sigmoid(g)
        o_ref[j * seg:(j + 1) * seg, :] = (y * jax.nn.silu(z_ref[j * seg:(j + 1) * seg, :])).astype(BF16)


def _s5_mixer(proj, B, L, u_blk, z_blk, width, prm):
    bblk, cblk, lam, lamseg, dvec, gw, gb = prm
    nchunk, cw, sw2 = bblk.shape
    sw = sw2 // 2
    rt = min(512, L)
    body = functools.partial(_s5_body, nchunk=nchunk, cw=cw, sw=sw, rt=rt)
    full = lambda a: pl.BlockSpec(a.shape, lambda b: (0,) * a.ndim)
    return pl.pallas_call(
        body,
        grid=(B,),
        in_specs=[pl.BlockSpec((L, width), lambda b: (b, u_blk)),
                  pl.BlockSpec((L, width), lambda b: (b, z_blk)),
                  full(bblk), full(cblk), full(lam), full(lamseg), full(dvec), full(gw), full(gb)],
        out_specs=pl.BlockSpec((L, width), lambda b: (b, 0)),
        out_shape=jax.ShapeDtypeStruct((B * L, width), BF16),
        scratch_shapes=[pltpu.VMEM((cw // LANE, L, LANE), F32),
                        pltpu.VMEM((L, 2 * sw), F32),
                        pltpu.VMEM((width // LANE, L, LANE), F32)],
        compiler_params=_cparams(("parallel",)),
        name="s5_mixer",
    )(proj, proj, bblk, cblk, lam, lamseg, dvec, gw, gb)


def _cpow(re, im, n):
    rr, ri = jnp.ones_like(re), jnp.zeros_like(im)
    br, bi = re, im
    while n:
        if n & 1:
            rr, ri = rr * br - ri * bi, rr * bi + ri * br
        br, bi = br * br - bi * bi, 2.0 * br * bi
        n >>= 1
    return rr, ri


def _s5_params(A_re, A_im, log_dt, B_re, B_im, C_re, C_im, D, glu_w, glu_b, seg, nchunk):
    G, P, C = B_re.shape
    dt = jnp.exp(log_dt)[:, None]
    lre = jnp.minimum(A_re, -1e-4)
    lim = A_im
    mag = jnp.exp(lre * dt)
    lbr = mag * jnp.cos(lim * dt)
    lbi = mag * jnp.sin(lim * dt)
    nr, ni = lbr - 1.0, lbi
    den = lre * lre + lim * lim
    fr = (nr * lre + ni * lim) / den
    fi = (ni * lre - nr * lim) / den
    bbr = fr[..., None] * B_re - fi[..., None] * B_im
    bbi = fr[..., None] * B_im + fi[..., None] * B_re
    gc = G // nchunk
    eye = jnp.eye(gc, dtype=F32)

    def blockdiag_in(bb):
        t = jnp.transpose(bb, (0, 2, 1)).reshape(nchunk, gc, C, P)
        return jnp.einsum('ngcp,gh->ngchp', t, eye).reshape(nchunk, gc * C, gc * P)

    def blockdiag_out(cc):
        t = jnp.transpose(cc, (0, 2, 1)).reshape(nchunk, gc, P, C)
        return jnp.einsum('ngpc,gh->ngphc', t, eye).reshape(nchunk, gc * P, gc * C)

    bblk = jnp.concatenate([blockdiag_in(bbr), blockdiag_in(bbi)], axis=-1).astype(BF16)
    cblk = jnp.concatenate([blockdiag_out(C_re), blockdiag_out(-C_im)], axis=1).astype(BF16)

    def rows(v):
        return jnp.broadcast_to(v.reshape(nchunk, 1, gc * P), (nchunk, SUBLANE, gc * P))

    lam = jnp.stack([rows(lbr), rows(lbi)], axis=1)
    pr, pi = _cpow(lbr, lbi, seg)
    lamseg = jnp.stack([rows(pr), rows(pi)], axis=1)
    return (bblk, cblk, lam, lamseg, D.reshape(1, G * C), glu_w.astype(BF16), glu_b.reshape(1, -1))


def _split3(x):
    h = x.astype(BF16)
    r = x - h.astype(F32)
    m = r.astype(BF16)
    l = (r - m.astype(F32)).astype(BF16)
    return h, m, l


def _dot_f32(a, b_bf16):
    h, m, l = _split3(a)
    dot = lambda p: jnp.dot(p, b_bf16, preferred_element_type=F32)
    return dot(h) + dot(m) + dot(l)


def _ssd_body(z_ref, xbc_ref, dt_ref, cw_ref, cb_ref, dtb_ref, a_ref, dx_ref, nw_ref, e_ref,
              o_ref, state_ref, xpad_ref, *, nheads, hd, ns, ngroups):
    Q = z_ref.shape[0]
    width = nheads * hd
    gw = width // ngroups
    halo = SUBLANE

    @pl.when(pl.program_id(1) == 0)
    def _():
        state_ref[...] = jnp.zeros_like(state_ref)
        xpad_ref[0:halo, :] = jnp.zeros((halo, xpad_ref.shape[1]), F32)

    xpad_ref[halo:halo + Q, :] = xbc_ref[...]
    acc = cb_ref[...] + cw_ref[0:1, :] * xpad_ref[halo - 3:halo - 3 + Q, :]
    for k in range(1, SSD_CONV):
        acc += cw_ref[k:k + 1, :] * xpad_ref[halo - 3 + k:halo - 3 + k + Q, :]
    xpad_ref[0:halo, :] = xpad_ref[Q:Q + halo, :]
    xc = jax.nn.silu(acc)
    xs = xc[:, 0:width]
    bm = xc[:, width:width + ngroups * ns].astype(BF16)
    cm = xc[:, width + ngroups * ns:].astype(BF16)

    lane = lax.broadcasted_iota(jnp.int32, (Q, LANE), 1)
    dt = jnp.where(lane < nheads, jax.nn.softplus(dt_ref[...] + dtb_ref[...]), 0.0)
    a = dt * a_ref[...]
    rowi = lax.broadcasted_iota(jnp.int32, (Q, Q), 0)
    coli = lax.broadcasted_iota(jnp.int32, (Q, Q), 1)
    tril = coli <= rowi
    acs = _dot_f32_lhs(tril.astype(BF16), a)
    acs_t = acs.T
    last = acs[Q - 1:Q, :]
    emat = e_ref[...]
    dt_x = _expand(dt, emat)
    dec_x = _expand(jnp.exp(acs), emat)
    dte_x = _expand(jnp.exp(last - acs), emat)
    dlast_x = dec_x[Q - 1:Q, :]

    xd = xs * dt_x
    xd_b = xd.astype(BF16)
    xw_b = (xd * dte_x).astype(BF16)
    st = state_ref[...]
    st_b = st.astype(BF16)

    y_parts = []
    new_state = []
    hlane = lax.broadcasted_iota(jnp.int32, (Q, LANE), 1) < hd
    heads_per_group = nheads // ngroups
    for g in range(ngroups):
        cg = cm[:, g * ns:(g + 1) * ns]
        bg = bm[:, g * ns:(g + 1) * ns]
        cb = lax.dot_general(cg, bg, (((1,), (1,)), ((), ())), preferred_element_type=F32)
        y_off = jnp.dot(cg, st_b[:, g * gw:(g + 1) * gw], preferred_element_type=F32)
        pair_out = []
        for pr in range(gw // LANE):
            xpair = xd_b[:, g * gw + pr * LANE: g * gw + (pr + 1) * LANE]
            res = []
            for hh in range(LANE // hd):
                h = g * heads_per_group + pr * (LANE // hd) + hh
                diff = acs[:, h:h + 1] - acs_t[h:h + 1, :]
                s_h = jnp.where(tril, cb * jnp.exp(jnp.minimum(diff, 0.0)), 0.0).astype(BF16)
                res.append(jnp.dot(s_h, xpair, preferred_element_type=F32))
            pair_out.append(jnp.where(hlane, res[0], res[1]))
        y_diag = jnp.concatenate(pair_out, axis=1)
        y_parts.append(y_diag + y_off * dec_x[:, g * gw:(g + 1) * gw])
        upd = lax.dot_general(bg, xw_b[:, g * gw:(g + 1) * gw], (((0,), (0,)), ((), ())),
                              preferred_element_type=F32)
        new_state.append(st[:, g * gw:(g + 1) * gw] * dlast_x[:, g * gw:(g + 1) * gw] + upd)
    state_ref[...] = jnp.concatenate(new_state, axis=1)

    y = jnp.concatenate(y_parts, axis=1) + xs * dx_ref[...]
    gt = y * jax.nn.silu(z_ref[...])
    ms = jnp.mean(gt * gt, axis=-1, keepdims=True)
    o_ref[...] = (gt * lax.rsqrt(ms + EPS) * nw_ref[...]).astype(BF16)


def _dot_f32_lhs(a_bf16, b):
    h, m, l = _split3(b)
    dot = lambda p: jnp.dot(a_bf16, p, preferred_element_type=F32)
    return dot(h) + dot(m) + dot(l)


def _expand(v, emat):
    return _dot_f32(v, emat)


def _ssd_mixer(proj, B, L, z_blk, xbc_blk, dt_blk, prm, Q):
    cw, cb, dtb, avec, dx, nw, emat = prm
    width = nw.shape[1]
    cdim = cw.shape[1]
    nheads = width // SSD_HEAD_DIM
    nt = L // Q
    body = functools.partial(_ssd_body, nheads=nheads, hd=SSD_HEAD_DIM, ns=SSD_STATE, ngroups=SSD_GROUPS)
    full = lambda a: pl.BlockSpec(a.shape, lambda b, t: (0,) * a.ndim)
    return pl.pallas_call(
        body,
        grid=(B, nt),
        in_specs=[pl.BlockSpec((Q, width), lambda b, t: (b * nt + t, z_blk)),
                  pl.BlockSpec((Q, cdim), lambda b, t: (b * nt + t, xbc_blk)),
                  pl.BlockSpec((Q, LANE), lambda b, t: (b * nt + t, dt_blk)),
                  full(cw), full(cb), full(dtb), full(avec), full(dx), full(nw), full(emat)],
        out_specs=pl.BlockSpec((Q, width), lambda b, t: (b * nt + t, 0)),
        out_shape=jax.ShapeDtypeStruct((B * L, width), BF16),
        scratch_shapes=[pltpu.VMEM((SSD_STATE, width), F32),
                        pltpu.VMEM((Q + SUBLANE, cdim), F32)],
        compiler_params=_cparams(("parallel", "arbitrary")),
        name="ssd_mixer",
    )(proj, proj, proj, cw, cb, dtb, avec, dx, nw, emat)


def _ssd_params(conv_w, conv_b, dt_bias, A_log, D, norm_w):
    nheads = dt_bias.shape[0]
    width = nheads * SSD_HEAD_DIM
    pad = lambda v: jnp.zeros((1, LANE), F32).at[0, :nheads].set(v)
    emat = jnp.zeros((LANE, width), F32).at[
        jnp.repeat(jnp.arange(nheads), SSD_HEAD_DIM), jnp.arange(width)].set(1.0).astype(BF16)
    return (conv_w, conv_b.reshape(1, -1), pad(dt_bias), pad(-jnp.exp(A_log)),
            jnp.repeat(D, SSD_HEAD_DIM).reshape(1, width), norm_w.reshape(1, width), emat)


def _float_key(x):
    b = lax.bitcast_convert_type(x, jnp.int32)
    return b ^ ((b >> 31) & 0x7FFFFFFF)


def _place(blk, lane0, width, dst):
    lane = lax.broadcasted_iota(jnp.int32, blk.shape, 1)
    shift = (dst - lane0) % LANE
    r = pltpu.roll(blk, shift, axis=1) if shift else blk
    lo = jnp.where((lane >= dst) & (lane < min(dst + width, LANE)), r, 0.0)
    hi = jnp.where(lane < dst + width - LANE, r, 0.0)
    return jnp.concatenate([lo, hi], axis=1)


def _lane_window(x, start, width):
    off = start % LANE
    base = start - off
    if off == 0:
        return x[:, start:start + width]
    lane = lax.broadcasted_iota(jnp.int32, (x.shape[0], LANE), 1)
    out = []
    for p in range(width // LANE):
        a = pltpu.roll(x[:, base + p * LANE:base + (p + 1) * LANE], LANE - off, axis=1)
        b = pltpu.roll(x[:, base + (p + 1) * LANE:base + (p + 2) * LANE], LANE - off, axis=1)
        out.append(jnp.where(lane < LANE - off, a, b))
    return jnp.concatenate(out, axis=1)


def _dsa_body(qx_ref, kv_ref, ki_ref, band_ref, far_ref, o_ref,
              kpad_ref, kipad_ref, vt_ref, qop_ref, key_ref, lg_ref, bound_ref, m_ref, l_ref, acc_ref,
              *, nheads, hd, topk, offs):
    i = pl.program_id(1)
    TQ = qx_ref.shape[0]
    KB = TQ
    L = kv_ref.shape[0]
    nkb_total = L // KB
    NT = (((1,), (1,)), ((), ()))
    WIN = 2 * LANE

    def head_window(field_lo, h):
        lo = field_lo + h * hd
        return (lo // LANE) * LANE, lo % LANE

    head_offs = sorted({head_window(offs['q'], h)[1] for h in range(nheads)}
                       | {head_window(offs['qi'], h)[1] for h in range(nheads)})

    @pl.when(i == 0)
    def _():
        for kb in range(nkb_total):
            r = slice(kb * KB, (kb + 1) * KB)
            kblk = kv_ref[r, 0:LANE]
            kiblk = ki_ref[r, :]
            for n, o in enumerate(head_offs):
                kpad_ref[n, r, :] = _place(kblk, offs['k'], hd, o).astype(BF16)
                kipad_ref[n, r, :] = _place(kiblk, offs['ki'], hd, o).astype(BF16)
            vt_ref[:, r] = kv_ref[r, :].T[offs['v']:offs['v'] + hd, :].astype(BF16)

    def stage_windows(slot, field_lo, scale):
        bases = sorted({head_window(field_lo, h)[0] for h in range(nheads)})
        for n, base in enumerate(bases):
            qop_ref[slot, n] = (qx_ref[:, base:base + WIN] * scale).astype(BF16)
        return [(bases.index(head_window(field_lo, h)[0]), head_offs.index(head_window(field_lo, h)[1]))
                for h in range(nheads)]

    qs = stage_windows(0, offs['q'], hd ** -0.5)
    qis = stage_windows(1, offs['qi'], 1.0)
    wbase = (offs['w'] // LANE) * LANE
    wt = qx_ref[:, wbase:wbase + LANE].T * ((nheads * IDX_DIM) ** -0.5)
    w_rows = [wt[offs['w'] - wbase + h:offs['w'] - wbase + h + 1, :] for h in range(nheads)]

    RB = LANE

    def rows(kb, r=0, n=KB):
        return pl.ds(pl.multiple_of(kb * KB + r, SUBLANE), n)

    PART = m_ref.shape[1]

    def fold_sum(x):
        return jnp.sum(x.reshape(x.shape[0] // PART, PART, TQ), axis=0)

    def fold_max(x):
        return jnp.max(x.reshape(x.shape[0] // PART, PART, TQ), axis=0)

    def idx_keys(kb, diagonal):
        for r in range(0, KB, RB):
            s = jnp.zeros((RB, TQ), F32)
            for h in range(nheads):
                win, ko = qis[h]
                lg = lax.dot_general(kipad_ref[ko, rows(kb, r, RB), :], qop_ref[1, win], NT,
                                     preferred_element_type=F32)
                s += jnp.maximum(lg, 0.0) * w_rows[h]
            key = _float_key(s)
            if diagonal:
                krow = lax.broadcasted_iota(jnp.int32, (RB, TQ), 0) + r
                qcol = lax.broadcasted_iota(jnp.int32, (RB, TQ), 1)
                key = jnp.where((krow // CHUNK) <= (qcol // CHUNK), key, INT_MIN)
            key_ref[rows(kb, r, RB), :] = key

    def far_keys(kb, _):
        idx_keys(kb, False)
        return 0

    lax.fori_loop(0, i, far_keys, 0)
    idx_keys(i, True)

    def count(preds):
        def body(kb, cs):
            blk = key_ref[rows(kb), :]
            return tuple(c + fold_sum(jnp.where(p(blk, kb), 1.0, 0.0)) for c, p in zip(cs, preds))
        z = jnp.zeros((PART, TQ), F32)
        cs = lax.fori_loop(0, i + 1, body, (z,) * len(preds))
        return [jnp.sum(c, axis=0, keepdims=True) for c in cs]

    def bisect(step, tu):
        bit = jnp.left_shift(jnp.int32(1), 31 - step)
        cand_u = tu | bit
        cand = cand_u ^ INT_MIN
        cnt, = count([lambda blk, kb: blk >= cand])
        return jnp.where(cnt >= topk, cand_u, tu)

    thr = lax.fori_loop(0, 32, bisect, jnp.zeros((1, TQ), jnp.int32)) ^ INT_MIN

    cgt, ceq = count([lambda blk, kb: blk > thr, lambda blk, kb: blk == thr])
    need = topk - cgt
    nbits = int(L).bit_length()
    bound_ref[...] = jnp.full((1, TQ), 2 ** nbits - 1, jnp.int32)

    @pl.when(jnp.max(ceq - need) > 0.0)
    def _():
        krow = lax.broadcasted_iota(jnp.int32, (KB, TQ), 0)

        def bisect_idx(step, bnd):
            bit = jnp.left_shift(jnp.int32(1), nbits - 1 - step)
            cand = bnd | bit
            cnt, = count([lambda blk, kb: (blk == thr) & (krow + kb * KB < cand)])
            return jnp.where(cnt <= need, cand, bnd)

        bound_ref[...] = lax.fori_loop(0, nbits, bisect_idx, jnp.zeros((1, TQ), jnp.int32))

    bound = bound_ref[...]

    m_ref[...] = jnp.full(m_ref.shape, NEG, F32)

    def store_logits(kb, bias_of_head):
        for r in range(0, KB, RB):
            blk = key_ref[rows(kb, r, RB), :]
            pos = lax.broadcasted_iota(jnp.int32, (RB, TQ), 0) + (kb * KB + r)
            keep = ((blk > thr) | ((blk == thr) & (pos < bound))) & (blk != INT_MIN)
            mask = jnp.where(keep, 0.0, NEG)
            for h in range(nheads):
                win, ko = qs[h]
                lg = lax.dot_general(kpad_ref[ko, rows(kb, r, RB), :], qop_ref[0, win], NT,
                                     preferred_element_type=F32)
                lg = lg + bias_of_head(h, r) + mask
                lg_ref[h, rows(kb, r, RB), :] = lg
                m_ref[h] = jnp.maximum(m_ref[h], fold_max(lg))

    def far_logits(kb, _):
        store_logits(kb, lambda h, r: far_ref[h])
        return 0

    lax.fori_loop(0, i - 1, far_logits, 0)

    @pl.when(i >= 1)
    def _():
        store_logits(i - 1, lambda h, r: band_ref[h, r:r + RB, :])

    store_logits(i, lambda h, r: band_ref[h, KB + r:KB + r + RB, :])

    m_fin = [jnp.max(m_ref[h], axis=0, keepdims=True) for h in range(nheads)]
    acc_ref[...] = jnp.zeros(acc_ref.shape, F32)
    l_ref[...] = jnp.zeros(l_ref.shape, F32)

    def accumulate(kb, _):
        vt = vt_ref[:, rows(kb)]
        for h in range(nheads):
            p = jnp.exp(lg_ref[h, rows(kb), :] - m_fin[h])
            l_ref[h] += fold_sum(p)
            acc_ref[h] += jnp.dot(vt, p.astype(BF16), preferred_element_type=F32)
        return 0

    lax.fori_loop(0, i + 1, accumulate, 0)

    outs = [acc_ref[h] / jnp.sum(l_ref[h], axis=0, keepdims=True) for h in range(nheads)]
    out = jnp.concatenate(outs, axis=0).T
    z = _lane_window(qx_ref, offs['z'], nheads * hd)
    o_ref[...] = (out * jax.nn.silu(z)).astype(BF16)


def _dsa_mixer(proj, B, L, src, band, far, nheads, TQ):
    width = nheads * ATT_HEAD_DIM
    nq = L // TQ
    hd = ATT_HEAD_DIM
    qx_w = 16 * LANE
    qx_blk = src['att_q'][0] // qx_w
    qx_lo = qx_blk * qx_w
    kv_w = 2 * LANE
    kv_blk = src['att_k'][0] // kv_w
    ki_blk = src['idx_k'][0] // LANE
    offs = {'q': src['att_q'][0] - qx_lo, 'qi': src['idx_q'][0] - qx_lo, 'w': src['idx_w'][0] - qx_lo,
            'z': src['att_z'][0] - qx_lo, 'k': src['att_k'][0] - kv_blk * kv_w,
            'v': src['att_v'][0] - kv_blk * kv_w, 'ki': src['idx_k'][0] - ki_blk * LANE}
    assert offs['z'] + width + LANE <= qx_w and qx_lo + qx_w <= proj.shape[1]
    assert offs['k'] + hd <= LANE and offs['ki'] + hd <= LANE
    assert offs['v'] % SUBLANE == 0 and offs['v'] + hd <= kv_w
    assert offs['w'] // LANE == (offs['w'] + nheads - 1) // LANE
    win_off = sorted({(offs[f] + h * hd) % LANE for f in ('q', 'qi') for h in range(nheads)})
    win_base = max((sorted({(offs[f] + h * hd) // LANE for h in range(nheads)}) for f in ('q', 'qi')), key=len)
    body = functools.partial(_dsa_body, nheads=nheads, hd=hd, topk=min(TOPK, L // 4), offs=offs)
    full = lambda a: pl.BlockSpec(a.shape, lambda b, i: (0,) * a.ndim)
    return pl.pallas_call(
        body,
        grid=(B, nq),
        in_specs=[pl.BlockSpec((TQ, qx_w), lambda b, i: (b * nq + i, qx_blk)),
                  pl.BlockSpec((L, kv_w), lambda b, i: (b, kv_blk)),
                  pl.BlockSpec((L, LANE), lambda b, i: (b, ki_blk)),
                  full(band), full(far)],
        out_specs=pl.BlockSpec((TQ, width), lambda b, i: (b * nq + i, 0)),
        out_shape=jax.ShapeDtypeStruct((B * L, width), BF16),
        scratch_shapes=[pltpu.VMEM((len(win_off), L, 2 * LANE), BF16),
                        pltpu.VMEM((len(win_off), L, 2 * LANE), BF16),
                        pltpu.VMEM((ATT_HEAD_DIM, L), BF16),
                        pltpu.VMEM((2, len(win_base), TQ, 2 * LANE), BF16),
                        pltpu.VMEM((L, TQ), jnp.int32),
                        pltpu.VMEM((nheads, L, TQ), F32),
                        pltpu.VMEM((1, TQ), jnp.int32),
                        pltpu.VMEM((nheads, 4 * SUBLANE, TQ), F32),
                        pltpu.VMEM((nheads, 4 * SUBLANE, TQ), F32),
                        pltpu.VMEM((nheads, ATT_HEAD_DIM, TQ), F32)],
        compiler_params=_cparams(("parallel", "arbitrary")),
        name="dsa_mixer",
    )(proj, proj, proj, band, far)


def _t5_bucket_static(rel):
    nb = N_BUCKETS // 2
    max_exact = nb // 2
    ret = np.where(rel > 0, nb, 0)
    n = np.abs(rel)
    nf = np.maximum(n, 1).astype(np.float64)
    large = max_exact + (np.log(nf / max_exact) / math.log(MAX_DISTANCE / max_exact)
                         * (nb - max_exact)).astype(np.int32)
    large = np.minimum(large, nb - 1)
    return ret + np.where(n < max_exact, n, large)


def _dsa_bias_tables(rel_bias, TQ):
    a = np.arange(2 * TQ)[:, None]
    j = np.arange(TQ)[None, :]
    band_idx = _t5_bucket_static(a - TQ - j)
    idx = jnp.asarray(band_idx, jnp.int32)[None]
    band = jnp.zeros((rel_bias.shape[1],) + band_idx.shape, F32)
    for bucket in np.unique(band_idx):
        band = jnp.where(idx == int(bucket), rel_bias[int(bucket)][:, None, None], band)
    far_bucket = int(_t5_bucket_static(np.array([-(TQ + 1)]))[0])
    assert far_bucket == int(_t5_bucket_static(np.array([-(10 ** 6)]))[0])
    far = jnp.broadcast_to(rel_bias[far_bucket][:, None, None], (rel_bias.shape[1], 1, TQ))
    return band, far


def _layout(d_model):
    s5w = d_model // 4
    ssdw = d_model // 2
    attw = d_model // 4
    cdim = ssdw + 2 * SSD_GROUPS * SSD_STATE
    nh_ssd = ssdw // SSD_HEAD_DIM
    nh_att = attw // ATT_HEAD_DIM
    splits = (s5w, s5w, ssdw, cdim, nh_ssd, attw, ATT_HEAD_DIM, ATT_HEAD_DIM,
              nh_att * IDX_DIM, IDX_DIM, nh_att, attw)
    names = ('s5_u', 's5_z', 'ssd_z', 'ssd_xbc', 'ssd_dt', 'att_q', 'att_k', 'att_v',
             'idx_q', 'idx_k', 'idx_w', 'att_z')
    src, o = {}, 0
    for nme, s in zip(names, splits):
        src[nme] = (o, s)
        o += s
    return src, o


def kernel(x, norm_w, w_in, s5_A_re, s5_A_im, s5_log_dt, s5_B_re, s5_B_im, s5_C_re, s5_C_im, s5_D, s5_glu_w, s5_glu_b, ssd_conv_w, ssd_conv_b, ssd_dt_bias, ssd_A_log, ssd_D, ssd_norm_w, rel_bias, w_out, final_norm_w):
    B, L, d = x.shape
    depth = w_in.shape[0]
    src, total = _layout(d)
    s5w, ssdw, attw = d // 4, d // 2, d // 4
    cdim = ssdw + 2 * SSD_GROUPS * SSD_STATE
    nh_att = attw // ATT_HEAD_DIM
    assert src['s5_u'][0] % s5w == 0 and src['s5_z'][0] % s5w == 0 and src['ssd_z'][0] % ssdw == 0
    assert src['ssd_xbc'][0] % cdim == 0 and src['ssd_dt'][0] % LANE == 0

    tn = 1024
    total_pad = -(-total // tn) * tn
    w_in_p = jnp.pad(w_in.astype(BF16), ((0, 0), (0, 0), (0, total_pad - total)))
    w_out_b = w_out.astype(BF16)

    TQ = 256
    Q = 128
    band, far = _dsa_bias_tables(rel_bias, TQ)
    tm_in = min(1024, B * L)
    tm_out = min(256, B * L)
    nchunk = 2
    seg = L // SUBLANE

    x2 = x.reshape(B * L, d)
    for l in range(depth):
        proj = _inproj(x2, norm_w[l].reshape(1, d), w_in_p, l, tm_in, tn)
        s5p = _s5_params(s5_A_re[l], s5_A_im[l], s5_log_dt[l], s5_B_re[l], s5_B_im[l], s5_C_re[l],
                         s5_C_im[l], s5_D[l], s5_glu_w[l], s5_glu_b[l], seg, nchunk)
        y_s5 = _s5_mixer(proj, B, L, src['s5_u'][0] // s5w, src['s5_z'][0] // s5w, s5w, s5p)
        ssdp = _ssd_params(ssd_conv_w[l], ssd_conv_b[l], ssd_dt_bias[l], ssd_A_log[l], ssd_D[l], ssd_norm_w[l])
        y_ssd = _ssd_mixer(proj, B, L, src['ssd_z'][0] // ssdw, src['ssd_xbc'][0] // cdim,
                           src['ssd_dt'][0] // LANE, ssdp, Q)
        y_att = _dsa_mixer(proj, B, L, src, band, far, nh_att, TQ)
        x2 = _outproj(y_s5, y_ssd, y_att, w_out_b, l, x2, final_norm_w.reshape(1, d), tm_out,
                      final=(l == depth - 1))
    return x2.reshape(B, L, d)
```

```python
import functools
import math

import numpy as np
import jax
import jax.numpy as jnp
from jax import lax
from jax.experimental import pallas as pl
from jax.experimental.pallas import tpu as pltpu

F32 = jnp.float32
BF16 = jnp.bfloat16

EPS = 1e-6
CHUNK = 64

S5_GROUP = 16
S5_STATE = 64
SSD_HEAD_DIM = 64
SSD_STATE = 128
SSD_GROUPS = 4
SSD_CONV = 4
ATT_HEAD_DIM = 64
IDX_DIM = 64
TOPK = 256
N_BUCKETS = 32
MAX_DISTANCE = 128

LANE = 128
SUBLANE = 8
VMEM_LIMIT = 56 * 1024 * 1024

INT_MIN = -2 ** 31
NEG = -1e30


def _cparams(sem):
    return pltpu.CompilerParams(dimension_semantics=sem, vmem_limit_bytes=VMEM_LIMIT)


def _inproj_body(x_ref, nw_ref, w_ref, o_ref, h_ref):
    @pl.when(pl.program_id(1) == 0)
    def _():
        xf = x_ref[...]
        ms = jnp.mean(xf * xf, axis=-1, keepdims=True)
        h_ref[...] = (xf * lax.rsqrt(ms + EPS) * nw_ref[...]).astype(BF16)

    o_ref[...] = jnp.dot(h_ref[...], w_ref[...], preferred_element_type=F32)


def _inproj(x2, nw, w, layer, tm, tn):
    m, d = x2.shape
    n = w.shape[2]
    return pl.pallas_call(
        _inproj_body,
        grid=(m // tm, n // tn),
        in_specs=[pl.BlockSpec((tm, d), lambda i, j: (i, 0)),
                  pl.BlockSpec((1, d), lambda i, j: (0, 0)),
                  pl.BlockSpec((None, d, tn), lambda i, j: (layer, 0, j))],
        out_specs=pl.BlockSpec((tm, tn), lambda i, j: (i, j)),
        out_shape=jax.ShapeDtypeStruct((m, n), F32),
        scratch_shapes=[pltpu.VMEM((tm, d), BF16)],
        compiler_params=_cparams(("parallel", "arbitrary")),
        name="inproj",
    )(x2, nw, w)


def _outproj_body(ys5_ref, yssd_ref, yatt_ref, w_ref, x_ref, fnw_ref, o_ref, *, w5, wssd, final):
    acc = x_ref[...]
    acc += jnp.dot(ys5_ref[...], w_ref[0:w5, :], preferred_element_type=F32)
    acc += jnp.dot(yssd_ref[...], w_ref[w5:w5 + wssd, :], preferred_element_type=F32)
    acc += jnp.dot(yatt_ref[...], w_ref[w5 + wssd:, :], preferred_element_type=F32)
    if final:
        ms = jnp.mean(acc * acc, axis=-1, keepdims=True)
        acc = acc * lax.rsqrt(ms + EPS) * fnw_ref[...]
    o_ref[...] = acc


def _outproj(ys5, yssd, yatt, w, layer, x2, fnw, tm, final):
    m, d = x2.shape
    w5, wssd, watt = ys5.shape[1], yssd.shape[1], yatt.shape[1]
    body = functools.partial(_outproj_body, w5=w5, wssd=wssd, final=final)
    return pl.pallas_call(
        body,
        grid=(m // tm,),
        in_specs=[pl.BlockSpec((tm, w5), lambda i: (i, 0)),
                  pl.BlockSpec((tm, wssd), lambda i: (i, 0)),
                  pl.BlockSpec((tm, watt), lambda i: (i, 0)),
                  pl.BlockSpec((None,) + w.shape[1:], lambda i: (layer, 0, 0)),
                  pl.BlockSpec((tm, d), lambda i: (i, 0)),
                  pl.BlockSpec((1, d), lambda i: (0, 0))],
        out_specs=pl.BlockSpec((tm, d), lambda i: (i, 0)),
        out_shape=jax.ShapeDtypeStruct((m, d), F32),
        compiler_params=_cparams(("parallel",)),
        name="outproj",
    )(ys5, yssd, yatt, w, x2, fnw)


def _shift_down_one(x):
    rolled = pltpu.roll(x, 1, axis=0)
    row = lax.broadcasted_iota(jnp.int32, x.shape, 0)
    return jnp.where(row == 0, 0.0, rolled)


def _s5_body(u_ref, z_ref, bblk_ref, cblk_ref, lam_ref, lamseg_ref, d_ref, gw_ref, gb_ref,
             o_ref, uperm_ref, xs_ref, yperm_ref, *, nchunk, cw, sw, rt):
    L = u_ref.shape[0]
    seg = L // SUBLANE

    for c in range(nchunk):
        for j in range(SUBLANE):
            for k in range(cw // LANE):
                lo = c * cw + k * LANE
                uperm_ref[k, pl.ds(j, seg, stride=SUBLANE), :] = u_ref[pl.ds(j * seg, seg), lo:lo + LANE]
        for r0 in range(0, L, rt):
            up = jnp.concatenate([uperm_ref[k, r0:r0 + rt, :] for k in range(cw // LANE)], axis=1)
            xs_ref[r0:r0 + rt, :] = jnp.dot(up.astype(BF16), bblk_ref[c], preferred_element_type=F32)
        lr = lam_ref[c, 0]
        li = lam_ref[c, 1]

        def scan_step(tau, carry):
            xr, xi = carry
            row = pl.multiple_of(tau * SUBLANE, SUBLANE)
            nxr = lr * xr - li * xi + xs_ref[pl.ds(row, SUBLANE), 0:sw]
            nxi = lr * xi + li * xr + xs_ref[pl.ds(row, SUBLANE), sw:2 * sw]
            xs_ref[pl.ds(row, SUBLANE), 0:sw] = nxr
            xs_ref[pl.ds(row, SUBLANE), sw:2 * sw] = nxi
            return nxr, nxi

        zero = jnp.zeros((SUBLANE, sw), F32)
        er, ei = lax.fori_loop(0, seg, scan_step, (zero, zero))

        sr = lamseg_ref[c, 0]
        si = lamseg_ref[c, 1]
        cr, ci = zero, zero
        for _ in range(SUBLANE - 1):
            tr = er + (sr * cr - si * ci)
            ti = ei + (sr * ci + si * cr)
            cr, ci = _shift_down_one(tr), _shift_down_one(ti)

        def fix_step(tau, carry):
            fr, fi = carry
            nfr = lr * fr - li * fi
            nfi = lr * fi + li * fr
            row = pl.multiple_of(tau * SUBLANE, SUBLANE)
            xs_ref[pl.ds(row, SUBLANE), 0:sw] += nfr
            xs_ref[pl.ds(row, SUBLANE), sw:2 * sw] += nfi
            return nfr, nfi

        lax.fori_loop(0, seg, fix_step, (cr, ci))

        for r0 in range(0, L, rt):
            yc = jnp.dot(xs_ref[r0:r0 + rt, :].astype(BF16), cblk_ref[c], preferred_element_type=F32)
            for k in range(cw // LANE):
                yperm_ref[c * (cw // LANE) + k, r0:r0 + rt, :] = yc[:, k * LANE:(k + 1) * LANE]

    for j in range(SUBLANE):
        y = jnp.concatenate([yperm_ref[k, pl.ds(j, seg, stride=SUBLANE), :]
                             for k in range(yperm_ref.shape[0])], axis=1)
        y = y + d_ref[...] * u_ref[j * seg:(j + 1) * seg, :]
        y = jax.nn.gelu(y)
        g = jnp.dot(y.astype(BF16), gw_ref[...], preferred_element_type=F32) + gb_ref[...]
        y = y * jax.nn.sigmoid(g)
        o_ref[j * seg:(j + 1) * seg, :] = (y * jax.nn.silu(z_ref[j * seg:(j + 1) * seg, :])).astype(BF16)


def _s5_mixer(proj, B, L, u_blk, z_blk, width, prm):
    bblk, cblk, lam, lamseg, dvec, gw, gb = prm
    nchunk, cw, sw2 = bblk.shape
    sw = sw2 // 2
    rt = min(512, L)
    body = functools.partial(_s5_body, nchunk=nchunk, cw=cw, sw=sw, rt=rt)
    full = lambda a: pl.BlockSpec(a.shape, lambda b: (0,) * a.ndim)
    return pl.pallas_call(
        body,
        grid=(B,),
        in_specs=[pl.BlockSpec((L, width), lambda b: (b, u_blk)),
                  pl.BlockSpec((L, width), lambda b: (b, z_blk)),
                  full(bblk), full(cblk), full(lam), full(lamseg), full(dvec), full(gw), full(gb)],
        out_specs=pl.BlockSpec((L, width), lambda b: (b, 0)),
        out_shape=jax.ShapeDtypeStruct((B * L, width), BF16),
        scratch_shapes=[pltpu.VMEM((cw // LANE, L, LANE), F32),
                        pltpu.VMEM((L, 2 * sw), F32),
                        pltpu.VMEM((width // LANE, L, LANE), F32)],
        compiler_params=_cparams(("parallel",)),
        name="s5_mixer",
    )(proj, proj, bblk, cblk, lam, lamseg, dvec, gw, gb)


def _cpow(re, im, n):
    rr, ri = jnp.ones_like(re), jnp.zeros_like(im)
    br, bi = re, im
    while n:
        if n & 1:
            rr, ri = rr * br - ri * bi, rr * bi + ri * br
        br, bi = br * br - bi * bi, 2.0 * br * bi
        n >>= 1
    return rr, ri


def _s5_params(A_re, A_im, log_dt, B_re, B_im, C_re, C_im, D, glu_w, glu_b, seg, nchunk):
    G, P, C = B_re.shape
    dt = jnp.exp(log_dt)[:, None]
    lre = jnp.minimum(A_re, -1e-4)
    lim = A_im
    mag = jnp.exp(lre * dt)
    lbr = mag * jnp.cos(lim * dt)
    lbi = mag * jnp.sin(lim * dt)
    nr, ni = lbr - 1.0, lbi
    den = lre * lre + lim * lim
    fr = (nr * lre + ni * lim) / den
    fi = (ni * lre - nr * lim) / den
    bbr = fr[..., None] * B_re - fi[..., None] * B_im
    bbi = fr[..., None] * B_im + fi[..., None] * B_re
    gc = G // nchunk
    eye = jnp.eye(gc, dtype=F32)

    def blockdiag_in(bb):
        t = jnp.transpose(bb, (0, 2, 1)).reshape(nchunk, gc, C, P)
        return jnp.einsum('ngcp,gh->ngchp', t, eye).reshape(nchunk, gc * C, gc * P)

    def blockdiag_out(cc):
        t = jnp.transpose(cc, (0, 2, 1)).reshape(nchunk, gc, P, C)
        return jnp.einsum('ngpc,gh->ngphc', t, eye).reshape(nchunk, gc * P, gc * C)

    bblk = jnp.concatenate([blockdiag_in(bbr), blockdiag_in(bbi)], axis=-1).astype(BF16)
    cblk = jnp.concatenate([blockdiag_out(C_re), blockdiag_out(-C_im)], axis=1).astype(BF16)

    def rows(v):
        return jnp.broadcast_to(v.reshape(nchunk, 1, gc * P), (nchunk, SUBLANE, gc * P))

    lam = jnp.stack([rows(lbr), rows(lbi)], axis=1)
    pr, pi = _cpow(lbr, lbi, seg)
    lamseg = jnp.stack([rows(pr), rows(pi)], axis=1)
    return (bblk, cblk, lam, lamseg, D.reshape(1, G * C), glu_w.astype(BF16), glu_b.reshape(1, -1))


def _split3(x):
    h = x.astype(BF16)
    r = x - h.astype(F32)
    m = r.astype(BF16)
    l = (r - m.astype(F32)).astype(BF16)
    return h, m, l


def _dot_f32(a, b_bf16):
    h, m, l = _split3(a)
    dot = lambda p: jnp.dot(p, b_bf16, preferred_element_type=F32)
    return dot(h) + dot(m) + dot(l)


def _ssd_body(z_ref, xbc_ref, dt_ref, cw_ref, cb_ref, dtb_ref, a_ref, dx_ref, nw_ref, e_ref,
              o_ref, state_ref, xpad_ref, *, nheads, hd, ns, ngroups):
    Q = z_ref.shape[0]
    width = nheads * hd
    gw = width // ngroups
    halo = SUBLANE

    @pl.when(pl.program_id(1) == 0)
    def _():
        state_ref[...] = jnp.zeros_like(state_ref)
        xpad_ref[0:halo, :] = jnp.zeros((halo, xpad_ref.shape[1]), F32)

    xpad_ref[halo:halo + Q, :] = xbc_ref[...]
    acc = cb_ref[...] + cw_ref[0:1, :] * xpad_ref[halo - 3:halo - 3 + Q, :]
    for k in range(1, SSD_CONV):
        acc += cw_ref[k:k + 1, :] * xpad_ref[halo - 3 + k:halo - 3 + k + Q, :]
    xpad_ref[0:halo, :] = xpad_ref[Q:Q + halo, :]
    xc = jax.nn.silu(acc)
    xs = xc[:, 0:width]
    bm = xc[:, width:width + ngroups * ns].astype(BF16)
    cm = xc[:, width + ngroups * ns:].astype(BF16)

    lane = lax.broadcasted_iota(jnp.int32, (Q, LANE), 1)
    dt = jnp.where(lane < nheads, jax.nn.softplus(dt_ref[...] + dtb_ref[...]), 0.0)
    a = dt * a_ref[...]
    rowi = lax.broadcasted_iota(jnp.int32, (Q, Q), 0)
    coli = lax.broadcasted_iota(jnp.int32, (Q, Q), 1)
    tril = coli <= rowi
    acs = _dot_f32_lhs(tril.astype(BF16), a)
    acs_t = acs.T
    last = acs[Q - 1:Q, :]
    emat = e_ref[...]
    dt_x = _expand(dt, emat)
    dec_x = _expand(jnp.exp(acs), emat)
    dte_x = _expand(jnp.exp(last - acs), emat)
    dlast_x = dec_x[Q - 1:Q, :]

    xd = xs * dt_x
    xd_b = xd.astype(BF16)
    xw_b = (xd * dte_x).astype(BF16)
    st = state_ref[...]
    st_b = st.astype(BF16)

    y_parts = []
    new_state = []
    hlane = lax.broadcasted_iota(jnp.int32, (Q, LANE), 1) < hd
    heads_per_group = nheads // ngroups
    for g in range(ngroups):
        cg = cm[:, g * ns:(g + 1) * ns]
        bg = bm[:, g * ns:(g + 1) * ns]
        cb = lax.dot_general(cg, bg, (((1,), (1,)), ((), ())), preferred_element_type=F32)
        y_off = jnp.dot(cg, st_b[:, g * gw:(g + 1) * gw], preferred_element_type=F32)
        pair_out = []
        for pr in range(gw // LANE):
            xpair = xd_b[:, g * gw + pr * LANE: g * gw + (pr + 1) * LANE]
            res = []
            for hh in range(LANE // hd):
                h = g * heads_per_group + pr * (LANE // hd) + hh
                diff = acs[:, h:h + 1] - acs_t[h:h + 1, :]
                s_h = jnp.where(tril, cb * jnp.exp(jnp.minimum(diff, 0.0)), 0.0).astype(BF16)
                res.append(jnp.dot(s_h, xpair, preferred_element_type=F32))
            pair_out.append(jnp.where(hlane, res[0], res[1]))
        y_diag = jnp.concatenate(pair_out, axis=1)
        y_parts.append(y_diag + y_off * dec_x[:, g * gw:(g + 1) * gw])
        upd = lax.dot_general(bg, xw_b[:, g * gw:(g + 1) * gw], (((0,), (0,)), ((), ())),
                              preferred_element_type=F32)
        new_state.append(st[:, g * gw:(g + 1) * gw] * dlast_x[:, g * gw:(g + 1) * gw] + upd)
    state_ref[...] = jnp.concatenate(new_state, axis=1)

    y = jnp.concatenate(y_parts, axis=1) + xs * dx_ref[...]
    gt = y * jax.nn.silu(z_ref[...])
    ms = jnp.mean(gt * gt, axis=-1, keepdims=True)
    o_ref[...] = (gt * lax.rsqrt(ms + EPS) * nw_ref[...]).astype(BF16)


def _dot_f32_lhs(a_bf16, b):
    h, m, l = _split3(b)
    dot = lambda p: jnp.dot(a_bf16, p, preferred_element_type=F32)
    return dot(h) + dot(m) + dot(l)


def _expand(v, emat):
    return _dot_f32(v, emat)


def _ssd_mixer(proj, B, L, z_blk, xbc_blk, dt_blk, prm, Q):
    cw, cb, dtb, avec, dx, nw, emat = prm
    width = nw.shape[1]
    cdim = cw.shape[1]
    nheads = width // SSD_HEAD_DIM
    nt = L // Q
    body = functools.partial(_ssd_body, nheads=nheads, hd=SSD_HEAD_DIM, ns=SSD_STATE, ngroups=SSD_GROUPS)
    full = lambda a: pl.BlockSpec(a.shape, lambda b, t: (0,) * a.ndim)
    return pl.pallas_call(
        body,
        grid=(B, nt),
        in_specs=[pl.BlockSpec((Q, width), lambda b, t: (b * nt + t, z_blk)),
                  pl.BlockSpec((Q, cdim), lambda b, t: (b * nt + t, xbc_blk)),
                  pl.BlockSpec((Q, LANE), lambda b, t: (b * nt + t, dt_blk)),
                  full(cw), full(cb), full(dtb), full(avec), full(dx), full(nw), full(emat)],
        out_specs=pl.BlockSpec((Q, width), lambda b, t: (b * nt + t, 0)),
        out_shape=jax.ShapeDtypeStruct((B * L, width), BF16),
        scratch_shapes=[pltpu.VMEM((SSD_STATE, width), F32),
                        pltpu.VMEM((Q + SUBLANE, cdim), F32)],
        compiler_params=_cparams(("parallel", "arbitrary")),
        name="ssd_mixer",
    )(proj, proj, proj, cw, cb, dtb, avec, dx, nw, emat)


def _ssd_params(conv_w, conv_b, dt_bias, A_log, D, norm_w):
    nheads = dt_bias.shape[0]
    width = nheads * SSD_HEAD_DIM
    pad = lambda v: jnp.zeros((1, LANE), F32).at[0, :nheads].set(v)
    emat = jnp.zeros((LANE, width), F32).at[
        jnp.repeat(jnp.arange(nheads), SSD_HEAD_DIM), jnp.arange(width)].set(1.0).astype(BF16)
    return (conv_w, conv_b.reshape(1, -1), pad(dt_bias), pad(-jnp.exp(A_log)),
            jnp.repeat(D, SSD_HEAD_DIM).reshape(1, width), norm_w.reshape(1, width), emat)


def _float_key(x):
    b = lax.bitcast_convert_type(x, jnp.int32)
    return b ^ ((b >> 31) & 0x7FFFFFFF)


def _place(blk, lane0, width, dst):
    lane = lax.broadcasted_iota(jnp.int32, blk.shape, 1)
    shift = (dst - lane0) % LANE
    r = pltpu.roll(blk, shift, axis=1) if shift else blk
    lo = jnp.where((lane >= dst) & (lane < min(dst + width, LANE)), r, 0.0)
    hi = jnp.where(lane < dst + width - LANE, r, 0.0)
    return jnp.concatenate([lo, hi], axis=1)


def _head_offsets(offs, nheads, hd):
    return sorted({(offs[f] + h * hd) % LANE for f in ('q', 'qi') for h in range(nheads)})


def _lane_window(x, start, width):
    off = start % LANE
    base = start - off
    if off == 0:
        return x[:, start:start + width]
    lane = lax.broadcasted_iota(jnp.int32, (x.shape[0], LANE), 1)
    out = []
    for p in range(width // LANE):
        a = pltpu.roll(x[:, base + p * LANE:base + (p + 1) * LANE], LANE - off, axis=1)
        b = pltpu.roll(x[:, base + (p + 1) * LANE:base + (p + 2) * LANE], LANE - off, axis=1)
        out.append(jnp.where(lane < LANE - off, a, b))
    return jnp.concatenate(out, axis=1)


def _dsa_body(qx_ref, kv_ref, ki_ref, band_ref, far_ref, o_ref,
              kpad_ref, kipad_ref, vt_ref, qop_ref, key_ref, hi_ref, lo_ref, lom_ref, lg_ref, bound_ref,
              m_ref, l_ref, acc_ref,
              *, nheads, hd, topk, offs):
    i = pl.program_id(1)
    TQ = qx_ref.shape[0]
    KB = TQ
    L = kv_ref.shape[0]
    nkb_total = L // KB
    NT = (((1,), (1,)), ((), ()))
    WIN = kpad_ref.shape[2]

    def head_window(field_lo, h):
        lo = field_lo + h * hd
        return (lo // LANE) * LANE, lo % LANE

    head_offs = _head_offsets(offs, nheads, hd)

    @pl.when(i == 0)
    def _():
        for kb in range(nkb_total):
            r = slice(kb * KB, (kb + 1) * KB)
            kgrp = (offs['k'] // LANE) * LANE
            kblk = kv_ref[r, kgrp:kgrp + LANE]
            kiblk = ki_ref[r, :]
            for n, o in enumerate(head_offs):
                kpad_ref[n, r, :] = _place(kblk, offs['k'] - kgrp, hd, o)[:, :WIN].astype(BF16)
                kipad_ref[n, r, :] = _place(kiblk, offs['ki'], hd, o)[:, :WIN].astype(BF16)
            vt_ref[:, r] = kv_ref[r, :].T[offs['v']:offs['v'] + hd, :].astype(BF16)

    def stage_windows(slot, field_lo, scale):
        bases = sorted({head_window(field_lo, h)[0] for h in range(nheads)})
        for n, base in enumerate(bases):
            qop_ref[slot, n] = (qx_ref[:, base:base + WIN] * scale).astype(BF16)
        return [(bases.index(head_window(field_lo, h)[0]), head_offs.index(head_window(field_lo, h)[1]))
                for h in range(nheads)]

    qs = stage_windows(0, offs['q'], hd ** -0.5)
    qis = stage_windows(1, offs['qi'], 1.0)
    wbase = (offs['w'] // LANE) * LANE
    wt = qx_ref[:, wbase:wbase + LANE].T * ((nheads * IDX_DIM) ** -0.5)
    w_rows = [wt[offs['w'] - wbase + h:offs['w'] - wbase + h + 1, :] for h in range(nheads)]

    RB = LANE
    I16 = jnp.int16
    HALF = 1 << 15

    def rows(kb, r=0, n=KB):
        return pl.ds(pl.multiple_of(kb * KB + r, SUBLANE), n)

    PART = m_ref.shape[1]

    def fold_sum(x):
        return jnp.sum(x.reshape(x.shape[0] // PART, PART, TQ), axis=0)

    def fold_max(x):
        return jnp.max(x.reshape(x.shape[0] // PART, PART, TQ), axis=0)

    def idx_keys(kb, diagonal):
        for r in range(0, KB, RB):
            s = jnp.zeros((RB, TQ), F32)
            for h in range(nheads):
                win, ko = qis[h]
                lg = lax.dot_general(kipad_ref[ko, rows(kb, r, RB), :], qop_ref[1, win], NT,
                                     preferred_element_type=F32)
                s += jnp.maximum(lg, 0.0) * w_rows[h]
            key = _float_key(s)
            if diagonal:
                krow = lax.broadcasted_iota(jnp.int32, (RB, TQ), 0) + r
                qcol = lax.broadcasted_iota(jnp.int32, (RB, TQ), 1)
                key = jnp.where((krow // CHUNK) <= (qcol // CHUNK), key, INT_MIN)
            key_ref[rows(kb, r, RB), :] = key
            hi_ref[rows(kb, r, RB), :] = (key >> 16).astype(I16)
            lo_ref[rows(kb, r, RB), :] = ((key & (2 * HALF - 1)) - HALF).astype(I16)

    def far_keys(kb, _):
        idx_keys(kb, False)
        return 0

    lax.fori_loop(0, i, far_keys, 0)
    idx_keys(i, True)

    def count(preds):
        def body(kb, cs):
            blk = key_ref[rows(kb), :]
            return tuple(c + fold_sum(jnp.where(p(blk, kb), 1.0, 0.0)) for c, p in zip(cs, preds))
        z = jnp.zeros((PART, TQ), F32)
        cs = lax.fori_loop(0, i + 1, body, (z,) * len(preds))
        return [jnp.sum(c, axis=0, keepdims=True) for c in cs]

    PACK = 2 * SUBLANE
    P16 = 4 * PACK

    def rows16(x):
        return x.reshape(x.shape[0] // PACK, PACK, TQ)

    def as16(v):
        return jnp.broadcast_to(v, (PACK, TQ)).astype(I16)

    def count16(src_ref, pred):
        def body(kb, c):
            ind = jnp.where(pred(rows16(src_ref[rows(kb), :])), I16(1), I16(0)).reshape(KB // P16, P16, TQ)
            for n in range(KB // P16):
                c = c + ind[n]
            return c
        c = lax.fori_loop(0, i + 1, body, jnp.zeros((P16, TQ), I16))
        return jnp.sum(c.astype(F32), axis=0, keepdims=True)

    def bisect16(src_ref, target):
        def step(s, tu):
            cand_u = tu | jnp.left_shift(jnp.int32(1), 15 - s)
            cand = as16(cand_u - HALF)[None]
            cnt = count16(src_ref, lambda blk: blk >= cand)
            return jnp.where(cnt >= target, cand_u, tu)
        return lax.fori_loop(0, 16, step, jnp.zeros((1, TQ), jnp.int32))

    thr_hi = bisect16(hi_ref, float(topk)) - HALF
    thr_hi16 = as16(thr_hi)[None]
    need_lo = topk - count16(hi_ref, lambda blk: blk > thr_hi16)

    def mask_low_halves(kb, _):
        keep_lo = rows16(hi_ref[rows(kb), :]) == thr_hi16
        lom_ref[rows(kb), :] = jnp.where(keep_lo, rows16(lo_ref[rows(kb), :]), I16(-HALF)).reshape(KB, TQ)
        return 0

    lax.fori_loop(0, i + 1, mask_low_halves, 0)
    thr = thr_hi * (2 * HALF) + bisect16(lom_ref, need_lo)

    cgt, ceq = count([lambda blk, kb: blk > thr, lambda blk, kb: blk == thr])
    need = topk - cgt
    nbits = int(L).bit_length()
    bound_ref[...] = jnp.full((1, TQ), 2 ** nbits - 1, jnp.int32)

    @pl.when(jnp.max(ceq - need) > 0.0)
    def _():
        krow = lax.broadcasted_iota(jnp.int32, (KB, TQ), 0)

        def bisect_idx(step, bnd):
            bit = jnp.left_shift(jnp.int32(1), nbits - 1 - step)
            cand = bnd | bit
            cnt, = count([lambda blk, kb: (blk == thr) & (krow + kb * KB < cand)])
            return jnp.where(cnt <= need, cand, bnd)

        bound_ref[...] = lax.fori_loop(0, nbits, bisect_idx, jnp.zeros((1, TQ), jnp.int32))

    bound = bound_ref[...]

    m_ref[...] = jnp.full(m_ref.shape, NEG, F32)

    def store_logits(kb, bias_of_head):
        for r in range(0, KB, RB):
            blk = key_ref[rows(kb, r, RB), :]
            pos = lax.broadcasted_iota(jnp.int32, (RB, TQ), 0) + (kb * KB + r)
            keep = ((blk > thr) | ((blk == thr) & (pos < bound))) & (blk != INT_MIN)
            mask = jnp.where(keep, 0.0, NEG)
            for h in range(nheads):
                win, ko = qs[h]
                lg = lax.dot_general(kpad_ref[ko, rows(kb, r, RB), :], qop_ref[0, win], NT,
                                     preferred_element_type=F32)
                lg = lg + bias_of_head(h, r) + mask
                lg_ref[h, rows(kb, r, RB), :] = lg
                m_ref[h] = jnp.maximum(m_ref[h], fold_max(lg))

    def far_logits(kb, _):
        store_logits(kb, lambda h, r: far_ref[h])
        return 0

    lax.fori_loop(0, i - 1, far_logits, 0)

    @pl.when(i >= 1)
    def _():
        store_logits(i - 1, lambda h, r: band_ref[h, r:r + RB, :])

    store_logits(i, lambda h, r: band_ref[h, KB + r:KB + r + RB, :])

    m_fin = [jnp.max(m_ref[h], axis=0, keepdims=True) for h in range(nheads)]
    acc_ref[...] = jnp.zeros(acc_ref.shape, F32)
    l_ref[...] = jnp.zeros(l_ref.shape, F32)

    def accumulate(kb, _):
        vt = vt_ref[:, rows(kb)]
        for h in range(nheads):
            p = jnp.exp(lg_ref[h, rows(kb), :] - m_fin[h])
            l_ref[h] += fold_sum(p)
            acc_ref[h] += jnp.dot(vt, p.astype(BF16), preferred_element_type=F32)
        return 0

    lax.fori_loop(0, i + 1, accumulate, 0)

    outs = [acc_ref[h] / jnp.sum(l_ref[h], axis=0, keepdims=True) for h in range(nheads)]
    out = jnp.concatenate(outs, axis=0).T
    z = _lane_window(qx_ref, offs['z'], nheads * hd)
    o_ref[...] = (out * jax.nn.silu(z)).astype(BF16)


def _dsa_mixer(proj, B, L, src, band, far, nheads, TQ):
    width = nheads * ATT_HEAD_DIM
    nq = L // TQ
    hd = ATT_HEAD_DIM
    qx_w = 16 * LANE
    qx_blk = src['att_q'][0] // qx_w
    qx_lo = qx_blk * qx_w
    kv_w = 2 * LANE
    kv_blk = src['att_k'][0] // kv_w
    ki_blk = src['idx_k'][0] // LANE
    offs = {'q': src['att_q'][0] - qx_lo, 'qi': src['idx_q'][0] - qx_lo, 'w': src['idx_w'][0] - qx_lo,
            'z': src['att_z'][0] - qx_lo, 'k': src['att_k'][0] - kv_blk * kv_w,
            'v': src['att_v'][0] - kv_blk * kv_w, 'ki': src['idx_k'][0] - ki_blk * LANE}
    assert offs['z'] + width + LANE <= qx_w and qx_lo + qx_w <= proj.shape[1]
    assert offs['k'] % LANE + hd <= LANE and offs['ki'] + hd <= LANE
    assert offs['v'] % SUBLANE == 0 and offs['v'] + hd <= kv_w
    assert offs['w'] // LANE == (offs['w'] + nheads - 1) // LANE
    win_off = _head_offsets(offs, nheads, hd)
    win = LANE if max(win_off) + hd <= LANE else 2 * LANE
    win_base = max((sorted({(offs[f] + h * hd) // LANE for h in range(nheads)}) for f in ('q', 'qi')), key=len)
    body = functools.partial(_dsa_body, nheads=nheads, hd=hd, topk=min(TOPK, L // 4), offs=offs)
    full = lambda a: pl.BlockSpec(a.shape, lambda b, i: (0,) * a.ndim)
    return pl.pallas_call(
        body,
        grid=(B, nq),
        in_specs=[pl.BlockSpec((TQ, qx_w), lambda b, i: (b * nq + i, qx_blk)),
                  pl.BlockSpec((L, kv_w), lambda b, i: (b, kv_blk)),
                  pl.BlockSpec((L, LANE), lambda b, i: (b, ki_blk)),
                  full(band), full(far)],
        out_specs=pl.BlockSpec((TQ, width), lambda b, i: (b * nq + i, 0)),
        out_shape=jax.ShapeDtypeStruct((B * L, width), BF16),
        scratch_shapes=[pltpu.VMEM((len(win_off), L, win), BF16),
                        pltpu.VMEM((len(win_off), L, win), BF16),
                        pltpu.VMEM((ATT_HEAD_DIM, L), BF16),
                        pltpu.VMEM((2, len(win_base), TQ, win), BF16),
                        pltpu.VMEM((L, TQ), jnp.int32),
                        pltpu.VMEM((L, TQ), jnp.int16),
                        pltpu.VMEM((L, TQ), jnp.int16),
                        pltpu.VMEM((L, TQ), jnp.int16),
                        pltpu.VMEM((nheads, L, TQ), F32),
                        pltpu.VMEM((1, TQ), jnp.int32),
                        pltpu.VMEM((nheads, 4 * SUBLANE, TQ), F32),
                        pltpu.VMEM((nheads, 4 * SUBLANE, TQ), F32),
                        pltpu.VMEM((nheads, ATT_HEAD_DIM, TQ), F32)],
        compiler_params=_cparams(("parallel", "arbitrary")),
        name="dsa_mixer",
    )(proj, proj, proj, band, far)


def _t5_bucket_static(rel):
    nb = N_BUCKETS // 2
    max_exact = nb // 2
    ret = np.where(rel > 0, nb, 0)
    n = np.abs(rel)
    nf = np.maximum(n, 1).astype(np.float64)
    large = max_exact + (np.log(nf / max_exact) / math.log(MAX_DISTANCE / max_exact)
                         * (nb - max_exact)).astype(np.int32)
    large = np.minimum(large, nb - 1)
    return ret + np.where(n < max_exact, n, large)


def _dsa_bias_tables(rel_bias, TQ):
    a = np.arange(2 * TQ)[:, None]
    j = np.arange(TQ)[None, :]
    band_idx = _t5_bucket_static(a - TQ - j)
    idx = jnp.asarray(band_idx, jnp.int32)[None]
    band = jnp.zeros((rel_bias.shape[1],) + band_idx.shape, F32)
    for bucket in np.unique(band_idx):
        band = jnp.where(idx == int(bucket), rel_bias[int(bucket)][:, None, None], band)
    far_bucket = int(_t5_bucket_static(np.array([-(TQ + 1)]))[0])
    assert far_bucket == int(_t5_bucket_static(np.array([-(10 ** 6)]))[0])
    far = jnp.broadcast_to(rel_bias[far_bucket][:, None, None], (rel_bias.shape[1], 1, TQ))
    return band, far


def _layout(d_model):
    s5w = d_model // 4
    ssdw = d_model // 2
    attw = d_model // 4
    cdim = ssdw + 2 * SSD_GROUPS * SSD_STATE
    nh_ssd = ssdw // SSD_HEAD_DIM
    nh_att = attw // ATT_HEAD_DIM
    splits = (s5w, s5w, ssdw, cdim, nh_ssd, attw, ATT_HEAD_DIM, ATT_HEAD_DIM,
              nh_att * IDX_DIM, IDX_DIM, nh_att, attw)
    names = ('s5_u', 's5_z', 'ssd_z', 'ssd_xbc', 'ssd_dt', 'att_q', 'att_k', 'att_v',
             'idx_q', 'idx_k', 'idx_w', 'att_z')
    src, o = {}, 0
    for nme, s in zip(names, splits):
        src[nme] = (o, s)
        o += s
    return src, o


def kernel(x, norm_w, w_in, s5_A_re, s5_A_im, s5_log_dt, s5_B_re, s5_B_im, s5_C_re, s5_C_im, s5_D, s5_glu_w, s5_glu_b, ssd_conv_w, ssd_conv_b, ssd_dt_bias, ssd_A_log, ssd_D, ssd_norm_w, rel_bias, w_out, final_norm_w):
    B, L, d = x.shape
    depth = w_in.shape[0]
    src, total = _layout(d)
    s5w, ssdw, attw = d // 4, d // 2, d // 4
    cdim = ssdw + 2 * SSD_GROUPS * SSD_STATE
    nh_att = attw // ATT_HEAD_DIM
    assert src['s5_u'][0] % s5w == 0 and src['s5_z'][0] % s5w == 0 and src['ssd_z'][0] % ssdw == 0
    assert src['ssd_xbc'][0] % cdim == 0 and src['ssd_dt'][0] % LANE == 0

    tn = 1024
    w_in_b = w_in.astype(BF16)
    pieces, cut, shift, padded = [], 0, 0, {}
    for nme, (so, sz) in sorted(src.items(), key=lambda kv: kv[1][0]):
        gap = -(so + shift) % LANE if nme in ('att_q', 'att_k', 'idx_q', 'idx_k', 'att_z') else 0
        if gap:
            pieces += [w_in_b[:, :, cut:so], jnp.zeros((depth, d, gap), BF16)]
            cut = so
            shift += gap
        padded[nme] = (so + shift, sz)
    src = padded
    total += shift
    total_pad = -(-total // tn) * tn
    pieces += [w_in_b[:, :, cut:], jnp.zeros((depth, d, total_pad - total), BF16)]
    w_in_p = jnp.concatenate(pieces, axis=2)
    w_out_b = w_out.astype(BF16)

    TQ = 256
    Q = 128
    band, far = _dsa_bias_tables(rel_bias, TQ)
    tm_in = min(1024, B * L)
    tm_out = min(256, B * L)
    nchunk = 2
    seg = L // SUBLANE

    x2 = x.reshape(B * L, d)
    for l in range(depth):
        proj = _inproj(x2, norm_w[l].reshape(1, d), w_in_p, l, tm_in, tn)
        s5p = _s5_params(s5_A_re[l], s5_A_im[l], s5_log_dt[l], s5_B_re[l], s5_B_im[l], s5_C_re[l],
                         s5_C_im[l], s5_D[l], s5_glu_w[l], s5_glu_b[l], seg, nchunk)
        y_s5 = _s5_mixer(proj, B, L, src['s5_u'][0] // s5w, src['s5_z'][0] // s5w, s5w, s5p)
        ssdp = _ssd_params(ssd_conv_w[l], ssd_conv_b[l], ssd_dt_bias[l], ssd_A_log[l], ssd_D[l], ssd_norm_w[l])
        y_ssd = _ssd_mixer(proj, B, L, src['ssd_z'][0] // ssdw, src['ssd_xbc'][0] // cdim,
                           src['ssd_dt'][0] // LANE, ssdp, Q)
        y_att = _dsa_mixer(proj, B, L, src, band, far, nh_att, TQ)
        x2 = _outproj(y_s5, y_ssd, y_att, w_out_b, l, x2, final_norm_w.reshape(1, d), tm_out,
                      final=(l == depth - 1))
    return x2.reshape(B, L, d)
```

```python
import functools
import math

import numpy as np
import jax
import jax.numpy as jnp
from jax import lax
from jax.experimental import pallas as pl
from jax.experimental.pallas import tpu as pltpu

F32 = jnp.float32
BF16 = jnp.bfloat16

EPS = 1e-6
CHUNK = 64

S5_GROUP = 16
S5_STATE = 64
SSD_HEAD_DIM = 64
SSD_STATE = 128
SSD_GROUPS = 4
SSD_CONV = 4
ATT_HEAD_DIM = 64
IDX_DIM = 64
TOPK = 256
N_BUCKETS = 32
MAX_DISTANCE = 128

LANE = 128
SUBLANE = 8
VMEM_LIMIT = 56 * 1024 * 1024

INT_MIN = -2 ** 31
NEG = -1e30


def _cparams(sem):
    return pltpu.CompilerParams(dimension_semantics=sem, vmem_limit_bytes=VMEM_LIMIT)


def _inproj_body(x_ref, nw_ref, w_ref, o_ref, h_ref):
    @pl.when(pl.program_id(1) == 0)
    def _():
        xf = x_ref[...]
        ms = jnp.mean(xf * xf, axis=-1, keepdims=True)
        h_ref[...] = (xf * lax.rsqrt(ms + EPS) * nw_ref[...]).astype(BF16)

    o_ref[...] = jnp.dot(h_ref[...], w_ref[...], preferred_element_type=F32)


def _inproj(x2, nw, w, layer, tm, tn):
    m, d = x2.shape
    n = w.shape[2]
    return pl.pallas_call(
        _inproj_body,
        grid=(m // tm, n // tn),
        in_specs=[pl.BlockSpec((tm, d), lambda i, j: (i, 0)),
                  pl.BlockSpec((None, 1, d), lambda i, j: (layer, 0, 0)),
                  pl.BlockSpec((None, d, tn), lambda i, j: (layer, 0, j))],
        out_specs=pl.BlockSpec((tm, tn), lambda i, j: (i, j)),
        out_shape=jax.ShapeDtypeStruct((m, n), F32),
        scratch_shapes=[pltpu.VMEM((tm, d), BF16)],
        compiler_params=_cparams(("parallel", "arbitrary")),
        name="inproj",
    )(x2, nw, w)


def _outproj_body(ys5_ref, yssd_ref, yatt_ref, w_ref, x_ref, fnw_ref, o_ref, *, w5, wssd, final):
    acc = x_ref[...]
    acc += jnp.dot(ys5_ref[...], w_ref[0:w5, :], preferred_element_type=F32)
    acc += jnp.dot(yssd_ref[...], w_ref[w5:w5 + wssd, :], preferred_element_type=F32)
    acc += jnp.dot(yatt_ref[...], w_ref[w5 + wssd:, :], preferred_element_type=F32)
    if final:
        ms = jnp.mean(acc * acc, axis=-1, keepdims=True)
        acc = acc * lax.rsqrt(ms + EPS) * fnw_ref[...]
    o_ref[...] = acc


def _outproj(ys5, yssd, yatt, w, layer, x2, fnw, tm, final):
    m, d = x2.shape
    w5, wssd, watt = ys5.shape[1], yssd.shape[1], yatt.shape[1]
    body = functools.partial(_outproj_body, w5=w5, wssd=wssd, final=final)
    return pl.pallas_call(
        body,
        grid=(m // tm,),
        in_specs=[pl.BlockSpec((tm, w5), lambda i: (i, 0)),
                  pl.BlockSpec((tm, wssd), lambda i: (i, 0)),
                  pl.BlockSpec((tm, watt), lambda i: (i, 0)),
                  pl.BlockSpec((None,) + w.shape[1:], lambda i: (layer, 0, 0)),
                  pl.BlockSpec((tm, d), lambda i: (i, 0)),
                  pl.BlockSpec((1, d), lambda i: (0, 0))],
        out_specs=pl.BlockSpec((tm, d), lambda i: (i, 0)),
        out_shape=jax.ShapeDtypeStruct((m, d), F32),
        compiler_params=_cparams(("parallel",)),
        name="outproj",
    )(ys5, yssd, yatt, w, x2, fnw)


def _shift_down_one(x):
    rolled = pltpu.roll(x, 1, axis=0)
    row = lax.broadcasted_iota(jnp.int32, x.shape, 0)
    return jnp.where(row == 0, 0.0, rolled)


def _s5_body(u_ref, z_ref, bblk_ref, cblk_ref, lam_ref, lamseg_ref, d_ref, gw_ref, gb_ref,
             o_ref, uperm_ref, xs_ref, yperm_ref, *, nchunk, cw, sw, rt):
    L = u_ref.shape[0]
    seg = L // SUBLANE

    for c in range(nchunk):
        for j in range(SUBLANE):
            for k in range(cw // LANE):
                lo = c * cw + k * LANE
                uperm_ref[k, pl.ds(j, seg, stride=SUBLANE), :] = u_ref[pl.ds(j * seg, seg), lo:lo + LANE]
        for r0 in range(0, L, rt):
            up = jnp.concatenate([uperm_ref[k, r0:r0 + rt, :] for k in range(cw // LANE)], axis=1)
            xs_ref[r0:r0 + rt, :] = jnp.dot(up.astype(BF16), bblk_ref[c], preferred_element_type=F32)
        lr = lam_ref[c, 0]
        li = lam_ref[c, 1]

        def scan_step(tau, carry):
            xr, xi = carry
            row = pl.multiple_of(tau * SUBLANE, SUBLANE)
            nxr = lr * xr - li * xi + xs_ref[pl.ds(row, SUBLANE), 0:sw]
            nxi = lr * xi + li * xr + xs_ref[pl.ds(row, SUBLANE), sw:2 * sw]
            xs_ref[pl.ds(row, SUBLANE), 0:sw] = nxr
            xs_ref[pl.ds(row, SUBLANE), sw:2 * sw] = nxi
            return nxr, nxi

        zero = jnp.zeros((SUBLANE, sw), F32)
        er, ei = lax.fori_loop(0, seg, scan_step, (zero, zero))

        sr = lamseg_ref[c, 0]
        si = lamseg_ref[c, 1]
        cr, ci = zero, zero
        for _ in range(SUBLANE - 1):
            tr = er + (sr * cr - si * ci)
            ti = ei + (sr * ci + si * cr)
            cr, ci = _shift_down_one(tr), _shift_down_one(ti)

        def fix_step(tau, carry):
            fr, fi = carry
            nfr = lr * fr - li * fi
            nfi = lr * fi + li * fr
            row = pl.multiple_of(tau * SUBLANE, SUBLANE)
            xs_ref[pl.ds(row, SUBLANE), 0:sw] += nfr
            xs_ref[pl.ds(row, SUBLANE), sw:2 * sw] += nfi
            return nfr, nfi

        lax.fori_loop(0, seg, fix_step, (cr, ci))

        for r0 in range(0, L, rt):
            yc = jnp.dot(xs_ref[r0:r0 + rt, :].astype(BF16), cblk_ref[c], preferred_element_type=F32)
            for k in range(cw // LANE):
                yperm_ref[c * (cw // LANE) + k, r0:r0 + rt, :] = yc[:, k * LANE:(k + 1) * LANE]

    for j in range(SUBLANE):
        y = jnp.concatenate([yperm_ref[k, pl.ds(j, seg, stride=SUBLANE), :]
                             for k in range(yperm_ref.shape[0])], axis=1)
        y = y + d_ref[...] * u_ref[j * seg:(j + 1) * seg, :]
        y = jax.nn.gelu(y)
        g = jnp.dot(y.astype(BF16), gw_ref[...], preferred_element_type=F32) + gb_ref[...]
        y = y * jax.nn.sigmoid(g)
        o_ref[j * seg:(j + 1) * seg, :] = (y * jax.nn.silu(z_ref[j * seg:(j + 1) * seg, :])).astype(BF16)


def _layer_spec(a, layer):
    return pl.BlockSpec((None,) + a.shape[1:], lambda *_: (layer,) + (0,) * (a.ndim - 1))


def _s5_mixer(proj, B, L, u_blk, z_blk, width, prm, layer):
    bblk, cblk, lam, lamseg, dvec, gw, gb = prm
    nchunk, cw, sw2 = bblk.shape[1:]
    sw = sw2 // 2
    rt = min(512, L)
    body = functools.partial(_s5_body, nchunk=nchunk, cw=cw, sw=sw, rt=rt)
    full = lambda a: _layer_spec(a, layer)
    return pl.pallas_call(
        body,
        grid=(B,),
        in_specs=[pl.BlockSpec((L, width), lambda b: (b, u_blk)),
                  pl.BlockSpec((L, width), lambda b: (b, z_blk)),
                  full(bblk), full(cblk), full(lam), full(lamseg), full(dvec), full(gw), full(gb)],
        out_specs=pl.BlockSpec((L, width), lambda b: (b, 0)),
        out_shape=jax.ShapeDtypeStruct((B * L, width), BF16),
        scratch_shapes=[pltpu.VMEM((cw // LANE, L, LANE), F32),
                        pltpu.VMEM((L, 2 * sw), F32),
                        pltpu.VMEM((width // LANE, L, LANE), F32)],
        compiler_params=_cparams(("parallel",)),
        name="s5_mixer",
    )(proj, proj, bblk, cblk, lam, lamseg, dvec, gw, gb)


def _cpow(re, im, n):
    rr, ri = jnp.ones_like(re), jnp.zeros_like(im)
    br, bi = re, im
    while n:
        if n & 1:
            rr, ri = rr * br - ri * bi, rr * bi + ri * br
        br, bi = br * br - bi * bi, 2.0 * br * bi
        n >>= 1
    return rr, ri


def _s5_params(A_re, A_im, log_dt, B_re, B_im, C_re, C_im, D, glu_w, glu_b, seg, nchunk):
    G, P, C = B_re.shape
    dt = jnp.exp(log_dt)[:, None]
    lre = jnp.minimum(A_re, -1e-4)
    lim = A_im
    mag = jnp.exp(lre * dt)
    lbr = mag * jnp.cos(lim * dt)
    lbi = mag * jnp.sin(lim * dt)
    nr, ni = lbr - 1.0, lbi
    den = lre * lre + lim * lim
    fr = (nr * lre + ni * lim) / den
    fi = (ni * lre - nr * lim) / den
    bbr = fr[..., None] * B_re - fi[..., None] * B_im
    bbi = fr[..., None] * B_im + fi[..., None] * B_re
    gc = G // nchunk
    eye = jnp.eye(gc, dtype=F32)

    def blockdiag_in(bb):
        t = jnp.transpose(bb, (0, 2, 1)).reshape(nchunk, gc, C, P)
        return jnp.einsum('ngcp,gh->ngchp', t, eye).reshape(nchunk, gc * C, gc * P)

    def blockdiag_out(cc):
        t = jnp.transpose(cc, (0, 2, 1)).reshape(nchunk, gc, P, C)
        return jnp.einsum('ngpc,gh->ngphc', t, eye).reshape(nchunk, gc * P, gc * C)

    bblk = jnp.concatenate([blockdiag_in(bbr), blockdiag_in(bbi)], axis=-1).astype(BF16)
    cblk = jnp.concatenate([blockdiag_out(C_re), blockdiag_out(-C_im)], axis=1).astype(BF16)

    def rows(v):
        return jnp.broadcast_to(v.reshape(nchunk, 1, gc * P), (nchunk, SUBLANE, gc * P))

    lam = jnp.stack([rows(lbr), rows(lbi)], axis=1)
    pr, pi = _cpow(lbr, lbi, seg)
    lamseg = jnp.stack([rows(pr), rows(pi)], axis=1)
    return (bblk, cblk, lam, lamseg, D.reshape(1, G * C), glu_w.astype(BF16), glu_b.reshape(1, -1))


def _split3(x):
    h = x.astype(BF16)
    r = x - h.astype(F32)
    m = r.astype(BF16)
    l = (r - m.astype(F32)).astype(BF16)
    return h, m, l


def _dot_f32(a, b_bf16):
    h, m, l = _split3(a)
    dot = lambda p: jnp.dot(p, b_bf16, preferred_element_type=F32)
    return dot(h) + dot(m) + dot(l)


def _ssd_body(z_ref, xbc_ref, dt_ref, cw_ref, cb_ref, dtb_ref, a_ref, dx_ref, nw_ref, e_ref,
              o_ref, state_ref, xpad_ref, *, nheads, hd, ns, ngroups):
    Q = z_ref.shape[0]
    width = nheads * hd
    gw = width // ngroups
    halo = SUBLANE

    @pl.when(pl.program_id(1) == 0)
    def _():
        state_ref[...] = jnp.zeros_like(state_ref)
        xpad_ref[0:halo, :] = jnp.zeros((halo, xpad_ref.shape[1]), F32)

    xpad_ref[halo:halo + Q, :] = xbc_ref[...]
    acc = cb_ref[...] + cw_ref[0:1, :] * xpad_ref[halo - 3:halo - 3 + Q, :]
    for k in range(1, SSD_CONV):
        acc += cw_ref[k:k + 1, :] * xpad_ref[halo - 3 + k:halo - 3 + k + Q, :]
    xpad_ref[0:halo, :] = xpad_ref[Q:Q + halo, :]
    xc = jax.nn.silu(acc)
    xs = xc[:, 0:width]
    bm = xc[:, width:width + ngroups * ns].astype(BF16)
    cm = xc[:, width + ngroups * ns:].astype(BF16)

    lane = lax.broadcasted_iota(jnp.int32, (Q, LANE), 1)
    dt = jnp.where(lane < nheads, jax.nn.softplus(dt_ref[...] + dtb_ref[...]), 0.0)
    a = dt * a_ref[...]
    rowi = lax.broadcasted_iota(jnp.int32, (Q, Q), 0)
    coli = lax.broadcasted_iota(jnp.int32, (Q, Q), 1)
    tril = coli <= rowi
    acs = _dot_f32_lhs(tril.astype(BF16), a)
    acs_t = acs.T
    last = acs[Q - 1:Q, :]
    emat = e_ref[...]
    dt_x = _expand(dt, emat)
    dec_x = _expand(jnp.exp(acs), emat)
    dte_x = _expand(jnp.exp(last - acs), emat)
    dlast_x = dec_x[Q - 1:Q, :]

    xd = xs * dt_x
    xd_b = xd.astype(BF16)
    xw_b = (xd * dte_x).astype(BF16)
    st = state_ref[...]
    st_b = st.astype(BF16)

    y_parts = []
    new_state = []
    hlane = lax.broadcasted_iota(jnp.int32, (Q, LANE), 1) < hd
    heads_per_group = nheads // ngroups
    for g in range(ngroups):
        cg = cm[:, g * ns:(g + 1) * ns]
        bg = bm[:, g * ns:(g + 1) * ns]
        cb = lax.dot_general(cg, bg, (((1,), (1,)), ((), ())), preferred_element_type=F32)
        y_off = jnp.dot(cg, st_b[:, g * gw:(g + 1) * gw], preferred_element_type=F32)
        pair_out = []
        for pr in range(gw // LANE):
            xpair = xd_b[:, g * gw + pr * LANE: g * gw + (pr + 1) * LANE]
            res = []
            for hh in range(LANE // hd):
                h = g * heads_per_group + pr * (LANE // hd) + hh
                diff = acs[:, h:h + 1] - acs_t[h:h + 1, :]
                s_h = jnp.where(tril, cb * jnp.exp(jnp.minimum(diff, 0.0)), 0.0).astype(BF16)
                res.append(jnp.dot(s_h, xpair, preferred_element_type=F32))
            pair_out.append(jnp.where(hlane, res[0], res[1]))
        y_diag = jnp.concatenate(pair_out, axis=1)
        y_parts.append(y_diag + y_off * dec_x[:, g * gw:(g + 1) * gw])
        upd = lax.dot_general(bg, xw_b[:, g * gw:(g + 1) * gw], (((0,), (0,)), ((), ())),
                              preferred_element_type=F32)
        new_state.append(st[:, g * gw:(g + 1) * gw] * dlast_x[:, g * gw:(g + 1) * gw] + upd)
    state_ref[...] = jnp.concatenate(new_state, axis=1)

    y = jnp.concatenate(y_parts, axis=1) + xs * dx_ref[...]
    gt = y * jax.nn.silu(z_ref[...])
    ms = jnp.mean(gt * gt, axis=-1, keepdims=True)
    o_ref[...] = (gt * lax.rsqrt(ms + EPS) * nw_ref[...]).astype(BF16)


def _dot_f32_lhs(a_bf16, b):
    h, m, l = _split3(b)
    dot = lambda p: jnp.dot(a_bf16, p, preferred_element_type=F32)
    return dot(h) + dot(m) + dot(l)


def _expand(v, emat):
    return _dot_f32(v, emat)


def _ssd_mixer(proj, B, L, z_blk, xbc_blk, dt_blk, prm, emat, layer, Q):
    cw, cb, dtb, avec, dx, nw = prm
    width = nw.shape[-1]
    cdim = cw.shape[-1]
    nheads = width // SSD_HEAD_DIM
    nt = L // Q
    body = functools.partial(_ssd_body, nheads=nheads, hd=SSD_HEAD_DIM, ns=SSD_STATE, ngroups=SSD_GROUPS)
    full = lambda a: _layer_spec(a, layer)
    return pl.pallas_call(
        body,
        grid=(B, nt),
        in_specs=[pl.BlockSpec((Q, width), lambda b, t: (b * nt + t, z_blk)),
                  pl.BlockSpec((Q, cdim), lambda b, t: (b * nt + t, xbc_blk)),
                  pl.BlockSpec((Q, LANE), lambda b, t: (b * nt + t, dt_blk)),
                  full(cw), full(cb), full(dtb), full(avec), full(dx), full(nw),
                  pl.BlockSpec(emat.shape, lambda b, t: (0, 0))],
        out_specs=pl.BlockSpec((Q, width), lambda b, t: (b * nt + t, 0)),
        out_shape=jax.ShapeDtypeStruct((B * L, width), BF16),
        scratch_shapes=[pltpu.VMEM((SSD_STATE, width), F32),
                        pltpu.VMEM((Q + SUBLANE, cdim), F32)],
        compiler_params=_cparams(("parallel", "arbitrary")),
        name="ssd_mixer",
    )(proj, proj, proj, cw, cb, dtb, avec, dx, nw, emat)


def _ssd_params(conv_w, conv_b, dt_bias, A_log, D, norm_w):
    depth, nheads = dt_bias.shape
    width = nheads * SSD_HEAD_DIM
    pad = lambda v: jnp.pad(v, ((0, 0), (0, LANE - nheads))).reshape(depth, 1, LANE)
    return (conv_w, conv_b.reshape(depth, 1, -1), pad(dt_bias), pad(-jnp.exp(A_log)),
            jnp.repeat(D, SSD_HEAD_DIM, axis=1).reshape(depth, 1, width), norm_w.reshape(depth, 1, width))


def _head_expansion(nheads):
    e = np.zeros((LANE, nheads * SSD_HEAD_DIM), np.float32)
    e[np.repeat(np.arange(nheads), SSD_HEAD_DIM), np.arange(nheads * SSD_HEAD_DIM)] = 1.0
    return jnp.asarray(e, BF16)


def _float_key(x):
    b = lax.bitcast_convert_type(x, jnp.int32)
    return b ^ ((b >> 31) & 0x7FFFFFFF)


def _place(blk, lane0, width, dst):
    lane = lax.broadcasted_iota(jnp.int32, blk.shape, 1)
    shift = (dst - lane0) % LANE
    r = pltpu.roll(blk, shift, axis=1) if shift else blk
    lo = jnp.where((lane >= dst) & (lane < min(dst + width, LANE)), r, 0.0)
    hi = jnp.where(lane < dst + width - LANE, r, 0.0)
    return jnp.concatenate([lo, hi], axis=1)


def _head_offsets(offs, nheads, hd):
    return sorted({(offs[f] + h * hd) % LANE for f in ('q', 'qi') for h in range(nheads)})


def _lane_window(x, start, width):
    off = start % LANE
    base = start - off
    if off == 0:
        return x[:, start:start + width]
    lane = lax.broadcasted_iota(jnp.int32, (x.shape[0], LANE), 1)
    out = []
    for p in range(width // LANE):
        a = pltpu.roll(x[:, base + p * LANE:base + (p + 1) * LANE], LANE - off, axis=1)
        b = pltpu.roll(x[:, base + (p + 1) * LANE:base + (p + 2) * LANE], LANE - off, axis=1)
        out.append(jnp.where(lane < LANE - off, a, b))
    return jnp.concatenate(out, axis=1)


def _dsa_body(qx_ref, kv_ref, ki_ref, band_ref, far_ref, o_ref,
              kpad_ref, kipad_ref, vt_ref, qop_ref, key_ref, hi_ref, lo_ref, lom_ref, lg_ref, bound_ref,
              m_ref, l_ref, acc_ref,
              *, nheads, hd, topk, offs):
    i = pl.program_id(1)
    TQ = qx_ref.shape[0]
    KB = TQ
    L = kv_ref.shape[0]
    nkb_total = L // KB
    NT = (((1,), (1,)), ((), ()))
    WIN = kpad_ref.shape[2]

    def head_window(field_lo, h):
        lo = field_lo + h * hd
        return (lo // LANE) * LANE, lo % LANE

    head_offs = _head_offsets(offs, nheads, hd)

    @pl.when(i == 0)
    def _():
        for kb in range(nkb_total):
            r = slice(kb * KB, (kb + 1) * KB)
            kgrp = (offs['k'] // LANE) * LANE
            kblk = kv_ref[r, kgrp:kgrp + LANE]
            kiblk = ki_ref[r, :]
            for n, o in enumerate(head_offs):
                kpad_ref[n, r, :] = _place(kblk, offs['k'] - kgrp, hd, o)[:, :WIN].astype(BF16)
                kipad_ref[n, r, :] = _place(kiblk, offs['ki'], hd, o)[:, :WIN].astype(BF16)
            vt_ref[:, r] = kv_ref[r, :].T[offs['v']:offs['v'] + hd, :].astype(BF16)

    def stage_windows(slot, field_lo, scale):
        bases = sorted({head_window(field_lo, h)[0] for h in range(nheads)})
        for n, base in enumerate(bases):
            qop_ref[slot, n] = (qx_ref[:, base:base + WIN] * scale).astype(BF16)
        return [(bases.index(head_window(field_lo, h)[0]), head_offs.index(head_window(field_lo, h)[1]))
                for h in range(nheads)]

    qs = stage_windows(0, offs['q'], hd ** -0.5)
    qis = stage_windows(1, offs['qi'], 1.0)
    wbase = (offs['w'] // LANE) * LANE
    wt = qx_ref[:, wbase:wbase + LANE].T * ((nheads * IDX_DIM) ** -0.5)
    w_rows = [wt[offs['w'] - wbase + h:offs['w'] - wbase + h + 1, :] for h in range(nheads)]

    RB = LANE
    I16 = jnp.int16
    HALF = 1 << 15

    def rows(kb, r=0, n=KB):
        return pl.ds(pl.multiple_of(kb * KB + r, SUBLANE), n)

    PART = m_ref.shape[1]

    def fold_sum(x):
        return jnp.sum(x.reshape(x.shape[0] // PART, PART, TQ), axis=0)

    def fold_max(x):
        return jnp.max(x.reshape(x.shape[0] // PART, PART, TQ), axis=0)

    def idx_keys(kb, diagonal):
        for r in range(0, KB, RB):
            s = jnp.zeros((RB, TQ), F32)
            for h in range(nheads):
                win, ko = qis[h]
                lg = lax.dot_general(kipad_ref[ko, rows(kb, r, RB), :], qop_ref[1, win], NT,
                                     preferred_element_type=F32)
                s += jnp.maximum(lg, 0.0) * w_rows[h]
            key = _float_key(s)
            if diagonal:
                krow = lax.broadcasted_iota(jnp.int32, (RB, TQ), 0) + r
                qcol = lax.broadcasted_iota(jnp.int32, (RB, TQ), 1)
                key = jnp.where((krow // CHUNK) <= (qcol // CHUNK), key, INT_MIN)
            key_ref[rows(kb, r, RB), :] = key
            hi_ref[rows(kb, r, RB), :] = (key >> 16).astype(I16)
            lo_ref[rows(kb, r, RB), :] = ((key & (2 * HALF - 1)) - HALF).astype(I16)

    def far_keys(kb, _):
        idx_keys(kb, False)
        return 0

    lax.fori_loop(0, i, far_keys, 0)
    idx_keys(i, True)

    def count(preds):
        def body(kb, cs):
            blk = key_ref[rows(kb), :]
            return tuple(c + fold_sum(jnp.where(p(blk, kb), 1.0, 0.0)) for c, p in zip(cs, preds))
        z = jnp.zeros((PART, TQ), F32)
        cs = lax.fori_loop(0, i + 1, body, (z,) * len(preds))
        return [jnp.sum(c, axis=0, keepdims=True) for c in cs]

    PACK = 2 * SUBLANE
    P16 = 4 * PACK

    def rows16(x):
        return x.reshape(x.shape[0] // PACK, PACK, TQ)

    def as16(v):
        return jnp.broadcast_to(v, (PACK, TQ)).astype(I16)

    def count16(src_ref, pred):
        def body(kb, c):
            ind = jnp.where(pred(rows16(src_ref[rows(kb), :])), I16(1), I16(0)).reshape(KB // P16, P16, TQ)
            for n in range(KB // P16):
                c = c + ind[n]
            return c
        c = lax.fori_loop(0, i + 1, body, jnp.zeros((P16, TQ), I16))
        return jnp.sum(c.astype(F32), axis=0, keepdims=True)

    def bisect16(src_ref, target):
        def step(s, tu):
            cand_u = tu | jnp.left_shift(jnp.int32(1), 15 - s)
            cand = as16(cand_u - HALF)[None]
            cnt = count16(src_ref, lambda blk: blk >= cand)
            return jnp.where(cnt >= target, cand_u, tu)
        return lax.fori_loop(0, 16, step, jnp.zeros((1, TQ), jnp.int32))

    thr_hi = bisect16(hi_ref, float(topk)) - HALF
    thr_hi16 = as16(thr_hi)[None]
    need_lo = topk - count16(hi_ref, lambda blk: blk > thr_hi16)

    def mask_low_halves(kb, _):
        keep_lo = rows16(hi_ref[rows(kb), :]) == thr_hi16
        lom_ref[rows(kb), :] = jnp.where(keep_lo, rows16(lo_ref[rows(kb), :]), I16(-HALF)).reshape(KB, TQ)
        return 0

    lax.fori_loop(0, i + 1, mask_low_halves, 0)
    thr = thr_hi * (2 * HALF) + bisect16(lom_ref, need_lo)

    cgt, ceq = count([lambda blk, kb: blk > thr, lambda blk, kb: blk == thr])
    need = topk - cgt
    nbits = int(L).bit_length()
    bound_ref[...] = jnp.full((1, TQ), 2 ** nbits - 1, jnp.int32)

    @pl.when(jnp.max(ceq - need) > 0.0)
    def _():
        krow = lax.broadcasted_iota(jnp.int32, (KB, TQ), 0)

        def bisect_idx(step, bnd):
            bit = jnp.left_shift(jnp.int32(1), nbits - 1 - step)
            cand = bnd | bit
            cnt, = count([lambda blk, kb: (blk == thr) & (krow + kb * KB < cand)])
            return jnp.where(cnt <= need, cand, bnd)

        bound_ref[...] = lax.fori_loop(0, nbits, bisect_idx, jnp.zeros((1, TQ), jnp.int32))

    bound = bound_ref[...]

    m_ref[...] = jnp.full(m_ref.shape, NEG, F32)

    def store_logits(kb, bias_of_head):
        for r in range(0, KB, RB):
            blk = key_ref[rows(kb, r, RB), :]
            pos = lax.broadcasted_iota(jnp.int32, (RB, TQ), 0) + (kb * KB + r)
            keep = ((blk > thr) | ((blk == thr) & (pos < bound))) & (blk != INT_MIN)
            mask = jnp.where(keep, 0.0, NEG)
            for h in range(nheads):
                win, ko = qs[h]
                lg = lax.dot_general(kpad_ref[ko, rows(kb, r, RB), :], qop_ref[0, win], NT,
                                     preferred_element_type=F32)
                lg = lg + bias_of_head(h, r) + mask
                lg_ref[h, rows(kb, r, RB), :] = lg
                m_ref[h] = jnp.maximum(m_ref[h], fold_max(lg))

    def far_logits(kb, _):
        store_logits(kb, lambda h, r: far_ref[h])
        return 0

    lax.fori_loop(0, i - 1, far_logits, 0)

    @pl.when(i >= 1)
    def _():
        store_logits(i - 1, lambda h, r: band_ref[h, r:r + RB, :])

    store_logits(i, lambda h, r: band_ref[h, KB + r:KB + r + RB, :])

    m_fin = [jnp.max(m_ref[h], axis=0, keepdims=True) for h in range(nheads)]
    acc_ref[...] = jnp.zeros(acc_ref.shape, F32)
    l_ref[...] = jnp.zeros(l_ref.shape, F32)

    def accumulate(kb, _):
        vt = vt_ref[:, rows(kb)]
        for h in range(nheads):
            p = jnp.exp(lg_ref[h, rows(kb), :] - m_fin[h])
            l_ref[h] += fold_sum(p)
            acc_ref[h] += jnp.dot(vt, p.astype(BF16), preferred_element_type=F32)
        return 0

    lax.fori_loop(0, i + 1, accumulate, 0)

    outs = [acc_ref[h] / jnp.sum(l_ref[h], axis=0, keepdims=True) for h in range(nheads)]
    out = jnp.concatenate(outs, axis=0).T
    z = _lane_window(qx_ref, offs['z'], nheads * hd)
    o_ref[...] = (out * jax.nn.silu(z)).astype(BF16)


def _dsa_mixer(proj, B, L, src, band, far, nheads, TQ):
    width = nheads * ATT_HEAD_DIM
    nq = L // TQ
    hd = ATT_HEAD_DIM
    qx_w = 16 * LANE
    qx_blk = src['att_q'][0] // qx_w
    qx_lo = qx_blk * qx_w
    kv_w = 2 * LANE
    kv_blk = src['att_k'][0] // kv_w
    ki_blk = src['idx_k'][0] // LANE
    offs = {'q': src['att_q'][0] - qx_lo, 'qi': src['idx_q'][0] - qx_lo, 'w': src['idx_w'][0] - qx_lo,
            'z': src['att_z'][0] - qx_lo, 'k': src['att_k'][0] - kv_blk * kv_w,
            'v': src['att_v'][0] - kv_blk * kv_w, 'ki': src['idx_k'][0] - ki_blk * LANE}
    assert offs['z'] + width + LANE <= qx_w and qx_lo + qx_w <= proj.shape[1]
    assert offs['k'] % LANE + hd <= LANE and offs['ki'] + hd <= LANE
    assert offs['v'] % SUBLANE == 0 and offs['v'] + hd <= kv_w
    assert offs['w'] // LANE == (offs['w'] + nheads - 1) // LANE
    win_off = _head_offsets(offs, nheads, hd)
    win = LANE if max(win_off) + hd <= LANE else 2 * LANE
    win_base = max((sorted({(offs[f] + h * hd) // LANE for h in range(nheads)}) for f in ('q', 'qi')), key=len)
    body = functools.partial(_dsa_body, nheads=nheads, hd=hd, topk=min(TOPK, L // 4), offs=offs)
    full = lambda a: pl.BlockSpec(a.shape, lambda b, i: (0,) * a.ndim)
    return pl.pallas_call(
        body,
        grid=(B, nq),
        in_specs=[pl.BlockSpec((TQ, qx_w), lambda b, i: (b * nq + i, qx_blk)),
                  pl.BlockSpec((L, kv_w), lambda b, i: (b, kv_blk)),
                  pl.BlockSpec((L, LANE), lambda b, i: (b, ki_blk)),
                  full(band), full(far)],
        out_specs=pl.BlockSpec((TQ, width), lambda b, i: (b * nq + i, 0)),
        out_shape=jax.ShapeDtypeStruct((B * L, width), BF16),
        scratch_shapes=[pltpu.VMEM((len(win_off), L, win), BF16),
                        pltpu.VMEM((len(win_off), L, win), BF16),
                        pltpu.VMEM((ATT_HEAD_DIM, L), BF16),
                        pltpu.VMEM((2, len(win_base), TQ, win), BF16),
                        pltpu.VMEM((L, TQ), jnp.int32),
                        pltpu.VMEM((L, TQ), jnp.int16),
                        pltpu.VMEM((L, TQ), jnp.int16),
                        pltpu.VMEM((L, TQ), jnp.int16),
                        pltpu.VMEM((nheads, L, TQ), F32),
                        pltpu.VMEM((1, TQ), jnp.int32),
                        pltpu.VMEM((nheads, 4 * SUBLANE, TQ), F32),
                        pltpu.VMEM((nheads, 4 * SUBLANE, TQ), F32),
                        pltpu.VMEM((nheads, ATT_HEAD_DIM, TQ), F32)],
        compiler_params=_cparams(("parallel", "arbitrary")),
        name="dsa_mixer",
    )(proj, proj, proj, band, far)


def _t5_bucket_static(rel):
    nb = N_BUCKETS // 2
    max_exact = nb // 2
    ret = np.where(rel > 0, nb, 0)
    n = np.abs(rel)
    nf = np.maximum(n, 1).astype(np.float64)
    large = max_exact + (np.log(nf / max_exact) / math.log(MAX_DISTANCE / max_exact)
                         * (nb - max_exact)).astype(np.int32)
    large = np.minimum(large, nb - 1)
    return ret + np.where(n < max_exact, n, large)


def _dsa_bias_tables(rel_bias, TQ):
    a = np.arange(2 * TQ)[:, None]
    j = np.arange(TQ)[None, :]
    band_idx = _t5_bucket_static(a - TQ - j)
    idx = jnp.asarray(band_idx, jnp.int32)[None]
    band = jnp.zeros((rel_bias.shape[1],) + band_idx.shape, F32)
    for bucket in np.unique(band_idx):
        band = jnp.where(idx == int(bucket), rel_bias[int(bucket)][:, None, None], band)
    far_bucket = int(_t5_bucket_static(np.array([-(TQ + 1)]))[0])
    assert far_bucket == int(_t5_bucket_static(np.array([-(10 ** 6)]))[0])
    far = jnp.broadcast_to(rel_bias[far_bucket][:, None, None], (rel_bias.shape[1], 1, TQ))
    return band, far


def _layout(d_model):
    s5w = d_model // 4
    ssdw = d_model // 2
    attw = d_model // 4
    cdim = ssdw + 2 * SSD_GROUPS * SSD_STATE
    nh_ssd = ssdw // SSD_HEAD_DIM
    nh_att = attw // ATT_HEAD_DIM
    splits = (s5w, s5w, ssdw, cdim, nh_ssd, attw, ATT_HEAD_DIM, ATT_HEAD_DIM,
              nh_att * IDX_DIM, IDX_DIM, nh_att, attw)
    names = ('s5_u', 's5_z', 'ssd_z', 'ssd_xbc', 'ssd_dt', 'att_q', 'att_k', 'att_v',
             'idx_q', 'idx_k', 'idx_w', 'att_z')
    src, o = {}, 0
    for nme, s in zip(names, splits):
        src[nme] = (o, s)
        o += s
    return src, o


def kernel(x, norm_w, w_in, s5_A_re, s5_A_im, s5_log_dt, s5_B_re, s5_B_im, s5_C_re, s5_C_im, s5_D, s5_glu_w, s5_glu_b, ssd_conv_w, ssd_conv_b, ssd_dt_bias, ssd_A_log, ssd_D, ssd_norm_w, rel_bias, w_out, final_norm_w):
    B, L, d = x.shape
    depth = w_in.shape[0]
    src, total = _layout(d)
    s5w, ssdw, attw = d // 4, d // 2, d // 4
    cdim = ssdw + 2 * SSD_GROUPS * SSD_STATE
    nh_att = attw // ATT_HEAD_DIM
    assert src['s5_u'][0] % s5w == 0 and src['s5_z'][0] % s5w == 0 and src['ssd_z'][0] % ssdw == 0
    assert src['ssd_xbc'][0] % cdim == 0 and src['ssd_dt'][0] % LANE == 0

    tn = 1024
    cuts, shift, padded = [(0, 0)], 0, {}
    for nme, (so, sz) in sorted(src.items(), key=lambda kv: kv[1][0]):
        gap = -(so + shift) % LANE if nme in ('att_q', 'att_k', 'idx_q', 'idx_k', 'att_z') else 0
        if gap:
            shift += gap
            cuts.append((so, shift))
        padded[nme] = (so + shift, sz)
    src = padded
    total_pad = -(-(total + shift) // tn) * tn
    w_in_p = None
    for n, (lo, sh) in enumerate(cuts):
        hi = cuts[n + 1][0] if n + 1 < len(cuts) else total
        piece = jnp.pad(w_in[:, :, lo:hi], ((0, 0), (0, 0), (lo + sh, total_pad - hi - sh)))
        w_in_p = piece if w_in_p is None else w_in_p + piece
    w_in_p = w_in_p.astype(BF16)
    w_out_b = w_out.astype(BF16)

    TQ = 256
    Q = 128
    band, far = _dsa_bias_tables(rel_bias, TQ)
    tm_in = min(1024, B * L)
    tm_out = min(256, B * L)
    nchunk = 2
    seg = L // SUBLANE

    s5p = jax.vmap(lambda *a: _s5_params(*a, seg, nchunk))(
        s5_A_re, s5_A_im, s5_log_dt, s5_B_re, s5_B_im, s5_C_re, s5_C_im, s5_D, s5_glu_w, s5_glu_b)
    ssdp = _ssd_params(ssd_conv_w, ssd_conv_b, ssd_dt_bias, ssd_A_log, ssd_D, ssd_norm_w)
    emat = _head_expansion(ssdw // SSD_HEAD_DIM)
    norm_w3 = norm_w.reshape(depth, 1, d)

    x2 = x.reshape(B * L, d)
    for l in range(depth):
        proj = _inproj(x2, norm_w3, w_in_p, l, tm_in, tn)
        y_s5 = _s5_mixer(proj, B, L, src['s5_u'][0] // s5w, src['s5_z'][0] // s5w, s5w, s5p, l)
        y_ssd = _ssd_mixer(proj, B, L, src['ssd_z'][0] // ssdw, src['ssd_xbc'][0] // cdim,
                           src['ssd_dt'][0] // LANE, ssdp, emat, l, Q)
        y_att = _dsa_mixer(proj, B, L, src, band, far, nh_att, TQ)
        x2 = _outproj(y_s5, y_ssd, y_att, w_out_b, l, x2, final_norm_w.reshape(1, d), tm_out,
                      final=(l == depth - 1))
    return x2.reshape(B, L, d)
```

```python
import functools
import math

import numpy as np
import jax
import jax.numpy as jnp
from jax import lax
from jax.experimental import pallas as pl
from jax.experimental.pallas import tpu as pltpu

F32 = jnp.float32
BF16 = jnp.bfloat16

EPS = 1e-6
CHUNK = 64

S5_GROUP = 16
S5_STATE = 64
SSD_HEAD_DIM = 64
SSD_STATE = 128
SSD_GROUPS = 4
SSD_CONV = 4
ATT_HEAD_DIM = 64
IDX_DIM = 64
TOPK = 256
N_BUCKETS = 32
MAX_DISTANCE = 128

LANE = 128
SUBLANE = 8
VMEM_LIMIT = 56 * 1024 * 1024

INT_MIN = -2 ** 31
NEG = -1e30


def _cparams(sem):
    return pltpu.CompilerParams(dimension_semantics=sem, vmem_limit_bytes=VMEM_LIMIT)


def _inproj_body(x_ref, nw_ref, w_ref, o_ref, h_ref):
    @pl.when(pl.program_id(1) == 0)
    def _():
        xf = x_ref[...]
        ms = jnp.mean(xf * xf, axis=-1, keepdims=True)
        h_ref[...] = (xf * lax.rsqrt(ms + EPS) * nw_ref[...]).astype(BF16)

    o_ref[...] = jnp.dot(h_ref[...], w_ref[...], preferred_element_type=F32)


def _inproj(x2, nw, w, layer, tm, tn):
    m, d = x2.shape
    n = w.shape[2]
    return pl.pallas_call(
        _inproj_body,
        grid=(m // tm, n // tn),
        in_specs=[pl.BlockSpec((tm, d), lambda i, j: (i, 0)),
                  pl.BlockSpec((None, 1, d), lambda i, j: (layer, 0, 0)),
                  pl.BlockSpec((None, d, tn), lambda i, j: (layer, 0, j))],
        out_specs=pl.BlockSpec((tm, tn), lambda i, j: (i, j)),
        out_shape=jax.ShapeDtypeStruct((m, n), F32),
        scratch_shapes=[pltpu.VMEM((tm, d), BF16)],
        compiler_params=_cparams(("parallel", "arbitrary")),
        name="inproj",
    )(x2, nw, w)


def _relayout_body(w_ref, tail_ref, o_ref, *, segs, n_main):
    rows = o_ref.shape[0]
    lane = lax.broadcasted_iota(jnp.int32, (rows, LANE), 1)

    def src_group(n):
        if 0 <= n < n_main:
            return w_ref[:, n * LANE:(n + 1) * LANE]
        if n == n_main:
            return tail_ref[...]
        return None

    for g in range(o_ref.shape[1] // LANE):
        g0 = g * LANE
        out = None
        for src_lo, src_hi, dst_lo in segs:
            lo, hi = max(g0, dst_lo), min(g0 + LANE, dst_lo + src_hi - src_lo)
            if lo >= hi:
                continue
            s0 = g0 - (dst_lo - src_lo)
            sg, off = s0 // LANE, s0 % LANE
            a, b = src_group(sg), src_group(sg + 1)
            if off == 0:
                piece = a
            else:
                ra = pltpu.roll(a, LANE - off, axis=1) if a is not None else None
                rb = pltpu.roll(b, LANE - off, axis=1) if b is not None else None
                piece = (rb if ra is None else ra if rb is None
                         else jnp.where(lane < LANE - off, ra, rb))
            if lo > g0 or hi < g0 + LANE:
                piece = jnp.where((lane >= lo - g0) & (lane < hi - g0), piece, 0.0)
            out = piece if out is None else out + piece
        o_ref[:, g0:g0 + LANE] = (jnp.zeros((rows, LANE), F32) if out is None else out).astype(BF16)


def _relayout_w_in(w_in, segs, total_pad, tk):
    depth, d, n = w_in.shape
    n_main = n // LANE
    tail = jnp.pad(w_in[:, :, n_main * LANE:], ((0, 0), (0, 0), (0, (n_main + 1) * LANE - n)))
    body = functools.partial(_relayout_body, segs=segs, n_main=n_main)
    return pl.pallas_call(
        body,
        grid=(depth, d // tk),
        in_specs=[pl.BlockSpec((None, tk, n_main * LANE), lambda l, i: (l, i, 0)),
                  pl.BlockSpec((None, tk, LANE), lambda l, i: (l, i, 0))],
        out_specs=pl.BlockSpec((None, tk, total_pad), lambda l, i: (l, i, 0)),
        out_shape=jax.ShapeDtypeStruct((depth, d, total_pad), BF16),
        compiler_params=_cparams(("parallel", "parallel")),
        name="w_in_relayout",
    )(w_in, tail)


def _outproj_body(ys5_ref, yssd_ref, yatt_ref, w_ref, x_ref, fnw_ref, o_ref, *, w5, wssd, final):
    acc = x_ref[...]
    acc += jnp.dot(ys5_ref[...], w_ref[0:w5, :], preferred_element_type=F32)
    acc += jnp.dot(yssd_ref[...], w_ref[w5:w5 + wssd, :], preferred_element_type=F32)
    acc += jnp.dot(yatt_ref[...], w_ref[w5 + wssd:, :], preferred_element_type=F32)
    if final:
        ms = jnp.mean(acc * acc, axis=-1, keepdims=True)
        acc = acc * lax.rsqrt(ms + EPS) * fnw_ref[...]
    o_ref[...] = acc


def _outproj(ys5, yssd, yatt, w, layer, x2, fnw, tm, final):
    m, d = x2.shape
    w5, wssd, watt = ys5.shape[1], yssd.shape[1], yatt.shape[1]
    body = functools.partial(_outproj_body, w5=w5, wssd=wssd, final=final)
    return pl.pallas_call(
        body,
        grid=(m // tm,),
        in_specs=[pl.BlockSpec((tm, w5), lambda i: (i, 0)),
                  pl.BlockSpec((tm, wssd), lambda i: (i, 0)),
                  pl.BlockSpec((tm, watt), lambda i: (i, 0)),
                  pl.BlockSpec((None,) + w.shape[1:], lambda i: (layer, 0, 0)),
                  pl.BlockSpec((tm, d), lambda i: (i, 0)),
                  pl.BlockSpec((1, d), lambda i: (0, 0))],
        out_specs=pl.BlockSpec((tm, d), lambda i: (i, 0)),
        out_shape=jax.ShapeDtypeStruct((m, d), F32),
        compiler_params=_cparams(("parallel",)),
        name="outproj",
    )(ys5, yssd, yatt, w, x2, fnw)


def _shift_down_one(x):
    rolled = pltpu.roll(x, 1, axis=0)
    row = lax.broadcasted_iota(jnp.int32, x.shape, 0)
    return jnp.where(row == 0, 0.0, rolled)


def _s5_body(u_ref, z_ref, bblk_ref, cblk_ref, lam_ref, lamseg_ref, d_ref, gw_ref, gb_ref,
             o_ref, uperm_ref, xs_ref, yperm_ref, *, nchunk, cw, sw, rt):
    L = u_ref.shape[0]
    seg = L // SUBLANE

    for c in range(nchunk):
        for j in range(SUBLANE):
            for k in range(cw // LANE):
                lo = c * cw + k * LANE
                uperm_ref[k, pl.ds(j, seg, stride=SUBLANE), :] = u_ref[pl.ds(j * seg, seg), lo:lo + LANE]
        for r0 in range(0, L, rt):
            up = jnp.concatenate([uperm_ref[k, r0:r0 + rt, :] for k in range(cw // LANE)], axis=1)
            xs_ref[r0:r0 + rt, :] = jnp.dot(up.astype(BF16), bblk_ref[c], preferred_element_type=F32)
        lr = lam_ref[c, 0]
        li = lam_ref[c, 1]

        def scan_step(tau, carry):
            xr, xi = carry
            row = pl.multiple_of(tau * SUBLANE, SUBLANE)
            nxr = lr * xr - li * xi + xs_ref[pl.ds(row, SUBLANE), 0:sw]
            nxi = lr * xi + li * xr + xs_ref[pl.ds(row, SUBLANE), sw:2 * sw]
            xs_ref[pl.ds(row, SUBLANE), 0:sw] = nxr
            xs_ref[pl.ds(row, SUBLANE), sw:2 * sw] = nxi
            return nxr, nxi

        zero = jnp.zeros((SUBLANE, sw), F32)
        er, ei = lax.fori_loop(0, seg, scan_step, (zero, zero))

        sr = lamseg_ref[c, 0]
        si = lamseg_ref[c, 1]
        cr, ci = zero, zero
        for _ in range(SUBLANE - 1):
            tr = er + (sr * cr - si * ci)
            ti = ei + (sr * ci + si * cr)
            cr, ci = _shift_down_one(tr), _shift_down_one(ti)

        def fix_step(tau, carry):
            fr, fi = carry
            nfr = lr * fr - li * fi
            nfi = lr * fi + li * fr
            row = pl.multiple_of(tau * SUBLANE, SUBLANE)
            xs_ref[pl.ds(row, SUBLANE), 0:sw] += nfr
            xs_ref[pl.ds(row, SUBLANE), sw:2 * sw] += nfi
            return nfr, nfi

        lax.fori_loop(0, seg, fix_step, (cr, ci))

        for r0 in range(0, L, rt):
            yc = jnp.dot(xs_ref[r0:r0 + rt, :].astype(BF16), cblk_ref[c], preferred_element_type=F32)
            for k in range(cw // LANE):
                yperm_ref[c * (cw // LANE) + k, r0:r0 + rt, :] = yc[:, k * LANE:(k + 1) * LANE]

    for j in range(SUBLANE):
        y = jnp.concatenate([yperm_ref[k, pl.ds(j, seg, stride=SUBLANE), :]
                             for k in range(yperm_ref.shape[0])], axis=1)
        y = y + d_ref[...] * u_ref[j * seg:(j + 1) * seg, :]
        y = jax.nn.gelu(y)
        g = jnp.dot(y.astype(BF16), gw_ref[...], preferred_element_type=F32) + gb_ref[...]
        y = y * jax.nn.sigmoid(g)
        o_ref[j * seg:(j + 1) * seg, :] = (y * jax.nn.silu(z_ref[j * seg:(j + 1) * seg, :])).astype(BF16)


def _layer_spec(a, layer):
    return pl.BlockSpec((None,) + a.shape[1:], lambda *_: (layer,) + (0,) * (a.ndim - 1))


def _s5_mixer(proj, B, L, u_blk, z_blk, width, prm, layer):
    bblk, cblk, lam, lamseg, dvec, gw, gb = prm
    nchunk, cw, sw2 = bblk.shape[1:]
    sw = sw2 // 2
    rt = min(512, L)
    body = functools.partial(_s5_body, nchunk=nchunk, cw=cw, sw=sw, rt=rt)
    full = lambda a: _layer_spec(a, layer)
    return pl.pallas_call(
        body,
        grid=(B,),
        in_specs=[pl.BlockSpec((L, width), lambda b: (b, u_blk)),
                  pl.BlockSpec((L, width), lambda b: (b, z_blk)),
                  full(bblk), full(cblk), full(lam), full(lamseg), full(dvec), full(gw), full(gb)],
        out_specs=pl.BlockSpec((L, width), lambda b: (b, 0)),
        out_shape=jax.ShapeDtypeStruct((B * L, width), BF16),
        scratch_shapes=[pltpu.VMEM((cw // LANE, L, LANE), F32),
                        pltpu.VMEM((L, 2 * sw), F32),
                        pltpu.VMEM((width // LANE, L, LANE), F32)],
        compiler_params=_cparams(("parallel",)),
        name="s5_mixer",
    )(proj, proj, bblk, cblk, lam, lamseg, dvec, gw, gb)


def _cpow(re, im, n):
    rr, ri = jnp.ones_like(re), jnp.zeros_like(im)
    br, bi = re, im
    while n:
        if n & 1:
            rr, ri = rr * br - ri * bi, rr * bi + ri * br
        br, bi = br * br - bi * bi, 2.0 * br * bi
        n >>= 1
    return rr, ri


def _s5_params(A_re, A_im, log_dt, B_re, B_im, C_re, C_im, D, glu_w, glu_b, seg, nchunk):
    G, P, C = B_re.shape
    dt = jnp.exp(log_dt)[:, None]
    lre = jnp.minimum(A_re, -1e-4)
    lim = A_im
    mag = jnp.exp(lre * dt)
    lbr = mag * jnp.cos(lim * dt)
    lbi = mag * jnp.sin(lim * dt)
    nr, ni = lbr - 1.0, lbi
    den = lre * lre + lim * lim
    fr = (nr * lre + ni * lim) / den
    fi = (ni * lre - nr * lim) / den
    bbr = fr[..., None] * B_re - fi[..., None] * B_im
    bbi = fr[..., None] * B_im + fi[..., None] * B_re
    gc = G // nchunk
    eye = jnp.eye(gc, dtype=F32)

    def blockdiag_in(bb):
        t = jnp.transpose(bb, (0, 2, 1)).reshape(nchunk, gc, C, P)
        return jnp.einsum('ngcp,gh->ngchp', t, eye).reshape(nchunk, gc * C, gc * P)

    def blockdiag_out(cc):
        t = jnp.transpose(cc, (0, 2, 1)).reshape(nchunk, gc, P, C)
        return jnp.einsum('ngpc,gh->ngphc', t, eye).reshape(nchunk, gc * P, gc * C)

    bblk = jnp.concatenate([blockdiag_in(bbr), blockdiag_in(bbi)], axis=-1).astype(BF16)
    cblk = jnp.concatenate([blockdiag_out(C_re), blockdiag_out(-C_im)], axis=1).astype(BF16)

    def rows(v):
        return jnp.broadcast_to(v.reshape(nchunk, 1, gc * P), (nchunk, SUBLANE, gc * P))

    lam = jnp.stack([rows(lbr), rows(lbi)], axis=1)
    pr, pi = _cpow(lbr, lbi, seg)
    lamseg = jnp.stack([rows(pr), rows(pi)], axis=1)
    return (bblk, cblk, lam, lamseg, D.reshape(1, G * C), glu_w.astype(BF16), glu_b.reshape(1, -1))


def _split3(x):
    h = x.astype(BF16)
    r = x - h.astype(F32)
    m = r.astype(BF16)
    l = (r - m.astype(F32)).astype(BF16)
    return h, m, l


def _dot_f32(a, b_bf16):
    h, m, l = _split3(a)
    dot = lambda p: jnp.dot(p, b_bf16, preferred_element_type=F32)
    return dot(h) + dot(m) + dot(l)


def _ssd_body(z_ref, xbc_ref, dt_ref, cw_ref, cb_ref, dtb_ref, a_ref, dx_ref, nw_ref, e_ref,
              o_ref, state_ref, xpad_ref, *, nheads, hd, ns, ngroups):
    Q = z_ref.shape[0]
    width = nheads * hd
    gw = width // ngroups
    halo = SUBLANE

    @pl.when(pl.program_id(1) == 0)
    def _():
        state_ref[...] = jnp.zeros_like(state_ref)
        xpad_ref[0:halo, :] = jnp.zeros((halo, xpad_ref.shape[1]), F32)

    xpad_ref[halo:halo + Q, :] = xbc_ref[...]
    acc = cb_ref[...] + cw_ref[0:1, :] * xpad_ref[halo - 3:halo - 3 + Q, :]
    for k in range(1, SSD_CONV):
        acc += cw_ref[k:k + 1, :] * xpad_ref[halo - 3 + k:halo - 3 + k + Q, :]
    xpad_ref[0:halo, :] = xpad_ref[Q:Q + halo, :]
    xc = jax.nn.silu(acc)
    xs = xc[:, 0:width]
    bm = xc[:, width:width + ngroups * ns].astype(BF16)
    cm = xc[:, width + ngroups * ns:].astype(BF16)

    lane = lax.broadcasted_iota(jnp.int32, (Q, LANE), 1)
    dt = jnp.where(lane < nheads, jax.nn.softplus(dt_ref[...] + dtb_ref[...]), 0.0)
    a = dt * a_ref[...]
    rowi = lax.broadcasted_iota(jnp.int32, (Q, Q), 0)
    coli = lax.broadcasted_iota(jnp.int32, (Q, Q), 1)
    tril = coli <= rowi
    acs = _dot_f32_lhs(tril.astype(BF16), a)
    acs_t = acs.T
    last = acs[Q - 1:Q, :]
    emat = e_ref[...]
    dt_x = _expand(dt, emat)
    dec_x = _expand(jnp.exp(acs), emat)
    dte_x = _expand(jnp.exp(last - acs), emat)
    dlast_x = dec_x[Q - 1:Q, :]

    xd = xs * dt_x
    xd_b = xd.astype(BF16)
    xw_b = (xd * dte_x).astype(BF16)
    st = state_ref[...]
    st_b = st.astype(BF16)

    y_parts = []
    new_state = []
    hlane = lax.broadcasted_iota(jnp.int32, (Q, LANE), 1) < hd
    heads_per_group = nheads // ngroups
    for g in range(ngroups):
        cg = cm[:, g * ns:(g + 1) * ns]
        bg = bm[:, g * ns:(g + 1) * ns]
        cb = lax.dot_general(cg, bg, (((1,), (1,)), ((), ())), preferred_element_type=F32)
        y_off = jnp.dot(cg, st_b[:, g * gw:(g + 1) * gw], preferred_element_type=F32)
        pair_out = []
        for pr in range(gw // LANE):
            xpair = xd_b[:, g * gw + pr * LANE: g * gw + (pr + 1) * LANE]
            res = []
            for hh in range(LANE // hd):
                h = g * heads_per_group + pr * (LANE // hd) + hh
                diff = acs[:, h:h + 1] - acs_t[h:h + 1, :]
                s_h = jnp.where(tril, cb * jnp.exp(jnp.minimum(diff, 0.0)), 0.0).astype(BF16)
                res.append(jnp.dot(s_h, xpair, preferred_element_type=F32))
            pair_out.append(jnp.where(hlane, res[0], res[1]))
        y_diag = jnp.concatenate(pair_out, axis=1)
        y_parts.append(y_diag + y_off * dec_x[:, g * gw:(g + 1) * gw])
        upd = lax.dot_general(bg, xw_b[:, g * gw:(g + 1) * gw], (((0,), (0,)), ((), ())),
                              preferred_element_type=F32)
        new_state.append(st[:, g * gw:(g + 1) * gw] * dlast_x[:, g * gw:(g + 1) * gw] + upd)
    state_ref[...] = jnp.concatenate(new_state, axis=1)

    y = jnp.concatenate(y_parts, axis=1) + xs * dx_ref[...]
    gt = y * jax.nn.silu(z_ref[...])
    ms = jnp.mean(gt * gt, axis=-1, keepdims=True)
    o_ref[...] = (gt * lax.rsqrt(ms + EPS) * nw_ref[...]).astype(BF16)


def _dot_f32_lhs(a_bf16, b):
    h, m, l = _split3(b)
    dot = lambda p: jnp.dot(a_bf16, p, preferred_element_type=F32)
    return dot(h) + dot(m) + dot(l)


def _expand(v, emat):
    return _dot_f32(v, emat)


def _ssd_mixer(proj, B, L, z_blk, xbc_blk, dt_blk, prm, emat, layer, Q):
    cw, cb, dtb, avec, dx, nw = prm
    width = nw.shape[-1]
    cdim = cw.shape[-1]
    nheads = width // SSD_HEAD_DIM
    nt = L // Q
    body = functools.partial(_ssd_body, nheads=nheads, hd=SSD_HEAD_DIM, ns=SSD_STATE, ngroups=SSD_GROUPS)
    full = lambda a: _layer_spec(a, layer)
    return pl.pallas_call(
        body,
        grid=(B, nt),
        in_specs=[pl.BlockSpec((Q, width), lambda b, t: (b * nt + t, z_blk)),
                  pl.BlockSpec((Q, cdim), lambda b, t: (b * nt + t, xbc_blk)),
                  pl.BlockSpec((Q, LANE), lambda b, t: (b * nt + t, dt_blk)),
                  full(cw), full(cb), full(dtb), full(avec), full(dx), full(nw),
                  pl.BlockSpec(emat.shape, lambda b, t: (0, 0))],
        out_specs=pl.BlockSpec((Q, width), lambda b, t: (b * nt + t, 0)),
        out_shape=jax.ShapeDtypeStruct((B * L, width), BF16),
        scratch_shapes=[pltpu.VMEM((SSD_STATE, width), F32),
                        pltpu.VMEM((Q + SUBLANE, cdim), F32)],
        compiler_params=_cparams(("parallel", "arbitrary")),
        name="ssd_mixer",
    )(proj, proj, proj, cw, cb, dtb, avec, dx, nw, emat)


def _ssd_params(conv_w, conv_b, dt_bias, A_log, D, norm_w):
    depth, nheads = dt_bias.shape
    width = nheads * SSD_HEAD_DIM
    pad = lambda v: jnp.pad(v, ((0, 0), (0, LANE - nheads))).reshape(depth, 1, LANE)
    return (conv_w, conv_b.reshape(depth, 1, -1), pad(dt_bias), pad(-jnp.exp(A_log)),
            jnp.repeat(D, SSD_HEAD_DIM, axis=1).reshape(depth, 1, width), norm_w.reshape(depth, 1, width))


def _head_expansion(nheads):
    e = np.zeros((LANE, nheads * SSD_HEAD_DIM), np.float32)
    e[np.repeat(np.arange(nheads), SSD_HEAD_DIM), np.arange(nheads * SSD_HEAD_DIM)] = 1.0
    return jnp.asarray(e, BF16)


def _float_key(x):
    b = lax.bitcast_convert_type(x, jnp.int32)
    return b ^ ((b >> 31) & 0x7FFFFFFF)


def _place(blk, lane0, width, dst):
    lane = lax.broadcasted_iota(jnp.int32, blk.shape, 1)
    shift = (dst - lane0) % LANE
    r = pltpu.roll(blk, shift, axis=1) if shift else blk
    lo = jnp.where((lane >= dst) & (lane < min(dst + width, LANE)), r, 0.0)
    hi = jnp.where(lane < dst + width - LANE, r, 0.0)
    return jnp.concatenate([lo, hi], axis=1)


def _head_offsets(offs, nheads, hd):
    return sorted({(offs[f] + h * hd) % LANE for f in ('q', 'qi') for h in range(nheads)})


def _lane_window(x, start, width):
    off = start % LANE
    base = start - off
    if off == 0:
        return x[:, start:start + width]
    lane = lax.broadcasted_iota(jnp.int32, (x.shape[0], LANE), 1)
    out = []
    for p in range(width // LANE):
        a = pltpu.roll(x[:, base + p * LANE:base + (p + 1) * LANE], LANE - off, axis=1)
        b = pltpu.roll(x[:, base + (p + 1) * LANE:base + (p + 2) * LANE], LANE - off, axis=1)
        out.append(jnp.where(lane < LANE - off, a, b))
    return jnp.concatenate(out, axis=1)


def _dsa_body(qx_ref, kv_ref, ki_ref, band_ref, far_ref, o_ref,
              kpad_ref, kipad_ref, vt_ref, qop_ref, key_ref, hi_ref, lo_ref, lom_ref, lg_ref, bound_ref,
              m_ref, l_ref, acc_ref,
              *, nheads, hd, topk, offs):
    i = pl.program_id(1)
    TQ = qx_ref.shape[0]
    KB = TQ
    L = kv_ref.shape[0]
    nkb_total = L // KB
    NT = (((1,), (1,)), ((), ()))
    WIN = kpad_ref.shape[2]

    def head_window(field_lo, h):
        lo = field_lo + h * hd
        return (lo // LANE) * LANE, lo % LANE

    head_offs = _head_offsets(offs, nheads, hd)

    @pl.when(i == 0)
    def _():
        for kb in range(nkb_total):
            r = slice(kb * KB, (kb + 1) * KB)
            kgrp = (offs['k'] // LANE) * LANE
            kblk = kv_ref[r, kgrp:kgrp + LANE]
            kiblk = ki_ref[r, :]
            for n, o in enumerate(head_offs):
                kpad_ref[n, r, :] = _place(kblk, offs['k'] - kgrp, hd, o)[:, :WIN].astype(BF16)
                kipad_ref[n, r, :] = _place(kiblk, offs['ki'], hd, o)[:, :WIN].astype(BF16)
            vt_ref[:, r] = kv_ref[r, :].T[offs['v']:offs['v'] + hd, :].astype(BF16)

    def stage_windows(slot, field_lo, scale):
        bases = sorted({head_window(field_lo, h)[0] for h in range(nheads)})
        for n, base in enumerate(bases):
            qop_ref[slot, n] = (qx_ref[:, base:base + WIN] * scale).astype(BF16)
        return [(bases.index(head_window(field_lo, h)[0]), head_offs.index(head_window(field_lo, h)[1]))
                for h in range(nheads)]

    qs = stage_windows(0, offs['q'], hd ** -0.5)
    qis = stage_windows(1, offs['qi'], 1.0)
    wbase = (offs['w'] // LANE) * LANE
    wt = qx_ref[:, wbase:wbase + LANE].T * ((nheads * IDX_DIM) ** -0.5)
    w_rows = [wt[offs['w'] - wbase + h:offs['w'] - wbase + h + 1, :] for h in range(nheads)]

    RB = LANE
    I16 = jnp.int16
    HALF = 1 << 15

    def rows(kb, r=0, n=KB):
        return pl.ds(pl.multiple_of(kb * KB + r, SUBLANE), n)

    PART = m_ref.shape[1]

    def fold_sum(x):
        return jnp.sum(x.reshape(x.shape[0] // PART, PART, TQ), axis=0)

    def fold_max(x):
        return jnp.max(x.reshape(x.shape[0] // PART, PART, TQ), axis=0)

    def idx_keys(kb, diagonal):
        for r in range(0, KB, RB):
            s = jnp.zeros((RB, TQ), F32)
            for h in range(nheads):
                win, ko = qis[h]
                lg = lax.dot_general(kipad_ref[ko, rows(kb, r, RB), :], qop_ref[1, win], NT,
                                     preferred_element_type=F32)
                s += jnp.maximum(lg, 0.0) * w_rows[h]
            key = _float_key(s)
            if diagonal:
                krow = lax.broadcasted_iota(jnp.int32, (RB, TQ), 0) + r
                qcol = lax.broadcasted_iota(jnp.int32, (RB, TQ), 1)
                key = jnp.where((krow // CHUNK) <= (qcol // CHUNK), key, INT_MIN)
            key_ref[rows(kb, r, RB), :] = key
            hi_ref[rows(kb, r, RB), :] = (key >> 16).astype(I16)
            lo_ref[rows(kb, r, RB), :] = ((key & (2 * HALF - 1)) - HALF).astype(I16)

    def far_keys(kb, _):
        idx_keys(kb, False)
        return 0

    lax.fori_loop(0, i, far_keys, 0)
    idx_keys(i, True)

    def count(preds):
        def body(kb, cs):
            blk = key_ref[rows(kb), :]
            return tuple(c + fold_sum(jnp.where(p(blk, kb), 1.0, 0.0)) for c, p in zip(cs, preds))
        z = jnp.zeros((PART, TQ), F32)
        cs = lax.fori_loop(0, i + 1, body, (z,) * len(preds))
        return [jnp.sum(c, axis=0, keepdims=True) for c in cs]

    PACK = 2 * SUBLANE
    P16 = 4 * PACK

    def rows16(x):
        return x.reshape(x.shape[0] // PACK, PACK, TQ)

    def as16(v):
        return jnp.broadcast_to(v, (PACK, TQ)).astype(I16)

    def count16(src_ref, pred):
        def body(kb, c):
            ind = jnp.where(pred(rows16(src_ref[rows(kb), :])), I16(1), I16(0)).reshape(KB // P16, P16, TQ)
            for n in range(KB // P16):
                c = c + ind[n]
            return c
        c = lax.fori_loop(0, i + 1, body, jnp.zeros((P16, TQ), I16))
        return jnp.sum(c.astype(F32), axis=0, keepdims=True)

    def bisect16(src_ref, target):
        def step(s, tu):
            cand_u = tu | jnp.left_shift(jnp.int32(1), 15 - s)
            cand = as16(cand_u - HALF)[None]
            cnt = count16(src_ref, lambda blk: blk >= cand)
            return jnp.where(cnt >= target, cand_u, tu)
        return lax.fori_loop(0, 16, step, jnp.zeros((1, TQ), jnp.int32))

    thr_hi = bisect16(hi_ref, float(topk)) - HALF
    thr_hi16 = as16(thr_hi)[None]
    need_lo = topk - count16(hi_ref, lambda blk: blk > thr_hi16)

    def mask_low_halves(kb, _):
        keep_lo = rows16(hi_ref[rows(kb), :]) == thr_hi16
        lom_ref[rows(kb), :] = jnp.where(keep_lo, rows16(lo_ref[rows(kb), :]), I16(-HALF)).reshape(KB, TQ)
        return 0

    lax.fori_loop(0, i + 1, mask_low_halves, 0)
    thr = thr_hi * (2 * HALF) + bisect16(lom_ref, need_lo)

    cgt, ceq = count([lambda blk, kb: blk > thr, lambda blk, kb: blk == thr])
    need = topk - cgt
    nbits = int(L).bit_length()
    bound_ref[...] = jnp.full((1, TQ), 2 ** nbits - 1, jnp.int32)

    @pl.when(jnp.max(ceq - need) > 0.0)
    def _():
        krow = lax.broadcasted_iota(jnp.int32, (KB, TQ), 0)

        def bisect_idx(step, bnd):
            bit = jnp.left_shift(jnp.int32(1), nbits - 1 - step)
            cand = bnd | bit
            cnt, = count([lambda blk, kb: (blk == thr) & (krow + kb * KB < cand)])
            return jnp.where(cnt <= need, cand, bnd)

        bound_ref[...] = lax.fori_loop(0, nbits, bisect_idx, jnp.zeros((1, TQ), jnp.int32))

    bound = bound_ref[...]

    m_ref[...] = jnp.full(m_ref.shape, NEG, F32)

    def store_logits(kb, bias_of_head):
        for r in range(0, KB, RB):
            blk = key_ref[rows(kb, r, RB), :]
            pos = lax.broadcasted_iota(jnp.int32, (RB, TQ), 0) + (kb * KB + r)
            keep = ((blk > thr) | ((blk == thr) & (pos < bound))) & (blk != INT_MIN)
            mask = jnp.where(keep, 0.0, NEG)
            for h in range(nheads):
                win, ko = qs[h]
                lg = lax.dot_general(kpad_ref[ko, rows(kb, r, RB), :], qop_ref[0, win], NT,
                                     preferred_element_type=F32)
                lg = lg + bias_of_head(h, r) + mask
                lg_ref[h, rows(kb, r, RB), :] = lg
                m_ref[h] = jnp.maximum(m_ref[h], fold_max(lg))

    def far_logits(kb, _):
        store_logits(kb, lambda h, r: far_ref[h])
        return 0

    lax.fori_loop(0, i - 1, far_logits, 0)

    @pl.when(i >= 1)
    def _():
        store_logits(i - 1, lambda h, r: band_ref[h, r:r + RB, :])

    store_logits(i, lambda h, r: band_ref[h, KB + r:KB + r + RB, :])

    m_fin = [jnp.max(m_ref[h], axis=0, keepdims=True) for h in range(nheads)]
    acc_ref[...] = jnp.zeros(acc_ref.shape, F32)
    l_ref[...] = jnp.zeros(l_ref.shape, F32)

    def accumulate(kb, _):
        vt = vt_ref[:, rows(kb)]
        for h in range(nheads):
            p = jnp.exp(lg_ref[h, rows(kb), :] - m_fin[h])
            l_ref[h] += fold_sum(p)
            acc_ref[h] += jnp.dot(vt, p.astype(BF16), preferred_element_type=F32)
        return 0

    lax.fori_loop(0, i + 1, accumulate, 0)

    outs = [acc_ref[h] / jnp.sum(l_ref[h], axis=0, keepdims=True) for h in range(nheads)]
    out = jnp.concatenate(outs, axis=0).T
    z = _lane_window(qx_ref, offs['z'], nheads * hd)
    o_ref[...] = (out * jax.nn.silu(z)).astype(BF16)


def _dsa_mixer(proj, B, L, src, band, far, nheads, TQ):
    width = nheads * ATT_HEAD_DIM
    nq = L // TQ
    hd = ATT_HEAD_DIM
    qx_w = 16 * LANE
    qx_blk = src['att_q'][0] // qx_w
    qx_lo = qx_blk * qx_w
    kv_w = 2 * LANE
    kv_blk = src['att_k'][0] // kv_w
    ki_blk = src['idx_k'][0] // LANE
    offs = {'q': src['att_q'][0] - qx_lo, 'qi': src['idx_q'][0] - qx_lo, 'w': src['idx_w'][0] - qx_lo,
            'z': src['att_z'][0] - qx_lo, 'k': src['att_k'][0] - kv_blk * kv_w,
            'v': src['att_v'][0] - kv_blk * kv_w, 'ki': src['idx_k'][0] - ki_blk * LANE}
    assert offs['z'] + width + LANE <= qx_w and qx_lo + qx_w <= proj.shape[1]
    assert offs['k'] % LANE + hd <= LANE and offs['ki'] + hd <= LANE
    assert offs['v'] % SUBLANE == 0 and offs['v'] + hd <= kv_w
    assert offs['w'] // LANE == (offs['w'] + nheads - 1) // LANE
    win_off = _head_offsets(offs, nheads, hd)
    win = LANE if max(win_off) + hd <= LANE else 2 * LANE
    win_base = max((sorted({(offs[f] + h * hd) // LANE for h in range(nheads)}) for f in ('q', 'qi')), key=len)
    body = functools.partial(_dsa_body, nheads=nheads, hd=hd, topk=min(TOPK, L // 4), offs=offs)
    full = lambda a: pl.BlockSpec(a.shape, lambda b, i: (0,) * a.ndim)
    return pl.pallas_call(
        body,
        grid=(B, nq),
        in_specs=[pl.BlockSpec((TQ, qx_w), lambda b, i: (b * nq + i, qx_blk)),
                  pl.BlockSpec((L, kv_w), lambda b, i: (b, kv_blk)),
                  pl.BlockSpec((L, LANE), lambda b, i: (b, ki_blk)),
                  full(band), full(far)],
        out_specs=pl.BlockSpec((TQ, width), lambda b, i: (b * nq + i, 0)),
        out_shape=jax.ShapeDtypeStruct((B * L, width), BF16),
        scratch_shapes=[pltpu.VMEM((len(win_off), L, win), BF16),
                        pltpu.VMEM((len(win_off), L, win), BF16),
                        pltpu.VMEM((ATT_HEAD_DIM, L), BF16),
                        pltpu.VMEM((2, len(win_base), TQ, win), BF16),
                        pltpu.VMEM((L, TQ), jnp.int32),
                        pltpu.VMEM((L, TQ), jnp.int16),
                        pltpu.VMEM((L, TQ), jnp.int16),
                        pltpu.VMEM((L, TQ), jnp.int16),
                        pltpu.VMEM((nheads, L, TQ), F32),
                        pltpu.VMEM((1, TQ), jnp.int32),
                        pltpu.VMEM((nheads, 4 * SUBLANE, TQ), F32),
                        pltpu.VMEM((nheads, 4 * SUBLANE, TQ), F32),
                        pltpu.VMEM((nheads, ATT_HEAD_DIM, TQ), F32)],
        compiler_params=_cparams(("parallel", "arbitrary")),
        name="dsa_mixer",
    )(proj, proj, proj, band, far)


def _t5_bucket_static(rel):
    nb = N_BUCKETS // 2
    max_exact = nb // 2
    ret = np.where(rel > 0, nb, 0)
    n = np.abs(rel)
    nf = np.maximum(n, 1).astype(np.float64)
    large = max_exact + (np.log(nf / max_exact) / math.log(MAX_DISTANCE / max_exact)
                         * (nb - max_exact)).astype(np.int32)
    large = np.minimum(large, nb - 1)
    return ret + np.where(n < max_exact, n, large)


def _dsa_bias_tables(rel_bias, TQ):
    a = np.arange(2 * TQ)[:, None]
    j = np.arange(TQ)[None, :]
    band_idx = _t5_bucket_static(a - TQ - j)
    idx = jnp.asarray(band_idx, jnp.int32)[None]
    band = jnp.zeros((rel_bias.shape[1],) + band_idx.shape, F32)
    for bucket in np.unique(band_idx):
        band = jnp.where(idx == int(bucket), rel_bias[int(bucket)][:, None, None], band)
    far_bucket = int(_t5_bucket_static(np.array([-(TQ + 1)]))[0])
    assert far_bucket == int(_t5_bucket_static(np.array([-(10 ** 6)]))[0])
    far = jnp.broadcast_to(rel_bias[far_bucket][:, None, None], (rel_bias.shape[1], 1, TQ))
    return band, far


def _layout(d_model):
    s5w = d_model // 4
    ssdw = d_model // 2
    attw = d_model // 4
    cdim = ssdw + 2 * SSD_GROUPS * SSD_STATE
    nh_ssd = ssdw // SSD_HEAD_DIM
    nh_att = attw // ATT_HEAD_DIM
    splits = (s5w, s5w, ssdw, cdim, nh_ssd, attw, ATT_HEAD_DIM, ATT_HEAD_DIM,
              nh_att * IDX_DIM, IDX_DIM, nh_att, attw)
    names = ('s5_u', 's5_z', 'ssd_z', 'ssd_xbc', 'ssd_dt', 'att_q', 'att_k', 'att_v',
             'idx_q', 'idx_k', 'idx_w', 'att_z')
    src, o = {}, 0
    for nme, s in zip(names, splits):
        src[nme] = (o, s)
        o += s
    return src, o


def kernel(x, norm_w, w_in, s5_A_re, s5_A_im, s5_log_dt, s5_B_re, s5_B_im, s5_C_re, s5_C_im, s5_D, s5_glu_w, s5_glu_b, ssd_conv_w, ssd_conv_b, ssd_dt_bias, ssd_A_log, ssd_D, ssd_norm_w, rel_bias, w_out, final_norm_w):
    B, L, d = x.shape
    depth = w_in.shape[0]
    src, total = _layout(d)
    s5w, ssdw, attw = d // 4, d // 2, d // 4
    cdim = ssdw + 2 * SSD_GROUPS * SSD_STATE
    nh_att = attw // ATT_HEAD_DIM
    assert src['s5_u'][0] % s5w == 0 and src['s5_z'][0] % s5w == 0 and src['ssd_z'][0] % ssdw == 0
    assert src['ssd_xbc'][0] % cdim == 0 and src['ssd_dt'][0] % LANE == 0

    tn = 1024
    cuts, shift, padded = [(0, 0)], 0, {}
    for nme, (so, sz) in sorted(src.items(), key=lambda kv: kv[1][0]):
        gap = -(so + shift) % LANE if nme in ('att_q', 'att_k', 'idx_q', 'idx_k', 'att_z') else 0
        if gap:
            shift += gap
            cuts.append((so, shift))
        padded[nme] = (so + shift, sz)
    src = padded
    total_pad = -(-(total + shift) // tn) * tn
    segs = tuple((lo, cuts[n + 1][0] if n + 1 < len(cuts) else total, lo + sh) for n, (lo, sh) in enumerate(cuts))
    w_in_p = _relayout_w_in(w_in, segs, total_pad, min(256, d))
    w_out_b = w_out.astype(BF16)

    TQ = 256
    Q = 128
    band, far = _dsa_bias_tables(rel_bias, TQ)
    tm_in = min(1024, B * L)
    tm_out = min(256, B * L)
    nchunk = 2
    seg = L // SUBLANE

    s5p = jax.vmap(lambda *a: _s5_params(*a, seg, nchunk))(
        s5_A_re, s5_A_im, s5_log_dt, s5_B_re, s5_B_im, s5_C_re, s5_C_im, s5_D, s5_glu_w, s5_glu_b)
    ssdp = _ssd_params(ssd_conv_w, ssd_conv_b, ssd_dt_bias, ssd_A_log, ssd_D, ssd_norm_w)
    emat = _head_expansion(ssdw // SSD_HEAD_DIM)
    norm_w3 = norm_w.reshape(depth, 1, d)

    x2 = x.reshape(B * L, d)
    for l in range(depth):
        proj = _inproj(x2, norm_w3, w_in_p, l, tm_in, tn)
        y_s5 = _s5_mixer(proj, B, L, src['s5_u'][0] // s5w, src['s5_z'][0] // s5w, s5w, s5p, l)
        y_ssd = _ssd_mixer(proj, B, L, src['ssd_z'][0] // ssdw, src['ssd_xbc'][0] // cdim,
                           src['ssd_dt'][0] // LANE, ssdp, emat, l, Q)
        y_att = _dsa_mixer(proj, B, L, src, band, far, nh_att, TQ)
        x2 = _outproj(y_s5, y_ssd, y_att, w_out_b, l, x2, final_norm_w.reshape(1, d), tm_out,
                      final=(l == depth - 1))
    return x2.reshape(B, L, d)
```

```python
import functools
import math

import numpy as np
import jax
import jax.numpy as jnp
from jax import lax
from jax.experimental import pallas as pl
from jax.experimental.pallas import tpu as pltpu

F32 = jnp.float32
BF16 = jnp.bfloat16

EPS = 1e-6
CHUNK = 64

S5_GROUP = 16
S5_STATE = 64
SSD_HEAD_DIM = 64
SSD_STATE = 128
SSD_GROUPS = 4
SSD_CONV = 4
ATT_HEAD_DIM = 64
IDX_DIM = 64
TOPK = 256
N_BUCKETS = 32
MAX_DISTANCE = 128

LANE = 128
SUBLANE = 8
VMEM_LIMIT = 56 * 1024 * 1024

INT_MIN = -2 ** 31
NEG = -1e30


def _cparams(sem):
    return pltpu.CompilerParams(dimension_semantics=sem, vmem_limit_bytes=VMEM_LIMIT)


def _inproj_body(x_ref, nw_ref, w_ref, o_ref, h_ref):
    @pl.when(pl.program_id(1) == 0)
    def _():
        xf = x_ref[...]
        ms = jnp.mean(xf * xf, axis=-1, keepdims=True)
        h_ref[...] = (xf * lax.rsqrt(ms + EPS) * nw_ref[...]).astype(BF16)

    o_ref[...] = lax.dot_general(h_ref[...], w_ref[...], (((1,), (1,)), ((), ())),
                                 preferred_element_type=F32)


def _inproj(x2, nw, w, layer, tm, tn):
    m, d = x2.shape
    n = w.shape[1]
    return pl.pallas_call(
        _inproj_body,
        grid=(m // tm, n // tn),
        in_specs=[pl.BlockSpec((tm, d), lambda i, j: (i, 0)),
                  pl.BlockSpec((None, 1, d), lambda i, j: (layer, 0, 0)),
                  pl.BlockSpec((None, tn, d), lambda i, j: (layer, j, 0))],
        out_specs=pl.BlockSpec((tm, tn), lambda i, j: (i, j)),
        out_shape=jax.ShapeDtypeStruct((m, n), F32),
        scratch_shapes=[pltpu.VMEM((tm, d), BF16)],
        compiler_params=_cparams(("parallel", "arbitrary")),
        name="inproj",
    )(x2, nw, w)


def _relayout_body(w_ref, o_ref, *, segs):
    pack = 2 * SUBLANE
    pos = 0
    for src_lo, src_hi, dst_lo in segs:
        if dst_lo > pos:
            o_ref[pos:dst_lo, :] = jnp.zeros((dst_lo - pos, o_ref.shape[1]), BF16)
        val = w_ref[src_lo:src_hi, :]
        fill = -(src_hi - src_lo) % pack
        if fill:
            val = jnp.concatenate([val, jnp.zeros((fill, val.shape[1]), F32)], axis=0)
        o_ref[dst_lo:dst_lo + val.shape[0], :] = val.astype(BF16)
        pos = dst_lo + val.shape[0]
    if pos < o_ref.shape[0]:
        o_ref[pos:, :] = jnp.zeros((o_ref.shape[0] - pos, o_ref.shape[1]), BF16)


def _relayout_w_in(w_t, segs, total_pad, tk):
    depth, n, d = w_t.shape
    assert all(lo % SUBLANE == 0 and (hi - lo) % SUBLANE == 0 and dst % (2 * SUBLANE) == 0
               for lo, hi, dst in segs)
    return pl.pallas_call(
        functools.partial(_relayout_body, segs=segs),
        grid=(depth, d // tk),
        in_specs=[pl.BlockSpec((None, n, tk), lambda l, i: (l, 0, i))],
        out_specs=pl.BlockSpec((None, total_pad, tk), lambda l, i: (l, 0, i)),
        out_shape=jax.ShapeDtypeStruct((depth, total_pad, d), BF16),
        compiler_params=_cparams(("parallel", "parallel")),
        name="w_in_relayout",
    )(w_t)


def _outproj_body(ys5_ref, yssd_ref, yatt_ref, w_ref, x_ref, fnw_ref, o_ref, *, w5, wssd, final):
    acc = x_ref[...]
    acc += jnp.dot(ys5_ref[...], w_ref[0:w5, :], preferred_element_type=F32)
    acc += jnp.dot(yssd_ref[...], w_ref[w5:w5 + wssd, :], preferred_element_type=F32)
    acc += jnp.dot(yatt_ref[...], w_ref[w5 + wssd:, :], preferred_element_type=F32)
    if final:
        ms = jnp.mean(acc * acc, axis=-1, keepdims=True)
        acc = acc * lax.rsqrt(ms + EPS) * fnw_ref[...]
    o_ref[...] = acc


def _outproj(ys5, yssd, yatt, w, layer, x2, fnw, tm, final):
    m, d = x2.shape
    w5, wssd, watt = ys5.shape[1], yssd.shape[1], yatt.shape[1]
    body = functools.partial(_outproj_body, w5=w5, wssd=wssd, final=final)
    return pl.pallas_call(
        body,
        grid=(m // tm,),
        in_specs=[pl.BlockSpec((tm, w5), lambda i: (i, 0)),
                  pl.BlockSpec((tm, wssd), lambda i: (i, 0)),
                  pl.BlockSpec((tm, watt), lambda i: (i, 0)),
                  pl.BlockSpec((None,) + w.shape[1:], lambda i: (layer, 0, 0)),
                  pl.BlockSpec((tm, d), lambda i: (i, 0)),
                  pl.BlockSpec((1, d), lambda i: (0, 0))],
        out_specs=pl.BlockSpec((tm, d), lambda i: (i, 0)),
        out_shape=jax.ShapeDtypeStruct((m, d), F32),
        compiler_params=_cparams(("parallel",)),
        name="outproj",
    )(ys5, yssd, yatt, w, x2, fnw)


def _shift_down_one(x):
    rolled = pltpu.roll(x, 1, axis=0)
    row = lax.broadcasted_iota(jnp.int32, x.shape, 0)
    return jnp.where(row == 0, 0.0, rolled)


def _s5_body(u_ref, z_ref, bblk_ref, cblk_ref, lam_ref, lamseg_ref, d_ref, gw_ref, gb_ref,
             o_ref, uperm_ref, xs_ref, yperm_ref, *, nchunk, cw, sw, rt):
    L = u_ref.shape[0]
    seg = L // SUBLANE

    for c in range(nchunk):
        for j in range(SUBLANE):
            for k in range(cw // LANE):
                lo = c * cw + k * LANE
                uperm_ref[k, pl.ds(j, seg, stride=SUBLANE), :] = u_ref[pl.ds(j * seg, seg), lo:lo + LANE]
        for r0 in range(0, L, rt):
            up = jnp.concatenate([uperm_ref[k, r0:r0 + rt, :] for k in range(cw // LANE)], axis=1)
            xs_ref[r0:r0 + rt, :] = jnp.dot(up.astype(BF16), bblk_ref[c], preferred_element_type=F32)
        lr = lam_ref[c, 0]
        li = lam_ref[c, 1]

        def scan_step(tau, carry):
            xr, xi = carry
            row = pl.multiple_of(tau * SUBLANE, SUBLANE)
            nxr = lr * xr - li * xi + xs_ref[pl.ds(row, SUBLANE), 0:sw]
            nxi = lr * xi + li * xr + xs_ref[pl.ds(row, SUBLANE), sw:2 * sw]
            xs_ref[pl.ds(row, SUBLANE), 0:sw] = nxr
            xs_ref[pl.ds(row, SUBLANE), sw:2 * sw] = nxi
            return nxr, nxi

        zero = jnp.zeros((SUBLANE, sw), F32)
        er, ei = lax.fori_loop(0, seg, scan_step, (zero, zero))

        sr = lamseg_ref[c, 0]
        si = lamseg_ref[c, 1]
        cr, ci = zero, zero
        for _ in range(SUBLANE - 1):
            tr = er + (sr * cr - si * ci)
            ti = ei + (sr * ci + si * cr)
            cr, ci = _shift_down_one(tr), _shift_down_one(ti)

        def fix_step(tau, carry):
            fr, fi = carry
            nfr = lr * fr - li * fi
            nfi = lr * fi + li * fr
            row = pl.multiple_of(tau * SUBLANE, SUBLANE)
            xs_ref[pl.ds(row, SUBLANE), 0:sw] += nfr
            xs_ref[pl.ds(row, SUBLANE), sw:2 * sw] += nfi
            return nfr, nfi

        lax.fori_loop(0, seg, fix_step, (cr, ci))

        for r0 in range(0, L, rt):
            yc = jnp.dot(xs_ref[r0:r0 + rt, :].astype(BF16), cblk_ref[c], preferred_element_type=F32)
            for k in range(cw // LANE):
                yperm_ref[c * (cw // LANE) + k, r0:r0 + rt, :] = yc[:, k * LANE:(k + 1) * LANE]

    for j in range(SUBLANE):
        y = jnp.concatenate([yperm_ref[k, pl.ds(j, seg, stride=SUBLANE), :]
                             for k in range(yperm_ref.shape[0])], axis=1)
        y = y + d_ref[...] * u_ref[j * seg:(j + 1) * seg, :]
        y = jax.nn.gelu(y)
        g = jnp.dot(y.astype(BF16), gw_ref[...], preferred_element_type=F32) + gb_ref[...]
        y = y * jax.nn.sigmoid(g)
        o_ref[j * seg:(j + 1) * seg, :] = (y * jax.nn.silu(z_ref[j * seg:(j + 1) * seg, :])).astype(BF16)


def _layer_spec(a, layer):
    return pl.BlockSpec((None,) + a.shape[1:], lambda *_: (layer,) + (0,) * (a.ndim - 1))


def _s5_mixer(proj, B, L, u_blk, z_blk, width, prm, layer):
    bblk, cblk, lam, lamseg, dvec, gw, gb = prm
    nchunk, cw, sw2 = bblk.shape[1:]
    sw = sw2 // 2
    rt = min(512, L)
    body = functools.partial(_s5_body, nchunk=nchunk, cw=cw, sw=sw, rt=rt)
    full = lambda a: _layer_spec(a, layer)
    return pl.pallas_call(
        body,
        grid=(B,),
        in_specs=[pl.BlockSpec((L, width), lambda b: (b, u_blk)),
                  pl.BlockSpec((L, width), lambda b: (b, z_blk)),
                  full(bblk), full(cblk), full(lam), full(lamseg), full(dvec), full(gw), full(gb)],
        out_specs=pl.BlockSpec((L, width), lambda b: (b, 0)),
        out_shape=jax.ShapeDtypeStruct((B * L, width), BF16),
        scratch_shapes=[pltpu.VMEM((cw // LANE, L, LANE), F32),
                        pltpu.VMEM((L, 2 * sw), F32),
                        pltpu.VMEM((width // LANE, L, LANE), F32)],
        compiler_params=_cparams(("parallel",)),
        name="s5_mixer",
    )(proj, proj, bblk, cblk, lam, lamseg, dvec, gw, gb)


def _cpow(re, im, n):
    rr, ri = jnp.ones_like(re), jnp.zeros_like(im)
    br, bi = re, im
    while n:
        if n & 1:
            rr, ri = rr * br - ri * bi, rr * bi + ri * br
        br, bi = br * br - bi * bi, 2.0 * br * bi
        n >>= 1
    return rr, ri


def _s5_params(A_re, A_im, log_dt, B_re, B_im, C_re, C_im, D, glu_w, glu_b, seg, nchunk):
    G, P, C = B_re.shape
    dt = jnp.exp(log_dt)[:, None]
    lre = jnp.minimum(A_re, -1e-4)
    lim = A_im
    mag = jnp.exp(lre * dt)
    lbr = mag * jnp.cos(lim * dt)
    lbi = mag * jnp.sin(lim * dt)
    nr, ni = lbr - 1.0, lbi
    den = lre * lre + lim * lim
    fr = (nr * lre + ni * lim) / den
    fi = (ni * lre - nr * lim) / den
    bbr = fr[..., None] * B_re - fi[..., None] * B_im
    bbi = fr[..., None] * B_im + fi[..., None] * B_re
    gc = G // nchunk
    eye = jnp.eye(gc, dtype=F32)

    def blockdiag_in(bb):
        t = jnp.transpose(bb, (0, 2, 1)).reshape(nchunk, gc, C, P)
        return jnp.einsum('ngcp,gh->ngchp', t, eye).reshape(nchunk, gc * C, gc * P)

    def blockdiag_out(cc):
        t = jnp.transpose(cc, (0, 2, 1)).reshape(nchunk, gc, P, C)
        return jnp.einsum('ngpc,gh->ngphc', t, eye).reshape(nchunk, gc * P, gc * C)

    bblk = jnp.concatenate([blockdiag_in(bbr), blockdiag_in(bbi)], axis=-1).astype(BF16)
    cblk = jnp.concatenate([blockdiag_out(C_re), blockdiag_out(-C_im)], axis=1).astype(BF16)

    def rows(v):
        return jnp.broadcast_to(v.reshape(nchunk, 1, gc * P), (nchunk, SUBLANE, gc * P))

    lam = jnp.stack([rows(lbr), rows(lbi)], axis=1)
    pr, pi = _cpow(lbr, lbi, seg)
    lamseg = jnp.stack([rows(pr), rows(pi)], axis=1)
    return (bblk, cblk, lam, lamseg, D.reshape(1, G * C), glu_w.astype(BF16), glu_b.reshape(1, -1))


def _split3(x):
    h = x.astype(BF16)
    r = x - h.astype(F32)
    m = r.astype(BF16)
    l = (r - m.astype(F32)).astype(BF16)
    return h, m, l


def _dot_f32(a, b_bf16):
    h, m, l = _split3(a)
    dot = lambda p: jnp.dot(p, b_bf16, preferred_element_type=F32)
    return dot(h) + dot(m) + dot(l)


def _ssd_body(z_ref, xbc_ref, dt_ref, cw_ref, cb_ref, dtb_ref, a_ref, dx_ref, nw_ref, e_ref,
              o_ref, state_ref, xpad_ref, *, nheads, hd, ns, ngroups):
    Q = z_ref.shape[0]
    width = nheads * hd
    gw = width // ngroups
    halo = SUBLANE

    @pl.when(pl.program_id(1) == 0)
    def _():
        state_ref[...] = jnp.zeros_like(state_ref)
        xpad_ref[0:halo, :] = jnp.zeros((halo, xpad_ref.shape[1]), F32)

    xpad_ref[halo:halo + Q, :] = xbc_ref[...]
    acc = cb_ref[...] + cw_ref[0:1, :] * xpad_ref[halo - 3:halo - 3 + Q, :]
    for k in range(1, SSD_CONV):
        acc += cw_ref[k:k + 1, :] * xpad_ref[halo - 3 + k:halo - 3 + k + Q, :]
    xpad_ref[0:halo, :] = xpad_ref[Q:Q + halo, :]
    xc = jax.nn.silu(acc)
    xs = xc[:, 0:width]
    bm = xc[:, width:width + ngroups * ns].astype(BF16)
    cm = xc[:, width + ngroups * ns:].astype(BF16)

    lane = lax.broadcasted_iota(jnp.int32, (Q, LANE), 1)
    dt = jnp.where(lane < nheads, jax.nn.softplus(dt_ref[...] + dtb_ref[...]), 0.0)
    a = dt * a_ref[...]
    rowi = lax.broadcasted_iota(jnp.int32, (Q, Q), 0)
    coli = lax.broadcasted_iota(jnp.int32, (Q, Q), 1)
    tril = coli <= rowi
    acs = _dot_f32_lhs(tril.astype(BF16), a)
    acs_t = acs.T
    last = acs[Q - 1:Q, :]
    emat = e_ref[...]
    dt_x = _expand(dt, emat)
    dec_x = _expand(jnp.exp(acs), emat)
    dte_x = _expand(jnp.exp(last - acs), emat)
    dlast_x = dec_x[Q - 1:Q, :]

    xd = xs * dt_x
    xd_b = xd.astype(BF16)
    xw_b = (xd * dte_x).astype(BF16)
    st = state_ref[...]
    st_b = st.astype(BF16)

    y_parts = []
    new_state = []
    hlane = lax.broadcasted_iota(jnp.int32, (Q, LANE), 1) < hd
    heads_per_group = nheads // ngroups
    for g in range(ngroups):
        cg = cm[:, g * ns:(g + 1) * ns]
        bg = bm[:, g * ns:(g + 1) * ns]
        cb = lax.dot_general(cg, bg, (((1,), (1,)), ((), ())), preferred_element_type=F32)
        y_off = jnp.dot(cg, st_b[:, g * gw:(g + 1) * gw], preferred_element_type=F32)
        pair_out = []
        for pr in range(gw // LANE):
            xpair = xd_b[:, g * gw + pr * LANE: g * gw + (pr + 1) * LANE]
            res = []
            for hh in range(LANE // hd):
                h = g * heads_per_group + pr * (LANE // hd) + hh
                diff = acs[:, h:h + 1] - acs_t[h:h + 1, :]
                s_h = jnp.where(tril, cb * jnp.exp(jnp.minimum(diff, 0.0)), 0.0).astype(BF16)
                res.append(jnp.dot(s_h, xpair, preferred_element_type=F32))
            pair_out.append(jnp.where(hlane, res[0], res[1]))
        y_diag = jnp.concatenate(pair_out, axis=1)
        y_parts.append(y_diag + y_off * dec_x[:, g * gw:(g + 1) * gw])
        upd = lax.dot_general(bg, xw_b[:, g * gw:(g + 1) * gw], (((0,), (0,)), ((), ())),
                              preferred_element_type=F32)
        new_state.append(st[:, g * gw:(g + 1) * gw] * dlast_x[:, g * gw:(g + 1) * gw] + upd)
    state_ref[...] = jnp.concatenate(new_state, axis=1)

    y = jnp.concatenate(y_parts, axis=1) + xs * dx_ref[...]
    gt = y * jax.nn.silu(z_ref[...])
    ms = jnp.mean(gt * gt, axis=-1, keepdims=True)
    o_ref[...] = (gt * lax.rsqrt(ms + EPS) * nw_ref[...]).astype(BF16)


def _dot_f32_lhs(a_bf16, b):
    h, m, l = _split3(b)
    dot = lambda p: jnp.dot(a_bf16, p, preferred_element_type=F32)
    return dot(h) + dot(m) + dot(l)


def _expand(v, emat):
    return _dot_f32(v, emat)


def _ssd_mixer(proj, B, L, z_blk, xbc_blk, dt_blk, prm, emat, layer, Q):
    cw, cb, dtb, avec, dx, nw = prm
    width = nw.shape[-1]
    cdim = cw.shape[-1]
    nheads = width // SSD_HEAD_DIM
    nt = L // Q
    body = functools.partial(_ssd_body, nheads=nheads, hd=SSD_HEAD_DIM, ns=SSD_STATE, ngroups=SSD_GROUPS)
    full = lambda a: _layer_spec(a, layer)
    return pl.pallas_call(
        body,
        grid=(B, nt),
        in_specs=[pl.BlockSpec((Q, width), lambda b, t: (b * nt + t, z_blk)),
                  pl.BlockSpec((Q, cdim), lambda b, t: (b * nt + t, xbc_blk)),
                  pl.BlockSpec((Q, LANE), lambda b, t: (b * nt + t, dt_blk)),
                  full(cw), full(cb), full(dtb), full(avec), full(dx), full(nw),
                  pl.BlockSpec(emat.shape, lambda b, t: (0, 0))],
        out_specs=pl.BlockSpec((Q, width), lambda b, t: (b * nt + t, 0)),
        out_shape=jax.ShapeDtypeStruct((B * L, width), BF16),
        scratch_shapes=[pltpu.VMEM((SSD_STATE, width), F32),
                        pltpu.VMEM((Q + SUBLANE, cdim), F32)],
        compiler_params=_cparams(("parallel", "arbitrary")),
        name="ssd_mixer",
    )(proj, proj, proj, cw, cb, dtb, avec, dx, nw, emat)


def _ssd_params(conv_w, conv_b, dt_bias, A_log, D, norm_w):
    depth, nheads = dt_bias.shape
    width = nheads * SSD_HEAD_DIM
    pad = lambda v: jnp.pad(v, ((0, 0), (0, LANE - nheads))).reshape(depth, 1, LANE)
    return (conv_w, conv_b.reshape(depth, 1, -1), pad(dt_bias), pad(-jnp.exp(A_log)),
            jnp.repeat(D, SSD_HEAD_DIM, axis=1).reshape(depth, 1, width), norm_w.reshape(depth, 1, width))


def _head_expansion(nheads):
    e = np.zeros((LANE, nheads * SSD_HEAD_DIM), np.float32)
    e[np.repeat(np.arange(nheads), SSD_HEAD_DIM), np.arange(nheads * SSD_HEAD_DIM)] = 1.0
    return jnp.asarray(e, BF16)


def _float_key(x):
    b = lax.bitcast_convert_type(x, jnp.int32)
    return b ^ ((b >> 31) & 0x7FFFFFFF)


def _place(blk, lane0, width, dst):
    lane = lax.broadcasted_iota(jnp.int32, blk.shape, 1)
    shift = (dst - lane0) % LANE
    r = pltpu.roll(blk, shift, axis=1) if shift else blk
    lo = jnp.where((lane >= dst) & (lane < min(dst + width, LANE)), r, 0.0)
    hi = jnp.where(lane < dst + width - LANE, r, 0.0)
    return jnp.concatenate([lo, hi], axis=1)


def _head_offsets(offs, nheads, hd):
    return sorted({(offs[f] + h * hd) % LANE for f in ('q', 'qi') for h in range(nheads)})


def _lane_window(x, start, width):
    off = start % LANE
    base = start - off
    if off == 0:
        return x[:, start:start + width]
    lane = lax.broadcasted_iota(jnp.int32, (x.shape[0], LANE), 1)
    out = []
    for p in range(width // LANE):
        a = pltpu.roll(x[:, base + p * LANE:base + (p + 1) * LANE], LANE - off, axis=1)
        b = pltpu.roll(x[:, base + (p + 1) * LANE:base + (p + 2) * LANE], LANE - off, axis=1)
        out.append(jnp.where(lane < LANE - off, a, b))
    return jnp.concatenate(out, axis=1)


def _dsa_body(qx_ref, kv_ref, ki_ref, band_ref, far_ref, o_ref,
              kpad_ref, kipad_ref, vt_ref, qop_ref, key_ref, hi_ref, lo_ref, lom_ref, lg_ref, bound_ref,
              m_ref, l_ref, acc_ref,
              *, nheads, hd, topk, offs):
    i = pl.program_id(1)
    TQ = qx_ref.shape[0]
    KB = TQ
    L = kv_ref.shape[0]
    nkb_total = L // KB
    NT = (((1,), (1,)), ((), ()))
    WIN = kpad_ref.shape[2]

    def head_window(field_lo, h):
        lo = field_lo + h * hd
        return (lo // LANE) * LANE, lo % LANE

    head_offs = _head_offsets(offs, nheads, hd)

    @pl.when(i == 0)
    def _():
        for kb in range(nkb_total):
            r = slice(kb * KB, (kb + 1) * KB)
            kgrp = (offs['k'] // LANE) * LANE
            kblk = kv_ref[r, kgrp:kgrp + LANE]
            kiblk = ki_ref[r, :]
            for n, o in enumerate(head_offs):
                kpad_ref[n, r, :] = _place(kblk, offs['k'] - kgrp, hd, o)[:, :WIN].astype(BF16)
                kipad_ref[n, r, :] = _place(kiblk, offs['ki'], hd, o)[:, :WIN].astype(BF16)
            vt_ref[:, r] = kv_ref[r, :].T[offs['v']:offs['v'] + hd, :].astype(BF16)

    def stage_windows(slot, field_lo, scale):
        bases = sorted({head_window(field_lo, h)[0] for h in range(nheads)})
        for n, base in enumerate(bases):
            qop_ref[slot, n] = (qx_ref[:, base:base + WIN] * scale).astype(BF16)
        return [(bases.index(head_window(field_lo, h)[0]), head_offs.index(head_window(field_lo, h)[1]))
                for h in range(nheads)]

    qs = stage_windows(0, offs['q'], hd ** -0.5)
    qis = stage_windows(1, offs['qi'], 1.0)
    wbase = (offs['w'] // LANE) * LANE
    wt = qx_ref[:, wbase:wbase + LANE].T * ((nheads * IDX_DIM) ** -0.5)
    w_rows = [wt[offs['w'] - wbase + h:offs['w'] - wbase + h + 1, :] for h in range(nheads)]

    RB = LANE
    I16 = jnp.int16
    HALF = 1 << 15

    def rows(kb, r=0, n=KB):
        return pl.ds(pl.multiple_of(kb * KB + r, SUBLANE), n)

    PART = m_ref.shape[1]

    def fold_sum(x):
        return jnp.sum(x.reshape(x.shape[0] // PART, PART, TQ), axis=0)

    def fold_max(x):
        return jnp.max(x.reshape(x.shape[0] // PART, PART, TQ), axis=0)

    def idx_keys(kb, diagonal):
        for r in range(0, KB, RB):
            s = jnp.zeros((RB, TQ), F32)
            for h in range(nheads):
                win, ko = qis[h]
                lg = lax.dot_general(kipad_ref[ko, rows(kb, r, RB), :], qop_ref[1, win], NT,
                                     preferred_element_type=F32)
                s += jnp.maximum(lg, 0.0) * w_rows[h]
            key = _float_key(s)
            if diagonal:
                krow = lax.broadcasted_iota(jnp.int32, (RB, TQ), 0) + r
                qcol = lax.broadcasted_iota(jnp.int32, (RB, TQ), 1)
                key = jnp.where((krow // CHUNK) <= (qcol // CHUNK), key, INT_MIN)
            key_ref[rows(kb, r, RB), :] = key
            hi_ref[rows(kb, r, RB), :] = (key >> 16).astype(I16)
            lo_ref[rows(kb, r, RB), :] = ((key & (2 * HALF - 1)) - HALF).astype(I16)

    def far_keys(kb, _):
        idx_keys(kb, False)
        return 0

    lax.fori_loop(0, i, far_keys, 0)
    idx_keys(i, True)

    def count(preds):
        def body(kb, cs):
            blk = key_ref[rows(kb), :]
            return tuple(c + fold_sum(jnp.where(p(blk, kb), 1.0, 0.0)) for c, p in zip(cs, preds))
        z = jnp.zeros((PART, TQ), F32)
        cs = lax.fori_loop(0, i + 1, body, (z,) * len(preds))
        return [jnp.sum(c, axis=0, keepdims=True) for c in cs]

    PACK = 2 * SUBLANE
    P16 = 4 * PACK

    def rows16(x):
        return x.reshape(x.shape[0] // PACK, PACK, TQ)

    def as16(v):
        return jnp.broadcast_to(v, (PACK, TQ)).astype(I16)

    def count16(src_ref, pred):
        def body(kb, c):
            ind = jnp.where(pred(rows16(src_ref[rows(kb), :])), I16(1), I16(0)).reshape(KB // P16, P16, TQ)
            for n in range(KB // P16):
                c = c + ind[n]
            return c
        c = lax.fori_loop(0, i + 1, body, jnp.zeros((P16, TQ), I16))
        return jnp.sum(c.astype(F32), axis=0, keepdims=True)

    def bisect16(src_ref, target):
        def step(s, tu):
            cand_u = tu | jnp.left_shift(jnp.int32(1), 15 - s)
            cand = as16(cand_u - HALF)[None]
            cnt = count16(src_ref, lambda blk: blk >= cand)
            return jnp.where(cnt >= target, cand_u, tu)
        return lax.fori_loop(0, 16, step, jnp.zeros((1, TQ), jnp.int32))

    thr_hi = bisect16(hi_ref, float(topk)) - HALF
    thr_hi16 = as16(thr_hi)[None]
    need_lo = topk - count16(hi_ref, lambda blk: blk > thr_hi16)

    def mask_low_halves(kb, _):
        keep_lo = rows16(hi_ref[rows(kb), :]) == thr_hi16
        lom_ref[rows(kb), :] = jnp.where(keep_lo, rows16(lo_ref[rows(kb), :]), I16(-HALF)).reshape(KB, TQ)
        return 0

    lax.fori_loop(0, i + 1, mask_low_halves, 0)
    thr = thr_hi * (2 * HALF) + bisect16(lom_ref, need_lo)

    cgt, ceq = count([lambda blk, kb: blk > thr, lambda blk, kb: blk == thr])
    need = topk - cgt
    nbits = int(L).bit_length()
    bound_ref[...] = jnp.full((1, TQ), 2 ** nbits - 1, jnp.int32)

    @pl.when(jnp.max(ceq - need) > 0.0)
    def _():
        krow = lax.broadcasted_iota(jnp.int32, (KB, TQ), 0)

        def bisect_idx(step, bnd):
            bit = jnp.left_shift(jnp.int32(1), nbits - 1 - step)
            cand = bnd | bit
            cnt, = count([lambda blk, kb: (blk == thr) & (krow + kb * KB < cand)])
            return jnp.where(cnt <= need, cand, bnd)

        bound_ref[...] = lax.fori_loop(0, nbits, bisect_idx, jnp.zeros((1, TQ), jnp.int32))

    bound = bound_ref[...]

    m_ref[...] = jnp.full(m_ref.shape, NEG, F32)

    def store_logits(kb, bias_of_head):
        for r in range(0, KB, RB):
            blk = key_ref[rows(kb, r, RB), :]
            pos = lax.broadcasted_iota(jnp.int32, (RB, TQ), 0) + (kb * KB + r)
            keep = ((blk > thr) | ((blk == thr) & (pos < bound))) & (blk != INT_MIN)
            mask = jnp.where(keep, 0.0, NEG)
            for h in range(nheads):
                win, ko = qs[h]
                lg = lax.dot_general(kpad_ref[ko, rows(kb, r, RB), :], qop_ref[0, win], NT,
                                     preferred_element_type=F32)
                lg = lg + bias_of_head(h, r) + mask
                lg_ref[h, rows(kb, r, RB), :] = lg
                m_ref[h] = jnp.maximum(m_ref[h], fold_max(lg))

    def far_logits(kb, _):
        store_logits(kb, lambda h, r: far_ref[h])
        return 0

    lax.fori_loop(0, i - 1, far_logits, 0)

    @pl.when(i >= 1)
    def _():
        store_logits(i - 1, lambda h, r: band_ref[h, r:r + RB, :])

    store_logits(i, lambda h, r: band_ref[h, KB + r:KB + r + RB, :])

    m_fin = [jnp.max(m_ref[h], axis=0, keepdims=True) for h in range(nheads)]
    acc_ref[...] = jnp.zeros(acc_ref.shape, F32)
    l_ref[...] = jnp.zeros(l_ref.shape, F32)

    def accumulate(kb, _):
        vt = vt_ref[:, rows(kb)]
        for h in range(nheads):
            p = jnp.exp(lg_ref[h, rows(kb), :] - m_fin[h])
            l_ref[h] += fold_sum(p)
            acc_ref[h] += jnp.dot(vt, p.astype(BF16), preferred_element_type=F32)
        return 0

    lax.fori_loop(0, i + 1, accumulate, 0)

    outs = [acc_ref[h] / jnp.sum(l_ref[h], axis=0, keepdims=True) for h in range(nheads)]
    out = jnp.concatenate(outs, axis=0).T
    z = _lane_window(qx_ref, offs['z'], nheads * hd)
    o_ref[...] = (out * jax.nn.silu(z)).astype(BF16)


def _dsa_mixer(proj, B, L, src, band, far, nheads, TQ):
    width = nheads * ATT_HEAD_DIM
    nq = L // TQ
    hd = ATT_HEAD_DIM
    qx_w = 16 * LANE
    qx_blk = src['att_q'][0] // qx_w
    qx_lo = qx_blk * qx_w
    kv_w = 2 * LANE
    kv_blk = src['att_k'][0] // kv_w
    ki_blk = src['idx_k'][0] // LANE
    offs = {'q': src['att_q'][0] - qx_lo, 'qi': src['idx_q'][0] - qx_lo, 'w': src['idx_w'][0] - qx_lo,
            'z': src['att_z'][0] - qx_lo, 'k': src['att_k'][0] - kv_blk * kv_w,
            'v': src['att_v'][0] - kv_blk * kv_w, 'ki': src['idx_k'][0] - ki_blk * LANE}
    assert offs['z'] + width + LANE <= qx_w and qx_lo + qx_w <= proj.shape[1]
    assert offs['k'] % LANE + hd <= LANE and offs['ki'] + hd <= LANE
    assert offs['v'] % SUBLANE == 0 and offs['v'] + hd <= kv_w
    assert offs['w'] // LANE == (offs['w'] + nheads - 1) // LANE
    win_off = _head_offsets(offs, nheads, hd)
    win = LANE if max(win_off) + hd <= LANE else 2 * LANE
    win_base = max((sorted({(offs[f] + h * hd) // LANE for h in range(nheads)}) for f in ('q', 'qi')), key=len)
    body = functools.partial(_dsa_body, nheads=nheads, hd=hd, topk=min(TOPK, L // 4), offs=offs)
    full = lambda a: pl.BlockSpec(a.shape, lambda b, i: (0,) * a.ndim)
    return pl.pallas_call(
        body,
        grid=(B, nq),
        in_specs=[pl.BlockSpec((TQ, qx_w), lambda b, i: (b * nq + i, qx_blk)),
                  pl.BlockSpec((L, kv_w), lambda b, i: (b, kv_blk)),
                  pl.BlockSpec((L, LANE), lambda b, i: (b, ki_blk)),
                  full(band), full(far)],
        out_specs=pl.BlockSpec((TQ, width), lambda b, i: (b * nq + i, 0)),
        out_shape=jax.ShapeDtypeStruct((B * L, width), BF16),
        scratch_shapes=[pltpu.VMEM((len(win_off), L, win), BF16),
                        pltpu.VMEM((len(win_off), L, win), BF16),
                        pltpu.VMEM((ATT_HEAD_DIM, L), BF16),
                        pltpu.VMEM((2, len(win_base), TQ, win), BF16),
                        pltpu.VMEM((L, TQ), jnp.int32),
                        pltpu.VMEM((L, TQ), jnp.int16),
                        pltpu.VMEM((L, TQ), jnp.int16),
                        pltpu.VMEM((L, TQ), jnp.int16),
                        pltpu.VMEM((nheads, L, TQ), F32),
                        pltpu.VMEM((1, TQ), jnp.int32),
                        pltpu.VMEM((nheads, 4 * SUBLANE, TQ), F32),
                        pltpu.VMEM((nheads, 4 * SUBLANE, TQ), F32),
                        pltpu.VMEM((nheads, ATT_HEAD_DIM, TQ), F32)],
        compiler_params=_cparams(("parallel", "arbitrary")),
        name="dsa_mixer",
    )(proj, proj, proj, band, far)


def _t5_bucket_static(rel):
    nb = N_BUCKETS // 2
    max_exact = nb // 2
    ret = np.where(rel > 0, nb, 0)
    n = np.abs(rel)
    nf = np.maximum(n, 1).astype(np.float64)
    large = max_exact + (np.log(nf / max_exact) / math.log(MAX_DISTANCE / max_exact)
                         * (nb - max_exact)).astype(np.int32)
    large = np.minimum(large, nb - 1)
    return ret + np.where(n < max_exact, n, large)


def _dsa_bias_tables(rel_bias, TQ):
    a = np.arange(2 * TQ)[:, None]
    j = np.arange(TQ)[None, :]
    band_idx = _t5_bucket_static(a - TQ - j)
    idx = jnp.asarray(band_idx, jnp.int32)[None]
    band = jnp.zeros((rel_bias.shape[1],) + band_idx.shape, F32)
    for bucket in np.unique(band_idx):
        band = jnp.where(idx == int(bucket), rel_bias[int(bucket)][:, None, None], band)
    far_bucket = int(_t5_bucket_static(np.array([-(TQ + 1)]))[0])
    assert far_bucket == int(_t5_bucket_static(np.array([-(10 ** 6)]))[0])
    far = jnp.broadcast_to(rel_bias[far_bucket][:, None, None], (rel_bias.shape[1], 1, TQ))
    return band, far


def _layout(d_model):
    s5w = d_model // 4
    ssdw = d_model // 2
    attw = d_model // 4
    cdim = ssdw + 2 * SSD_GROUPS * SSD_STATE
    nh_ssd = ssdw // SSD_HEAD_DIM
    nh_att = attw // ATT_HEAD_DIM
    splits = (s5w, s5w, ssdw, cdim, nh_ssd, attw, ATT_HEAD_DIM, ATT_HEAD_DIM,
              nh_att * IDX_DIM, IDX_DIM, nh_att, attw)
    names = ('s5_u', 's5_z', 'ssd_z', 'ssd_xbc', 'ssd_dt', 'att_q', 'att_k', 'att_v',
             'idx_q', 'idx_k', 'idx_w', 'att_z')
    src, o = {}, 0
    for nme, s in zip(names, splits):
        src[nme] = (o, s)
        o += s
    return src, o


def kernel(x, norm_w, w_in, s5_A_re, s5_A_im, s5_log_dt, s5_B_re, s5_B_im, s5_C_re, s5_C_im, s5_D, s5_glu_w, s5_glu_b, ssd_conv_w, ssd_conv_b, ssd_dt_bias, ssd_A_log, ssd_D, ssd_norm_w, rel_bias, w_out, final_norm_w):
    B, L, d = x.shape
    depth = w_in.shape[0]
    src, total = _layout(d)
    s5w, ssdw, attw = d // 4, d // 2, d // 4
    cdim = ssdw + 2 * SSD_GROUPS * SSD_STATE
    nh_att = attw // ATT_HEAD_DIM
    assert src['s5_u'][0] % s5w == 0 and src['s5_z'][0] % s5w == 0 and src['ssd_z'][0] % ssdw == 0
    assert src['ssd_xbc'][0] % cdim == 0 and src['ssd_dt'][0] % LANE == 0

    tn = 1024
    cuts, shift, padded = [(0, 0)], 0, {}
    for nme, (so, sz) in sorted(src.items(), key=lambda kv: kv[1][0]):
        gap = -(so + shift) % LANE if nme in ('att_q', 'att_k', 'idx_q', 'idx_k', 'att_z') else 0
        if gap:
            shift += gap
            cuts.append((so, shift))
        padded[nme] = (so + shift, sz)
    src = padded
    total_pad = -(-(total + shift) // tn) * tn
    segs = tuple((lo, cuts[n + 1][0] if n + 1 < len(cuts) else total, lo + sh) for n, (lo, sh) in enumerate(cuts))
    w_in_p = _relayout_w_in(jnp.swapaxes(w_in, 1, 2), segs, total_pad, min(256, d))
    w_out_b = w_out.astype(BF16)

    TQ = 256
    Q = 128
    band, far = _dsa_bias_tables(rel_bias, TQ)
    tm_in = min(1024, B * L)
    tm_out = min(256, B * L)
    nchunk = 2
    seg = L // SUBLANE

    s5p = jax.vmap(lambda *a: _s5_params(*a, seg, nchunk))(
        s5_A_re, s5_A_im, s5_log_dt, s5_B_re, s5_B_im, s5_C_re, s5_C_im, s5_D, s5_glu_w, s5_glu_b)
    ssdp = _ssd_params(ssd_conv_w, ssd_conv_b, ssd_dt_bias, ssd_A_log, ssd_D, ssd_norm_w)
    emat = _head_expansion(ssdw // SSD_HEAD_DIM)
    norm_w3 = norm_w.reshape(depth, 1, d)

    x2 = x.reshape(B * L, d)
    for l in range(depth):
        proj = _inproj(x2, norm_w3, w_in_p, l, tm_in, tn)
        y_s5 = _s5_mixer(proj, B, L, src['s5_u'][0] // s5w, src['s5_z'][0] // s5w, s5w, s5p, l)
        y_ssd = _ssd_mixer(proj, B, L, src['ssd_z'][0] // ssdw, src['ssd_xbc'][0] // cdim,
                           src['ssd_dt'][0] // LANE, ssdp, emat, l, Q)
        y_att = _dsa_mixer(proj, B, L, src, band, far, nh_att, TQ)
        x2 = _outproj(y_s5, y_ssd, y_att, w_out_b, l, x2, final_norm_w.reshape(1, d), tm_out,
                      final=(l == depth - 1))
    return x2.reshape(B, L, d)
```

```python
import functools
import math

import numpy as np
import jax
import jax.numpy as jnp
from jax import lax
from jax.experimental import pallas as pl
from jax.experimental.pallas import tpu as pltpu

F32 = jnp.float32
BF16 = jnp.bfloat16

EPS = 1e-6
CHUNK = 64

S5_GROUP = 16
S5_STATE = 64
SSD_HEAD_DIM = 64
SSD_STATE = 128
SSD_GROUPS = 4
SSD_CONV = 4
ATT_HEAD_DIM = 64
IDX_DIM = 64
TOPK = 256
N_BUCKETS = 32
MAX_DISTANCE = 128

LANE = 128
SUBLANE = 8
VMEM_LIMIT = 56 * 1024 * 1024

INT_MIN = -2 ** 31
NEG = -1e30


def _cparams(sem):
    return pltpu.CompilerParams(dimension_semantics=sem, vmem_limit_bytes=VMEM_LIMIT)


def _inproj_body(x_ref, nw_ref, w_ref, o_ref, h_ref):
    @pl.when(pl.program_id(1) == 0)
    def _():
        xf = x_ref[...]
        ms = jnp.mean(xf * xf, axis=-1, keepdims=True)
        h_ref[...] = (xf * lax.rsqrt(ms + EPS) * nw_ref[...]).astype(BF16)

    o_ref[...] = lax.dot_general(h_ref[...], w_ref[...], (((1,), (1,)), ((), ())),
                                 preferred_element_type=F32)


def _inproj(x2, nw, w, layer, tm, tn):
    m, d = x2.shape
    n = w.shape[1]
    return pl.pallas_call(
        _inproj_body,
        grid=(m // tm, n // tn),
        in_specs=[pl.BlockSpec((tm, d), lambda i, j: (i, 0)),
                  pl.BlockSpec((None, 1, d), lambda i, j: (layer, 0, 0)),
                  pl.BlockSpec((None, tn, d), lambda i, j: (layer, j, 0))],
        out_specs=pl.BlockSpec((tm, tn), lambda i, j: (i, j)),
        out_shape=jax.ShapeDtypeStruct((m, n), F32),
        scratch_shapes=[pltpu.VMEM((tm, d), BF16)],
        compiler_params=_cparams(("parallel", "arbitrary")),
        name="inproj",
    )(x2, nw, w)


def _relayout_body(w_ref, o_ref, *, segs):
    pack = 2 * SUBLANE
    pos = 0
    for src_lo, src_hi, dst_lo in segs:
        if dst_lo > pos:
            o_ref[pos:dst_lo, :] = jnp.zeros((dst_lo - pos, o_ref.shape[1]), BF16)
        val = w_ref[src_lo:src_hi, :]
        fill = -(src_hi - src_lo) % pack
        if fill:
            val = jnp.concatenate([val, jnp.zeros((fill, val.shape[1]), F32)], axis=0)
        o_ref[dst_lo:dst_lo + val.shape[0], :] = val.astype(BF16)
        pos = dst_lo + val.shape[0]
    if pos < o_ref.shape[0]:
        o_ref[pos:, :] = jnp.zeros((o_ref.shape[0] - pos, o_ref.shape[1]), BF16)


def _relayout_w_in(w_t, segs, total_pad, tk):
    depth, n, d = w_t.shape
    assert all(lo % SUBLANE == 0 and (hi - lo) % SUBLANE == 0 and dst % (2 * SUBLANE) == 0
               for lo, hi, dst in segs)
    return pl.pallas_call(
        functools.partial(_relayout_body, segs=segs),
        grid=(depth, d // tk),
        in_specs=[pl.BlockSpec((None, n, tk), lambda l, i: (l, 0, i))],
        out_specs=pl.BlockSpec((None, total_pad, tk), lambda l, i: (l, 0, i)),
        out_shape=jax.ShapeDtypeStruct((depth, total_pad, d), BF16),
        compiler_params=_cparams(("parallel", "parallel")),
        name="w_in_relayout",
    )(w_t)


def _outproj_body(ys5_ref, yssd_ref, yatt_ref, w_ref, x_ref, fnw_ref, o_ref, *, w5, wssd, final):
    acc = x_ref[...]
    acc += jnp.dot(ys5_ref[...], w_ref[0:w5, :], preferred_element_type=F32)
    acc += jnp.dot(yssd_ref[...], w_ref[w5:w5 + wssd, :], preferred_element_type=F32)
    acc += jnp.dot(yatt_ref[...], w_ref[w5 + wssd:, :], preferred_element_type=F32)
    if final:
        ms = jnp.mean(acc * acc, axis=-1, keepdims=True)
        acc = acc * lax.rsqrt(ms + EPS) * fnw_ref[...]
    o_ref[...] = acc


def _outproj(ys5, yssd, yatt, w, layer, x2, fnw, tm, final):
    m, d = x2.shape
    w5, wssd, watt = ys5.shape[1], yssd.shape[1], yatt.shape[1]
    body = functools.partial(_outproj_body, w5=w5, wssd=wssd, final=final)
    return pl.pallas_call(
        body,
        grid=(m // tm,),
        in_specs=[pl.BlockSpec((tm, w5), lambda i: (i, 0)),
                  pl.BlockSpec((tm, wssd), lambda i: (i, 0)),
                  pl.BlockSpec((tm, watt), lambda i: (i, 0)),
                  pl.BlockSpec((None,) + w.shape[1:], lambda i: (layer, 0, 0)),
                  pl.BlockSpec((tm, d), lambda i: (i, 0)),
                  pl.BlockSpec((1, d), lambda i: (0, 0))],
        out_specs=pl.BlockSpec((tm, d), lambda i: (i, 0)),
        out_shape=jax.ShapeDtypeStruct((m, d), F32),
        compiler_params=_cparams(("parallel",)),
        name="outproj",
    )(ys5, yssd, yatt, w, x2, fnw)


def _shift_down_one(x):
    rolled = pltpu.roll(x, 1, axis=0)
    row = lax.broadcasted_iota(jnp.int32, x.shape, 0)
    return jnp.where(row == 0, 0.0, rolled)


def _s5_body(u_ref, z_ref, bblk_ref, cblk_ref, lam_ref, lamseg_ref, d_ref, gw_ref, gb_ref,
             o_ref, uperm_ref, xs_ref, yperm_ref, *, nchunk, cw, sw, rt):
    L = u_ref.shape[0]
    seg = L // SUBLANE

    for c in range(nchunk):
        for j in range(SUBLANE):
            for k in range(cw // LANE):
                lo = c * cw + k * LANE
                uperm_ref[k, pl.ds(j, seg, stride=SUBLANE), :] = u_ref[pl.ds(j * seg, seg), lo:lo + LANE]
        for r0 in range(0, L, rt):
            up = jnp.concatenate([uperm_ref[k, r0:r0 + rt, :] for k in range(cw // LANE)], axis=1)
            xs_ref[r0:r0 + rt, :] = jnp.dot(up.astype(BF16), bblk_ref[c], preferred_element_type=F32)
        lr = lam_ref[c, 0]
        li = lam_ref[c, 1]

        def scan_step(tau, carry):
            xr, xi = carry
            row = pl.multiple_of(tau * SUBLANE, SUBLANE)
            nxr = lr * xr - li * xi + xs_ref[pl.ds(row, SUBLANE), 0:sw]
            nxi = lr * xi + li * xr + xs_ref[pl.ds(row, SUBLANE), sw:2 * sw]
            xs_ref[pl.ds(row, SUBLANE), 0:sw] = nxr
            xs_ref[pl.ds(row, SUBLANE), sw:2 * sw] = nxi
            return nxr, nxi

        zero = jnp.zeros((SUBLANE, sw), F32)
        er, ei = lax.fori_loop(0, seg, scan_step, (zero, zero))

        sr = lamseg_ref[c, 0]
        si = lamseg_ref[c, 1]
        cr, ci = zero, zero
        for _ in range(SUBLANE - 1):
            tr = er + (sr * cr - si * ci)
            ti = ei + (sr * ci + si * cr)
            cr, ci = _shift_down_one(tr), _shift_down_one(ti)

        def fix_step(tau, carry):
            fr, fi = carry
            nfr = lr * fr - li * fi
            nfi = lr * fi + li * fr
            row = pl.multiple_of(tau * SUBLANE, SUBLANE)
            xs_ref[pl.ds(row, SUBLANE), 0:sw] += nfr
            xs_ref[pl.ds(row, SUBLANE), sw:2 * sw] += nfi
            return nfr, nfi

        lax.fori_loop(0, seg, fix_step, (cr, ci))

        for r0 in range(0, L, rt):
            yc = jnp.dot(xs_ref[r0:r0 + rt, :].astype(BF16), cblk_ref[c], preferred_element_type=F32)
            for k in range(cw // LANE):
                yperm_ref[c * (cw // LANE) + k, r0:r0 + rt, :] = yc[:, k * LANE:(k + 1) * LANE]

    for j in range(SUBLANE):
        y = jnp.concatenate([yperm_ref[k, pl.ds(j, seg, stride=SUBLANE), :]
                             for k in range(yperm_ref.shape[0])], axis=1)
        y = y + d_ref[...] * u_ref[j * seg:(j + 1) * seg, :]
        y = jax.nn.gelu(y)
        g = jnp.dot(y.astype(BF16), gw_ref[...], preferred_element_type=F32) + gb_ref[...]
        y = y * jax.nn.sigmoid(g)
        o_ref[j * seg:(j + 1) * seg, :] = (y * jax.nn.silu(z_ref[j * seg:(j + 1) * seg, :])).astype(BF16)


def _layer_spec(a, layer):
    return pl.BlockSpec((None,) + a.shape[1:], lambda *_: (layer,) + (0,) * (a.ndim - 1))


def _s5_mixer(proj, B, L, u_blk, z_blk, width, prm, layer):
    bblk, cblk, lam, lamseg, dvec, gw, gb = prm
    nchunk, cw, sw2 = bblk.shape[1:]
    sw = sw2 // 2
    rt = min(512, L)
    body = functools.partial(_s5_body, nchunk=nchunk, cw=cw, sw=sw, rt=rt)
    full = lambda a: _layer_spec(a, layer)
    return pl.pallas_call(
        body,
        grid=(B,),
        in_specs=[pl.BlockSpec((L, width), lambda b: (b, u_blk)),
                  pl.BlockSpec((L, width), lambda b: (b, z_blk)),
                  full(bblk), full(cblk), full(lam), full(lamseg), full(dvec), full(gw), full(gb)],
        out_specs=pl.BlockSpec((L, width), lambda b: (b, 0)),
        out_shape=jax.ShapeDtypeStruct((B * L, width), BF16),
        scratch_shapes=[pltpu.VMEM((cw // LANE, L, LANE), F32),
                        pltpu.VMEM((L, 2 * sw), F32),
                        pltpu.VMEM((width // LANE, L, LANE), F32)],
        compiler_params=_cparams(("parallel",)),
        name="s5_mixer",
    )(proj, proj, bblk, cblk, lam, lamseg, dvec, gw, gb)


def _cpow(re, im, n):
    rr, ri = jnp.ones_like(re), jnp.zeros_like(im)
    br, bi = re, im
    while n:
        if n & 1:
            rr, ri = rr * br - ri * bi, rr * bi + ri * br
        br, bi = br * br - bi * bi, 2.0 * br * bi
        n >>= 1
    return rr, ri


def _s5_params(A_re, A_im, log_dt, B_re, B_im, C_re, C_im, D, glu_w, glu_b, seg, nchunk):
    G, P, C = B_re.shape
    dt = jnp.exp(log_dt)[:, None]
    lre = jnp.minimum(A_re, -1e-4)
    lim = A_im
    mag = jnp.exp(lre * dt)
    lbr = mag * jnp.cos(lim * dt)
    lbi = mag * jnp.sin(lim * dt)
    nr, ni = lbr - 1.0, lbi
    den = lre * lre + lim * lim
    fr = (nr * lre + ni * lim) / den
    fi = (ni * lre - nr * lim) / den
    bbr = fr[..., None] * B_re - fi[..., None] * B_im
    bbi = fr[..., None] * B_im + fi[..., None] * B_re
    gc = G // nchunk
    eye = jnp.eye(gc, dtype=F32)

    def blockdiag_in(bb):
        t = jnp.transpose(bb, (0, 2, 1)).reshape(nchunk, gc, C, P)
        return jnp.einsum('ngcp,gh->ngchp', t, eye).reshape(nchunk, gc * C, gc * P)

    def blockdiag_out(cc):
        t = jnp.transpose(cc, (0, 2, 1)).reshape(nchunk, gc, P, C)
        return jnp.einsum('ngpc,gh->ngphc', t, eye).reshape(nchunk, gc * P, gc * C)

    bblk = jnp.concatenate([blockdiag_in(bbr), blockdiag_in(bbi)], axis=-1).astype(BF16)
    cblk = jnp.concatenate([blockdiag_out(C_re), blockdiag_out(-C_im)], axis=1).astype(BF16)

    def rows(v):
        return jnp.broadcast_to(v.reshape(nchunk, 1, gc * P), (nchunk, SUBLANE, gc * P))

    lam = jnp.stack([rows(lbr), rows(lbi)], axis=1)
    pr, pi = _cpow(lbr, lbi, seg)
    lamseg = jnp.stack([rows(pr), rows(pi)], axis=1)
    return (bblk, cblk, lam, lamseg, D.reshape(1, G * C), glu_w.astype(BF16), glu_b.reshape(1, -1))


def _split3(x):
    h = x.astype(BF16)
    r = x - h.astype(F32)
    m = r.astype(BF16)
    l = (r - m.astype(F32)).astype(BF16)
    return h, m, l


def _dot_f32(a, b3_bf16):
    return jnp.dot(jnp.concatenate(_split3(a), axis=1), b3_bf16, preferred_element_type=F32)


def _ssd_body(z_ref, xbc_ref, dt_ref, cw_ref, cb_ref, dtb_ref, a_ref, dx_ref, nw_ref, e_ref,
              o_ref, state_ref, xpad_ref, *, nheads, hd, ns, ngroups):
    Q = z_ref.shape[0]
    width = nheads * hd
    gw = width // ngroups
    halo = SUBLANE

    @pl.when(pl.program_id(1) == 0)
    def _():
        state_ref[...] = jnp.zeros_like(state_ref)
        xpad_ref[0:halo, :] = jnp.zeros((halo, xpad_ref.shape[1]), F32)

    xpad_ref[halo:halo + Q, :] = xbc_ref[...]
    xp = xpad_ref[...]
    acc = cw_ref[0:1, :] * xp
    for k in range(1, SSD_CONV):
        acc = cw_ref[k:k + 1, :] * xp + pltpu.roll(acc, 1, axis=0)
    xpad_ref[0:halo, :] = xpad_ref[Q:Q + halo, :]
    xc = jax.nn.silu(acc[halo:, :] + cb_ref[...])
    xs = xc[:, 0:width]
    bm = xc[:, width:width + ngroups * ns].astype(BF16)
    cm = xc[:, width + ngroups * ns:].astype(BF16)

    lane = lax.broadcasted_iota(jnp.int32, (Q, LANE), 1)
    dt = jnp.where(lane < nheads, jax.nn.softplus(dt_ref[...] + dtb_ref[...]), 0.0)
    a = dt * a_ref[...]
    rowi = lax.broadcasted_iota(jnp.int32, (Q, Q), 0)
    coli = lax.broadcasted_iota(jnp.int32, (Q, Q), 1)
    tril = coli <= rowi
    acs = _dot_f32_lhs(tril.astype(BF16), a)
    acs_t = acs.T
    last = acs[Q - 1:Q, :]
    emat = e_ref[...]
    dt_x = _expand(dt, emat)
    dec_x = _expand(jnp.exp(acs), emat)
    dte_x = _expand(jnp.exp(last - acs), emat)
    dlast_x = dec_x[Q - 1:Q, :]

    xd = xs * dt_x
    xd_b = xd.astype(BF16)
    xw_b = (xd * dte_x).astype(BF16)
    st = state_ref[...]
    st_b = st.astype(BF16)

    y_parts = []
    new_state = []
    hlane = lax.broadcasted_iota(jnp.int32, (Q, LANE), 1) < hd
    heads_per_group = nheads // ngroups
    for g in range(ngroups):
        cg = cm[:, g * ns:(g + 1) * ns]
        bg = bm[:, g * ns:(g + 1) * ns]
        cb = lax.dot_general(cg, bg, (((1,), (1,)), ((), ())), preferred_element_type=F32)
        y_off = jnp.dot(cg, st_b[:, g * gw:(g + 1) * gw], preferred_element_type=F32)
        pair_out = []
        for pr in range(gw // LANE):
            xpair = xd_b[:, g * gw + pr * LANE: g * gw + (pr + 1) * LANE]
            res = []
            for hh in range(LANE // hd):
                h = g * heads_per_group + pr * (LANE // hd) + hh
                diff = acs[:, h:h + 1] - acs_t[h:h + 1, :]
                s_h = jnp.where(tril, cb * jnp.exp(jnp.minimum(diff, 0.0)), 0.0).astype(BF16)
                res.append(jnp.dot(s_h, xpair, preferred_element_type=F32))
            pair_out.append(jnp.where(hlane, res[0], res[1]))
        y_diag = jnp.concatenate(pair_out, axis=1)
        y_parts.append(y_diag + y_off * dec_x[:, g * gw:(g + 1) * gw])
        upd = lax.dot_general(bg, xw_b[:, g * gw:(g + 1) * gw], (((0,), (0,)), ((), ())),
                              preferred_element_type=F32)
        new_state.append(st[:, g * gw:(g + 1) * gw] * dlast_x[:, g * gw:(g + 1) * gw] + upd)
    state_ref[...] = jnp.concatenate(new_state, axis=1)

    y = jnp.concatenate(y_parts, axis=1) + xs * dx_ref[...]
    gt = y * jax.nn.silu(z_ref[...])
    ms = jnp.mean(gt * gt, axis=-1, keepdims=True)
    o_ref[...] = (gt * lax.rsqrt(ms + EPS) * nw_ref[...]).astype(BF16)


def _dot_f32_lhs(a_bf16, b):
    return jnp.dot(jnp.concatenate([a_bf16] * 3, axis=1), jnp.concatenate(_split3(b), axis=0),
                   preferred_element_type=F32)


def _expand(v, emat):
    return _dot_f32(v, emat)


def _ssd_mixer(proj, B, L, z_blk, xbc_blk, dt_blk, prm, emat, layer, Q):
    cw, cb, dtb, avec, dx, nw = prm
    width = nw.shape[-1]
    cdim = cw.shape[-1]
    nheads = width // SSD_HEAD_DIM
    nt = L // Q
    body = functools.partial(_ssd_body, nheads=nheads, hd=SSD_HEAD_DIM, ns=SSD_STATE, ngroups=SSD_GROUPS)
    full = lambda a: _layer_spec(a, layer)
    return pl.pallas_call(
        body,
        grid=(B, nt),
        in_specs=[pl.BlockSpec((Q, width), lambda b, t: (b * nt + t, z_blk)),
                  pl.BlockSpec((Q, cdim), lambda b, t: (b * nt + t, xbc_blk)),
                  pl.BlockSpec((Q, LANE), lambda b, t: (b * nt + t, dt_blk)),
                  full(cw), full(cb), full(dtb), full(avec), full(dx), full(nw),
                  pl.BlockSpec(emat.shape, lambda b, t: (0, 0))],
        out_specs=pl.BlockSpec((Q, width), lambda b, t: (b * nt + t, 0)),
        out_shape=jax.ShapeDtypeStruct((B * L, width), BF16),
        scratch_shapes=[pltpu.VMEM((SSD_STATE, width), F32),
                        pltpu.VMEM((Q + SUBLANE, cdim), F32)],
        compiler_params=_cparams(("parallel", "arbitrary")),
        name="ssd_mixer",
    )(proj, proj, proj, cw, cb, dtb, avec, dx, nw, emat)


def _ssd_params(conv_w, conv_b, dt_bias, A_log, D, norm_w):
    depth, nheads = dt_bias.shape
    width = nheads * SSD_HEAD_DIM
    pad = lambda v: jnp.pad(v, ((0, 0), (0, LANE - nheads))).reshape(depth, 1, LANE)
    return (conv_w, conv_b.reshape(depth, 1, -1), pad(dt_bias), pad(-jnp.exp(A_log)),
            jnp.repeat(D, SSD_HEAD_DIM, axis=1).reshape(depth, 1, width), norm_w.reshape(depth, 1, width))


def _head_expansion(nheads):
    e = np.zeros((LANE, nheads * SSD_HEAD_DIM), np.float32)
    e[np.repeat(np.arange(nheads), SSD_HEAD_DIM), np.arange(nheads * SSD_HEAD_DIM)] = 1.0
    return jnp.asarray(np.concatenate([e] * 3, axis=0), BF16)


def _float_key(x):
    b = lax.bitcast_convert_type(x, jnp.int32)
    return b ^ ((b >> 31) & 0x7FFFFFFF)


def _place(blk, lane0, width, dst):
    lane = lax.broadcasted_iota(jnp.int32, blk.shape, 1)
    shift = (dst - lane0) % LANE
    r = pltpu.roll(blk, shift, axis=1) if shift else blk
    lo = jnp.where((lane >= dst) & (lane < min(dst + width, LANE)), r, 0.0)
    hi = jnp.where(lane < dst + width - LANE, r, 0.0)
    return jnp.concatenate([lo, hi], axis=1)


def _head_offsets(offs, nheads, hd):
    return sorted({(offs[f] + h * hd) % LANE for f in ('q', 'qi') for h in range(nheads)})


def _lane_window(x, start, width):
    off = start % LANE
    base = start - off
    if off == 0:
        return x[:, start:start + width]
    lane = lax.broadcasted_iota(jnp.int32, (x.shape[0], LANE), 1)
    out = []
    for p in range(width // LANE):
        a = pltpu.roll(x[:, base + p * LANE:base + (p + 1) * LANE], LANE - off, axis=1)
        b = pltpu.roll(x[:, base + (p + 1) * LANE:base + (p + 2) * LANE], LANE - off, axis=1)
        out.append(jnp.where(lane < LANE - off, a, b))
    return jnp.concatenate(out, axis=1)


def _dsa_body(qx_ref, kv_ref, ki_ref, band_ref, far_ref, o_ref,
              kpad_ref, kipad_ref, vt_ref, qop_ref, key_ref, hi_ref, lo_ref, lom_ref, lg_ref, bound_ref,
              m_ref, l_ref, acc_ref,
              *, nheads, hd, topk, offs):
    i = pl.program_id(1)
    TQ = qx_ref.shape[0]
    KB = TQ
    L = kv_ref.shape[0]
    nkb_total = L // KB
    NT = (((1,), (1,)), ((), ()))
    WIN = kpad_ref.shape[2]

    def head_window(field_lo, h):
        lo = field_lo + h * hd
        return (lo // LANE) * LANE, lo % LANE

    head_offs = _head_offsets(offs, nheads, hd)

    @pl.when(i == 0)
    def _():
        for kb in range(nkb_total):
            r = slice(kb * KB, (kb + 1) * KB)
            kgrp = (offs['k'] // LANE) * LANE
            kblk = kv_ref[r, kgrp:kgrp + LANE]
            kiblk = ki_ref[r, :]
            for n, o in enumerate(head_offs):
                kpad_ref[n, r, :] = _place(kblk, offs['k'] - kgrp, hd, o)[:, :WIN].astype(BF16)
                kipad_ref[n, r, :] = _place(kiblk, offs['ki'], hd, o)[:, :WIN].astype(BF16)
            vt_ref[:, r] = kv_ref[r, :].T[offs['v']:offs['v'] + hd, :].astype(BF16)

    def stage_windows(slot, field_lo, scale):
        bases = sorted({head_window(field_lo, h)[0] for h in range(nheads)})
        for n, base in enumerate(bases):
            qop_ref[slot, n] = (qx_ref[:, base:base + WIN] * scale).astype(BF16)
        return [(bases.index(head_window(field_lo, h)[0]), head_offs.index(head_window(field_lo, h)[1]))
                for h in range(nheads)]

    qs = stage_windows(0, offs['q'], hd ** -0.5)
    qis = stage_windows(1, offs['qi'], 1.0)
    wbase = (offs['w'] // LANE) * LANE
    wt = qx_ref[:, wbase:wbase + LANE].T * ((nheads * IDX_DIM) ** -0.5)
    w_rows = [wt[offs['w'] - wbase + h:offs['w'] - wbase + h + 1, :] for h in range(nheads)]

    RB = LANE
    I16 = jnp.int16
    HALF = 1 << 15

    def rows(kb, r=0, n=KB):
        return pl.ds(pl.multiple_of(kb * KB + r, SUBLANE), n)

    PART = m_ref.shape[1]

    def fold_sum(x):
        return jnp.sum(x.reshape(x.shape[0] // PART, PART, TQ), axis=0)

    def fold_max(x):
        return jnp.max(x.reshape(x.shape[0] // PART, PART, TQ), axis=0)

    def idx_keys(kb, diagonal):
        for r in range(0, KB, RB):
            s = jnp.zeros((RB, TQ), F32)
            for h in range(nheads):
                win, ko = qis[h]
                lg = lax.dot_general(kipad_ref[ko, rows(kb, r, RB), :], qop_ref[1, win], NT,
                                     preferred_element_type=F32)
                s += jnp.maximum(lg, 0.0) * w_rows[h]
            key = _float_key(s)
            if diagonal:
                krow = lax.broadcasted_iota(jnp.int32, (RB, TQ), 0) + r
                qcol = lax.broadcasted_iota(jnp.int32, (RB, TQ), 1)
                key = jnp.where((krow // CHUNK) <= (qcol // CHUNK), key, INT_MIN)
            key_ref[rows(kb, r, RB), :] = key
            hi_ref[rows(kb, r, RB), :] = (key >> 16).astype(I16)
            lo_ref[rows(kb, r, RB), :] = ((key & (2 * HALF - 1)) - HALF).astype(I16)

    def far_keys(kb, _):
        idx_keys(kb, False)
        return 0

    lax.fori_loop(0, i, far_keys, 0)
    idx_keys(i, True)

    def count(preds):
        def body(kb, cs):
            blk = key_ref[rows(kb), :]
            return tuple(c + fold_sum(jnp.where(p(blk, kb), 1.0, 0.0)) for c, p in zip(cs, preds))
        z = jnp.zeros((PART, TQ), F32)
        cs = lax.fori_loop(0, i + 1, body, (z,) * len(preds))
        return [jnp.sum(c, axis=0, keepdims=True) for c in cs]

    PACK = 2 * SUBLANE
    P16 = 4 * PACK

    def rows16(x):
        return x.reshape(x.shape[0] // PACK, PACK, TQ)

    def as16(v):
        return jnp.broadcast_to(v, (PACK, TQ)).astype(I16)

    def count16(src_ref, pred, flags=False):
        def body(kb, c):
            blk = rows16(src_ref[rows(kb), :])
            ind = jnp.where(pred(blk, kb), blk if flags else I16(1), I16(0)).reshape(KB // P16, P16, TQ)
            for n in range(KB // P16):
                c = c + ind[n]
            return c
        c = lax.fori_loop(0, i + 1, body, jnp.zeros((P16, TQ), I16))
        return jnp.sum(c.astype(F32), axis=0, keepdims=True)

    def bisect16(src_ref, target):
        def step(s, tu):
            cand_u = tu | jnp.left_shift(jnp.int32(1), 15 - s)
            cand = as16(cand_u - HALF)[None]
            cnt = count16(src_ref, lambda blk, kb: blk >= cand)
            return jnp.where(cnt >= target, cand_u, tu)
        return lax.fori_loop(0, 16, step, jnp.zeros((1, TQ), jnp.int32))

    thr_hi = bisect16(hi_ref, float(topk)) - HALF
    thr_hi16 = as16(thr_hi)[None]
    need_lo = topk - count16(hi_ref, lambda blk, kb: blk > thr_hi16)

    def mask_low_halves(kb, _):
        keep_lo = rows16(hi_ref[rows(kb), :]) == thr_hi16
        lom_ref[rows(kb), :] = jnp.where(keep_lo, rows16(lo_ref[rows(kb), :]), I16(-HALF)).reshape(KB, TQ)
        return 0

    lax.fori_loop(0, i + 1, mask_low_halves, 0)
    thr_lo = bisect16(lom_ref, need_lo)
    thr = thr_hi * (2 * HALF) + thr_lo

    cgt, ceq = count([lambda blk, kb: blk > thr, lambda blk, kb: blk == thr])
    need = topk - cgt
    nbits = int(L).bit_length()
    bound_ref[...] = jnp.full((1, TQ), 2 ** nbits - 1, jnp.int32)

    @pl.when(jnp.max(ceq - need) > 0.0)
    def _():
        thr_lo16 = as16(thr_lo - HALF)[None]

        def flag_ties(kb, _):
            tie = ((rows16(hi_ref[rows(kb), :]) == thr_hi16) & (rows16(lo_ref[rows(kb), :]) == thr_lo16))
            lom_ref[rows(kb), :] = jnp.where(tie, I16(1), I16(0)).reshape(KB, TQ)
            return 0

        lax.fori_loop(0, i + 1, flag_ties, 0)
        krow16 = rows16(lax.broadcasted_iota(jnp.int32, (KB, TQ), 0).astype(I16))

        def bisect_idx(step, bnd):
            bit = jnp.left_shift(jnp.int32(1), nbits - 1 - step)
            cand = bnd | bit
            cnt = count16(lom_ref, lambda blk, kb: krow16 < as16(cand - kb * KB)[None], flags=True)
            return jnp.where(cnt <= need, cand, bnd)

        bound_ref[...] = lax.fori_loop(0, nbits, bisect_idx, jnp.zeros((1, TQ), jnp.int32))

    bound = bound_ref[...]

    m_ref[...] = jnp.full(m_ref.shape, NEG, F32)

    def store_logits(kb, bias_of_head):
        for r in range(0, KB, RB):
            blk = key_ref[rows(kb, r, RB), :]
            pos = lax.broadcasted_iota(jnp.int32, (RB, TQ), 0) + (kb * KB + r)
            keep = ((blk > thr) | ((blk == thr) & (pos < bound))) & (blk != INT_MIN)
            mask = jnp.where(keep, 0.0, NEG)
            for h in range(nheads):
                win, ko = qs[h]
                lg = lax.dot_general(kpad_ref[ko, rows(kb, r, RB), :], qop_ref[0, win], NT,
                                     preferred_element_type=F32)
                lg = lg + bias_of_head(h, r) + mask
                lg_ref[h, rows(kb, r, RB), :] = lg
                m_ref[h] = jnp.maximum(m_ref[h], fold_max(lg))

    def far_logits(kb, _):
        store_logits(kb, lambda h, r: far_ref[h])
        return 0

    lax.fori_loop(0, i - 1, far_logits, 0)

    @pl.when(i >= 1)
    def _():
        store_logits(i - 1, lambda h, r: band_ref[h, r:r + RB, :])

    store_logits(i, lambda h, r: band_ref[h, KB + r:KB + r + RB, :])

    m_fin = [jnp.max(m_ref[h], axis=0, keepdims=True) for h in range(nheads)]
    acc_ref[...] = jnp.zeros(acc_ref.shape, F32)
    l_ref[...] = jnp.zeros(l_ref.shape, F32)

    def accumulate(kb, _):
        vt = vt_ref[:, rows(kb)]
        for h in range(nheads):
            p = jnp.exp(lg_ref[h, rows(kb), :] - m_fin[h])
            l_ref[h] += fold_sum(p)
            acc_ref[h] += jnp.dot(vt, p.astype(BF16), preferred_element_type=F32)
        return 0

    lax.fori_loop(0, i + 1, accumulate, 0)

    outs = [acc_ref[h] / jnp.sum(l_ref[h], axis=0, keepdims=True) for h in range(nheads)]
    out = jnp.concatenate(outs, axis=0).T
    z = _lane_window(qx_ref, offs['z'], nheads * hd)
    o_ref[...] = (out * jax.nn.silu(z)).astype(BF16)


def _dsa_mixer(proj, B, L, src, band, far, nheads, TQ):
    width = nheads * ATT_HEAD_DIM
    nq = L // TQ
    hd = ATT_HEAD_DIM
    qx_w = 16 * LANE
    qx_blk = src['att_q'][0] // qx_w
    qx_lo = qx_blk * qx_w
    kv_w = 2 * LANE
    kv_blk = src['att_k'][0] // kv_w
    ki_blk = src['idx_k'][0] // LANE
    offs = {'q': src['att_q'][0] - qx_lo, 'qi': src['idx_q'][0] - qx_lo, 'w': src['idx_w'][0] - qx_lo,
            'z': src['att_z'][0] - qx_lo, 'k': src['att_k'][0] - kv_blk * kv_w,
            'v': src['att_v'][0] - kv_blk * kv_w, 'ki': src['idx_k'][0] - ki_blk * LANE}
    assert offs['z'] + width + LANE <= qx_w and qx_lo + qx_w <= proj.shape[1]
    assert offs['k'] % LANE + hd <= LANE and offs['ki'] + hd <= LANE
    assert offs['v'] % SUBLANE == 0 and offs['v'] + hd <= kv_w
    assert offs['w'] // LANE == (offs['w'] + nheads - 1) // LANE
    win_off = _head_offsets(offs, nheads, hd)
    win = LANE if max(win_off) + hd <= LANE else 2 * LANE
    win_base = max((sorted({(offs[f] + h * hd) // LANE for h in range(nheads)}) for f in ('q', 'qi')), key=len)
    body = functools.partial(_dsa_body, nheads=nheads, hd=hd, topk=min(TOPK, L // 4), offs=offs)
    full = lambda a: pl.BlockSpec(a.shape, lambda b, i: (0,) * a.ndim)
    return pl.pallas_call(
        body,
        grid=(B, nq),
        in_specs=[pl.BlockSpec((TQ, qx_w), lambda b, i: (b * nq + i, qx_blk)),
                  pl.BlockSpec((L, kv_w), lambda b, i: (b, kv_blk)),
                  pl.BlockSpec((L, LANE), lambda b, i: (b, ki_blk)),
                  full(band), full(far)],
        out_specs=pl.BlockSpec((TQ, width), lambda b, i: (b * nq + i, 0)),
        out_shape=jax.ShapeDtypeStruct((B * L, width), BF16),
        scratch_shapes=[pltpu.VMEM((len(win_off), L, win), BF16),
                        pltpu.VMEM((len(win_off), L, win), BF16),
                        pltpu.VMEM((ATT_HEAD_DIM, L), BF16),
                        pltpu.VMEM((2, len(win_base), TQ, win), BF16),
                        pltpu.VMEM((L, TQ), jnp.int32),
                        pltpu.VMEM((L, TQ), jnp.int16),
                        pltpu.VMEM((L, TQ), jnp.int16),
                        pltpu.VMEM((L, TQ), jnp.int16),
                        pltpu.VMEM((nheads, L, TQ), F32),
                        pltpu.VMEM((1, TQ), jnp.int32),
                        pltpu.VMEM((nheads, 4 * SUBLANE, TQ), F32),
                        pltpu.VMEM((nheads, 4 * SUBLANE, TQ), F32),
                        pltpu.VMEM((nheads, ATT_HEAD_DIM, TQ), F32)],
        compiler_params=_cparams(("parallel", "arbitrary")),
        name="dsa_mixer",
    )(proj, proj, proj, band, far)


def _t5_bucket_static(rel):
    nb = N_BUCKETS // 2
    max_exact = nb // 2
    ret = np.where(rel > 0, nb, 0)
    n = np.abs(rel)
    nf = np.maximum(n, 1).astype(np.float64)
    large = max_exact + (np.log(nf / max_exact) / math.log(MAX_DISTANCE / max_exact)
                         * (nb - max_exact)).astype(np.int32)
    large = np.minimum(large, nb - 1)
    return ret + np.where(n < max_exact, n, large)


def _dsa_bias_tables(rel_bias, TQ):
    a = np.arange(2 * TQ)[:, None]
    j = np.arange(TQ)[None, :]
    band_idx = _t5_bucket_static(a - TQ - j)
    idx = jnp.asarray(band_idx, jnp.int32)[None]
    band = jnp.zeros((rel_bias.shape[1],) + band_idx.shape, F32)
    for bucket in np.unique(band_idx):
        band = jnp.where(idx == int(bucket), rel_bias[int(bucket)][:, None, None], band)
    far_bucket = int(_t5_bucket_static(np.array([-(TQ + 1)]))[0])
    assert far_bucket == int(_t5_bucket_static(np.array([-(10 ** 6)]))[0])
    far = jnp.broadcast_to(rel_bias[far_bucket][:, None, None], (rel_bias.shape[1], 1, TQ))
    return band, far


def _layout(d_model):
    s5w = d_model // 4
    ssdw = d_model // 2
    attw = d_model // 4
    cdim = ssdw + 2 * SSD_GROUPS * SSD_STATE
    nh_ssd = ssdw // SSD_HEAD_DIM
    nh_att = attw // ATT_HEAD_DIM
    splits = (s5w, s5w, ssdw, cdim, nh_ssd, attw, ATT_HEAD_DIM, ATT_HEAD_DIM,
              nh_att * IDX_DIM, IDX_DIM, nh_att, attw)
    names = ('s5_u', 's5_z', 'ssd_z', 'ssd_xbc', 'ssd_dt', 'att_q', 'att_k', 'att_v',
             'idx_q', 'idx_k', 'idx_w', 'att_z')
    src, o = {}, 0
    for nme, s in zip(names, splits):
        src[nme] = (o, s)
        o += s
    return src, o


def kernel(x, norm_w, w_in, s5_A_re, s5_A_im, s5_log_dt, s5_B_re, s5_B_im, s5_C_re, s5_C_im, s5_D, s5_glu_w, s5_glu_b, ssd_conv_w, ssd_conv_b, ssd_dt_bias, ssd_A_log, ssd_D, ssd_norm_w, rel_bias, w_out, final_norm_w):
    B, L, d = x.shape
    depth = w_in.shape[0]
    src, total = _layout(d)
    s5w, ssdw, attw = d // 4, d // 2, d // 4
    cdim = ssdw + 2 * SSD_GROUPS * SSD_STATE
    nh_att = attw // ATT_HEAD_DIM
    assert src['s5_u'][0] % s5w == 0 and src['s5_z'][0] % s5w == 0 and src['ssd_z'][0] % ssdw == 0
    assert src['ssd_xbc'][0] % cdim == 0 and src['ssd_dt'][0] % LANE == 0

    tn = 1024
    cuts, shift, padded = [(0, 0)], 0, {}
    for nme, (so, sz) in sorted(src.items(), key=lambda kv: kv[1][0]):
        gap = -(so + shift) % LANE if nme in ('att_q', 'att_k', 'idx_q', 'idx_k', 'att_z') else 0
        if gap:
            shift += gap
            cuts.append((so, shift))
        padded[nme] = (so + shift, sz)
    src = padded
    total_pad = -(-(total + shift) // tn) * tn
    segs = tuple((lo, cuts[n + 1][0] if n + 1 < len(cuts) else total, lo + sh) for n, (lo, sh) in enumerate(cuts))
    w_in_p = _relayout_w_in(jnp.swapaxes(w_in, 1, 2), segs, total_pad, min(256, d))
    w_out_b = w_out.astype(BF16)

    TQ = 256
    Q = 128
    band, far = _dsa_bias_tables(rel_bias, TQ)
    tm_in = min(1024, B * L)
    tm_out = min(512, B * L)
    nchunk = 2
    seg = L // SUBLANE

    s5p = jax.vmap(lambda *a: _s5_params(*a, seg, nchunk))(
        s5_A_re, s5_A_im, s5_log_dt, s5_B_re, s5_B_im, s5_C_re, s5_C_im, s5_D, s5_glu_w, s5_glu_b)
    ssdp = _ssd_params(ssd_conv_w, ssd_conv_b, ssd_dt_bias, ssd_A_log, ssd_D, ssd_norm_w)
    emat = _head_expansion(ssdw // SSD_HEAD_DIM)
    norm_w3 = norm_w.reshape(depth, 1, d)

    x2 = x.reshape(B * L, d)
    for l in range(depth):
        proj = _inproj(x2, norm_w3, w_in_p, l, tm_in, tn)
        y_s5 = _s5_mixer(proj, B, L, src['s5_u'][0] // s5w, src['s5_z'][0] // s5w, s5w, s5p, l)
        y_ssd = _ssd_mixer(proj, B, L, src['ssd_z'][0] // ssdw, src['ssd_xbc'][0] // cdim,
                           src['ssd_dt'][0] // LANE, ssdp, emat, l, Q)
        y_att = _dsa_mixer(proj, B, L, src, band, far, nh_att, TQ)
        x2 = _outproj(y_s5, y_ssd, y_att, w_out_b, l, x2, final_norm_w.reshape(1, d), tm_out,
                      final=(l == depth - 1))
    return x2.reshape(B, L, d)
```

```python
import functools
import math

import numpy as np
import jax
import jax.numpy as jnp
from jax import lax
from jax.experimental import pallas as pl
from jax.experimental.pallas import tpu as pltpu

F32 = jnp.float32
BF16 = jnp.bfloat16

EPS = 1e-6
CHUNK = 64

S5_GROUP = 16
S5_STATE = 64
SSD_HEAD_DIM = 64
SSD_STATE = 128
SSD_GROUPS = 4
SSD_CONV = 4
ATT_HEAD_DIM = 64
IDX_DIM = 64
TOPK = 256
N_BUCKETS = 32
MAX_DISTANCE = 128

LANE = 128
SUBLANE = 8
VMEM_LIMIT = 56 * 1024 * 1024

INT_MIN = -2 ** 31
NEG = -1e30
LOG2E = math.log2(math.e)


def _cparams(sem):
    return pltpu.CompilerParams(dimension_semantics=sem, vmem_limit_bytes=VMEM_LIMIT)


def _inproj_body(x_ref, nw_ref, w_ref, o_ref, h_ref):
    @pl.when(pl.program_id(1) == 0)
    def _():
        xf = x_ref[...]
        ms = jnp.mean(xf * xf, axis=-1, keepdims=True)
        h_ref[...] = (xf * lax.rsqrt(ms + EPS) * nw_ref[...]).astype(BF16)

    o_ref[...] = lax.dot_general(h_ref[...], w_ref[...], (((1,), (1,)), ((), ())),
                                 preferred_element_type=F32)


def _inproj(x2, nw, w, layer, tm, tn):
    m, d = x2.shape
    n = w.shape[1]
    return pl.pallas_call(
        _inproj_body,
        grid=(m // tm, n // tn),
        in_specs=[pl.BlockSpec((tm, d), lambda i, j: (i, 0)),
                  pl.BlockSpec((None, 1, d), lambda i, j: (layer, 0, 0)),
                  pl.BlockSpec((None, tn, d), lambda i, j: (layer, j, 0))],
        out_specs=pl.BlockSpec((tm, tn), lambda i, j: (i, j)),
        out_shape=jax.ShapeDtypeStruct((m, n), F32),
        scratch_shapes=[pltpu.VMEM((tm, d), BF16)],
        compiler_params=_cparams(("parallel", "arbitrary")),
        name="inproj",
    )(x2, nw, w)


def _relayout_body(w_ref, o_ref, *, segs):
    pack = 2 * SUBLANE
    pos = 0
    for src_lo, src_hi, dst_lo in segs:
        if dst_lo > pos:
            o_ref[pos:dst_lo, :] = jnp.zeros((dst_lo - pos, o_ref.shape[1]), BF16)
        val = w_ref[src_lo:src_hi, :]
        fill = -(src_hi - src_lo) % pack
        if fill:
            val = jnp.concatenate([val, jnp.zeros((fill, val.shape[1]), F32)], axis=0)
        o_ref[dst_lo:dst_lo + val.shape[0], :] = val.astype(BF16)
        pos = dst_lo + val.shape[0]
    if pos < o_ref.shape[0]:
        o_ref[pos:, :] = jnp.zeros((o_ref.shape[0] - pos, o_ref.shape[1]), BF16)


def _relayout_w_in(w_t, segs, total_pad, tk):
    depth, n, d = w_t.shape
    assert all(lo % SUBLANE == 0 and (hi - lo) % SUBLANE == 0 and dst % (2 * SUBLANE) == 0
               for lo, hi, dst in segs)
    return pl.pallas_call(
        functools.partial(_relayout_body, segs=segs),
        grid=(depth, d // tk),
        in_specs=[pl.BlockSpec((None, n, tk), lambda l, i: (l, 0, i))],
        out_specs=pl.BlockSpec((None, total_pad, tk), lambda l, i: (l, 0, i)),
        out_shape=jax.ShapeDtypeStruct((depth, total_pad, d), BF16),
        compiler_params=_cparams(("parallel", "parallel")),
        name="w_in_relayout",
    )(w_t)


def _outproj_body(ys5_ref, yssd_ref, yatt_ref, w_ref, x_ref, fnw_ref, o_ref, wb_ref, *, w5, wssd, final):
    @pl.when(pl.program_id(0) == 0)
    def _():
        step = 256
        for r in range(0, w_ref.shape[0], step):
            wb_ref[r:r + step, :] = w_ref[r:r + step, :].astype(BF16)

    acc = x_ref[...]
    acc += jnp.dot(ys5_ref[...], wb_ref[0:w5, :], preferred_element_type=F32)
    acc += jnp.dot(yssd_ref[...], wb_ref[w5:w5 + wssd, :], preferred_element_type=F32)
    acc += jnp.dot(yatt_ref[...], wb_ref[w5 + wssd:, :], preferred_element_type=F32)
    if final:
        ms = jnp.mean(acc * acc, axis=-1, keepdims=True)
        acc = acc * lax.rsqrt(ms + EPS) * fnw_ref[...]
    o_ref[...] = acc


def _outproj(ys5, yssd, yatt, w, layer, x2, fnw, tm, final):
    m, d = x2.shape
    w5, wssd, watt = ys5.shape[1], yssd.shape[1], yatt.shape[1]
    body = functools.partial(_outproj_body, w5=w5, wssd=wssd, final=final)
    return pl.pallas_call(
        body,
        grid=(m // tm,),
        in_specs=[pl.BlockSpec((tm, w5), lambda i: (i, 0)),
                  pl.BlockSpec((tm, wssd), lambda i: (i, 0)),
                  pl.BlockSpec((tm, watt), lambda i: (i, 0)),
                  pl.BlockSpec((None,) + w.shape[1:], lambda i: (layer, 0, 0), pipeline_mode=pl.Buffered(1)),
                  pl.BlockSpec((tm, d), lambda i: (i, 0)),
                  pl.BlockSpec((1, d), lambda i: (0, 0))],
        out_specs=pl.BlockSpec((tm, d), lambda i: (i, 0)),
        out_shape=jax.ShapeDtypeStruct((m, d), F32),
        scratch_shapes=[pltpu.VMEM(w.shape[1:], BF16)],
        compiler_params=_cparams(("arbitrary",)),
        name="outproj",
    )(ys5, yssd, yatt, w, x2, fnw)


def _shift_down_one(x):
    rolled = pltpu.roll(x, 1, axis=0)
    row = lax.broadcasted_iota(jnp.int32, x.shape, 0)
    return jnp.where(row == 0, 0.0, rolled)


def _s5_body(u_ref, z_ref, bblk_ref, cblk_ref, lam_ref, lamseg_ref, d_ref, gw_ref, gb_ref,
             o_ref, uperm_ref, xs_ref, yperm_ref, *, nchunk, cw, sw, rt):
    L = u_ref.shape[0]
    seg = L // SUBLANE

    for c in range(nchunk):
        for j in range(SUBLANE):
            for k in range(cw // LANE):
                lo = c * cw + k * LANE
                uperm_ref[k, pl.ds(j, seg, stride=SUBLANE), :] = u_ref[pl.ds(j * seg, seg), lo:lo + LANE]
        for r0 in range(0, L, rt):
            up = jnp.concatenate([uperm_ref[k, r0:r0 + rt, :] for k in range(cw // LANE)], axis=1)
            xs_ref[r0:r0 + rt, :] = jnp.dot(up.astype(BF16), bblk_ref[c], preferred_element_type=F32)
        lr = lam_ref[c, 0]
        li = lam_ref[c, 1]

        def scan_step(tau, carry):
            xr, xi = carry
            row = pl.multiple_of(tau * SUBLANE, SUBLANE)
            nxr = lr * xr - li * xi + xs_ref[pl.ds(row, SUBLANE), 0:sw]
            nxi = lr * xi + li * xr + xs_ref[pl.ds(row, SUBLANE), sw:2 * sw]
            xs_ref[pl.ds(row, SUBLANE), 0:sw] = nxr
            xs_ref[pl.ds(row, SUBLANE), sw:2 * sw] = nxi
            return nxr, nxi

        zero = jnp.zeros((SUBLANE, sw), F32)
        er, ei = lax.fori_loop(0, seg, scan_step, (zero, zero))

        sr = lamseg_ref[c, 0]
        si = lamseg_ref[c, 1]
        cr, ci = zero, zero
        for _ in range(SUBLANE - 1):
            tr = er + (sr * cr - si * ci)
            ti = ei + (sr * ci + si * cr)
            cr, ci = _shift_down_one(tr), _shift_down_one(ti)

        def fix_step(tau, carry):
            fr, fi = carry
            nfr = lr * fr - li * fi
            nfi = lr * fi + li * fr
            row = pl.multiple_of(tau * SUBLANE, SUBLANE)
            xs_ref[pl.ds(row, SUBLANE), 0:sw] += nfr
            xs_ref[pl.ds(row, SUBLANE), sw:2 * sw] += nfi
            return nfr, nfi

        lax.fori_loop(0, seg, fix_step, (cr, ci))

        for r0 in range(0, L, rt):
            yc = jnp.dot(xs_ref[r0:r0 + rt, :].astype(BF16), cblk_ref[c], preferred_element_type=F32)
            for k in range(cw // LANE):
                yperm_ref[c * (cw // LANE) + k, r0:r0 + rt, :] = yc[:, k * LANE:(k + 1) * LANE]

    for j in range(SUBLANE):
        y = jnp.concatenate([yperm_ref[k, pl.ds(j, seg, stride=SUBLANE), :]
                             for k in range(yperm_ref.shape[0])], axis=1)
        y = y + d_ref[...] * u_ref[j * seg:(j + 1) * seg, :]
        y = jax.nn.gelu(y)
        g = jnp.dot(y.astype(BF16), gw_ref[...], preferred_element_type=F32) + gb_ref[...]
        y = y * jax.nn.sigmoid(g)
        o_ref[j * seg:(j + 1) * seg, :] = (y * jax.nn.silu(z_ref[j * seg:(j + 1) * seg, :])).astype(BF16)


def _layer_spec(a, layer):
    return pl.BlockSpec((None,) + a.shape[1:], lambda *_: (layer,) + (0,) * (a.ndim - 1))


def _s5_mixer(proj, B, L, u_blk, z_blk, width, prm, layer):
    bblk, cblk, lam, lamseg, dvec, gw, gb = prm
    nchunk, cw, sw2 = bblk.shape[1:]
    sw = sw2 // 2
    rt = min(512, L)
    body = functools.partial(_s5_body, nchunk=nchunk, cw=cw, sw=sw, rt=rt)
    full = lambda a: _layer_spec(a, layer)
    return pl.pallas_call(
        body,
        grid=(B,),
        in_specs=[pl.BlockSpec((L, width), lambda b: (b, u_blk)),
                  pl.BlockSpec((L, width), lambda b: (b, z_blk)),
                  full(bblk), full(cblk), full(lam), full(lamseg), full(dvec), full(gw), full(gb)],
        out_specs=pl.BlockSpec((L, width), lambda b: (b, 0)),
        out_shape=jax.ShapeDtypeStruct((B * L, width), BF16),
        scratch_shapes=[pltpu.VMEM((cw // LANE, L, LANE), F32),
                        pltpu.VMEM((L, 2 * sw), F32),
                        pltpu.VMEM((width // LANE, L, LANE), F32)],
        compiler_params=_cparams(("parallel",)),
        name="s5_mixer",
    )(proj, proj, bblk, cblk, lam, lamseg, dvec, gw, gb)


def _cpow(re, im, n):
    rr, ri = jnp.ones_like(re), jnp.zeros_like(im)
    br, bi = re, im
    while n:
        if n & 1:
            rr, ri = rr * br - ri * bi, rr * bi + ri * br
        br, bi = br * br - bi * bi, 2.0 * br * bi
        n >>= 1
    return rr, ri


def _s5_params(A_re, A_im, log_dt, B_re, B_im, C_re, C_im, D, glu_w, glu_b, seg, nchunk):
    G, P, C = B_re.shape
    dt = jnp.exp(log_dt)[:, None]
    lre = jnp.minimum(A_re, -1e-4)
    lim = A_im
    mag = jnp.exp(lre * dt)
    lbr = mag * jnp.cos(lim * dt)
    lbi = mag * jnp.sin(lim * dt)
    nr, ni = lbr - 1.0, lbi
    den = lre * lre + lim * lim
    fr = (nr * lre + ni * lim) / den
    fi = (ni * lre - nr * lim) / den
    bbr = fr[..., None] * B_re - fi[..., None] * B_im
    bbi = fr[..., None] * B_im + fi[..., None] * B_re
    gc = G // nchunk
    eye = jnp.eye(gc, dtype=F32)

    def blockdiag_in(bb):
        t = jnp.transpose(bb, (0, 2, 1)).reshape(nchunk, gc, C, P)
        return jnp.einsum('ngcp,gh->ngchp', t, eye).reshape(nchunk, gc * C, gc * P)

    def blockdiag_out(cc):
        t = jnp.transpose(cc, (0, 2, 1)).reshape(nchunk, gc, P, C)
        return jnp.einsum('ngpc,gh->ngphc', t, eye).reshape(nchunk, gc * P, gc * C)

    bblk = jnp.concatenate([blockdiag_in(bbr), blockdiag_in(bbi)], axis=-1).astype(BF16)
    cblk = jnp.concatenate([blockdiag_out(C_re), blockdiag_out(-C_im)], axis=1).astype(BF16)

    def rows(v):
        return jnp.broadcast_to(v.reshape(nchunk, 1, gc * P), (nchunk, SUBLANE, gc * P))

    lam = jnp.stack([rows(lbr), rows(lbi)], axis=1)
    pr, pi = _cpow(lbr, lbi, seg)
    lamseg = jnp.stack([rows(pr), rows(pi)], axis=1)
    return (bblk, cblk, lam, lamseg, D.reshape(1, G * C), glu_w.astype(BF16), glu_b.reshape(1, -1))


def _split3(x):
    h = x.astype(BF16)
    r = x - h.astype(F32)
    m = r.astype(BF16)
    l = (r - m.astype(F32)).astype(BF16)
    return h, m, l


def _dot_f32(a, b3_bf16):
    return jnp.dot(jnp.concatenate(_split3(a), axis=1), b3_bf16, preferred_element_type=F32)


def _ssd_body(z_ref, xbc_ref, dt_ref, cw_ref, cb_ref, dtb_ref, a_ref, dx_ref, nw_ref, e_ref,
              o_ref, state_ref, xpad_ref, *, nheads, hd, ns, ngroups):
    Q = z_ref.shape[0]
    width = nheads * hd
    gw = width // ngroups
    halo = xpad_ref.shape[0] - Q

    @pl.when(pl.program_id(1) == 0)
    def _():
        state_ref[...] = jnp.zeros_like(state_ref)
        xpad_ref[0:halo, :] = jnp.zeros((halo, xpad_ref.shape[1]), F32)

    xpad_ref[halo:halo + Q, :] = xbc_ref[...]
    xp_b = xpad_ref[...].astype(BF16)
    srow = lax.broadcasted_iota(jnp.int32, (Q, Q + halo), 0)
    scol = lax.broadcasted_iota(jnp.int32, (Q, Q + halo), 1)
    acc = cb_ref[...] + cw_ref[SSD_CONV - 1:SSD_CONV, :] * xpad_ref[halo:halo + Q, :]
    for k in range(SSD_CONV - 1):
        shift = (scol == srow + (halo - (SSD_CONV - 1 - k))).astype(BF16)
        acc += cw_ref[k:k + 1, :] * jnp.dot(shift, xp_b, preferred_element_type=F32)
    xpad_ref[0:halo, :] = xpad_ref[Q:Q + halo, :]
    xc = jax.nn.silu(acc)
    xs = xc[:, 0:width]
    bm = xc[:, width:width + ngroups * ns].astype(BF16)
    cm = xc[:, width + ngroups * ns:].astype(BF16)

    lane = lax.broadcasted_iota(jnp.int32, (Q, LANE), 1)
    dt = jnp.where(lane < nheads, jax.nn.softplus(dt_ref[...] + dtb_ref[...]), 0.0)
    a = dt * a_ref[...]
    rowi = lax.broadcasted_iota(jnp.int32, (Q, Q), 0)
    coli = lax.broadcasted_iota(jnp.int32, (Q, Q), 1)
    tril = coli <= rowi
    acs = _dot_f32_lhs(tril.astype(BF16), a)
    acs_t = acs.T
    last = acs[Q - 1:Q, :]
    emat = e_ref[...]
    dt_x = _expand(dt, emat)
    dec_x = _expand(jnp.exp(acs), emat)
    dte_x = _expand(jnp.exp(last - acs), emat)
    dlast_x = dec_x[Q - 1:Q, :]

    xd = xs * dt_x
    xd_b = xd.astype(BF16)
    xw_b = (xd * dte_x).astype(BF16)
    st = state_ref[...]
    st_b = st.astype(BF16)

    y_parts = []
    new_state = []
    hlane = lax.broadcasted_iota(jnp.int32, (Q, LANE), 1) < hd
    heads_per_group = nheads // ngroups
    for g in range(ngroups):
        cg = cm[:, g * ns:(g + 1) * ns]
        bg = bm[:, g * ns:(g + 1) * ns]
        cb = lax.dot_general(cg, bg, (((1,), (1,)), ((), ())), preferred_element_type=F32)
        y_off = jnp.dot(cg, st_b[:, g * gw:(g + 1) * gw], preferred_element_type=F32)
        pair_out = []
        for pr in range(gw // LANE):
            xpair = xd_b[:, g * gw + pr * LANE: g * gw + (pr + 1) * LANE]
            res = []
            for hh in range(LANE // hd):
                h = g * heads_per_group + pr * (LANE // hd) + hh
                diff = acs[:, h:h + 1] - acs_t[h:h + 1, :]
                s_h = jnp.where(tril, cb * jnp.exp(jnp.minimum(diff, 0.0)), 0.0).astype(BF16)
                res.append(jnp.dot(s_h, xpair, preferred_element_type=F32))
            pair_out.append(jnp.where(hlane, res[0], res[1]))
        y_diag = jnp.concatenate(pair_out, axis=1)
        y_parts.append(y_diag + y_off * dec_x[:, g * gw:(g + 1) * gw])
        upd = lax.dot_general(bg, xw_b[:, g * gw:(g + 1) * gw], (((0,), (0,)), ((), ())),
                              preferred_element_type=F32)
        new_state.append(st[:, g * gw:(g + 1) * gw] * dlast_x[:, g * gw:(g + 1) * gw] + upd)
    state_ref[...] = jnp.concatenate(new_state, axis=1)

    y = jnp.concatenate(y_parts, axis=1) + xs * dx_ref[...]
    gt = y * jax.nn.silu(z_ref[...])
    ms = jnp.mean(gt * gt, axis=-1, keepdims=True)
    o_ref[...] = (gt * lax.rsqrt(ms + EPS) * nw_ref[...]).astype(BF16)


def _dot_f32_lhs(a_bf16, b):
    return jnp.dot(jnp.concatenate([a_bf16] * 3, axis=1), jnp.concatenate(_split3(b), axis=0),
                   preferred_element_type=F32)


def _expand(v, emat):
    return _dot_f32(v, emat)


def _ssd_mixer(proj, B, L, z_blk, xbc_blk, dt_blk, prm, emat, layer, Q):
    cw, cb, dtb, avec, dx, nw = prm
    width = nw.shape[-1]
    cdim = cw.shape[-1]
    nheads = width // SSD_HEAD_DIM
    nt = L // Q
    body = functools.partial(_ssd_body, nheads=nheads, hd=SSD_HEAD_DIM, ns=SSD_STATE, ngroups=SSD_GROUPS)
    full = lambda a: _layer_spec(a, layer)
    return pl.pallas_call(
        body,
        grid=(B, nt),
        in_specs=[pl.BlockSpec((Q, width), lambda b, t: (b * nt + t, z_blk)),
                  pl.BlockSpec((Q, cdim), lambda b, t: (b * nt + t, xbc_blk)),
                  pl.BlockSpec((Q, LANE), lambda b, t: (b * nt + t, dt_blk)),
                  full(cw), full(cb), full(dtb), full(avec), full(dx), full(nw),
                  pl.BlockSpec(emat.shape, lambda b, t: (0, 0))],
        out_specs=pl.BlockSpec((Q, width), lambda b, t: (b * nt + t, 0)),
        out_shape=jax.ShapeDtypeStruct((B * L, width), BF16),
        scratch_shapes=[pltpu.VMEM((SSD_STATE, width), F32),
                        pltpu.VMEM((Q + 2 * SUBLANE, cdim), F32)],
        compiler_params=_cparams(("parallel", "arbitrary")),
        name="ssd_mixer",
    )(proj, proj, proj, cw, cb, dtb, avec, dx, nw, emat)


def _ssd_params(conv_w, conv_b, dt_bias, A_log, D, norm_w):
    depth, nheads = dt_bias.shape
    width = nheads * SSD_HEAD_DIM
    pad = lambda v: jnp.pad(v, ((0, 0), (0, LANE - nheads))).reshape(depth, 1, LANE)
    return (conv_w, conv_b.reshape(depth, 1, -1), pad(dt_bias), pad(-jnp.exp(A_log)),
            jnp.repeat(D, SSD_HEAD_DIM, axis=1).reshape(depth, 1, width), norm_w.reshape(depth, 1, width))


def _head_expansion(nheads):
    e = np.zeros((LANE, nheads * SSD_HEAD_DIM), np.float32)
    e[np.repeat(np.arange(nheads), SSD_HEAD_DIM), np.arange(nheads * SSD_HEAD_DIM)] = 1.0
    return jnp.asarray(np.concatenate([e] * 3, axis=0), BF16)


def _float_key(x):
    b = lax.bitcast_convert_type(x, jnp.int32)
    return b ^ ((b >> 31) & 0x7FFFFFFF)


def _place(blk, lane0, width, dst):
    lane = lax.broadcasted_iota(jnp.int32, blk.shape, 1)
    shift = (dst - lane0) % LANE
    r = pltpu.roll(blk, shift, axis=1) if shift else blk
    lo = jnp.where((lane >= dst) & (lane < min(dst + width, LANE)), r, 0.0)
    hi = jnp.where(lane < dst + width - LANE, r, 0.0)
    return jnp.concatenate([lo, hi], axis=1)


def _head_offsets(offs, nheads, hd):
    return sorted({(offs[f] + h * hd) % LANE for f in ('q', 'qi') for h in range(nheads)})


def _lane_window(x, start, width):
    off = start % LANE
    base = start - off
    if off == 0:
        return x[:, start:start + width]
    lane = lax.broadcasted_iota(jnp.int32, (x.shape[0], LANE), 1)
    out = []
    for p in range(width // LANE):
        a = pltpu.roll(x[:, base + p * LANE:base + (p + 1) * LANE], LANE - off, axis=1)
        b = pltpu.roll(x[:, base + (p + 1) * LANE:base + (p + 2) * LANE], LANE - off, axis=1)
        out.append(jnp.where(lane < LANE - off, a, b))
    return jnp.concatenate(out, axis=1)


def _dsa_body(qx_ref, kv_ref, ki_ref, band_ref, far_ref, o_ref,
              kpad_ref, kipad_ref, vt_ref, qop_ref, key_ref, hi_ref, lo_ref, lom_ref, lg_ref, bound_ref,
              m_ref, l_ref, acc_ref,
              *, nheads, hd, topk, offs):
    i = pl.program_id(1)
    TQ = qx_ref.shape[0]
    KB = TQ
    L = kv_ref.shape[0]
    nkb_total = L // KB
    NT = (((1,), (1,)), ((), ()))
    WIN = kpad_ref.shape[2]

    def head_window(field_lo, h):
        lo = field_lo + h * hd
        return (lo // LANE) * LANE, lo % LANE

    head_offs = _head_offsets(offs, nheads, hd)

    @pl.when(i == 0)
    def _():
        for kb in range(nkb_total):
            r = slice(kb * KB, (kb + 1) * KB)
            kgrp = (offs['k'] // LANE) * LANE
            kblk = kv_ref[r, kgrp:kgrp + LANE]
            kiblk = ki_ref[r, :]
            for n, o in enumerate(head_offs):
                kpad_ref[n, r, :] = _place(kblk, offs['k'] - kgrp, hd, o)[:, :WIN].astype(BF16)
                kipad_ref[n, r, :] = _place(kiblk, offs['ki'], hd, o)[:, :WIN].astype(BF16)
            vt_ref[:, r] = kv_ref[r, :].T[offs['v']:offs['v'] + hd, :].astype(BF16)

    def stage_windows(slot, field_lo, scale):
        bases = sorted({head_window(field_lo, h)[0] for h in range(nheads)})
        for n, base in enumerate(bases):
            qop_ref[slot, n] = (qx_ref[:, base:base + WIN] * scale).astype(BF16)
        return [(bases.index(head_window(field_lo, h)[0]), head_offs.index(head_window(field_lo, h)[1]))
                for h in range(nheads)]

    qs = stage_windows(0, offs['q'], hd ** -0.5)
    qis = stage_windows(1, offs['qi'], 1.0)
    wbase = (offs['w'] // LANE) * LANE
    wt = qx_ref[:, wbase:wbase + LANE].T * ((nheads * IDX_DIM) ** -0.5)
    w_rows = [wt[offs['w'] - wbase + h:offs['w'] - wbase + h + 1, :] for h in range(nheads)]

    RB = LANE
    I16 = jnp.int16
    HALF = 1 << 15

    def rows(kb, r=0, n=KB):
        return pl.ds(pl.multiple_of(kb * KB + r, SUBLANE), n)

    PART = m_ref.shape[1]

    def fold_sum(x):
        return jnp.sum(x.reshape(x.shape[0] // PART, PART, TQ), axis=0)

    def fold_max(x):
        return jnp.max(x.reshape(x.shape[0] // PART, PART, TQ), axis=0)

    def idx_keys(kb, diagonal):
        for r in range(0, KB, RB):
            s = jnp.zeros((RB, TQ), F32)
            for h in range(nheads):
                win, ko = qis[h]
                lg = lax.dot_general(kipad_ref[ko, rows(kb, r, RB), :], qop_ref[1, win], NT,
                                     preferred_element_type=F32)
                s += jnp.maximum(lg, 0.0) * w_rows[h]
            key = _float_key(s)
            if diagonal:
                krow = lax.broadcasted_iota(jnp.int32, (RB, TQ), 0) + r
                qcol = lax.broadcasted_iota(jnp.int32, (RB, TQ), 1)
                key = jnp.where((krow // CHUNK) <= (qcol // CHUNK), key, INT_MIN)
            key_ref[rows(kb, r, RB), :] = key
            hi_ref[rows(kb, r, RB), :] = (key >> 16).astype(I16)
            lo_ref[rows(kb, r, RB), :] = ((key & (2 * HALF - 1)) - HALF).astype(I16)

    def far_keys(kb, _):
        idx_keys(kb, False)
        return 0

    lax.fori_loop(0, i, far_keys, 0)
    idx_keys(i, True)

    def count(preds):
        def body(kb, cs):
            blk = key_ref[rows(kb), :]
            return tuple(c + fold_sum(jnp.where(p(blk, kb), 1.0, 0.0)) for c, p in zip(cs, preds))
        z = jnp.zeros((PART, TQ), F32)
        cs = lax.fori_loop(0, i + 1, body, (z,) * len(preds))
        return [jnp.sum(c, axis=0, keepdims=True) for c in cs]

    PACK = 2 * SUBLANE
    P16 = 4 * PACK

    def rows16(x):
        return x.reshape(x.shape[0] // PACK, PACK, TQ)

    def as16(v):
        return jnp.broadcast_to(v, (PACK, TQ)).astype(I16)

    def count16(src_ref, pred, flags=False):
        def body(kb, c):
            blk = rows16(src_ref[rows(kb), :])
            ind = jnp.where(pred(blk, kb), blk if flags else I16(1), I16(0)).reshape(KB // P16, P16, TQ)
            for n in range(KB // P16):
                c = c + ind[n]
            return c
        c = lax.fori_loop(0, i + 1, body, jnp.zeros((P16, TQ), I16))
        return jnp.sum(c.astype(F32), axis=0, keepdims=True)

    def bisect16(src_ref, target):
        def step(s, tu):
            cand_u = tu | jnp.left_shift(jnp.int32(1), 15 - s)
            cand = as16(cand_u - HALF)[None]
            cnt = count16(src_ref, lambda blk, kb: blk >= cand)
            return jnp.where(cnt >= target, cand_u, tu)
        return lax.fori_loop(0, 16, step, jnp.zeros((1, TQ), jnp.int32))

    thr_hi = bisect16(hi_ref, float(topk)) - HALF
    thr_hi16 = as16(thr_hi)[None]
    need_lo = topk - count16(hi_ref, lambda blk, kb: blk > thr_hi16)

    def mask_low_halves(kb, _):
        keep_lo = rows16(hi_ref[rows(kb), :]) == thr_hi16
        lom_ref[rows(kb), :] = jnp.where(keep_lo, rows16(lo_ref[rows(kb), :]), I16(-HALF)).reshape(KB, TQ)
        return 0

    lax.fori_loop(0, i + 1, mask_low_halves, 0)
    thr_lo = bisect16(lom_ref, need_lo)
    thr = thr_hi * (2 * HALF) + thr_lo

    cgt, ceq = count([lambda blk, kb: blk > thr, lambda blk, kb: blk == thr])
    need = topk - cgt
    nbits = int(L).bit_length()
    bound_ref[...] = jnp.full((1, TQ), 2 ** nbits - 1, jnp.int32)

    @pl.when(jnp.max(ceq - need) > 0.0)
    def _():
        thr_lo16 = as16(thr_lo - HALF)[None]

        def flag_ties(kb, _):
            tie = ((rows16(hi_ref[rows(kb), :]) == thr_hi16) & (rows16(lo_ref[rows(kb), :]) == thr_lo16))
            lom_ref[rows(kb), :] = jnp.where(tie, I16(1), I16(0)).reshape(KB, TQ)
            return 0

        lax.fori_loop(0, i + 1, flag_ties, 0)
        krow16 = rows16(lax.broadcasted_iota(jnp.int32, (KB, TQ), 0).astype(I16))

        def bisect_idx(step, bnd):
            bit = jnp.left_shift(jnp.int32(1), nbits - 1 - step)
            cand = bnd | bit
            cnt = count16(lom_ref, lambda blk, kb: krow16 < as16(cand - kb * KB)[None], flags=True)
            return jnp.where(cnt <= need, cand, bnd)

        bound_ref[...] = lax.fori_loop(0, nbits, bisect_idx, jnp.zeros((1, TQ), jnp.int32))

    bound = bound_ref[...]

    m_ref[...] = jnp.full(m_ref.shape, NEG, F32)

    def store_logits(kb, bias_of_head):
        for r in range(0, KB, RB):
            blk = key_ref[rows(kb, r, RB), :]
            pos = lax.broadcasted_iota(jnp.int32, (RB, TQ), 0) + (kb * KB + r)
            keep = ((blk > thr) | ((blk == thr) & (pos < bound))) & (blk != INT_MIN)
            mask = jnp.where(keep, 0.0, NEG)
            for h in range(nheads):
                win, ko = qs[h]
                lg = lax.dot_general(kpad_ref[ko, rows(kb, r, RB), :], qop_ref[0, win], NT,
                                     preferred_element_type=F32)
                lg = (lg + bias_of_head(h, r)) * LOG2E + mask
                lg_ref[h, rows(kb, r, RB), :] = lg
                m_ref[h] = jnp.maximum(m_ref[h], fold_max(lg))

    def far_logits(kb, _):
        store_logits(kb, lambda h, r: far_ref[h])
        return 0

    lax.fori_loop(0, i - 1, far_logits, 0)

    @pl.when(i >= 1)
    def _():
        store_logits(i - 1, lambda h, r: band_ref[h, r:r + RB, :])

    store_logits(i, lambda h, r: band_ref[h, KB + r:KB + r + RB, :])

    m_fin = [jnp.max(m_ref[h], axis=0, keepdims=True) for h in range(nheads)]
    acc_ref[...] = jnp.zeros(acc_ref.shape, F32)
    l_ref[...] = jnp.zeros(l_ref.shape, F32)

    def accumulate(kb, _):
        vt = vt_ref[:, rows(kb)]
        for h in range(nheads):
            p = jnp.exp2(lg_ref[h, rows(kb), :] - m_fin[h])
            l_ref[h] += fold_sum(p)
            acc_ref[h] += jnp.dot(vt, p.astype(BF16), preferred_element_type=F32)
        return 0

    lax.fori_loop(0, i + 1, accumulate, 0)

    outs = [acc_ref[h] / jnp.sum(l_ref[h], axis=0, keepdims=True) for h in range(nheads)]
    out = jnp.concatenate(outs, axis=0).T
    z = _lane_window(qx_ref, offs['z'], nheads * hd)
    o_ref[...] = (out * jax.nn.silu(z)).astype(BF16)


def _dsa_mixer(proj, B, L, src, band, far, nheads, TQ):
    width = nheads * ATT_HEAD_DIM
    nq = L // TQ
    hd = ATT_HEAD_DIM
    qx_w = 16 * LANE
    qx_blk = src['att_q'][0] // qx_w
    qx_lo = qx_blk * qx_w
    kv_w = 2 * LANE
    kv_blk = src['att_k'][0] // kv_w
    ki_blk = src['idx_k'][0] // LANE
    offs = {'q': src['att_q'][0] - qx_lo, 'qi': src['idx_q'][0] - qx_lo, 'w': src['idx_w'][0] - qx_lo,
            'z': src['att_z'][0] - qx_lo, 'k': src['att_k'][0] - kv_blk * kv_w,
            'v': src['att_v'][0] - kv_blk * kv_w, 'ki': src['idx_k'][0] - ki_blk * LANE}
    assert offs['z'] + width + LANE <= qx_w and qx_lo + qx_w <= proj.shape[1]
    assert offs['k'] % LANE + hd <= LANE and offs['ki'] + hd <= LANE
    assert offs['v'] % SUBLANE == 0 and offs['v'] + hd <= kv_w
    assert offs['w'] // LANE == (offs['w'] + nheads - 1) // LANE
    win_off = _head_offsets(offs, nheads, hd)
    win = LANE if max(win_off) + hd <= LANE else 2 * LANE
    win_base = max((sorted({(offs[f] + h * hd) // LANE for h in range(nheads)}) for f in ('q', 'qi')), key=len)
    body = functools.partial(_dsa_body, nheads=nheads, hd=hd, topk=min(TOPK, L // 4), offs=offs)
    full = lambda a: pl.BlockSpec(a.shape, lambda b, i: (0,) * a.ndim)
    return pl.pallas_call(
        body,
        grid=(B, nq),
        in_specs=[pl.BlockSpec((TQ, qx_w), lambda b, i: (b * nq + i, qx_blk)),
                  pl.BlockSpec((L, kv_w), lambda b, i: (b, kv_blk)),
                  pl.BlockSpec((L, LANE), lambda b, i: (b, ki_blk)),
                  full(band), full(far)],
        out_specs=pl.BlockSpec((TQ, width), lambda b, i: (b * nq + i, 0)),
        out_shape=jax.ShapeDtypeStruct((B * L, width), BF16),
        scratch_shapes=[pltpu.VMEM((len(win_off), L, win), BF16),
                        pltpu.VMEM((len(win_off), L, win), BF16),
                        pltpu.VMEM((ATT_HEAD_DIM, L), BF16),
                        pltpu.VMEM((2, len(win_base), TQ, win), BF16),
                        pltpu.VMEM((L, TQ), jnp.int32),
                        pltpu.VMEM((L, TQ), jnp.int16),
                        pltpu.VMEM((L, TQ), jnp.int16),
                        pltpu.VMEM((L, TQ), jnp.int16),
                        pltpu.VMEM((nheads, L, TQ), F32),
                        pltpu.VMEM((1, TQ), jnp.int32),
                        pltpu.VMEM((nheads, 4 * SUBLANE, TQ), F32),
                        pltpu.VMEM((nheads, 4 * SUBLANE, TQ), F32),
                        pltpu.VMEM((nheads, ATT_HEAD_DIM, TQ), F32)],
        compiler_params=_cparams(("parallel", "arbitrary")),
        name="dsa_mixer",
    )(proj, proj, proj, band, far)


def _t5_bucket_static(rel):
    nb = N_BUCKETS // 2
    max_exact = nb // 2
    ret = np.where(rel > 0, nb, 0)
    n = np.abs(rel)
    nf = np.maximum(n, 1).astype(np.float64)
    large = max_exact + (np.log(nf / max_exact) / math.log(MAX_DISTANCE / max_exact)
                         * (nb - max_exact)).astype(np.int32)
    large = np.minimum(large, nb - 1)
    return ret + np.where(n < max_exact, n, large)


def _dsa_bias_tables(rel_bias, TQ):
    a = np.arange(2 * TQ)[:, None]
    j = np.arange(TQ)[None, :]
    band_idx = _t5_bucket_static(a - TQ - j)
    idx = jnp.asarray(band_idx, jnp.int32)[None]
    band = jnp.zeros((rel_bias.shape[1],) + band_idx.shape, F32)
    for bucket in np.unique(band_idx):
        band = jnp.where(idx == int(bucket), rel_bias[int(bucket)][:, None, None], band)
    far_bucket = int(_t5_bucket_static(np.array([-(TQ + 1)]))[0])
    assert far_bucket == int(_t5_bucket_static(np.array([-(10 ** 6)]))[0])
    far = jnp.broadcast_to(rel_bias[far_bucket][:, None, None], (rel_bias.shape[1], 1, TQ))
    return band, far


def _layout(d_model):
    s5w = d_model // 4
    ssdw = d_model // 2
    attw = d_model // 4
    cdim = ssdw + 2 * SSD_GROUPS * SSD_STATE
    nh_ssd = ssdw // SSD_HEAD_DIM
    nh_att = attw // ATT_HEAD_DIM
    splits = (s5w, s5w, ssdw, cdim, nh_ssd, attw, ATT_HEAD_DIM, ATT_HEAD_DIM,
              nh_att * IDX_DIM, IDX_DIM, nh_att, attw)
    names = ('s5_u', 's5_z', 'ssd_z', 'ssd_xbc', 'ssd_dt', 'att_q', 'att_k', 'att_v',
             'idx_q', 'idx_k', 'idx_w', 'att_z')
    src, o = {}, 0
    for nme, s in zip(names, splits):
        src[nme] = (o, s)
        o += s
    return src, o


def kernel(x, norm_w, w_in, s5_A_re, s5_A_im, s5_log_dt, s5_B_re, s5_B_im, s5_C_re, s5_C_im, s5_D, s5_glu_w, s5_glu_b, ssd_conv_w, ssd_conv_b, ssd_dt_bias, ssd_A_log, ssd_D, ssd_norm_w, rel_bias, w_out, final_norm_w):
    B, L, d = x.shape
    depth = w_in.shape[0]
    src, total = _layout(d)
    s5w, ssdw, attw = d // 4, d // 2, d // 4
    cdim = ssdw + 2 * SSD_GROUPS * SSD_STATE
    nh_att = attw // ATT_HEAD_DIM
    assert src['s5_u'][0] % s5w == 0 and src['s5_z'][0] % s5w == 0 and src['ssd_z'][0] % ssdw == 0
    assert src['ssd_xbc'][0] % cdim == 0 and src['ssd_dt'][0] % LANE == 0

    tn = 1024
    cuts, shift, padded = [(0, 0)], 0, {}
    for nme, (so, sz) in sorted(src.items(), key=lambda kv: kv[1][0]):
        gap = -(so + shift) % LANE if nme in ('att_q', 'att_k', 'idx_q', 'idx_k', 'att_z') else 0
        if gap:
            shift += gap
            cuts.append((so, shift))
        padded[nme] = (so + shift, sz)
    src = padded
    total_pad = -(-(total + shift) // tn) * tn
    segs = tuple((lo, cuts[n + 1][0] if n + 1 < len(cuts) else total, lo + sh) for n, (lo, sh) in enumerate(cuts))
    w_in_p = _relayout_w_in(jnp.swapaxes(w_in, 1, 2), segs, total_pad, min(256, d))

    TQ = 256
    Q = 128
    band, far = _dsa_bias_tables(rel_bias, TQ)
    tm_in = min(1024, B * L)
    tm_out = min(512, B * L)
    nchunk = 2
    seg = L // SUBLANE

    s5p = jax.vmap(lambda *a: _s5_params(*a, seg, nchunk))(
        s5_A_re, s5_A_im, s5_log_dt, s5_B_re, s5_B_im, s5_C_re, s5_C_im, s5_D, s5_glu_w, s5_glu_b)
    ssdp = _ssd_params(ssd_conv_w, ssd_conv_b, ssd_dt_bias, ssd_A_log, ssd_D, ssd_norm_w)
    emat = _head_expansion(ssdw // SSD_HEAD_DIM)
    norm_w3 = norm_w.reshape(depth, 1, d)

    x2 = x.reshape(B * L, d)
    for l in range(depth):
        proj = _inproj(x2, norm_w3, w_in_p, l, tm_in, tn)
        y_s5 = _s5_mixer(proj, B, L, src['s5_u'][0] // s5w, src['s5_z'][0] // s5w, s5w, s5p, l)
        y_ssd = _ssd_mixer(proj, B, L, src['ssd_z'][0] // ssdw, src['ssd_xbc'][0] // cdim,
                           src['ssd_dt'][0] // LANE, ssdp, emat, l, Q)
        y_att = _dsa_mixer(proj, B, L, src, band, far, nh_att, TQ)
        x2 = _outproj(y_s5, y_ssd, y_att, w_out, l, x2, final_norm_w.reshape(1, d), tm_out,
                      final=(l == depth - 1))
    return x2.reshape(B, L, d)
```

```python
import functools
import math

import numpy as np
import jax
import jax.numpy as jnp
from jax import lax
from jax.experimental import pallas as pl
from jax.experimental.pallas import tpu as pltpu

F32 = jnp.float32
BF16 = jnp.bfloat16

EPS = 1e-6
CHUNK = 64

SSD_HEAD_DIM = 64
SSD_STATE = 128
SSD_GROUPS = 4
SSD_CONV = 4
ATT_HEAD_DIM = 64
IDX_DIM = 64
TOPK = 256
N_BUCKETS = 32
MAX_DISTANCE = 128

LANE = 128
SUBLANE = 8
VMEM_LIMIT = 56 * 1024 * 1024

INT_MIN = -2 ** 31
NEG = -1e30


def _cparams(sem):
    return pltpu.CompilerParams(dimension_semantics=sem, vmem_limit_bytes=VMEM_LIMIT)


def _inproj_body(x_ref, nw_ref, w_ref, o_ref, h_ref):
    @pl.when(pl.program_id(1) == 0)
    def _():
        xf = x_ref[...]
        ms = jnp.mean(xf * xf, axis=-1, keepdims=True)
        h_ref[...] = (xf * lax.rsqrt(ms + EPS) * nw_ref[...]).astype(BF16)

    o_ref[...] = lax.dot_general(h_ref[...], w_ref[...], (((1,), (1,)), ((), ())),
                                 preferred_element_type=F32)


def _inproj(x2, nw, w, layer, tm, tn):
    m, d = x2.shape
    n = w.shape[1]
    return pl.pallas_call(
        _inproj_body,
        grid=(m // tm, n // tn),
        in_specs=[pl.BlockSpec((tm, d), lambda i, j: (i, 0)),
                  pl.BlockSpec((None, 1, d), lambda i, j: (layer, 0, 0)),
                  pl.BlockSpec((None, tn, d), lambda i, j: (layer, j, 0))],
        out_specs=pl.BlockSpec((tm, tn), lambda i, j: (i, j)),
        out_shape=jax.ShapeDtypeStruct((m, n), F32),
        scratch_shapes=[pltpu.VMEM((tm, d), BF16)],
        compiler_params=_cparams(("parallel", "arbitrary")),
        name="inproj",
    )(x2, nw, w)


def _relayout_body(w_ref, o_ref, *, segs):
    pack = 2 * SUBLANE
    pos = 0
    for src_lo, src_hi, dst_lo in segs:
        if dst_lo > pos:
            o_ref[pos:dst_lo, :] = jnp.zeros((dst_lo - pos, o_ref.shape[1]), BF16)
        val = w_ref[src_lo:src_hi, :]
        fill = -(src_hi - src_lo) % pack
        if fill:
            val = jnp.concatenate([val, jnp.zeros((fill, val.shape[1]), F32)], axis=0)
        o_ref[dst_lo:dst_lo + val.shape[0], :] = val.astype(BF16)
        pos = dst_lo + val.shape[0]
    if pos < o_ref.shape[0]:
        o_ref[pos:, :] = jnp.zeros((o_ref.shape[0] - pos, o_ref.shape[1]), BF16)


def _relayout_w_in(w_t, segs, total_pad, tk):
    depth, n, d = w_t.shape
    assert all(lo % SUBLANE == 0 and (hi - lo) % SUBLANE == 0 and dst % (2 * SUBLANE) == 0
               for lo, hi, dst in segs)
    return pl.pallas_call(
        functools.partial(_relayout_body, segs=segs),
        grid=(depth, d // tk),
        in_specs=[pl.BlockSpec((None, n, tk), lambda l, i: (l, 0, i))],
        out_specs=pl.BlockSpec((None, total_pad, tk), lambda l, i: (l, 0, i)),
        out_shape=jax.ShapeDtypeStruct((depth, total_pad, d), BF16),
        compiler_params=_cparams(("parallel", "parallel")),
        name="w_in_relayout",
    )(w_t)


def _outproj_body(ys5_ref, yssd_ref, yatt_ref, w_ref, x_ref, fnw_ref, o_ref, wb_ref, *, w5, wssd, final):
    @pl.when(pl.program_id(0) == 0)
    def _():
        step = 256
        for r in range(0, w_ref.shape[0], step):
            wb_ref[r:r + step, :] = w_ref[r:r + step, :].astype(BF16)

    acc = x_ref[...]
    acc += jnp.dot(ys5_ref[...], wb_ref[0:w5, :], preferred_element_type=F32)
    acc += jnp.dot(yssd_ref[...], wb_ref[w5:w5 + wssd, :], preferred_element_type=F32)
    acc += jnp.dot(yatt_ref[...], wb_ref[w5 + wssd:, :], preferred_element_type=F32)
    if final:
        ms = jnp.mean(acc * acc, axis=-1, keepdims=True)
        acc = acc * lax.rsqrt(ms + EPS) * fnw_ref[...]
    o_ref[...] = acc


def _outproj(ys5, yssd, yatt, w, layer, x2, fnw, tm, final):
    m, d = x2.shape
    w5, wssd, watt = ys5.shape[1], yssd.shape[1], yatt.shape[1]
    body = functools.partial(_outproj_body, w5=w5, wssd=wssd, final=final)
    return pl.pallas_call(
        body,
        grid=(m // tm,),
        in_specs=[pl.BlockSpec((tm, w5), lambda i: (i, 0)),
                  pl.BlockSpec((tm, wssd), lambda i: (i, 0)),
                  pl.BlockSpec((tm, watt), lambda i: (i, 0)),
                  pl.BlockSpec((None,) + w.shape[1:], lambda i: (layer, 0, 0), pipeline_mode=pl.Buffered(1)),
                  pl.BlockSpec((tm, d), lambda i: (i, 0)),
                  pl.BlockSpec((1, d), lambda i: (0, 0))],
        out_specs=pl.BlockSpec((tm, d), lambda i: (i, 0)),
        out_shape=jax.ShapeDtypeStruct((m, d), F32),
        scratch_shapes=[pltpu.VMEM(w.shape[1:], BF16)],
        compiler_params=_cparams(("arbitrary",)),
        name="outproj",
    )(ys5, yssd, yatt, w, x2, fnw)


def _shift_down_one(x):
    rolled = pltpu.roll(x, 1, axis=0)
    row = lax.broadcasted_iota(jnp.int32, x.shape, 0)
    return jnp.where(row == 0, 0.0, rolled)


def _s5_body(u_ref, z_ref, bblk_ref, cblk_ref, lam_ref, lamseg_ref, d_ref, gw_ref, gb_ref,
             o_ref, uperm_ref, xs_ref, yperm_ref, *, nchunk, cw, sw, rt):
    L = u_ref.shape[0]
    seg = L // SUBLANE

    for c in range(nchunk):
        for j in range(SUBLANE):
            for k in range(cw // LANE):
                lo = c * cw + k * LANE
                uperm_ref[k, pl.ds(j, seg, stride=SUBLANE), :] = u_ref[pl.ds(j * seg, seg), lo:lo + LANE]
        for r0 in range(0, L, rt):
            up = jnp.concatenate([uperm_ref[k, r0:r0 + rt, :] for k in range(cw // LANE)], axis=1)
            xs_ref[r0:r0 + rt, :] = jnp.dot(up.astype(BF16), bblk_ref[c], preferred_element_type=F32)
        lr = lam_ref[c, 0]
        li = lam_ref[c, 1]

        def scan_step(tau, carry):
            xr, xi = carry
            row = pl.multiple_of(tau * SUBLANE, SUBLANE)
            nxr = lr * xr - li * xi + xs_ref[pl.ds(row, SUBLANE), 0:sw]
            nxi = lr * xi + li * xr + xs_ref[pl.ds(row, SUBLANE), sw:2 * sw]
            xs_ref[pl.ds(row, SUBLANE), 0:sw] = nxr
            xs_ref[pl.ds(row, SUBLANE), sw:2 * sw] = nxi
            return nxr, nxi

        zero = jnp.zeros((SUBLANE, sw), F32)
        er, ei = lax.fori_loop(0, seg, scan_step, (zero, zero), unroll=8)

        sr = lamseg_ref[c, 0]
        si = lamseg_ref[c, 1]
        cr, ci = zero, zero
        for _ in range(SUBLANE - 1):
            tr = er + (sr * cr - si * ci)
            ti = ei + (sr * ci + si * cr)
            cr, ci = _shift_down_one(tr), _shift_down_one(ti)

        def fix_step(tau, carry):
            fr, fi = carry
            nfr = lr * fr - li * fi
            nfi = lr * fi + li * fr
            row = pl.multiple_of(tau * SUBLANE, SUBLANE)
            xs_ref[pl.ds(row, SUBLANE), 0:sw] += nfr
            xs_ref[pl.ds(row, SUBLANE), sw:2 * sw] += nfi
            return nfr, nfi

        lax.fori_loop(0, seg, fix_step, (cr, ci), unroll=8)

        for r0 in range(0, L, rt):
            yc = jnp.dot(xs_ref[r0:r0 + rt, :].astype(BF16), cblk_ref[c], preferred_element_type=F32)
            for k in range(cw // LANE):
                yperm_ref[c * (cw // LANE) + k, r0:r0 + rt, :] = yc[:, k * LANE:(k + 1) * LANE]

    for j in range(SUBLANE):
        y = jnp.concatenate([yperm_ref[k, pl.ds(j, seg, stride=SUBLANE), :]
                             for k in range(yperm_ref.shape[0])], axis=1)
        y = y + d_ref[...] * u_ref[j * seg:(j + 1) * seg, :]
        y = jax.nn.gelu(y)
        g = jnp.dot(y.astype(BF16), gw_ref[...], preferred_element_type=F32) + gb_ref[...]
        y = y * jax.nn.sigmoid(g)
        o_ref[j * seg:(j + 1) * seg, :] = (y * jax.nn.silu(z_ref[j * seg:(j + 1) * seg, :])).astype(BF16)


def _layer_spec(a, layer):
    return pl.BlockSpec((None,) + a.shape[1:], lambda *_: (layer,) + (0,) * (a.ndim - 1))


def _s5_mixer(proj, B, L, u_blk, z_blk, width, prm, layer):
    bblk, cblk, lam, lamseg, dvec, gw, gb = prm
    nchunk, cw, sw2 = bblk.shape[1:]
    sw = sw2 // 2
    rt = min(512, L)
    body = functools.partial(_s5_body, nchunk=nchunk, cw=cw, sw=sw, rt=rt)
    full = lambda a: _layer_spec(a, layer)
    return pl.pallas_call(
        body,
        grid=(B,),
        in_specs=[pl.BlockSpec((L, width), lambda b: (b, u_blk)),
                  pl.BlockSpec((L, width), lambda b: (b, z_blk)),
                  full(bblk), full(cblk), full(lam), full(lamseg), full(dvec), full(gw), full(gb)],
        out_specs=pl.BlockSpec((L, width), lambda b: (b, 0)),
        out_shape=jax.ShapeDtypeStruct((B * L, width), BF16),
        scratch_shapes=[pltpu.VMEM((cw // LANE, L, LANE), F32),
                        pltpu.VMEM((L, 2 * sw), F32),
                        pltpu.VMEM((width // LANE, L, LANE), F32)],
        compiler_params=_cparams(("parallel",)),
        name="s5_mixer",
    )(proj, proj, bblk, cblk, lam, lamseg, dvec, gw, gb)


def _cpow(re, im, n):
    rr, ri = jnp.ones_like(re), jnp.zeros_like(im)
    br, bi = re, im
    while n:
        if n & 1:
            rr, ri = rr * br - ri * bi, rr * bi + ri * br
        br, bi = br * br - bi * bi, 2.0 * br * bi
        n >>= 1
    return rr, ri


def _s5_params(A_re, A_im, log_dt, B_re, B_im, C_re, C_im, D, glu_w, glu_b, seg, nchunk):
    G, P, C = B_re.shape
    dt = jnp.exp(log_dt)[:, None]
    lre = jnp.minimum(A_re, -1e-4)
    lim = A_im
    mag = jnp.exp(lre * dt)
    lbr = mag * jnp.cos(lim * dt)
    lbi = mag * jnp.sin(lim * dt)
    nr, ni = lbr - 1.0, lbi
    den = lre * lre + lim * lim
    fr = (nr * lre + ni * lim) / den
    fi = (ni * lre - nr * lim) / den
    bbr = fr[..., None] * B_re - fi[..., None] * B_im
    bbi = fr[..., None] * B_im + fi[..., None] * B_re
    gc = G // nchunk
    eye = jnp.eye(gc, dtype=F32)

    def blockdiag_in(bb):
        t = jnp.transpose(bb, (0, 2, 1)).reshape(nchunk, gc, C, P)
        return jnp.einsum('ngcp,gh->ngchp', t, eye).reshape(nchunk, gc * C, gc * P)

    def blockdiag_out(cc):
        t = jnp.transpose(cc, (0, 2, 1)).reshape(nchunk, gc, P, C)
        return jnp.einsum('ngpc,gh->ngphc', t, eye).reshape(nchunk, gc * P, gc * C)

    bblk = jnp.concatenate([blockdiag_in(bbr), blockdiag_in(bbi)], axis=-1).astype(BF16)
    cblk = jnp.concatenate([blockdiag_out(C_re), blockdiag_out(-C_im)], axis=1).astype(BF16)

    def rows(v):
        return jnp.broadcast_to(v.reshape(nchunk, 1, gc * P), (nchunk, SUBLANE, gc * P))

    lam = jnp.stack([rows(lbr), rows(lbi)], axis=1)
    pr, pi = _cpow(lbr, lbi, seg)
    lamseg = jnp.stack([rows(pr), rows(pi)], axis=1)
    return (bblk, cblk, lam, lamseg, D.reshape(1, G * C), glu_w.astype(BF16), glu_b.reshape(1, -1))


def _split3(x):
    h = x.astype(BF16)
    r = x - h.astype(F32)
    m = r.astype(BF16)
    l = (r - m.astype(F32)).astype(BF16)
    return h, m, l


def _dot_f32(a, b3_bf16):
    return jnp.dot(jnp.concatenate(_split3(a), axis=1), b3_bf16, preferred_element_type=F32)


def _ssd_body(z_ref, xbc_ref, dt_ref, cw_ref, cb_ref, dtb_ref, a_ref, dx_ref, nw_ref, e_ref,
              o_ref, state_ref, xpad_ref, *, nheads, hd, ns, ngroups):
    Q = z_ref.shape[0]
    width = nheads * hd
    gw = width // ngroups
    halo = xpad_ref.shape[0] - Q

    @pl.when(pl.program_id(1) == 0)
    def _():
        state_ref[...] = jnp.zeros_like(state_ref)
        xpad_ref[0:halo, :] = jnp.zeros((halo, xpad_ref.shape[1]), F32)

    xpad_ref[halo:halo + Q, :] = xbc_ref[...]
    xp = xpad_ref[...]
    acc = cw_ref[0:1, :] * xp
    for k in range(1, SSD_CONV):
        acc = cw_ref[k:k + 1, :] * xp + pltpu.roll(acc, 1, axis=0)
    xpad_ref[0:halo, :] = xpad_ref[Q:Q + halo, :]
    xc = jax.nn.silu(acc[halo:, :] + cb_ref[...])
    xs = xc[:, 0:width]
    bm = xc[:, width:width + ngroups * ns].astype(BF16)
    cm = xc[:, width + ngroups * ns:].astype(BF16)

    lane = lax.broadcasted_iota(jnp.int32, (Q, LANE), 1)
    dt = jnp.where(lane < nheads, jax.nn.softplus(dt_ref[...] + dtb_ref[...]), 0.0)
    a = dt * a_ref[...]
    rowi = lax.broadcasted_iota(jnp.int32, (Q, Q), 0)
    coli = lax.broadcasted_iota(jnp.int32, (Q, Q), 1)
    tril = coli <= rowi
    acs = _dot_f32_lhs(tril.astype(BF16), a)
    acs_t = acs.T
    last = acs[Q - 1:Q, :]
    emat = e_ref[...]
    dt_x = _expand(dt, emat)
    dec_x = _expand(jnp.exp(acs), emat)
    dte_x = _expand(jnp.exp(last - acs), emat)
    dlast_x = dec_x[Q - 1:Q, :]

    xd = xs * dt_x
    xd_b = xd.astype(BF16)
    xw_b = (xd * dte_x).astype(BF16)
    st = state_ref[...]
    st_b = st.astype(BF16)

    y_parts = []
    new_state = []
    hlane = lax.broadcasted_iota(jnp.int32, (Q, LANE), 1) < hd
    heads_per_group = nheads // ngroups
    for g in range(ngroups):
        cg = cm[:, g * ns:(g + 1) * ns]
        bg = bm[:, g * ns:(g + 1) * ns]
        cb = lax.dot_general(cg, bg, (((1,), (1,)), ((), ())), preferred_element_type=F32)
        y_off = jnp.dot(cg, st_b[:, g * gw:(g + 1) * gw], preferred_element_type=F32)
        pair_out = []
        for pr in range(gw // LANE):
            xpair = xd_b[:, g * gw + pr * LANE: g * gw + (pr + 1) * LANE]
            res = []
            for hh in range(LANE // hd):
                h = g * heads_per_group + pr * (LANE // hd) + hh
                diff = acs[:, h:h + 1] - acs_t[h:h + 1, :]
                s_h = jnp.where(tril, cb * jnp.exp(jnp.minimum(diff, 0.0)), 0.0).astype(BF16)
                res.append(jnp.dot(s_h, xpair, preferred_element_type=F32))
            pair_out.append(jnp.where(hlane, res[0], res[1]))
        y_diag = jnp.concatenate(pair_out, axis=1)
        y_parts.append(y_diag + y_off * dec_x[:, g * gw:(g + 1) * gw])
        upd = lax.dot_general(bg, xw_b[:, g * gw:(g + 1) * gw], (((0,), (0,)), ((), ())),
                              preferred_element_type=F32)
        new_state.append(st[:, g * gw:(g + 1) * gw] * dlast_x[:, g * gw:(g + 1) * gw] + upd)
    state_ref[...] = jnp.concatenate(new_state, axis=1)

    y = jnp.concatenate(y_parts, axis=1) + xs * dx_ref[...]
    gt = y * jax.nn.silu(z_ref[...])
    ms = jnp.mean(gt * gt, axis=-1, keepdims=True)
    o_ref[...] = (gt * lax.rsqrt(ms + EPS) * nw_ref[...]).astype(BF16)


def _dot_f32_lhs(a_bf16, b):
    return jnp.dot(jnp.concatenate([a_bf16] * 3, axis=1), jnp.concatenate(_split3(b), axis=0),
                   preferred_element_type=F32)


def _expand(v, emat):
    return _dot_f32(v, emat)


def _ssd_mixer(proj, B, L, z_blk, xbc_blk, dt_blk, prm, emat, layer, Q):
    cw, cb, dtb, avec, dx, nw = prm
    width = nw.shape[-1]
    cdim = cw.shape[-1]
    nheads = width // SSD_HEAD_DIM
    nt = L // Q
    body = functools.partial(_ssd_body, nheads=nheads, hd=SSD_HEAD_DIM, ns=SSD_STATE, ngroups=SSD_GROUPS)
    full = lambda a: _layer_spec(a, layer)
    return pl.pallas_call(
        body,
        grid=(B, nt),
        in_specs=[pl.BlockSpec((Q, width), lambda b, t: (b * nt + t, z_blk)),
                  pl.BlockSpec((Q, cdim), lambda b, t: (b * nt + t, xbc_blk)),
                  pl.BlockSpec((Q, LANE), lambda b, t: (b * nt + t, dt_blk)),
                  full(cw), full(cb), full(dtb), full(avec), full(dx), full(nw),
                  pl.BlockSpec(emat.shape, lambda b, t: (0, 0))],
        out_specs=pl.BlockSpec((Q, width), lambda b, t: (b * nt + t, 0)),
        out_shape=jax.ShapeDtypeStruct((B * L, width), BF16),
        scratch_shapes=[pltpu.VMEM((SSD_STATE, width), F32),
                        pltpu.VMEM((Q + SUBLANE, cdim), F32)],
        compiler_params=_cparams(("parallel", "arbitrary")),
        name="ssd_mixer",
    )(proj, proj, proj, cw, cb, dtb, avec, dx, nw, emat)


def _ssd_params(conv_w, conv_b, dt_bias, A_log, D, norm_w):
    depth, nheads = dt_bias.shape
    width = nheads * SSD_HEAD_DIM
    pad = lambda v: jnp.pad(v, ((0, 0), (0, LANE - nheads))).reshape(depth, 1, LANE)
    return (conv_w, conv_b.reshape(depth, 1, -1), pad(dt_bias), pad(-jnp.exp(A_log)),
            jnp.repeat(D, SSD_HEAD_DIM, axis=1).reshape(depth, 1, width), norm_w.reshape(depth, 1, width))


def _head_expansion(nheads):
    e = np.zeros((LANE, nheads * SSD_HEAD_DIM), np.float32)
    e[np.repeat(np.arange(nheads), SSD_HEAD_DIM), np.arange(nheads * SSD_HEAD_DIM)] = 1.0
    return jnp.asarray(np.concatenate([e] * 3, axis=0), BF16)


def _float_key(x):
    b = lax.bitcast_convert_type(x, jnp.int32)
    return b ^ ((b >> 31) & 0x7FFFFFFF)


def _place(blk, lane0, width, dst):
    lane = lax.broadcasted_iota(jnp.int32, blk.shape, 1)
    shift = (dst - lane0) % LANE
    r = pltpu.roll(blk, shift, axis=1) if shift else blk
    lo = jnp.where((lane >= dst) & (lane < min(dst + width, LANE)), r, 0.0)
    hi = jnp.where(lane < dst + width - LANE, r, 0.0)
    return jnp.concatenate([lo, hi], axis=1)


def _head_offsets(offs, nheads, hd):
    return sorted({(offs[f] + h * hd) % LANE for f in ('q', 'qi') for h in range(nheads)})


def _lane_window(x, start, width):
    off = start % LANE
    base = start - off
    if off == 0:
        return x[:, start:start + width]
    lane = lax.broadcasted_iota(jnp.int32, (x.shape[0], LANE), 1)
    out = []
    for p in range(width // LANE):
        a = pltpu.roll(x[:, base + p * LANE:base + (p + 1) * LANE], LANE - off, axis=1)
        b = pltpu.roll(x[:, base + (p + 1) * LANE:base + (p + 2) * LANE], LANE - off, axis=1)
        out.append(jnp.where(lane < LANE - off, a, b))
    return jnp.concatenate(out, axis=1)


def _dsa_body(qx_ref, kv_ref, ki_ref, band_ref, far_ref, o_ref,
              kpad_ref, kipad_ref, vt_ref, qop_ref, key_ref, hi_ref, lo_ref, lom_ref, lg_ref, bound_ref,
              m_ref, acc_ref,
              *, nheads, hd, topk, offs):
    i = pl.program_id(1)
    TQ = qx_ref.shape[0]
    KB = TQ
    L = kv_ref.shape[0]
    nkb_total = L // KB
    NT = (((1,), (1,)), ((), ()))
    WIN = kpad_ref.shape[2]

    def head_window(field_lo, h):
        lo = field_lo + h * hd
        return (lo // LANE) * LANE, lo % LANE

    head_offs = _head_offsets(offs, nheads, hd)

    @pl.when(i == 0)
    def _():
        for kb in range(nkb_total):
            r = slice(kb * KB, (kb + 1) * KB)
            kgrp = (offs['k'] // LANE) * LANE
            kblk = kv_ref[r, kgrp:kgrp + LANE]
            kiblk = ki_ref[r, :]
            for n, o in enumerate(head_offs):
                kpad_ref[n, r, :] = _place(kblk, offs['k'] - kgrp, hd, o)[:, :WIN].astype(BF16)
                kipad_ref[n, r, :] = _place(kiblk, offs['ki'], hd, o)[:, :WIN].astype(BF16)
            vt_ref[0:hd, r] = kv_ref[r, :].T[offs['v']:offs['v'] + hd, :].astype(BF16)
            vt_ref[hd:, r] = jnp.ones((vt_ref.shape[0] - hd, KB), BF16)

    def stage_windows(slot, field_lo, scale):
        bases = sorted({head_window(field_lo, h)[0] for h in range(nheads)})
        for n, base in enumerate(bases):
            qop_ref[slot, n] = (qx_ref[:, base:base + WIN] * scale).astype(BF16)
        return [(bases.index(head_window(field_lo, h)[0]), head_offs.index(head_window(field_lo, h)[1]))
                for h in range(nheads)]

    qs = stage_windows(0, offs['q'], hd ** -0.5)
    qis = stage_windows(1, offs['qi'], 1.0)
    wbase = (offs['w'] // LANE) * LANE
    wt = qx_ref[:, wbase:wbase + LANE].T * ((nheads * IDX_DIM) ** -0.5)
    w_rows = [wt[offs['w'] - wbase + h:offs['w'] - wbase + h + 1, :] for h in range(nheads)]

    RB = LANE
    I16 = jnp.int16
    HALF = 1 << 15

    def rows(kb, r=0, n=KB):
        return pl.ds(pl.multiple_of(kb * KB + r, SUBLANE), n)

    PART = m_ref.shape[1]

    def fold_sum(x):
        return jnp.sum(x.reshape(x.shape[0] // PART, PART, TQ), axis=0)

    def idx_keys(kb, diagonal):
        for r in range(0, KB, RB):
            s = jnp.zeros((RB, TQ), F32)
            for h in range(nheads):
                win, ko = qis[h]
                lg = lax.dot_general(kipad_ref[ko, rows(kb, r, RB), :], qop_ref[1, win], NT,
                                     preferred_element_type=F32)
                s += jnp.maximum(lg, 0.0) * w_rows[h]
            key = _float_key(s)
            if diagonal:
                krow = lax.broadcasted_iota(jnp.int32, (RB, TQ), 0) + r
                qcol = lax.broadcasted_iota(jnp.int32, (RB, TQ), 1)
                key = jnp.where((krow // CHUNK) <= (qcol // CHUNK), key, INT_MIN)
            key_ref[rows(kb, r, RB), :] = key
            hi_ref[rows(kb, r, RB), :] = (key >> 16).astype(I16)
            lo_ref[rows(kb, r, RB), :] = ((key & (2 * HALF - 1)) - HALF).astype(I16)

    def far_keys(kb, _):
        idx_keys(kb, False)
        return 0

    lax.fori_loop(0, i, far_keys, 0)
    idx_keys(i, True)

    def count(preds):
        def body(kb, cs):
            blk = key_ref[rows(kb), :]
            return tuple(c + fold_sum(jnp.where(p(blk, kb), 1.0, 0.0)) for c, p in zip(cs, preds))
        z = jnp.zeros((PART, TQ), F32)
        cs = lax.fori_loop(0, i + 1, body, (z,) * len(preds))
        return [jnp.sum(c, axis=0, keepdims=True) for c in cs]

    PACK = 2 * SUBLANE
    P16 = 4 * PACK

    def rows16(x):
        return x.reshape(x.shape[0] // PACK, PACK, TQ)

    def as16(v):
        return jnp.broadcast_to(v, (PACK, TQ)).astype(I16)

    def count16(src_ref, pred, flags=False):
        def body(kb, c):
            blk = rows16(src_ref[rows(kb), :])
            ind = jnp.where(pred(blk, kb), blk if flags else I16(1), I16(0)).reshape(KB // P16, P16, TQ)
            for n in range(KB // P16):
                c = c + ind[n]
            return c
        c = lax.fori_loop(0, i + 1, body, jnp.zeros((P16, TQ), I16))
        return jnp.sum(c.astype(F32), axis=0, keepdims=True)

    def bisect16(src_ref, target):
        def step(s, tu):
            cand_u = tu | jnp.left_shift(jnp.int32(1), 15 - s)
            cand = as16(cand_u - HALF)[None]
            cnt = count16(src_ref, lambda blk, kb: blk >= cand)
            return jnp.where(cnt >= target, cand_u, tu)
        return lax.fori_loop(0, 16, step, jnp.zeros((1, TQ), jnp.int32))

    thr_hi = bisect16(hi_ref, float(topk)) - HALF
    thr_hi16 = as16(thr_hi)[None]
    need_lo = topk - count16(hi_ref, lambda blk, kb: blk > thr_hi16)

    def mask_low_halves(kb, _):
        keep_lo = rows16(hi_ref[rows(kb), :]) == thr_hi16
        lom_ref[rows(kb), :] = jnp.where(keep_lo, rows16(lo_ref[rows(kb), :]), I16(-HALF)).reshape(KB, TQ)
        return 0

    lax.fori_loop(0, i + 1, mask_low_halves, 0)
    thr_lo = bisect16(lom_ref, need_lo)
    thr = thr_hi * (2 * HALF) + thr_lo

    cgt, ceq = count([lambda blk, kb: blk > thr, lambda blk, kb: blk == thr])
    need = topk - cgt
    nbits = int(L).bit_length()
    bound_ref[...] = jnp.full((1, TQ), 2 ** nbits - 1, jnp.int32)

    @pl.when(jnp.max(ceq - need) > 0.0)
    def _():
        thr_lo16 = as16(thr_lo - HALF)[None]

        def flag_ties(kb, _):
            tie = ((rows16(hi_ref[rows(kb), :]) == thr_hi16) & (rows16(lo_ref[rows(kb), :]) == thr_lo16))
            lom_ref[rows(kb), :] = jnp.where(tie, I16(1), I16(0)).reshape(KB, TQ)
            return 0

        lax.fori_loop(0, i + 1, flag_ties, 0)
        krow16 = rows16(lax.broadcasted_iota(jnp.int32, (KB, TQ), 0).astype(I16))

        def bisect_idx(step, bnd):
            bit = jnp.left_shift(jnp.int32(1), nbits - 1 - step)
            cand = bnd | bit
            cnt = count16(lom_ref, lambda blk, kb: krow16 < as16(cand - kb * KB)[None], flags=True)
            return jnp.where(cnt <= need, cand, bnd)

        bound_ref[...] = lax.fori_loop(0, nbits, bisect_idx, jnp.zeros((1, TQ), jnp.int32))

    bound = bound_ref[...]

    m_ref[...] = jnp.full(m_ref.shape, NEG, F32)

    def fold_max(x):
        return jnp.max(x.reshape(x.shape[0] // PART, PART, TQ), axis=0)

    def store_logits(kb, bias_of_head):
        for r in range(0, KB, RB):
            blk = key_ref[rows(kb, r, RB), :]
            pos = lax.broadcasted_iota(jnp.int32, (RB, TQ), 0) + (kb * KB + r)
            keep = ((blk > thr) | ((blk == thr) & (pos < bound))) & (blk != INT_MIN)
            mask = jnp.where(keep, 0.0, NEG)
            for h in range(nheads):
                win, ko = qs[h]
                lg = lax.dot_general(kpad_ref[ko, rows(kb, r, RB), :], qop_ref[0, win], NT,
                                     preferred_element_type=F32)
                lg = lg + bias_of_head(h, r) + mask
                lg_ref[h, rows(kb, r, RB), :] = lg
                m_ref[h] = jnp.maximum(m_ref[h], fold_max(lg))

    def far_logits(kb, _):
        store_logits(kb, lambda h, r: far_ref[h])
        return 0

    lax.fori_loop(0, i - 1, far_logits, 0)

    @pl.when(i >= 1)
    def _():
        store_logits(i - 1, lambda h, r: band_ref[h, r:r + RB, :])

    store_logits(i, lambda h, r: band_ref[h, KB + r:KB + r + RB, :])

    m_fin = [jnp.max(m_ref[h], axis=0, keepdims=True) for h in range(nheads)]
    acc_ref[...] = jnp.zeros(acc_ref.shape, F32)

    def accumulate(kb, _):
        vt = vt_ref[:, rows(kb)]
        for h in range(nheads):
            p = jnp.exp(lg_ref[h, rows(kb), :] - m_fin[h])
            acc_ref[h] += jnp.dot(vt, p.astype(BF16), preferred_element_type=F32)
        return 0

    lax.fori_loop(0, i + 1, accumulate, 0)

    outs = [acc_ref[h, 0:hd, :] / acc_ref[h, hd:hd + 1, :] for h in range(nheads)]
    out = jnp.concatenate(outs, axis=0).T
    z = _lane_window(qx_ref, offs['z'], nheads * hd)
    o_ref[...] = (out * jax.nn.silu(z)).astype(BF16)


def _dsa_mixer(proj, B, L, src, band, far, nheads, TQ):
    width = nheads * ATT_HEAD_DIM
    nq = L // TQ
    hd = ATT_HEAD_DIM
    qx_w = 16 * LANE
    qx_blk = src['att_q'][0] // qx_w
    qx_lo = qx_blk * qx_w
    kv_w = 2 * LANE
    kv_blk = src['att_k'][0] // kv_w
    ki_blk = src['idx_k'][0] // LANE
    offs = {'q': src['att_q'][0] - qx_lo, 'qi': src['idx_q'][0] - qx_lo, 'w': src['idx_w'][0] - qx_lo,
            'z': src['att_z'][0] - qx_lo, 'k': src['att_k'][0] - kv_blk * kv_w,
            'v': src['att_v'][0] - kv_blk * kv_w, 'ki': src['idx_k'][0] - ki_blk * LANE}
    assert offs['z'] + width + LANE <= qx_w and qx_lo + qx_w <= proj.shape[1]
    assert offs['k'] % LANE + hd <= LANE and offs['ki'] + hd <= LANE
    assert offs['v'] % SUBLANE == 0 and offs['v'] + hd <= kv_w
    assert offs['w'] // LANE == (offs['w'] + nheads - 1) // LANE
    win_off = _head_offsets(offs, nheads, hd)
    win = LANE if max(win_off) + hd <= LANE else 2 * LANE
    win_base = max((sorted({(offs[f] + h * hd) // LANE for h in range(nheads)}) for f in ('q', 'qi')), key=len)
    body = functools.partial(_dsa_body, nheads=nheads, hd=hd, topk=min(TOPK, L // 4), offs=offs)
    full = lambda a: pl.BlockSpec(a.shape, lambda b, i: (0,) * a.ndim)
    return pl.pallas_call(
        body,
        grid=(B, nq),
        in_specs=[pl.BlockSpec((TQ, qx_w), lambda b, i: (b * nq + i, qx_blk)),
                  pl.BlockSpec((L, kv_w), lambda b, i: (b, kv_blk)),
                  pl.BlockSpec((L, LANE), lambda b, i: (b, ki_blk)),
                  full(band), full(far)],
        out_specs=pl.BlockSpec((TQ, width), lambda b, i: (b * nq + i, 0)),
        out_shape=jax.ShapeDtypeStruct((B * L, width), BF16),
        scratch_shapes=[pltpu.VMEM((len(win_off), L, win), BF16),
                        pltpu.VMEM((len(win_off), L, win), BF16),
                        pltpu.VMEM((ATT_HEAD_DIM + 2 * SUBLANE, L), BF16),
                        pltpu.VMEM((2, len(win_base), TQ, win), BF16),
                        pltpu.VMEM((L, TQ), jnp.int32),
                        pltpu.VMEM((L, TQ), jnp.int16),
                        pltpu.VMEM((L, TQ), jnp.int16),
                        pltpu.VMEM((L, TQ), jnp.int16),
                        pltpu.VMEM((nheads, L, TQ), F32),
                        pltpu.VMEM((1, TQ), jnp.int32),
                        pltpu.VMEM((nheads, 4 * SUBLANE, TQ), F32),
                        pltpu.VMEM((nheads, ATT_HEAD_DIM + 2 * SUBLANE, TQ), F32)],
        compiler_params=_cparams(("parallel", "arbitrary")),
        name="dsa_mixer",
    )(proj, proj, proj, band, far)


def _t5_bucket_static(rel):
    nb = N_BUCKETS // 2
    max_exact = nb // 2
    ret = np.where(rel > 0, nb, 0)
    n = np.abs(rel)
    nf = np.maximum(n, 1).astype(np.float64)
    large = max_exact + (np.log(nf / max_exact) / math.log(MAX_DISTANCE / max_exact)
                         * (nb - max_exact)).astype(np.int32)
    large = np.minimum(large, nb - 1)
    return ret + np.where(n < max_exact, n, large)


def _dsa_bias_tables(rel_bias, TQ):
    a = np.arange(2 * TQ)[:, None]
    j = np.arange(TQ)[None, :]
    band_idx = _t5_bucket_static(a - TQ - j)
    idx = jnp.asarray(band_idx, jnp.int32)[None]
    band = jnp.zeros((rel_bias.shape[1],) + band_idx.shape, F32)
    for bucket in np.unique(band_idx):
        band = jnp.where(idx == int(bucket), rel_bias[int(bucket)][:, None, None], band)
    far_bucket = int(_t5_bucket_static(np.array([-(TQ + 1)]))[0])
    assert far_bucket == int(_t5_bucket_static(np.array([-(10 ** 6)]))[0])
    far = jnp.broadcast_to(rel_bias[far_bucket][:, None, None], (rel_bias.shape[1], 1, TQ))
    return band, far


def _layout(d_model):
    s5w = d_model // 4
    ssdw = d_model // 2
    attw = d_model // 4
    cdim = ssdw + 2 * SSD_GROUPS * SSD_STATE
    nh_ssd = ssdw // SSD_HEAD_DIM
    nh_att = attw // ATT_HEAD_DIM
    splits = (s5w, s5w, ssdw, cdim, nh_ssd, attw, ATT_HEAD_DIM, ATT_HEAD_DIM,
              nh_att * IDX_DIM, IDX_DIM, nh_att, attw)
    names = ('s5_u', 's5_z', 'ssd_z', 'ssd_xbc', 'ssd_dt', 'att_q', 'att_k', 'att_v',
             'idx_q', 'idx_k', 'idx_w', 'att_z')
    src, o = {}, 0
    for nme, s in zip(names, splits):
        src[nme] = (o, s)
        o += s
    return src, o


def kernel(x, norm_w, w_in, s5_A_re, s5_A_im, s5_log_dt, s5_B_re, s5_B_im, s5_C_re, s5_C_im, s5_D, s5_glu_w, s5_glu_b, ssd_conv_w, ssd_conv_b, ssd_dt_bias, ssd_A_log, ssd_D, ssd_norm_w, rel_bias, w_out, final_norm_w):
    B, L, d = x.shape
    depth = w_in.shape[0]
    src, total = _layout(d)
    s5w, ssdw, attw = d // 4, d // 2, d // 4
    cdim = ssdw + 2 * SSD_GROUPS * SSD_STATE
    nh_att = attw // ATT_HEAD_DIM
    assert src['s5_u'][0] % s5w == 0 and src['s5_z'][0] % s5w == 0 and src['ssd_z'][0] % ssdw == 0
    assert src['ssd_xbc'][0] % cdim == 0 and src['ssd_dt'][0] % LANE == 0

    tn = 1024
    cuts, shift, padded = [(0, 0)], 0, {}
    for nme, (so, sz) in sorted(src.items(), key=lambda kv: kv[1][0]):
        gap = -(so + shift) % LANE if nme in ('att_q', 'att_k', 'idx_q', 'idx_k', 'att_z') else 0
        if gap:
            shift += gap
            cuts.append((so, shift))
        padded[nme] = (so + shift, sz)
    src = padded
    total_pad = -(-(total + shift) // tn) * tn
    segs = tuple((lo, cuts[n + 1][0] if n + 1 < len(cuts) else total, lo + sh) for n, (lo, sh) in enumerate(cuts))
    w_in_p = _relayout_w_in(jnp.swapaxes(w_in, 1, 2), segs, total_pad, min(256, d))

    TQ = 256
    Q = 128
    band, far = _dsa_bias_tables(rel_bias, TQ)
    tm_in = min(1024, B * L)
    tm_out = min(512, B * L)
    nchunk = 2
    seg = L // SUBLANE

    s5p = jax.vmap(lambda *a: _s5_params(*a, seg, nchunk))(
        s5_A_re, s5_A_im, s5_log_dt, s5_B_re, s5_B_im, s5_C_re, s5_C_im, s5_D, s5_glu_w, s5_glu_b)
    ssdp = _ssd_params(ssd_conv_w, ssd_conv_b, ssd_dt_bias, ssd_A_log, ssd_D, ssd_norm_w)
    emat = _head_expansion(ssdw // SSD_HEAD_DIM)
    norm_w3 = norm_w.reshape(depth, 1, d)

    x2 = x.reshape(B * L, d)
    for l in range(depth):
        proj = _inproj(x2, norm_w3, w_in_p, l, tm_in, tn)
        y_s5 = _s5_mixer(proj, B, L, src['s5_u'][0] // s5w, src['s5_z'][0] // s5w, s5w, s5p, l)
        y_ssd = _ssd_mixer(proj, B, L, src['ssd_z'][0] // ssdw, src['ssd_xbc'][0] // cdim,
                           src['ssd_dt'][0] // LANE, ssdp, emat, l, Q)
        y_att = _dsa_mixer(proj, B, L, src, band, far, nh_att, TQ)
        x2 = _outproj(y_s5, y_ssd, y_att, w_out, l, x2, final_norm_w.reshape(1, d), tm_out,
                      final=(l == depth - 1))
    return x2.reshape(B, L, d)
```

```python
import functools
import math

import numpy as np
import jax
import jax.numpy as jnp
from jax import lax
from jax.experimental import pallas as pl
from jax.experimental.pallas import tpu as pltpu

F32 = jnp.float32
BF16 = jnp.bfloat16

EPS = 1e-6
CHUNK = 64

SSD_HEAD_DIM = 64
SSD_STATE = 128
SSD_GROUPS = 4
SSD_CONV = 4
ATT_HEAD_DIM = 64
IDX_DIM = 64
TOPK = 256
N_BUCKETS = 32
MAX_DISTANCE = 128

LANE = 128
SUBLANE = 8
VMEM_LIMIT = 56 * 1024 * 1024

INT_MIN = -2 ** 31
NEG = -1e30


def _cparams(sem):
    return pltpu.CompilerParams(dimension_semantics=sem, vmem_limit_bytes=VMEM_LIMIT)


def _inproj_body(x_ref, nw_ref, w_ref, o_ref, h_ref):
    @pl.when(pl.program_id(1) == 0)
    def _():
        xf = x_ref[...]
        ms = jnp.mean(xf * xf, axis=-1, keepdims=True)
        h_ref[...] = (xf * lax.rsqrt(ms + EPS) * nw_ref[...]).astype(BF16)

    o_ref[...] = lax.dot_general(h_ref[...], w_ref[...], (((1,), (1,)), ((), ())),
                                 preferred_element_type=F32)


def _inproj(x2, nw, w, layer, tm, tn):
    m, d = x2.shape
    n = w.shape[1]
    return pl.pallas_call(
        _inproj_body,
        grid=(m // tm, n // tn),
        in_specs=[pl.BlockSpec((tm, d), lambda i, j: (i, 0)),
                  pl.BlockSpec((None, 1, d), lambda i, j: (layer, 0, 0)),
                  pl.BlockSpec((None, tn, d), lambda i, j: (layer, j, 0))],
        out_specs=pl.BlockSpec((tm, tn), lambda i, j: (i, j)),
        out_shape=jax.ShapeDtypeStruct((m, n), F32),
        scratch_shapes=[pltpu.VMEM((tm, d), BF16)],
        compiler_params=_cparams(("parallel", "arbitrary")),
        name="inproj",
    )(x2, nw, w)


def _relayout_body(w_ref, o_ref, *, segs):
    pack = 2 * SUBLANE
    pos = 0
    for src_lo, src_hi, dst_lo in segs:
        if dst_lo > pos:
            o_ref[pos:dst_lo, :] = jnp.zeros((dst_lo - pos, o_ref.shape[1]), BF16)
        val = w_ref[src_lo:src_hi, :]
        fill = -(src_hi - src_lo) % pack
        if fill:
            val = jnp.concatenate([val, jnp.zeros((fill, val.shape[1]), F32)], axis=0)
        o_ref[dst_lo:dst_lo + val.shape[0], :] = val.astype(BF16)
        pos = dst_lo + val.shape[0]
    if pos < o_ref.shape[0]:
        o_ref[pos:, :] = jnp.zeros((o_ref.shape[0] - pos, o_ref.shape[1]), BF16)


def _relayout_w_in(w_t, segs, total_pad, tk):
    depth, n, d = w_t.shape
    assert all(lo % SUBLANE == 0 and (hi - lo) % SUBLANE == 0 and dst % (2 * SUBLANE) == 0
               for lo, hi, dst in segs)
    return pl.pallas_call(
        functools.partial(_relayout_body, segs=segs),
        grid=(depth, d // tk),
        in_specs=[pl.BlockSpec((None, n, tk), lambda l, i: (l, 0, i))],
        out_specs=pl.BlockSpec((None, total_pad, tk), lambda l, i: (l, 0, i)),
        out_shape=jax.ShapeDtypeStruct((depth, total_pad, d), BF16),
        compiler_params=_cparams(("parallel", "parallel")),
        name="w_in_relayout",
    )(w_t)


def _outproj_body(ys5_ref, yssd_ref, yatt_ref, w_ref, x_ref, fnw_ref, o_ref, wb_ref, *, w5, wssd, final):
    @pl.when(pl.program_id(0) == 0)
    def _():
        step = 256
        for r in range(0, w_ref.shape[0], step):
            wb_ref[r:r + step, :] = w_ref[r:r + step, :].astype(BF16)

    acc = x_ref[...]
    acc += jnp.dot(ys5_ref[...], wb_ref[0:w5, :], preferred_element_type=F32)
    acc += jnp.dot(yssd_ref[...], wb_ref[w5:w5 + wssd, :], preferred_element_type=F32)
    acc += jnp.dot(yatt_ref[...], wb_ref[w5 + wssd:, :], preferred_element_type=F32)
    if final:
        ms = jnp.mean(acc * acc, axis=-1, keepdims=True)
        acc = acc * lax.rsqrt(ms + EPS) * fnw_ref[...]
    o_ref[...] = acc


def _outproj(ys5, yssd, yatt, w, layer, x2, fnw, tm, final):
    m, d = x2.shape
    w5, wssd, watt = ys5.shape[1], yssd.shape[1], yatt.shape[1]
    body = functools.partial(_outproj_body, w5=w5, wssd=wssd, final=final)
    return pl.pallas_call(
        body,
        grid=(m // tm,),
        in_specs=[pl.BlockSpec((tm, w5), lambda i: (i, 0)),
                  pl.BlockSpec((tm, wssd), lambda i: (i, 0)),
                  pl.BlockSpec((tm, watt), lambda i: (i, 0)),
                  pl.BlockSpec((None,) + w.shape[1:], lambda i: (layer, 0, 0), pipeline_mode=pl.Buffered(1)),
                  pl.BlockSpec((tm, d), lambda i: (i, 0)),
                  pl.BlockSpec((1, d), lambda i: (0, 0))],
        out_specs=pl.BlockSpec((tm, d), lambda i: (i, 0)),
        out_shape=jax.ShapeDtypeStruct((m, d), F32),
        scratch_shapes=[pltpu.VMEM(w.shape[1:], BF16)],
        compiler_params=_cparams(("arbitrary",)),
        name="outproj",
    )(ys5, yssd, yatt, w, x2, fnw)


def _shift_down_one(x):
    rolled = pltpu.roll(x, 1, axis=0)
    row = lax.broadcasted_iota(jnp.int32, x.shape, 0)
    return jnp.where(row == 0, 0.0, rolled)


def _s5_body(u_ref, z_ref, bblk_ref, cblk_ref, lam_ref, lamseg_ref, d_ref, gw_ref, gb_ref,
             o_ref, uperm_ref, xs_ref, yperm_ref, *, nchunk, cw, sw, rt):
    L = u_ref.shape[0]
    seg = L // SUBLANE

    for c in range(nchunk):
        for j in range(SUBLANE):
            for k in range(cw // LANE):
                lo = c * cw + k * LANE
                uperm_ref[k, pl.ds(j, seg, stride=SUBLANE), :] = u_ref[pl.ds(j * seg, seg), lo:lo + LANE]
        for r0 in range(0, L, rt):
            up = jnp.concatenate([uperm_ref[k, r0:r0 + rt, :] for k in range(cw // LANE)], axis=1)
            xs_ref[r0:r0 + rt, :] = jnp.dot(up.astype(BF16), bblk_ref[c], preferred_element_type=F32)
        lr = lam_ref[c, 0]
        li = lam_ref[c, 1]

        def scan_step(tau, carry):
            xr, xi = carry
            row = pl.multiple_of(tau * SUBLANE, SUBLANE)
            nxr = lr * xr - li * xi + xs_ref[pl.ds(row, SUBLANE), 0:sw]
            nxi = lr * xi + li * xr + xs_ref[pl.ds(row, SUBLANE), sw:2 * sw]
            xs_ref[pl.ds(row, SUBLANE), 0:sw] = nxr
            xs_ref[pl.ds(row, SUBLANE), sw:2 * sw] = nxi
            return nxr, nxi

        zero = jnp.zeros((SUBLANE, sw), F32)
        er, ei = lax.fori_loop(0, seg, scan_step, (zero, zero), unroll=8)

        sr = lamseg_ref[c, 0]
        si = lamseg_ref[c, 1]
        cr, ci = zero, zero
        for _ in range(SUBLANE - 1):
            tr = er + (sr * cr - si * ci)
            ti = ei + (sr * ci + si * cr)
            cr, ci = _shift_down_one(tr), _shift_down_one(ti)

        def fix_step(tau, carry):
            fr, fi = carry
            nfr = lr * fr - li * fi
            nfi = lr * fi + li * fr
            row = pl.multiple_of(tau * SUBLANE, SUBLANE)
            xs_ref[pl.ds(row, SUBLANE), 0:sw] += nfr
            xs_ref[pl.ds(row, SUBLANE), sw:2 * sw] += nfi
            return nfr, nfi

        lax.fori_loop(0, seg, fix_step, (cr, ci), unroll=8)

        for r0 in range(0, L, rt):
            yc = jnp.dot(xs_ref[r0:r0 + rt, :].astype(BF16), cblk_ref[c], preferred_element_type=F32)
            for k in range(cw // LANE):
                yperm_ref[c * (cw // LANE) + k, r0:r0 + rt, :] = yc[:, k * LANE:(k + 1) * LANE]

    for j in range(SUBLANE):
        y = jnp.concatenate([yperm_ref[k, pl.ds(j, seg, stride=SUBLANE), :]
                             for k in range(yperm_ref.shape[0])], axis=1)
        y = y + d_ref[...] * u_ref[j * seg:(j + 1) * seg, :]
        y = jax.nn.gelu(y)
        g = jnp.dot(y.astype(BF16), gw_ref[...], preferred_element_type=F32) + gb_ref[...]
        y = y * jax.nn.sigmoid(g)
        o_ref[j * seg:(j + 1) * seg, :] = (y * jax.nn.silu(z_ref[j * seg:(j + 1) * seg, :])).astype(BF16)


def _layer_spec(a, layer):
    return pl.BlockSpec((None,) + a.shape[1:], lambda *_: (layer,) + (0,) * (a.ndim - 1))


def _s5_mixer(proj, B, L, u_blk, z_blk, width, prm, layer):
    bblk, cblk, lam, lamseg, dvec, gw, gb = prm
    nchunk, cw, sw2 = bblk.shape[1:]
    sw = sw2 // 2
    rt = min(512, L)
    body = functools.partial(_s5_body, nchunk=nchunk, cw=cw, sw=sw, rt=rt)
    full = lambda a: _layer_spec(a, layer)
    return pl.pallas_call(
        body,
        grid=(B,),
        in_specs=[pl.BlockSpec((L, width), lambda b: (b, u_blk)),
                  pl.BlockSpec((L, width), lambda b: (b, z_blk)),
                  full(bblk), full(cblk), full(lam), full(lamseg), full(dvec), full(gw), full(gb)],
        out_specs=pl.BlockSpec((L, width), lambda b: (b, 0)),
        out_shape=jax.ShapeDtypeStruct((B * L, width), BF16),
        scratch_shapes=[pltpu.VMEM((cw // LANE, L, LANE), F32),
                        pltpu.VMEM((L, 2 * sw), F32),
                        pltpu.VMEM((width // LANE, L, LANE), F32)],
        compiler_params=_cparams(("parallel",)),
        name="s5_mixer",
    )(proj, proj, bblk, cblk, lam, lamseg, dvec, gw, gb)


def _cpow(re, im, n):
    rr, ri = jnp.ones_like(re), jnp.zeros_like(im)
    br, bi = re, im
    while n:
        if n & 1:
            rr, ri = rr * br - ri * bi, rr * bi + ri * br
        br, bi = br * br - bi * bi, 2.0 * br * bi
        n >>= 1
    return rr, ri


def _s5_params(A_re, A_im, log_dt, B_re, B_im, C_re, C_im, D, glu_w, glu_b, seg, nchunk):
    G, P, C = B_re.shape
    dt = jnp.exp(log_dt)[:, None]
    lre = jnp.minimum(A_re, -1e-4)
    lim = A_im
    mag = jnp.exp(lre * dt)
    lbr = mag * jnp.cos(lim * dt)
    lbi = mag * jnp.sin(lim * dt)
    nr, ni = lbr - 1.0, lbi
    den = lre * lre + lim * lim
    fr = (nr * lre + ni * lim) / den
    fi = (ni * lre - nr * lim) / den
    bbr = fr[..., None] * B_re - fi[..., None] * B_im
    bbi = fr[..., None] * B_im + fi[..., None] * B_re
    gc = G // nchunk
    eye = jnp.eye(gc, dtype=F32)

    def blockdiag_in(bb):
        t = jnp.transpose(bb, (0, 2, 1)).reshape(nchunk, gc, C, P)
        return jnp.einsum('ngcp,gh->ngchp', t, eye).reshape(nchunk, gc * C, gc * P)

    def blockdiag_out(cc):
        t = jnp.transpose(cc, (0, 2, 1)).reshape(nchunk, gc, P, C)
        return jnp.einsum('ngpc,gh->ngphc', t, eye).reshape(nchunk, gc * P, gc * C)

    bblk = jnp.concatenate([blockdiag_in(bbr), blockdiag_in(bbi)], axis=-1).astype(BF16)
    cblk = jnp.concatenate([blockdiag_out(C_re), blockdiag_out(-C_im)], axis=1).astype(BF16)

    def rows(v):
        return jnp.broadcast_to(v.reshape(nchunk, 1, gc * P), (nchunk, SUBLANE, gc * P))

    lam = jnp.stack([rows(lbr), rows(lbi)], axis=1)
    pr, pi = _cpow(lbr, lbi, seg)
    lamseg = jnp.stack([rows(pr), rows(pi)], axis=1)
    return (bblk, cblk, lam, lamseg, D.reshape(1, G * C), glu_w.astype(BF16), glu_b.reshape(1, -1))


def _split3(x):
    h = x.astype(BF16)
    r = x - h.astype(F32)
    m = r.astype(BF16)
    l = (r - m.astype(F32)).astype(BF16)
    return h, m, l


def _dot_f32(a, b3_bf16):
    return jnp.dot(jnp.concatenate(_split3(a), axis=1), b3_bf16, preferred_element_type=F32)


def _ssd_body(z_ref, xbc_ref, dt_ref, cw_ref, cb_ref, dtb_ref, a_ref, dx_ref, nw_ref, e_ref,
              o_ref, state_ref, xpad_ref, *, nheads, hd, ns, ngroups):
    Q = z_ref.shape[0]
    width = nheads * hd
    gw = width // ngroups
    halo = xpad_ref.shape[0] - Q

    @pl.when(pl.program_id(1) == 0)
    def _():
        state_ref[...] = jnp.zeros_like(state_ref)
        xpad_ref[0:halo, :] = jnp.zeros((halo, xpad_ref.shape[1]), F32)

    xpad_ref[halo:halo + Q, :] = xbc_ref[...]
    xp = xpad_ref[...]
    acc = cw_ref[0:1, :] * xp
    for k in range(1, SSD_CONV):
        acc = cw_ref[k:k + 1, :] * xp + pltpu.roll(acc, 1, axis=0)
    xpad_ref[0:halo, :] = xpad_ref[Q:Q + halo, :]
    xc = jax.nn.silu(acc[halo:, :] + cb_ref[...])
    xs = xc[:, 0:width]
    bm = xc[:, width:width + ngroups * ns].astype(BF16)
    cm = xc[:, width + ngroups * ns:].astype(BF16)

    lane = lax.broadcasted_iota(jnp.int32, (Q, LANE), 1)
    dt = jnp.where(lane < nheads, jax.nn.softplus(dt_ref[...] + dtb_ref[...]), 0.0)
    a = dt * a_ref[...]
    rowi = lax.broadcasted_iota(jnp.int32, (Q, Q), 0)
    coli = lax.broadcasted_iota(jnp.int32, (Q, Q), 1)
    tril = coli <= rowi
    acs = _dot_f32_lhs(tril.astype(BF16), a)
    acs_t = acs.T
    last = acs[Q - 1:Q, :]
    emat = e_ref[...]
    dt_x = _expand(dt, emat)
    dec_x = _expand(jnp.exp(acs), emat)
    dte_x = _expand(jnp.exp(last - acs), emat)
    dlast_x = dec_x[Q - 1:Q, :]

    xd = xs * dt_x
    xd_b = xd.astype(BF16)
    xw_b = (xd * dte_x).astype(BF16)
    st = state_ref[...]
    st_b = st.astype(BF16)

    y_parts = []
    new_state = []
    hlane = lax.broadcasted_iota(jnp.int32, (Q, LANE), 1) < hd
    heads_per_group = nheads // ngroups
    for g in range(ngroups):
        cg = cm[:, g * ns:(g + 1) * ns]
        bg = bm[:, g * ns:(g + 1) * ns]
        cb = lax.dot_general(cg, bg, (((1,), (1,)), ((), ())), preferred_element_type=F32)
        y_off = jnp.dot(cg, st_b[:, g * gw:(g + 1) * gw], preferred_element_type=F32)
        pair_out = []
        for pr in range(gw // LANE):
            xpair = xd_b[:, g * gw + pr * LANE: g * gw + (pr + 1) * LANE]
            res = []
            for hh in range(LANE // hd):
                h = g * heads_per_group + pr * (LANE // hd) + hh
                diff = acs[:, h:h + 1] - acs_t[h:h + 1, :]
                s_h = jnp.where(tril, cb * jnp.exp(jnp.minimum(diff, 0.0)), 0.0).astype(BF16)
                res.append(jnp.dot(s_h, xpair, preferred_element_type=F32))
            pair_out.append(jnp.where(hlane, res[0], res[1]))
        y_diag = jnp.concatenate(pair_out, axis=1)
        y_parts.append(y_diag + y_off * dec_x[:, g * gw:(g + 1) * gw])
        upd = lax.dot_general(bg, xw_b[:, g * gw:(g + 1) * gw], (((0,), (0,)), ((), ())),
                              preferred_element_type=F32)
        new_state.append(st[:, g * gw:(g + 1) * gw] * dlast_x[:, g * gw:(g + 1) * gw] + upd)
    state_ref[...] = jnp.concatenate(new_state, axis=1)

    y = jnp.concatenate(y_parts, axis=1) + xs * dx_ref[...]
    gt = y * jax.nn.silu(z_ref[...])
    ms = jnp.mean(gt * gt, axis=-1, keepdims=True)
    o_ref[...] = (gt * lax.rsqrt(ms + EPS) * nw_ref[...]).astype(BF16)


def _dot_f32_lhs(a_bf16, b):
    return jnp.dot(jnp.concatenate([a_bf16] * 3, axis=1), jnp.concatenate(_split3(b), axis=0),
                   preferred_element_type=F32)


def _expand(v, emat):
    return _dot_f32(v, emat)


def _ssd_mixer(proj, B, L, z_blk, xbc_blk, dt_blk, prm, emat, layer, Q):
    cw, cb, dtb, avec, dx, nw = prm
    width = nw.shape[-1]
    cdim = cw.shape[-1]
    nheads = width // SSD_HEAD_DIM
    nt = L // Q
    body = functools.partial(_ssd_body, nheads=nheads, hd=SSD_HEAD_DIM, ns=SSD_STATE, ngroups=SSD_GROUPS)
    full = lambda a: _layer_spec(a, layer)
    return pl.pallas_call(
        body,
        grid=(B, nt),
        in_specs=[pl.BlockSpec((Q, width), lambda b, t: (b * nt + t, z_blk)),
                  pl.BlockSpec((Q, cdim), lambda b, t: (b * nt + t, xbc_blk)),
                  pl.BlockSpec((Q, LANE), lambda b, t: (b * nt + t, dt_blk)),
                  full(cw), full(cb), full(dtb), full(avec), full(dx), full(nw),
                  pl.BlockSpec(emat.shape, lambda b, t: (0, 0))],
        out_specs=pl.BlockSpec((Q, width), lambda b, t: (b * nt + t, 0)),
        out_shape=jax.ShapeDtypeStruct((B * L, width), BF16),
        scratch_shapes=[pltpu.VMEM((SSD_STATE, width), F32),
                        pltpu.VMEM((Q + SUBLANE, cdim), F32)],
        compiler_params=_cparams(("parallel", "arbitrary")),
        name="ssd_mixer",
    )(proj, proj, proj, cw, cb, dtb, avec, dx, nw, emat)


def _ssd_params(conv_w, conv_b, dt_bias, A_log, D, norm_w):
    depth, nheads = dt_bias.shape
    width = nheads * SSD_HEAD_DIM
    pad = lambda v: jnp.pad(v, ((0, 0), (0, LANE - nheads))).reshape(depth, 1, LANE)
    return (conv_w, conv_b.reshape(depth, 1, -1), pad(dt_bias), pad(-jnp.exp(A_log)),
            jnp.repeat(D, SSD_HEAD_DIM, axis=1).reshape(depth, 1, width), norm_w.reshape(depth, 1, width))


def _head_expansion(nheads):
    e = np.zeros((LANE, nheads * SSD_HEAD_DIM), np.float32)
    e[np.repeat(np.arange(nheads), SSD_HEAD_DIM), np.arange(nheads * SSD_HEAD_DIM)] = 1.0
    return jnp.asarray(np.concatenate([e] * 3, axis=0), BF16)


def _float_key(x):
    b = lax.bitcast_convert_type(x, jnp.int32)
    return b ^ ((b >> 31) & 0x7FFFFFFF)


def _place(blk, lane0, width, dst):
    lane = lax.broadcasted_iota(jnp.int32, blk.shape, 1)
    shift = (dst - lane0) % LANE
    r = pltpu.roll(blk, shift, axis=1) if shift else blk
    lo = jnp.where((lane >= dst) & (lane < min(dst + width, LANE)), r, 0.0)
    hi = jnp.where(lane < dst + width - LANE, r, 0.0)
    return jnp.concatenate([lo, hi], axis=1)


def _head_offsets(offs, nheads, hd):
    return sorted({(offs[f] + h * hd) % LANE for f in ('q', 'qi') for h in range(nheads)})


def _lane_window(x, start, width):
    off = start % LANE
    base = start - off
    if off == 0:
        return x[:, start:start + width]
    lane = lax.broadcasted_iota(jnp.int32, (x.shape[0], LANE), 1)
    out = []
    for p in range(width // LANE):
        a = pltpu.roll(x[:, base + p * LANE:base + (p + 1) * LANE], LANE - off, axis=1)
        b = pltpu.roll(x[:, base + (p + 1) * LANE:base + (p + 2) * LANE], LANE - off, axis=1)
        out.append(jnp.where(lane < LANE - off, a, b))
    return jnp.concatenate(out, axis=1)


def _dsa_body(qx_ref, qxn_ref, kv_ref, ki_ref, band_ref, far_ref, o_ref,
              kpad_ref, kipad_ref, vt_ref, qop_ref, key2_ref, hi2_ref, lo2_ref, lom_ref, lg_ref, bound_ref,
              m_ref, acc_ref,
              *, nheads, hd, topk, offs):
    i = pl.program_id(1)
    cur = lax.rem(i, 2)
    key_ref, hi_ref, lo_ref = key2_ref.at[cur], hi2_ref.at[cur], lo2_ref.at[cur]
    TQ = qx_ref.shape[0]
    KB = TQ
    L = kv_ref.shape[0]
    nkb_total = L // KB
    NT = (((1,), (1,)), ((), ()))
    WIN = kpad_ref.shape[2]

    def head_window(field_lo, h):
        lo = field_lo + h * hd
        return (lo // LANE) * LANE, lo % LANE

    head_offs = _head_offsets(offs, nheads, hd)

    @pl.when(i == 0)
    def _():
        for kb in range(nkb_total):
            r = slice(kb * KB, (kb + 1) * KB)
            kgrp = (offs['k'] // LANE) * LANE
            kblk = kv_ref[r, kgrp:kgrp + LANE]
            kiblk = ki_ref[r, :]
            for n, o in enumerate(head_offs):
                kpad_ref[n, r, :] = _place(kblk, offs['k'] - kgrp, hd, o)[:, :WIN].astype(BF16)
                kipad_ref[n, r, :] = _place(kiblk, offs['ki'], hd, o)[:, :WIN].astype(BF16)
            vt_ref[0:hd, r] = kv_ref[r, :].T[offs['v']:offs['v'] + hd, :].astype(BF16)
            vt_ref[hd:, r] = jnp.ones((vt_ref.shape[0] - hd, KB), BF16)

    def stage_windows(slot, field_lo, scale, src_ref):
        bases = sorted({head_window(field_lo, h)[0] for h in range(nheads)})
        for n, base in enumerate(bases):
            qop_ref[slot, n] = (src_ref[:, base:base + WIN] * scale).astype(BF16)
        return [(bases.index(head_window(field_lo, h)[0]), head_offs.index(head_window(field_lo, h)[1]))
                for h in range(nheads)]

    qs = stage_windows(0, offs['q'], hd ** -0.5, qx_ref)

    def stage_indexer(src_ref):
        table = stage_windows(1, offs['qi'], 1.0, src_ref)
        wbase = (offs['w'] // LANE) * LANE
        wt = src_ref[:, wbase:wbase + LANE].T * ((nheads * IDX_DIM) ** -0.5)
        return table, [wt[offs['w'] - wbase + h:offs['w'] - wbase + h + 1, :] for h in range(nheads)]

    RB = LANE
    I16 = jnp.int16
    HALF = 1 << 15

    def rows(kb, r=0, n=KB):
        return pl.ds(pl.multiple_of(kb * KB + r, SUBLANE), n)

    PART = m_ref.shape[1]

    def fold_sum(x):
        return jnp.sum(x.reshape(x.shape[0] // PART, PART, TQ), axis=0)

    def idx_keys(kb, diagonal, slot, indexer):
        qis, w_rows = indexer
        for r in range(0, KB, RB):
            s = jnp.zeros((RB, TQ), F32)
            for h in range(nheads):
                win, ko = qis[h]
                lg = lax.dot_general(kipad_ref[ko, rows(kb, r, RB), :], qop_ref[1, win], NT,
                                     preferred_element_type=F32)
                s += jnp.maximum(lg, 0.0) * w_rows[h]
            key = _float_key(s)
            if diagonal:
                krow = lax.broadcasted_iota(jnp.int32, (RB, TQ), 0) + r
                qcol = lax.broadcasted_iota(jnp.int32, (RB, TQ), 1)
                key = jnp.where((krow // CHUNK) <= (qcol // CHUNK), key, INT_MIN)
            key2_ref[slot, rows(kb, r, RB), :] = key
            hi2_ref[slot, rows(kb, r, RB), :] = (key >> 16).astype(I16)
            lo2_ref[slot, rows(kb, r, RB), :] = ((key & (2 * HALF - 1)) - HALF).astype(I16)

    @pl.when(i == 0)
    def _():
        idx_keys(0, True, cur, stage_indexer(qx_ref))

    def count(preds):
        def body(kb, cs):
            blk = key_ref[rows(kb), :]
            return tuple(c + fold_sum(jnp.where(p(blk, kb), 1.0, 0.0)) for c, p in zip(cs, preds))
        z = jnp.zeros((PART, TQ), F32)
        cs = lax.fori_loop(0, i + 1, body, (z,) * len(preds))
        return [jnp.sum(c, axis=0, keepdims=True) for c in cs]

    PACK = 2 * SUBLANE
    P16 = 4 * PACK

    def rows16(x):
        return x.reshape(x.shape[0] // PACK, PACK, TQ)

    def as16(v):
        return jnp.broadcast_to(v, (PACK, TQ)).astype(I16)

    def count16(src_ref, pred, flags=False):
        def body(kb, c):
            blk = rows16(src_ref[rows(kb), :])
            ind = jnp.where(pred(blk, kb), blk if flags else I16(1), I16(0)).reshape(KB // P16, P16, TQ)
            for n in range(KB // P16):
                c = c + ind[n]
            return c
        c = lax.fori_loop(0, i + 1, body, jnp.zeros((P16, TQ), I16))
        return jnp.sum(c.astype(F32), axis=0, keepdims=True)

    def bisect16(src_ref, target):
        def step(s, tu):
            cand_u = tu | jnp.left_shift(jnp.int32(1), 15 - s)
            cand = as16(cand_u - HALF)[None]
            cnt = count16(src_ref, lambda blk, kb: blk >= cand)
            return jnp.where(cnt >= target, cand_u, tu)
        return lax.fori_loop(0, 16, step, jnp.zeros((1, TQ), jnp.int32))

    thr_hi = bisect16(hi_ref, float(topk)) - HALF
    thr_hi16 = as16(thr_hi)[None]
    need_lo = topk - count16(hi_ref, lambda blk, kb: blk > thr_hi16)

    def mask_low_halves(kb, _):
        keep_lo = rows16(hi_ref[rows(kb), :]) == thr_hi16
        lom_ref[rows(kb), :] = jnp.where(keep_lo, rows16(lo_ref[rows(kb), :]), I16(-HALF)).reshape(KB, TQ)
        return 0

    lax.fori_loop(0, i + 1, mask_low_halves, 0)
    thr_lo = bisect16(lom_ref, need_lo)
    thr = thr_hi * (2 * HALF) + thr_lo

    cgt, ceq = count([lambda blk, kb: blk > thr, lambda blk, kb: blk == thr])
    need = topk - cgt
    nbits = int(L).bit_length()
    bound_ref[...] = jnp.full((1, TQ), 2 ** nbits - 1, jnp.int32)

    @pl.when(jnp.max(ceq - need) > 0.0)
    def _():
        thr_lo16 = as16(thr_lo - HALF)[None]

        def flag_ties(kb, _):
            tie = ((rows16(hi_ref[rows(kb), :]) == thr_hi16) & (rows16(lo_ref[rows(kb), :]) == thr_lo16))
            lom_ref[rows(kb), :] = jnp.where(tie, I16(1), I16(0)).reshape(KB, TQ)
            return 0

        lax.fori_loop(0, i + 1, flag_ties, 0)
        krow16 = rows16(lax.broadcasted_iota(jnp.int32, (KB, TQ), 0).astype(I16))

        def bisect_idx(step, bnd):
            bit = jnp.left_shift(jnp.int32(1), nbits - 1 - step)
            cand = bnd | bit
            cnt = count16(lom_ref, lambda blk, kb: krow16 < as16(cand - kb * KB)[None], flags=True)
            return jnp.where(cnt <= need, cand, bnd)

        bound_ref[...] = lax.fori_loop(0, nbits, bisect_idx, jnp.zeros((1, TQ), jnp.int32))

    bound = bound_ref[...]

    m_ref[...] = jnp.full(m_ref.shape, NEG, F32)

    def fold_max(x):
        return jnp.max(x.reshape(x.shape[0] // PART, PART, TQ), axis=0)

    def store_logits(kb, bias_of_head):
        for r in range(0, KB, RB):
            blk = key_ref[rows(kb, r, RB), :]
            pos = lax.broadcasted_iota(jnp.int32, (RB, TQ), 0) + (kb * KB + r)
            keep = ((blk > thr) | ((blk == thr) & (pos < bound))) & (blk != INT_MIN)
            mask = jnp.where(keep, 0.0, NEG)
            for h in range(nheads):
                win, ko = qs[h]
                lg = lax.dot_general(kpad_ref[ko, rows(kb, r, RB), :], qop_ref[0, win], NT,
                                     preferred_element_type=F32)
                lg = lg + bias_of_head(h, r) + mask
                lg_ref[h, rows(kb, r, RB), :] = lg
                m_ref[h] = jnp.maximum(m_ref[h], fold_max(lg))

    def far_logits(kb, _):
        store_logits(kb, lambda h, r: far_ref[h])
        return 0

    lax.fori_loop(0, i - 1, far_logits, 0)

    @pl.when(i >= 1)
    def _():
        store_logits(i - 1, lambda h, r: band_ref[h, r:r + RB, :])

    store_logits(i, lambda h, r: band_ref[h, KB + r:KB + r + RB, :])

    m_fin = [jnp.max(m_ref[h], axis=0, keepdims=True) for h in range(nheads)]
    acc_ref[...] = jnp.zeros(acc_ref.shape, F32)

    def accumulate(kb):
        vt = vt_ref[:, rows(kb)]
        for h in range(nheads):
            p = jnp.exp(lg_ref[h, rows(kb), :] - m_fin[h])
            acc_ref[h] += jnp.dot(vt, p.astype(BF16), preferred_element_type=F32)

    has_next = i + 1 < pl.num_programs(1)

    @pl.when(has_next)
    def _():
        indexer = stage_indexer(qxn_ref)

        def both(kb, _):
            accumulate(kb)
            idx_keys(kb, False, 1 - cur, indexer)
            return 0

        lax.fori_loop(0, i + 1, both, 0)
        idx_keys(i + 1, True, 1 - cur, indexer)

    @pl.when(jnp.logical_not(has_next))
    def _():
        def only(kb, _):
            accumulate(kb)
            return 0

        lax.fori_loop(0, i + 1, only, 0)

    outs = [acc_ref[h, 0:hd, :] / acc_ref[h, hd:hd + 1, :] for h in range(nheads)]
    out = jnp.concatenate(outs, axis=0).T
    z = _lane_window(qx_ref, offs['z'], nheads * hd)
    o_ref[...] = (out * jax.nn.silu(z)).astype(BF16)


def _dsa_mixer(proj, B, L, src, band, far, nheads, TQ):
    width = nheads * ATT_HEAD_DIM
    nq = L // TQ
    hd = ATT_HEAD_DIM
    qx_w = 16 * LANE
    qx_blk = src['att_q'][0] // qx_w
    qx_lo = qx_blk * qx_w
    kv_w = 2 * LANE
    kv_blk = src['att_k'][0] // kv_w
    ki_blk = src['idx_k'][0] // LANE
    offs = {'q': src['att_q'][0] - qx_lo, 'qi': src['idx_q'][0] - qx_lo, 'w': src['idx_w'][0] - qx_lo,
            'z': src['att_z'][0] - qx_lo, 'k': src['att_k'][0] - kv_blk * kv_w,
            'v': src['att_v'][0] - kv_blk * kv_w, 'ki': src['idx_k'][0] - ki_blk * LANE}
    assert offs['z'] + width + LANE <= qx_w and qx_lo + qx_w <= proj.shape[1]
    assert offs['k'] % LANE + hd <= LANE and offs['ki'] + hd <= LANE
    assert offs['v'] % SUBLANE == 0 and offs['v'] + hd <= kv_w
    assert offs['w'] // LANE == (offs['w'] + nheads - 1) // LANE
    win_off = _head_offsets(offs, nheads, hd)
    win = LANE if max(win_off) + hd <= LANE else 2 * LANE
    win_base = max((sorted({(offs[f] + h * hd) // LANE for h in range(nheads)}) for f in ('q', 'qi')), key=len)
    body = functools.partial(_dsa_body, nheads=nheads, hd=hd, topk=min(TOPK, L // 4), offs=offs)
    full = lambda a: pl.BlockSpec(a.shape, lambda b, i: (0,) * a.ndim)
    return pl.pallas_call(
        body,
        grid=(B, nq),
        in_specs=[pl.BlockSpec((TQ, qx_w), lambda b, i: (b * nq + i, qx_blk)),
                  pl.BlockSpec((TQ, qx_w), lambda b, i: (b * nq + jnp.minimum(i + 1, nq - 1), qx_blk)),
                  pl.BlockSpec((L, kv_w), lambda b, i: (b, kv_blk)),
                  pl.BlockSpec((L, LANE), lambda b, i: (b, ki_blk)),
                  full(band), full(far)],
        out_specs=pl.BlockSpec((TQ, width), lambda b, i: (b * nq + i, 0)),
        out_shape=jax.ShapeDtypeStruct((B * L, width), BF16),
        scratch_shapes=[pltpu.VMEM((len(win_off), L, win), BF16),
                        pltpu.VMEM((len(win_off), L, win), BF16),
                        pltpu.VMEM((ATT_HEAD_DIM + 2 * SUBLANE, L), BF16),
                        pltpu.VMEM((2, len(win_base), TQ, win), BF16),
                        pltpu.VMEM((2, L, TQ), jnp.int32),
                        pltpu.VMEM((2, L, TQ), jnp.int16),
                        pltpu.VMEM((2, L, TQ), jnp.int16),
                        pltpu.VMEM((L, TQ), jnp.int16),
                        pltpu.VMEM((nheads, L, TQ), F32),
                        pltpu.VMEM((1, TQ), jnp.int32),
                        pltpu.VMEM((nheads, 4 * SUBLANE, TQ), F32),
                        pltpu.VMEM((nheads, ATT_HEAD_DIM + 2 * SUBLANE, TQ), F32)],
        compiler_params=_cparams(("parallel", "arbitrary")),
        name="dsa_mixer",
    )(proj, proj, proj, proj, band, far)


def _t5_bucket_static(rel):
    nb = N_BUCKETS // 2
    max_exact = nb // 2
    ret = np.where(rel > 0, nb, 0)
    n = np.abs(rel)
    nf = np.maximum(n, 1).astype(np.float64)
    large = max_exact + (np.log(nf / max_exact) / math.log(MAX_DISTANCE / max_exact)
                         * (nb - max_exact)).astype(np.int32)
    large = np.minimum(large, nb - 1)
    return ret + np.where(n < max_exact, n, large)


def _dsa_bias_tables(rel_bias, TQ):
    a = np.arange(2 * TQ)[:, None]
    j = np.arange(TQ)[None, :]
    band_idx = _t5_bucket_static(a - TQ - j)
    idx = jnp.asarray(band_idx, jnp.int32)[None]
    band = jnp.zeros((rel_bias.shape[1],) + band_idx.shape, F32)
    for bucket in np.unique(band_idx):
        band = jnp.where(idx == int(bucket), rel_bias[int(bucket)][:, None, None], band)
    far_bucket = int(_t5_bucket_static(np.array([-(TQ + 1)]))[0])
    assert far_bucket == int(_t5_bucket_static(np.array([-(10 ** 6)]))[0])
    far = jnp.broadcast_to(rel_bias[far_bucket][:, None, None], (rel_bias.shape[1], 1, TQ))
    return band, far


def _layout(d_model):
    s5w = d_model // 4
    ssdw = d_model // 2
    attw = d_model // 4
    cdim = ssdw + 2 * SSD_GROUPS * SSD_STATE
    nh_ssd = ssdw // SSD_HEAD_DIM
    nh_att = attw // ATT_HEAD_DIM
    splits = (s5w, s5w, ssdw, cdim, nh_ssd, attw, ATT_HEAD_DIM, ATT_HEAD_DIM,
              nh_att * IDX_DIM, IDX_DIM, nh_att, attw)
    names = ('s5_u', 's5_z', 'ssd_z', 'ssd_xbc', 'ssd_dt', 'att_q', 'att_k', 'att_v',
             'idx_q', 'idx_k', 'idx_w', 'att_z')
    src, o = {}, 0
    for nme, s in zip(names, splits):
        src[nme] = (o, s)
        o += s
    return src, o


def kernel(x, norm_w, w_in, s5_A_re, s5_A_im, s5_log_dt, s5_B_re, s5_B_im, s5_C_re, s5_C_im, s5_D, s5_glu_w, s5_glu_b, ssd_conv_w, ssd_conv_b, ssd_dt_bias, ssd_A_log, ssd_D, ssd_norm_w, rel_bias, w_out, final_norm_w):
    B, L, d = x.shape
    depth = w_in.shape[0]
    src, total = _layout(d)
    s5w, ssdw, attw = d // 4, d // 2, d // 4
    cdim = ssdw + 2 * SSD_GROUPS * SSD_STATE
    nh_att = attw // ATT_HEAD_DIM
    assert src['s5_u'][0] % s5w == 0 and src['s5_z'][0] % s5w == 0 and src['ssd_z'][0] % ssdw == 0
    assert src['ssd_xbc'][0] % cdim == 0 and src['ssd_dt'][0] % LANE == 0

    tn = 1024
    cuts, shift, padded = [(0, 0)], 0, {}
    for nme, (so, sz) in sorted(src.items(), key=lambda kv: kv[1][0]):
        gap = -(so + shift) % LANE if nme in ('att_q', 'att_k', 'idx_q', 'idx_k', 'att_z') else 0
        if gap:
            shift += gap
            cuts.append((so, shift))
        padded[nme] = (so + shift, sz)
    src = padded
    total_pad = -(-(total + shift) // tn) * tn
    segs = tuple((lo, cuts[n + 1][0] if n + 1 < len(cuts) else total, lo + sh) for n, (lo, sh) in enumerate(cuts))
    w_in_p = _relayout_w_in(jnp.swapaxes(w_in, 1, 2), segs, total_pad, min(256, d))

    TQ = 256
    Q = 128
    band, far = _dsa_bias_tables(rel_bias, TQ)
    tm_in = min(1024, B * L)
    tm_out = min(512, B * L)
    nchunk = 2
    seg = L // SUBLANE

    s5p = jax.vmap(lambda *a: _s5_params(*a, seg, nchunk))(
        s5_A_re, s5_A_im, s5_log_dt, s5_B_re, s5_B_im, s5_C_re, s5_C_im, s5_D, s5_glu_w, s5_glu_b)
    ssdp = _ssd_params(ssd_conv_w, ssd_conv_b, ssd_dt_bias, ssd_A_log, ssd_D, ssd_norm_w)
    emat = _head_expansion(ssdw // SSD_HEAD_DIM)
    norm_w3 = norm_w.reshape(depth, 1, d)

    x2 = x.reshape(B * L, d)
    for l in range(depth):
        proj = _inproj(x2, norm_w3, w_in_p, l, tm_in, tn)
        y_s5 = _s5_mixer(proj, B, L, src['s5_u'][0] // s5w, src['s5_z'][0] // s5w, s5w, s5p, l)
        y_ssd = _ssd_mixer(proj, B, L, src['ssd_z'][0] // ssdw, src['ssd_xbc'][0] // cdim,
                           src['ssd_dt'][0] // LANE, ssdp, emat, l, Q)
        y_att = _dsa_mixer(proj, B, L, src, band, far, nh_att, TQ)
        x2 = _outproj(y_s5, y_ssd, y_att, w_out, l, x2, final_norm_w.reshape(1, d), tm_out,
                      final=(l == depth - 1))
    return x2.reshape(B, L, d)
```

```python
import functools
import math

import numpy as np
import jax
import jax.numpy as jnp
from jax import lax
from jax.experimental import pallas as pl
from jax.experimental.pallas import tpu as pltpu

F32 = jnp.float32
BF16 = jnp.bfloat16

EPS = 1e-6
CHUNK = 64

SSD_HEAD_DIM = 64
SSD_STATE = 128
SSD_GROUPS = 4
SSD_CONV = 4
ATT_HEAD_DIM = 64
IDX_DIM = 64
TOPK = 256
N_BUCKETS = 32
MAX_DISTANCE = 128

LANE = 128
SUBLANE = 8
VMEM_LIMIT = 56 * 1024 * 1024

INT_MIN = -2 ** 31
NEG = -1e30


def _cparams(sem):
    return pltpu.CompilerParams(dimension_semantics=sem, vmem_limit_bytes=VMEM_LIMIT)


def _inproj_body(x_ref, nw_ref, w_ref, o_ref, h_ref):
    @pl.when(pl.program_id(1) == 0)
    def _():
        xf = x_ref[...]
        ms = jnp.mean(xf * xf, axis=-1, keepdims=True)
        h_ref[...] = (xf * lax.rsqrt(ms + EPS) * nw_ref[...]).astype(BF16)

    o_ref[...] = lax.dot_general(h_ref[...], w_ref[...], (((1,), (1,)), ((), ())),
                                 preferred_element_type=F32)


def _inproj(x2, nw, w, layer, tm, tn):
    m, d = x2.shape
    n = w.shape[1]
    return pl.pallas_call(
        _inproj_body,
        grid=(m // tm, n // tn),
        in_specs=[pl.BlockSpec((tm, d), lambda i, j: (i, 0)),
                  pl.BlockSpec((None, 1, d), lambda i, j: (layer, 0, 0)),
                  pl.BlockSpec((None, tn, d), lambda i, j: (layer, j, 0))],
        out_specs=pl.BlockSpec((tm, tn), lambda i, j: (i, j)),
        out_shape=jax.ShapeDtypeStruct((m, n), F32),
        scratch_shapes=[pltpu.VMEM((tm, d), BF16)],
        compiler_params=_cparams(("parallel", "arbitrary")),
        name="inproj",
    )(x2, nw, w)


def _relayout_body(w_ref, o_ref, *, segs):
    pack = 2 * SUBLANE
    pos = 0
    for src_lo, src_hi, dst_lo in segs:
        if dst_lo > pos:
            o_ref[pos:dst_lo, :] = jnp.zeros((dst_lo - pos, o_ref.shape[1]), BF16)
        val = w_ref[src_lo:src_hi, :]
        fill = -(src_hi - src_lo) % pack
        if fill:
            val = jnp.concatenate([val, jnp.zeros((fill, val.shape[1]), F32)], axis=0)
        o_ref[dst_lo:dst_lo + val.shape[0], :] = val.astype(BF16)
        pos = dst_lo + val.shape[0]
    if pos < o_ref.shape[0]:
        o_ref[pos:, :] = jnp.zeros((o_ref.shape[0] - pos, o_ref.shape[1]), BF16)


def _relayout_w_in(w_t, segs, total_pad, tk):
    depth, n, d = w_t.shape
    assert all(lo % SUBLANE == 0 and (hi - lo) % SUBLANE == 0 and dst % (2 * SUBLANE) == 0
               for lo, hi, dst in segs)
    return pl.pallas_call(
        functools.partial(_relayout_body, segs=segs),
        grid=(depth, d // tk),
        in_specs=[pl.BlockSpec((None, n, tk), lambda l, i: (l, 0, i))],
        out_specs=pl.BlockSpec((None, total_pad, tk), lambda l, i: (l, 0, i)),
        out_shape=jax.ShapeDtypeStruct((depth, total_pad, d), BF16),
        compiler_params=_cparams(("parallel", "parallel")),
        name="w_in_relayout",
    )(w_t)


def _outproj_body(ys5_ref, yssd_ref, yatt_ref, w_ref, x_ref, fnw_ref, o_ref, wb_ref, *, w5, wssd, final):
    @pl.when(pl.program_id(0) == 0)
    def _():
        step = 256
        for r in range(0, w_ref.shape[0], step):
            wb_ref[r:r + step, :] = w_ref[r:r + step, :].astype(BF16)

    acc = x_ref[...]
    acc += jnp.dot(ys5_ref[...], wb_ref[0:w5, :], preferred_element_type=F32)
    acc += jnp.dot(yssd_ref[...], wb_ref[w5:w5 + wssd, :], preferred_element_type=F32)
    acc += jnp.dot(yatt_ref[...], wb_ref[w5 + wssd:, :], preferred_element_type=F32)
    if final:
        ms = jnp.mean(acc * acc, axis=-1, keepdims=True)
        acc = acc * lax.rsqrt(ms + EPS) * fnw_ref[...]
    o_ref[...] = acc


def _outproj(ys5, yssd, yatt, w, layer, x2, fnw, tm, final):
    m, d = x2.shape
    w5, wssd, watt = ys5.shape[1], yssd.shape[1], yatt.shape[1]
    body = functools.partial(_outproj_body, w5=w5, wssd=wssd, final=final)
    return pl.pallas_call(
        body,
        grid=(m // tm,),
        in_specs=[pl.BlockSpec((tm, w5), lambda i: (i, 0)),
                  pl.BlockSpec((tm, wssd), lambda i: (i, 0)),
                  pl.BlockSpec((tm, watt), lambda i: (i, 0)),
                  pl.BlockSpec((None,) + w.shape[1:], lambda i: (layer, 0, 0), pipeline_mode=pl.Buffered(1)),
                  pl.BlockSpec((tm, d), lambda i: (i, 0)),
                  pl.BlockSpec((1, d), lambda i: (0, 0))],
        out_specs=pl.BlockSpec((tm, d), lambda i: (i, 0)),
        out_shape=jax.ShapeDtypeStruct((m, d), F32),
        scratch_shapes=[pltpu.VMEM(w.shape[1:], BF16)],
        compiler_params=_cparams(("arbitrary",)),
        name="outproj",
    )(ys5, yssd, yatt, w, x2, fnw)


def _shift_down_one(x):
    rolled = pltpu.roll(x, 1, axis=0)
    row = lax.broadcasted_iota(jnp.int32, x.shape, 0)
    return jnp.where(row == 0, 0.0, rolled)


def _s5_body(u_ref, z_ref, bblk_ref, cblk_ref, lam_ref, lamseg_ref, d_ref, gw_ref, gb_ref,
             o_ref, uperm_ref, xs_ref, yperm_ref, *, nchunk, cw, sw, rt):
    L = u_ref.shape[0]
    seg = L // SUBLANE

    for c in range(nchunk):
        for j in range(SUBLANE):
            for k in range(cw // LANE):
                lo = c * cw + k * LANE
                uperm_ref[k, pl.ds(j, seg, stride=SUBLANE), :] = u_ref[pl.ds(j * seg, seg), lo:lo + LANE]
        for r0 in range(0, L, rt):
            up = jnp.concatenate([uperm_ref[k, r0:r0 + rt, :] for k in range(cw // LANE)], axis=1)
            xs_ref[r0:r0 + rt, :] = jnp.dot(up.astype(BF16), bblk_ref[c], preferred_element_type=F32)
        lr = lam_ref[c, 0]
        li = lam_ref[c, 1]

        def scan_step(tau, carry):
            xr, xi = carry
            row = pl.multiple_of(tau * SUBLANE, SUBLANE)
            nxr = lr * xr - li * xi + xs_ref[pl.ds(row, SUBLANE), 0:sw]
            nxi = lr * xi + li * xr + xs_ref[pl.ds(row, SUBLANE), sw:2 * sw]
            xs_ref[pl.ds(row, SUBLANE), 0:sw] = nxr
            xs_ref[pl.ds(row, SUBLANE), sw:2 * sw] = nxi
            return nxr, nxi

        zero = jnp.zeros((SUBLANE, sw), F32)
        er, ei = lax.fori_loop(0, seg, scan_step, (zero, zero), unroll=8)

        sr = lamseg_ref[c, 0]
        si = lamseg_ref[c, 1]
        cr, ci = zero, zero
        for _ in range(SUBLANE - 1):
            tr = er + (sr * cr - si * ci)
            ti = ei + (sr * ci + si * cr)
            cr, ci = _shift_down_one(tr), _shift_down_one(ti)

        def fix_step(tau, carry):
            fr, fi = carry
            nfr = lr * fr - li * fi
            nfi = lr * fi + li * fr
            row = pl.multiple_of(tau * SUBLANE, SUBLANE)
            xs_ref[pl.ds(row, SUBLANE), 0:sw] += nfr
            xs_ref[pl.ds(row, SUBLANE), sw:2 * sw] += nfi
            return nfr, nfi

        lax.fori_loop(0, seg, fix_step, (cr, ci), unroll=8)

        for r0 in range(0, L, rt):
            yc = jnp.dot(xs_ref[r0:r0 + rt, :].astype(BF16), cblk_ref[c], preferred_element_type=F32)
            for k in range(cw // LANE):
                yperm_ref[c * (cw // LANE) + k, r0:r0 + rt, :] = yc[:, k * LANE:(k + 1) * LANE]

    for j in range(SUBLANE):
        y = jnp.concatenate([yperm_ref[k, pl.ds(j, seg, stride=SUBLANE), :]
                             for k in range(yperm_ref.shape[0])], axis=1)
        y = y + d_ref[...] * u_ref[j * seg:(j + 1) * seg, :]
        y = jax.nn.gelu(y)
        g = jnp.dot(y.astype(BF16), gw_ref[...], preferred_element_type=F32) + gb_ref[...]
        y = y * jax.nn.sigmoid(g)
        o_ref[j * seg:(j + 1) * seg, :] = (y * jax.nn.silu(z_ref[j * seg:(j + 1) * seg, :])).astype(BF16)


def _layer_spec(a, layer):
    return pl.BlockSpec((None,) + a.shape[1:], lambda *_: (layer,) + (0,) * (a.ndim - 1))


def _s5_mixer(proj, B, L, u_blk, z_blk, width, prm, layer):
    bblk, cblk, lam, lamseg, dvec, gw, gb = prm
    nchunk, cw, sw2 = bblk.shape[1:]
    sw = sw2 // 2
    rt = min(512, L)
    body = functools.partial(_s5_body, nchunk=nchunk, cw=cw, sw=sw, rt=rt)
    full = lambda a: _layer_spec(a, layer)
    return pl.pallas_call(
        body,
        grid=(B,),
        in_specs=[pl.BlockSpec((L, width), lambda b: (b, u_blk)),
                  pl.BlockSpec((L, width), lambda b: (b, z_blk)),
                  full(bblk), full(cblk), full(lam), full(lamseg), full(dvec), full(gw), full(gb)],
        out_specs=pl.BlockSpec((L, width), lambda b: (b, 0)),
        out_shape=jax.ShapeDtypeStruct((B * L, width), BF16),
        scratch_shapes=[pltpu.VMEM((cw // LANE, L, LANE), F32),
                        pltpu.VMEM((L, 2 * sw), F32),
                        pltpu.VMEM((width // LANE, L, LANE), F32)],
        compiler_params=_cparams(("parallel",)),
        name="s5_mixer",
    )(proj, proj, bblk, cblk, lam, lamseg, dvec, gw, gb)


def _cpow(re, im, n):
    rr, ri = jnp.ones_like(re), jnp.zeros_like(im)
    br, bi = re, im
    while n:
        if n & 1:
            rr, ri = rr * br - ri * bi, rr * bi + ri * br
        br, bi = br * br - bi * bi, 2.0 * br * bi
        n >>= 1
    return rr, ri


def _s5_params(A_re, A_im, log_dt, B_re, B_im, C_re, C_im, D, glu_w, glu_b, seg, nchunk):
    G, P, C = B_re.shape
    dt = jnp.exp(log_dt)[:, None]
    lre = jnp.minimum(A_re, -1e-4)
    lim = A_im
    mag = jnp.exp(lre * dt)
    lbr = mag * jnp.cos(lim * dt)
    lbi = mag * jnp.sin(lim * dt)
    nr, ni = lbr - 1.0, lbi
    den = lre * lre + lim * lim
    fr = (nr * lre + ni * lim) / den
    fi = (ni * lre - nr * lim) / den
    bbr = fr[..., None] * B_re - fi[..., None] * B_im
    bbi = fr[..., None] * B_im + fi[..., None] * B_re
    gc = G // nchunk
    eye = jnp.eye(gc, dtype=F32)

    def blockdiag_in(bb):
        t = jnp.transpose(bb, (0, 2, 1)).reshape(nchunk, gc, C, P)
        return jnp.einsum('ngcp,gh->ngchp', t, eye).reshape(nchunk, gc * C, gc * P)

    def blockdiag_out(cc):
        t = jnp.transpose(cc, (0, 2, 1)).reshape(nchunk, gc, P, C)
        return jnp.einsum('ngpc,gh->ngphc', t, eye).reshape(nchunk, gc * P, gc * C)

    bblk = jnp.concatenate([blockdiag_in(bbr), blockdiag_in(bbi)], axis=-1).astype(BF16)
    cblk = jnp.concatenate([blockdiag_out(C_re), blockdiag_out(-C_im)], axis=1).astype(BF16)

    def rows(v):
        return jnp.broadcast_to(v.reshape(nchunk, 1, gc * P), (nchunk, SUBLANE, gc * P))

    lam = jnp.stack([rows(lbr), rows(lbi)], axis=1)
    pr, pi = _cpow(lbr, lbi, seg)
    lamseg = jnp.stack([rows(pr), rows(pi)], axis=1)
    return (bblk, cblk, lam, lamseg, D.reshape(1, G * C), glu_w.astype(BF16), glu_b.reshape(1, -1))


def _split3(x):
    h = x.astype(BF16)
    r = x - h.astype(F32)
    m = r.astype(BF16)
    l = (r - m.astype(F32)).astype(BF16)
    return h, m, l


def _dot_f32(a, b3_bf16):
    return jnp.dot(jnp.concatenate(_split3(a), axis=1), b3_bf16, preferred_element_type=F32)


def _ssd_body(z_ref, xbc_ref, dt_ref, cw_ref, cb_ref, dtb_ref, a_ref, dx_ref, nw_ref, e_ref,
              o_ref, state_ref, xpad_ref, *, nheads, hd, ns, ngroups):
    Q = z_ref.shape[0]
    width = nheads * hd
    gw = width // ngroups
    halo = xpad_ref.shape[0] - Q

    @pl.when(pl.program_id(1) == 0)
    def _():
        state_ref[...] = jnp.zeros_like(state_ref)
        xpad_ref[0:halo, :] = jnp.zeros((halo, xpad_ref.shape[1]), F32)

    xpad_ref[halo:halo + Q, :] = xbc_ref[...]
    xp = xpad_ref[...]
    acc = cw_ref[0:1, :] * xp
    for k in range(1, SSD_CONV):
        acc = cw_ref[k:k + 1, :] * xp + pltpu.roll(acc, 1, axis=0)
    xpad_ref[0:halo, :] = xpad_ref[Q:Q + halo, :]
    xc = jax.nn.silu(acc[halo:, :] + cb_ref[...])
    xs = xc[:, 0:width]
    bm = xc[:, width:width + ngroups * ns].astype(BF16)
    cm = xc[:, width + ngroups * ns:].astype(BF16)

    lane = lax.broadcasted_iota(jnp.int32, (Q, LANE), 1)
    dt = jnp.where(lane < nheads, jax.nn.softplus(dt_ref[...] + dtb_ref[...]), 0.0)
    a = dt * a_ref[...]
    rowi = lax.broadcasted_iota(jnp.int32, (Q, Q), 0)
    coli = lax.broadcasted_iota(jnp.int32, (Q, Q), 1)
    tril = coli <= rowi
    acs = _dot_f32_lhs(tril.astype(BF16), a)
    acs_t = acs.T
    last = acs[Q - 1:Q, :]
    emat = e_ref[...]
    dt_x = _expand(dt, emat)
    dec_x = _expand(jnp.exp(acs), emat)
    dte_x = _expand(jnp.exp(last - acs), emat)
    dlast_x = dec_x[Q - 1:Q, :]

    xd = xs * dt_x
    xd_b = xd.astype(BF16)
    xw_b = (xd * dte_x).astype(BF16)
    st = state_ref[...]
    st_b = st.astype(BF16)

    y_parts = []
    new_state = []
    hlane = lax.broadcasted_iota(jnp.int32, (Q, LANE), 1) < hd
    heads_per_group = nheads // ngroups
    for g in range(ngroups):
        cg = cm[:, g * ns:(g + 1) * ns]
        bg = bm[:, g * ns:(g + 1) * ns]
        cb = lax.dot_general(cg, bg, (((1,), (1,)), ((), ())), preferred_element_type=F32)
        y_off = jnp.dot(cg, st_b[:, g * gw:(g + 1) * gw], preferred_element_type=F32)
        pair_out = []
        for pr in range(gw // LANE):
            xpair = xd_b[:, g * gw + pr * LANE: g * gw + (pr + 1) * LANE]
            res = []
            for hh in range(LANE // hd):
                h = g * heads_per_group + pr * (LANE // hd) + hh
                diff = acs[:, h:h + 1] - acs_t[h:h + 1, :]
                s_h = jnp.where(tril, cb * jnp.exp(jnp.minimum(diff, 0.0)), 0.0).astype(BF16)
                res.append(jnp.dot(s_h, xpair, preferred_element_type=F32))
            pair_out.append(jnp.where(hlane, res[0], res[1]))
        y_diag = jnp.concatenate(pair_out, axis=1)
        y_parts.append(y_diag + y_off * dec_x[:, g * gw:(g + 1) * gw])
        upd = lax.dot_general(bg, xw_b[:, g * gw:(g + 1) * gw], (((0,), (0,)), ((), ())),
                              preferred_element_type=F32)
        new_state.append(st[:, g * gw:(g + 1) * gw] * dlast_x[:, g * gw:(g + 1) * gw] + upd)
    state_ref[...] = jnp.concatenate(new_state, axis=1)

    y = jnp.concatenate(y_parts, axis=1) + xs * dx_ref[...]
    gt = y * jax.nn.silu(z_ref[...])
    ms = jnp.mean(gt * gt, axis=-1, keepdims=True)
    o_ref[...] = (gt * lax.rsqrt(ms + EPS) * nw_ref[...]).astype(BF16)


def _dot_f32_lhs(a_bf16, b):
    return jnp.dot(jnp.concatenate([a_bf16] * 3, axis=1), jnp.concatenate(_split3(b), axis=0),
                   preferred_element_type=F32)


def _expand(v, emat):
    return _dot_f32(v, emat)


def _ssd_mixer(proj, B, L, z_blk, xbc_blk, dt_blk, prm, emat, layer, Q):
    cw, cb, dtb, avec, dx, nw = prm
    width = nw.shape[-1]
    cdim = cw.shape[-1]
    nheads = width // SSD_HEAD_DIM
    nt = L // Q
    body = functools.partial(_ssd_body, nheads=nheads, hd=SSD_HEAD_DIM, ns=SSD_STATE, ngroups=SSD_GROUPS)
    full = lambda a: _layer_spec(a, layer)
    return pl.pallas_call(
        body,
        grid=(B, nt),
        in_specs=[pl.BlockSpec((Q, width), lambda b, t: (b * nt + t, z_blk)),
                  pl.BlockSpec((Q, cdim), lambda b, t: (b * nt + t, xbc_blk)),
                  pl.BlockSpec((Q, LANE), lambda b, t: (b * nt + t, dt_blk)),
                  full(cw), full(cb), full(dtb), full(avec), full(dx), full(nw),
                  pl.BlockSpec(emat.shape, lambda b, t: (0, 0))],
        out_specs=pl.BlockSpec((Q, width), lambda b, t: (b * nt + t, 0)),
        out_shape=jax.ShapeDtypeStruct((B * L, width), BF16),
        scratch_shapes=[pltpu.VMEM((SSD_STATE, width), F32),
                        pltpu.VMEM((Q + SUBLANE, cdim), F32)],
        compiler_params=_cparams(("parallel", "arbitrary")),
        name="ssd_mixer",
    )(proj, proj, proj, cw, cb, dtb, avec, dx, nw, emat)


def _ssd_params(conv_w, conv_b, dt_bias, A_log, D, norm_w):
    depth, nheads = dt_bias.shape
    width = nheads * SSD_HEAD_DIM
    pad = lambda v: jnp.pad(v, ((0, 0), (0, LANE - nheads))).reshape(depth, 1, LANE)
    return (conv_w, conv_b.reshape(depth, 1, -1), pad(dt_bias), pad(-jnp.exp(A_log)),
            jnp.repeat(D, SSD_HEAD_DIM, axis=1).reshape(depth, 1, width), norm_w.reshape(depth, 1, width))


def _head_expansion(nheads):
    e = np.zeros((LANE, nheads * SSD_HEAD_DIM), np.float32)
    e[np.repeat(np.arange(nheads), SSD_HEAD_DIM), np.arange(nheads * SSD_HEAD_DIM)] = 1.0
    return jnp.asarray(np.concatenate([e] * 3, axis=0), BF16)


def _float_key(x):
    b = lax.bitcast_convert_type(x, jnp.int32)
    return b ^ ((b >> 31) & 0x7FFFFFFF)


def _place(blk, lane0, width, dst):
    lane = lax.broadcasted_iota(jnp.int32, blk.shape, 1)
    shift = (dst - lane0) % LANE
    r = pltpu.roll(blk, shift, axis=1) if shift else blk
    lo = jnp.where((lane >= dst) & (lane < min(dst + width, LANE)), r, 0.0)
    hi = jnp.where(lane < dst + width - LANE, r, 0.0)
    return jnp.concatenate([lo, hi], axis=1)


def _head_offsets(offs, nheads, hd):
    return sorted({(offs[f] + h * hd) % LANE for f in ('q', 'qi') for h in range(nheads)})


def _lane_window(x, start, width):
    off = start % LANE
    base = start - off
    if off == 0:
        return x[:, start:start + width]
    lane = lax.broadcasted_iota(jnp.int32, (x.shape[0], LANE), 1)
    out = []
    for p in range(width // LANE):
        a = pltpu.roll(x[:, base + p * LANE:base + (p + 1) * LANE], LANE - off, axis=1)
        b = pltpu.roll(x[:, base + (p + 1) * LANE:base + (p + 2) * LANE], LANE - off, axis=1)
        out.append(jnp.where(lane < LANE - off, a, b))
    return jnp.concatenate(out, axis=1)


def _dsa_body(qx_ref, qxn_ref, kv_ref, ki_ref, band_ref, far_ref, o_ref,
              kpad_ref, kipad_ref, vt_ref, qop_ref, key2_ref, hi2_ref, lo2_ref, lom_ref, lg_ref, bound_ref,
              m_ref, acc_ref,
              *, nheads, hd, topk, offs):
    i = pl.program_id(1)
    cur = lax.rem(i, 2)
    key_ref, hi_ref, lo_ref = key2_ref.at[cur], hi2_ref.at[cur], lo2_ref.at[cur]
    TQ = qx_ref.shape[0]
    KB = TQ
    L = kv_ref.shape[0]
    nkb_total = L // KB
    NT = (((1,), (1,)), ((), ()))
    WIN = kpad_ref.shape[2]

    def head_window(field_lo, h):
        lo = field_lo + h * hd
        return (lo // LANE) * LANE, lo % LANE

    head_offs = _head_offsets(offs, nheads, hd)

    @pl.when(i == 0)
    def _():
        for kb in range(nkb_total):
            r = slice(kb * KB, (kb + 1) * KB)
            kgrp = (offs['k'] // LANE) * LANE
            kblk = kv_ref[r, kgrp:kgrp + LANE]
            kiblk = ki_ref[r, :]
            for n, o in enumerate(head_offs):
                kpad_ref[n, r, :] = _place(kblk, offs['k'] - kgrp, hd, o)[:, :WIN].astype(BF16)
                kipad_ref[n, r, :] = _place(kiblk, offs['ki'], hd, o)[:, :WIN].astype(BF16)
            vt_ref[0:hd, r] = kv_ref[r, :].T[offs['v']:offs['v'] + hd, :].astype(BF16)
            vt_ref[hd:, r] = jnp.ones((vt_ref.shape[0] - hd, KB), BF16)

    def stage_windows(slot, field_lo, scale, src_ref):
        bases = sorted({head_window(field_lo, h)[0] for h in range(nheads)})
        for n, base in enumerate(bases):
            qop_ref[slot, n] = (src_ref[:, base:base + WIN] * scale).astype(BF16)
        return [(bases.index(head_window(field_lo, h)[0]), head_offs.index(head_window(field_lo, h)[1]))
                for h in range(nheads)]

    qs = stage_windows(0, offs['q'], hd ** -0.5, qx_ref)

    def stage_indexer(src_ref):
        table = stage_windows(1, offs['qi'], 1.0, src_ref)
        wbase = (offs['w'] // LANE) * LANE
        wt = src_ref[:, wbase:wbase + LANE].T * ((nheads * IDX_DIM) ** -0.5)
        return table, [wt[offs['w'] - wbase + h:offs['w'] - wbase + h + 1, :] for h in range(nheads)]

    RB = LANE
    I16 = jnp.int16
    HALF = 1 << 15

    def rows(kb, r=0, n=KB):
        return pl.ds(pl.multiple_of(kb * KB + r, SUBLANE), n)

    PART = m_ref.shape[1]

    def fold_sum(x):
        return jnp.sum(x.reshape(x.shape[0] // PART, PART, TQ), axis=0)

    def idx_keys(kb, diagonal, slot, indexer):
        qis, w_rows = indexer
        for r in range(0, KB, RB):
            s = jnp.zeros((RB, TQ), F32)
            for h in range(nheads):
                win, ko = qis[h]
                lg = lax.dot_general(kipad_ref[ko, rows(kb, r, RB), :], qop_ref[1, win], NT,
                                     preferred_element_type=F32)
                s += jnp.maximum(lg, 0.0) * w_rows[h]
            key = _float_key(s)
            if diagonal:
                krow = lax.broadcasted_iota(jnp.int32, (RB, TQ), 0) + r
                qcol = lax.broadcasted_iota(jnp.int32, (RB, TQ), 1)
                key = jnp.where((krow // CHUNK) <= (qcol // CHUNK), key, INT_MIN)
            key2_ref[slot, rows(kb, r, RB), :] = key
            hi2_ref[slot, rows(kb, r, RB), :] = (key >> 16).astype(I16)
            lo2_ref[slot, rows(kb, r, RB), :] = ((key & (2 * HALF - 1)) - HALF).astype(I16)

    @pl.when(i == 0)
    def _():
        idx_keys(0, True, cur, stage_indexer(qx_ref))

    def count(preds):
        def body(kb, cs):
            blk = key_ref[rows(kb), :]
            return tuple(c + fold_sum(jnp.where(p(blk, kb), 1.0, 0.0)) for c, p in zip(cs, preds))
        z = jnp.zeros((PART, TQ), F32)
        cs = lax.fori_loop(0, i + 1, body, (z,) * len(preds))
        return [jnp.sum(c, axis=0, keepdims=True) for c in cs]

    PACK = 2 * SUBLANE
    P16 = 4 * PACK

    def rows16(x):
        return x.reshape(x.shape[0] // PACK, PACK, TQ)

    def as16(v):
        return jnp.broadcast_to(v, (PACK, TQ)).astype(I16)

    def count16(src_ref, pred, flags=False):
        def body(kb, c):
            blk = rows16(src_ref[rows(kb), :])
            ind = jnp.where(pred(blk, kb), blk if flags else I16(1), I16(0)).reshape(KB // P16, P16, TQ)
            for n in range(KB // P16):
                c = c + ind[n]
            return c
        c = lax.fori_loop(0, i + 1, body, jnp.zeros((P16, TQ), I16))
        c = rows16(c)
        c = (c[0] + c[1]) + (c[2] + c[3])
        return jnp.sum(c.astype(F32), axis=0, keepdims=True)

    def bisect16(src_ref, target):
        def step(s, tu):
            cand_u = tu | jnp.left_shift(jnp.int32(1), 15 - s)
            cand = as16(cand_u - HALF)[None]
            cnt = count16(src_ref, lambda blk, kb: blk >= cand)
            return jnp.where(cnt >= target, cand_u, tu)
        return lax.fori_loop(0, 16, step, jnp.zeros((1, TQ), jnp.int32))

    thr_hi = bisect16(hi_ref, float(topk)) - HALF
    thr_hi16 = as16(thr_hi)[None]
    need_lo = topk - count16(hi_ref, lambda blk, kb: blk > thr_hi16)

    def mask_low_halves(kb, _):
        keep_lo = rows16(hi_ref[rows(kb), :]) == thr_hi16
        lom_ref[rows(kb), :] = jnp.where(keep_lo, rows16(lo_ref[rows(kb), :]), I16(-HALF)).reshape(KB, TQ)
        return 0

    lax.fori_loop(0, i + 1, mask_low_halves, 0)
    thr_lo = bisect16(lom_ref, need_lo)
    thr = thr_hi * (2 * HALF) + thr_lo

    cgt, ceq = count([lambda blk, kb: blk > thr, lambda blk, kb: blk == thr])
    need = topk - cgt
    nbits = int(L).bit_length()
    bound_ref[...] = jnp.full((1, TQ), 2 ** nbits - 1, jnp.int32)

    @pl.when(jnp.max(ceq - need) > 0.0)
    def _():
        thr_lo16 = as16(thr_lo - HALF)[None]

        def flag_ties(kb, _):
            tie = ((rows16(hi_ref[rows(kb), :]) == thr_hi16) & (rows16(lo_ref[rows(kb), :]) == thr_lo16))
            lom_ref[rows(kb), :] = jnp.where(tie, I16(1), I16(0)).reshape(KB, TQ)
            return 0

        lax.fori_loop(0, i + 1, flag_ties, 0)
        krow16 = rows16(lax.broadcasted_iota(jnp.int32, (KB, TQ), 0).astype(I16))

        def bisect_idx(step, bnd):
            bit = jnp.left_shift(jnp.int32(1), nbits - 1 - step)
            cand = bnd | bit
            cnt = count16(lom_ref, lambda blk, kb: krow16 < as16(cand - kb * KB)[None], flags=True)
            return jnp.where(cnt <= need, cand, bnd)

        bound_ref[...] = lax.fori_loop(0, nbits, bisect_idx, jnp.zeros((1, TQ), jnp.int32))

    bound = bound_ref[...]

    m_ref[...] = jnp.full(m_ref.shape, NEG, F32)

    def fold_max(x):
        return jnp.max(x.reshape(x.shape[0] // PART, PART, TQ), axis=0)

    def store_logits(kb, bias_of_head):
        for r in range(0, KB, RB):
            blk = key_ref[rows(kb, r, RB), :]
            pos = lax.broadcasted_iota(jnp.int32, (RB, TQ), 0) + (kb * KB + r)
            keep = ((blk > thr) | ((blk == thr) & (pos < bound))) & (blk != INT_MIN)
            mask = jnp.where(keep, 0.0, NEG)
            for h in range(nheads):
                win, ko = qs[h]
                lg = lax.dot_general(kpad_ref[ko, rows(kb, r, RB), :], qop_ref[0, win], NT,
                                     preferred_element_type=F32)
                lg = lg + bias_of_head(h, r) + mask
                lg_ref[h, rows(kb, r, RB), :] = lg
                m_ref[h] = jnp.maximum(m_ref[h], fold_max(lg))

    def far_logits(kb, _):
        store_logits(kb, lambda h, r: far_ref[h])
        return 0

    def far_logits2(j, _):
        store_logits(2 * j, lambda h, r: far_ref[h])
        store_logits(2 * j + 1, lambda h, r: far_ref[h])
        return 0

    nfar = jnp.maximum(i - 1, 0)
    lax.fori_loop(0, lax.shift_right_logical(nfar, 1), far_logits2, 0)

    @pl.when((nfar & 1) == 1)
    def _():
        far_logits(nfar - 1, 0)

    @pl.when(i >= 1)
    def _():
        store_logits(i - 1, lambda h, r: band_ref[h, r:r + RB, :])

    store_logits(i, lambda h, r: band_ref[h, KB + r:KB + r + RB, :])

    m_fin = [jnp.max(m_ref[h], axis=0, keepdims=True) for h in range(nheads)]
    acc_ref[...] = jnp.zeros(acc_ref.shape, F32)

    def accumulate(kb):
        vt = vt_ref[:, rows(kb)]
        for h in range(nheads):
            p = jnp.exp(lg_ref[h, rows(kb), :] - m_fin[h])
            acc_ref[h] += jnp.dot(vt, p.astype(BF16), preferred_element_type=F32)

    has_next = i + 1 < pl.num_programs(1)

    @pl.when(has_next)
    def _():
        indexer = stage_indexer(qxn_ref)

        def both(kb, _):
            accumulate(kb)
            idx_keys(kb, False, 1 - cur, indexer)
            return 0

        lax.fori_loop(0, i + 1, both, 0)
        idx_keys(i + 1, True, 1 - cur, indexer)

    @pl.when(jnp.logical_not(has_next))
    def _():
        def only(kb, _):
            accumulate(kb)
            return 0

        lax.fori_loop(0, i + 1, only, 0)

    outs = [acc_ref[h, 0:hd, :] / acc_ref[h, hd:hd + 1, :] for h in range(nheads)]
    out = jnp.concatenate(outs, axis=0).T
    z = _lane_window(qx_ref, offs['z'], nheads * hd)
    o_ref[...] = (out * jax.nn.silu(z)).astype(BF16)


def _dsa_mixer(proj, B, L, src, band, far, nheads, TQ):
    width = nheads * ATT_HEAD_DIM
    nq = L // TQ
    hd = ATT_HEAD_DIM
    qx_w = 16 * LANE
    qx_blk = src['att_q'][0] // qx_w
    qx_lo = qx_blk * qx_w
    kv_w = 2 * LANE
    kv_blk = src['att_k'][0] // kv_w
    ki_blk = src['idx_k'][0] // LANE
    offs = {'q': src['att_q'][0] - qx_lo, 'qi': src['idx_q'][0] - qx_lo, 'w': src['idx_w'][0] - qx_lo,
            'z': src['att_z'][0] - qx_lo, 'k': src['att_k'][0] - kv_blk * kv_w,
            'v': src['att_v'][0] - kv_blk * kv_w, 'ki': src['idx_k'][0] - ki_blk * LANE}
    assert offs['z'] + width + LANE <= qx_w and qx_lo + qx_w <= proj.shape[1]
    assert offs['k'] % LANE + hd <= LANE and offs['ki'] + hd <= LANE
    assert offs['v'] % SUBLANE == 0 and offs['v'] + hd <= kv_w
    assert offs['w'] // LANE == (offs['w'] + nheads - 1) // LANE
    win_off = _head_offsets(offs, nheads, hd)
    win = LANE if max(win_off) + hd <= LANE else 2 * LANE
    win_base = max((sorted({(offs[f] + h * hd) // LANE for h in range(nheads)}) for f in ('q', 'qi')), key=len)
    body = functools.partial(_dsa_body, nheads=nheads, hd=hd, topk=min(TOPK, L // 4), offs=offs)
    full = lambda a: pl.BlockSpec(a.shape, lambda b, i: (0,) * a.ndim)
    return pl.pallas_call(
        body,
        grid=(B, nq),
        in_specs=[pl.BlockSpec((TQ, qx_w), lambda b, i: (b * nq + i, qx_blk)),
                  pl.BlockSpec((TQ, qx_w), lambda b, i: (b * nq + jnp.minimum(i + 1, nq - 1), qx_blk)),
                  pl.BlockSpec((L, kv_w), lambda b, i: (b, kv_blk)),
                  pl.BlockSpec((L, LANE), lambda b, i: (b, ki_blk)),
                  full(band), full(far)],
        out_specs=pl.BlockSpec((TQ, width), lambda b, i: (b * nq + i, 0)),
        out_shape=jax.ShapeDtypeStruct((B * L, width), BF16),
        scratch_shapes=[pltpu.VMEM((len(win_off), L, win), BF16),
                        pltpu.VMEM((len(win_off), L, win), BF16),
                        pltpu.VMEM((ATT_HEAD_DIM + 2 * SUBLANE, L), BF16),
                        pltpu.VMEM((2, len(win_base), TQ, win), BF16),
                        pltpu.VMEM((2, L, TQ), jnp.int32),
                        pltpu.VMEM((2, L, TQ), jnp.int16),
                        pltpu.VMEM((2, L, TQ), jnp.int16),
                        pltpu.VMEM((L, TQ), jnp.int16),
                        pltpu.VMEM((nheads, L, TQ), F32),
                        pltpu.VMEM((1, TQ), jnp.int32),
                        pltpu.VMEM((nheads, 4 * SUBLANE, TQ), F32),
                        pltpu.VMEM((nheads, ATT_HEAD_DIM + 2 * SUBLANE, TQ), F32)],
        compiler_params=_cparams(("parallel", "arbitrary")),
        name="dsa_mixer",
    )(proj, proj, proj, proj, band, far)


def _t5_bucket_static(rel):
    nb = N_BUCKETS // 2
    max_exact = nb // 2
    ret = np.where(rel > 0, nb, 0)
    n = np.abs(rel)
    nf = np.maximum(n, 1).astype(np.float64)
    large = max_exact + (np.log(nf / max_exact) / math.log(MAX_DISTANCE / max_exact)
                         * (nb - max_exact)).astype(np.int32)
    large = np.minimum(large, nb - 1)
    return ret + np.where(n < max_exact, n, large)


def _dsa_bias_tables(rel_bias, TQ):
    a = np.arange(2 * TQ)[:, None]
    j = np.arange(TQ)[None, :]
    band_idx = _t5_bucket_static(a - TQ - j)
    idx = jnp.asarray(band_idx, jnp.int32)[None]
    band = jnp.zeros((rel_bias.shape[1],) + band_idx.shape, F32)
    for bucket in np.unique(band_idx):
        band = jnp.where(idx == int(bucket), rel_bias[int(bucket)][:, None, None], band)
    far_bucket = int(_t5_bucket_static(np.array([-(TQ + 1)]))[0])
    assert far_bucket == int(_t5_bucket_static(np.array([-(10 ** 6)]))[0])
    far = jnp.broadcast_to(rel_bias[far_bucket][:, None, None], (rel_bias.shape[1], 1, TQ))
    return band, far


def _layout(d_model):
    s5w = d_model // 4
    ssdw = d_model // 2
    attw = d_model // 4
    cdim = ssdw + 2 * SSD_GROUPS * SSD_STATE
    nh_ssd = ssdw // SSD_HEAD_DIM
    nh_att = attw // ATT_HEAD_DIM
    splits = (s5w, s5w, ssdw, cdim, nh_ssd, attw, ATT_HEAD_DIM, ATT_HEAD_DIM,
              nh_att * IDX_DIM, IDX_DIM, nh_att, attw)
    names = ('s5_u', 's5_z', 'ssd_z', 'ssd_xbc', 'ssd_dt', 'att_q', 'att_k', 'att_v',
             'idx_q', 'idx_k', 'idx_w', 'att_z')
    src, o = {}, 0
    for nme, s in zip(names, splits):
        src[nme] = (o, s)
        o += s
    return src, o


def kernel(x, norm_w, w_in, s5_A_re, s5_A_im, s5_log_dt, s5_B_re, s5_B_im, s5_C_re, s5_C_im, s5_D, s5_glu_w, s5_glu_b, ssd_conv_w, ssd_conv_b, ssd_dt_bias, ssd_A_log, ssd_D, ssd_norm_w, rel_bias, w_out, final_norm_w):
    B, L, d = x.shape
    depth = w_in.shape[0]
    src, total = _layout(d)
    s5w, ssdw, attw = d // 4, d // 2, d // 4
    cdim = ssdw + 2 * SSD_GROUPS * SSD_STATE
    nh_att = attw // ATT_HEAD_DIM
    assert src['s5_u'][0] % s5w == 0 and src['s5_z'][0] % s5w == 0 and src['ssd_z'][0] % ssdw == 0
    assert src['ssd_xbc'][0] % cdim == 0 and src['ssd_dt'][0] % LANE == 0

    tn = 1024
    cuts, shift, padded = [(0, 0)], 0, {}
    for nme, (so, sz) in sorted(src.items(), key=lambda kv: kv[1][0]):
        gap = -(so + shift) % LANE if nme in ('att_q', 'att_k', 'idx_q', 'idx_k', 'att_z') else 0
        if gap:
            shift += gap
            cuts.append((so, shift))
        padded[nme] = (so + shift, sz)
    src = padded
    total_pad = -(-(total + shift) // tn) * tn
    segs = tuple((lo, cuts[n + 1][0] if n + 1 < len(cuts) else total, lo + sh) for n, (lo, sh) in enumerate(cuts))
    w_in_p = _relayout_w_in(jnp.swapaxes(w_in, 1, 2), segs, total_pad, min(256, d))

    TQ = 256
    Q = 128
    band, far = _dsa_bias_tables(rel_bias, TQ)
    tm_in = min(1024, B * L)
    tm_out = min(512, B * L)
    nchunk = 2
    seg = L // SUBLANE

    s5p = jax.vmap(lambda *a: _s5_params(*a, seg, nchunk))(
        s5_A_re, s5_A_im, s5_log_dt, s5_B_re, s5_B_im, s5_C_re, s5_C_im, s5_D, s5_glu_w, s5_glu_b)
    ssdp = _ssd_params(ssd_conv_w, ssd_conv_b, ssd_dt_bias, ssd_A_log, ssd_D, ssd_norm_w)
    emat = _head_expansion(ssdw // SSD_HEAD_DIM)
    norm_w3 = norm_w.reshape(depth, 1, d)

    x2 = x.reshape(B * L, d)
    for l in range(depth):
        proj = _inproj(x2, norm_w3, w_in_p, l, tm_in, tn)
        y_s5 = _s5_mixer(proj, B, L, src['s5_u'][0] // s5w, src['s5_z'][0] // s5w, s5w, s5p, l)
        y_ssd = _ssd_mixer(proj, B, L, src['ssd_z'][0] // ssdw, src['ssd_xbc'][0] // cdim,
                           src['ssd_dt'][0] // LANE, ssdp, emat, l, Q)
        y_att = _dsa_mixer(proj, B, L, src, band, far, nh_att, TQ)
        x2 = _outproj(y_s5, y_ssd, y_att, w_out, l, x2, final_norm_w.reshape(1, d), tm_out,
                      final=(l == depth - 1))
    return x2.reshape(B, L, d)
```

```python
import functools
import math

import numpy as np
import jax
import jax.numpy as jnp
from jax import lax
from jax.experimental import pallas as pl
from jax.experimental.pallas import tpu as pltpu

F32 = jnp.float32
BF16 = jnp.bfloat16

EPS = 1e-6
CHUNK = 64

SSD_HEAD_DIM = 64
SSD_STATE = 128
SSD_GROUPS = 4
SSD_CONV = 4
ATT_HEAD_DIM = 64
IDX_DIM = 64
TOPK = 256
N_BUCKETS = 32
MAX_DISTANCE = 128

LANE = 128
SUBLANE = 8
VMEM_LIMIT = 56 * 1024 * 1024

INT_MIN = -2 ** 31
NEG = -1e30


def _cparams(sem):
    return pltpu.CompilerParams(dimension_semantics=sem, vmem_limit_bytes=VMEM_LIMIT)


def _inproj_body(x_ref, nw_ref, w_ref, o_ref, h_ref):
    @pl.when(pl.program_id(1) == 0)
    def _():
        xf = x_ref[...]
        ms = jnp.mean(xf * xf, axis=-1, keepdims=True)
        h_ref[...] = (xf * lax.rsqrt(ms + EPS) * nw_ref[...]).astype(BF16)

    o_ref[...] = lax.dot_general(h_ref[...], w_ref[...], (((1,), (1,)), ((), ())),
                                 preferred_element_type=F32)


def _inproj(x2, nw, w, layer, tm, tn):
    m, d = x2.shape
    n = w.shape[1]
    return pl.pallas_call(
        _inproj_body,
        grid=(m // tm, n // tn),
        in_specs=[pl.BlockSpec((tm, d), lambda i, j: (i, 0)),
                  pl.BlockSpec((None, 1, d), lambda i, j: (layer, 0, 0)),
                  pl.BlockSpec((None, tn, d), lambda i, j: (layer, j, 0))],
        out_specs=pl.BlockSpec((tm, tn), lambda i, j: (i, j)),
        out_shape=jax.ShapeDtypeStruct((m, n), F32),
        scratch_shapes=[pltpu.VMEM((tm, d), BF16)],
        compiler_params=_cparams(("parallel", "arbitrary")),
        name="inproj",
    )(x2, nw, w)


def _relayout_body(w_ref, o_ref, *, segs):
    pack = 2 * SUBLANE
    pos = 0
    for src_lo, src_hi, dst_lo in segs:
        if dst_lo > pos:
            o_ref[pos:dst_lo, :] = jnp.zeros((dst_lo - pos, o_ref.shape[1]), BF16)
        val = w_ref[src_lo:src_hi, :]
        fill = -(src_hi - src_lo) % pack
        if fill:
            val = jnp.concatenate([val, jnp.zeros((fill, val.shape[1]), F32)], axis=0)
        o_ref[dst_lo:dst_lo + val.shape[0], :] = val.astype(BF16)
        pos = dst_lo + val.shape[0]
    if pos < o_ref.shape[0]:
        o_ref[pos:, :] = jnp.zeros((o_ref.shape[0] - pos, o_ref.shape[1]), BF16)


def _relayout_w_in(w_t, segs, total_pad, tk):
    depth, n, d = w_t.shape
    assert all(lo % SUBLANE == 0 and (hi - lo) % SUBLANE == 0 and dst % (2 * SUBLANE) == 0
               for lo, hi, dst in segs)
    return pl.pallas_call(
        functools.partial(_relayout_body, segs=segs),
        grid=(depth, d // tk),
        in_specs=[pl.BlockSpec((None, n, tk), lambda l, i: (l, 0, i))],
        out_specs=pl.BlockSpec((None, total_pad, tk), lambda l, i: (l, 0, i)),
        out_shape=jax.ShapeDtypeStruct((depth, total_pad, d), BF16),
        compiler_params=_cparams(("parallel", "parallel")),
        name="w_in_relayout",
    )(w_t)


def _outproj_body(ys5_ref, yssd_ref, yatt_ref, w_ref, x_ref, fnw_ref, o_ref, wb_ref, *, w5, wssd, final):
    @pl.when(pl.program_id(0) == 0)
    def _():
        step = 256
        for r in range(0, w_ref.shape[0], step):
            wb_ref[r:r + step, :] = w_ref[r:r + step, :].astype(BF16)

    acc = x_ref[...]
    acc += jnp.dot(ys5_ref[...], wb_ref[0:w5, :], preferred_element_type=F32)
    acc += jnp.dot(yssd_ref[...], wb_ref[w5:w5 + wssd, :], preferred_element_type=F32)
    acc += jnp.dot(yatt_ref[...], wb_ref[w5 + wssd:, :], preferred_element_type=F32)
    if final:
        ms = jnp.mean(acc * acc, axis=-1, keepdims=True)
        acc = acc * lax.rsqrt(ms + EPS) * fnw_ref[...]
    o_ref[...] = acc


def _outproj(ys5, yssd, yatt, w, layer, x2, fnw, tm, final):
    m, d = x2.shape
    w5, wssd, watt = ys5.shape[1], yssd.shape[1], yatt.shape[1]
    body = functools.partial(_outproj_body, w5=w5, wssd=wssd, final=final)
    return pl.pallas_call(
        body,
        grid=(m // tm,),
        in_specs=[pl.BlockSpec((tm, w5), lambda i: (i, 0)),
                  pl.BlockSpec((tm, wssd), lambda i: (i, 0)),
                  pl.BlockSpec((tm, watt), lambda i: (i, 0)),
                  pl.BlockSpec((None,) + w.shape[1:], lambda i: (layer, 0, 0), pipeline_mode=pl.Buffered(1)),
                  pl.BlockSpec((tm, d), lambda i: (i, 0)),
                  pl.BlockSpec((1, d), lambda i: (0, 0))],
        out_specs=pl.BlockSpec((tm, d), lambda i: (i, 0)),
        out_shape=jax.ShapeDtypeStruct((m, d), F32),
        scratch_shapes=[pltpu.VMEM(w.shape[1:], BF16)],
        compiler_params=_cparams(("arbitrary",)),
        name="outproj",
    )(ys5, yssd, yatt, w, x2, fnw)


def _shift_down_one(x):
    rolled = pltpu.roll(x, 1, axis=0)
    row = lax.broadcasted_iota(jnp.int32, x.shape, 0)
    return jnp.where(row == 0, 0.0, rolled)


def _s5_body(u_ref, z_ref, bblk_ref, cblk_ref, lam_ref, lamseg_ref, d_ref, gw_ref, gb_ref,
             o_ref, uperm_ref, xs_ref, yperm_ref, *, nchunk, cw, sw, rt):
    L = u_ref.shape[0]
    seg = L // SUBLANE

    for c in range(nchunk):
        for j in range(SUBLANE):
            for k in range(cw // LANE):
                lo = c * cw + k * LANE
                uperm_ref[k, pl.ds(j, seg, stride=SUBLANE), :] = u_ref[pl.ds(j * seg, seg), lo:lo + LANE]
        for r0 in range(0, L, rt):
            up = jnp.concatenate([uperm_ref[k, r0:r0 + rt, :] for k in range(cw // LANE)], axis=1)
            xs_ref[r0:r0 + rt, :] = jnp.dot(up.astype(BF16), bblk_ref[c], preferred_element_type=F32)
        lr = lam_ref[c, 0]
        li = lam_ref[c, 1]

        def scan_step(tau, carry):
            xr, xi = carry
            row = pl.multiple_of(tau * SUBLANE, SUBLANE)
            nxr = lr * xr - li * xi + xs_ref[pl.ds(row, SUBLANE), 0:sw]
            nxi = lr * xi + li * xr + xs_ref[pl.ds(row, SUBLANE), sw:2 * sw]
            xs_ref[pl.ds(row, SUBLANE), 0:sw] = nxr
            xs_ref[pl.ds(row, SUBLANE), sw:2 * sw] = nxi
            return nxr, nxi

        zero = jnp.zeros((SUBLANE, sw), F32)
        er, ei = lax.fori_loop(0, seg, scan_step, (zero, zero), unroll=8)

        sr = lamseg_ref[c, 0]
        si = lamseg_ref[c, 1]
        cr, ci = zero, zero
        for _ in range(SUBLANE - 1):
            tr = er + (sr * cr - si * ci)
            ti = ei + (sr * ci + si * cr)
            cr, ci = _shift_down_one(tr), _shift_down_one(ti)

        def fix_step(tau, carry):
            fr, fi = carry
            nfr = lr * fr - li * fi
            nfi = lr * fi + li * fr
            row = pl.multiple_of(tau * SUBLANE, SUBLANE)
            xs_ref[pl.ds(row, SUBLANE), 0:sw] += nfr
            xs_ref[pl.ds(row, SUBLANE), sw:2 * sw] += nfi
            return nfr, nfi

        lax.fori_loop(0, seg, fix_step, (cr, ci), unroll=8)

        for r0 in range(0, L, rt):
            yc = jnp.dot(xs_ref[r0:r0 + rt, :].astype(BF16), cblk_ref[c], preferred_element_type=F32)
            for k in range(cw // LANE):
                yperm_ref[c * (cw // LANE) + k, r0:r0 + rt, :] = yc[:, k * LANE:(k + 1) * LANE]

    for j in range(SUBLANE):
        y = jnp.concatenate([yperm_ref[k, pl.ds(j, seg, stride=SUBLANE), :]
                             for k in range(yperm_ref.shape[0])], axis=1)
        y = y + d_ref[...] * u_ref[j * seg:(j + 1) * seg, :]
        y = jax.nn.gelu(y)
        g = jnp.dot(y.astype(BF16), gw_ref[...], preferred_element_type=F32) + gb_ref[...]
        y = y * jax.nn.sigmoid(g)
        o_ref[j * seg:(j + 1) * seg, :] = (y * jax.nn.silu(z_ref[j * seg:(j + 1) * seg, :])).astype(BF16)


def _layer_spec(a, layer):
    return pl.BlockSpec((None,) + a.shape[1:], lambda *_: (layer,) + (0,) * (a.ndim - 1))


def _s5_mixer(proj, B, L, u_blk, z_blk, width, prm, layer):
    bblk, cblk, lam, lamseg, dvec, gw, gb = prm
    nchunk, cw, sw2 = bblk.shape[1:]
    sw = sw2 // 2
    rt = min(512, L)
    body = functools.partial(_s5_body, nchunk=nchunk, cw=cw, sw=sw, rt=rt)
    full = lambda a: _layer_spec(a, layer)
    return pl.pallas_call(
        body,
        grid=(B,),
        in_specs=[pl.BlockSpec((L, width), lambda b: (b, u_blk)),
                  pl.BlockSpec((L, width), lambda b: (b, z_blk)),
                  full(bblk), full(cblk), full(lam), full(lamseg), full(dvec), full(gw), full(gb)],
        out_specs=pl.BlockSpec((L, width), lambda b: (b, 0)),
        out_shape=jax.ShapeDtypeStruct((B * L, width), BF16),
        scratch_shapes=[pltpu.VMEM((cw // LANE, L, LANE), F32),
                        pltpu.VMEM((L, 2 * sw), F32),
                        pltpu.VMEM((width // LANE, L, LANE), F32)],
        compiler_params=_cparams(("parallel",)),
        name="s5_mixer",
    )(proj, proj, bblk, cblk, lam, lamseg, dvec, gw, gb)


def _cpow(re, im, n):
    rr, ri = jnp.ones_like(re), jnp.zeros_like(im)
    br, bi = re, im
    while n:
        if n & 1:
            rr, ri = rr * br - ri * bi, rr * bi + ri * br
        br, bi = br * br - bi * bi, 2.0 * br * bi
        n >>= 1
    return rr, ri


def _s5_params(A_re, A_im, log_dt, B_re, B_im, C_re, C_im, D, glu_w, glu_b, seg, nchunk):
    G, P, C = B_re.shape
    dt = jnp.exp(log_dt)[:, None]
    lre = jnp.minimum(A_re, -1e-4)
    lim = A_im
    mag = jnp.exp(lre * dt)
    lbr = mag * jnp.cos(lim * dt)
    lbi = mag * jnp.sin(lim * dt)
    nr, ni = lbr - 1.0, lbi
    den = lre * lre + lim * lim
    fr = (nr * lre + ni * lim) / den
    fi = (ni * lre - nr * lim) / den
    bbr = fr[..., None] * B_re - fi[..., None] * B_im
    bbi = fr[..., None] * B_im + fi[..., None] * B_re
    gc = G // nchunk
    eye = jnp.eye(gc, dtype=F32)

    def blockdiag_in(bb):
        t = jnp.transpose(bb, (0, 2, 1)).reshape(nchunk, gc, C, P)
        return jnp.einsum('ngcp,gh->ngchp', t, eye).reshape(nchunk, gc * C, gc * P)

    def blockdiag_out(cc):
        t = jnp.transpose(cc, (0, 2, 1)).reshape(nchunk, gc, P, C)
        return jnp.einsum('ngpc,gh->ngphc', t, eye).reshape(nchunk, gc * P, gc * C)

    bblk = jnp.concatenate([blockdiag_in(bbr), blockdiag_in(bbi)], axis=-1).astype(BF16)
    cblk = jnp.concatenate([blockdiag_out(C_re), blockdiag_out(-C_im)], axis=1).astype(BF16)

    def rows(v):
        return jnp.broadcast_to(v.reshape(nchunk, 1, gc * P), (nchunk, SUBLANE, gc * P))

    lam = jnp.stack([rows(lbr), rows(lbi)], axis=1)
    pr, pi = _cpow(lbr, lbi, seg)
    lamseg = jnp.stack([rows(pr), rows(pi)], axis=1)
    return (bblk, cblk, lam, lamseg, D.reshape(1, G * C), glu_w.astype(BF16), glu_b.reshape(1, -1))


def _split3(x):
    h = x.astype(BF16)
    r = x - h.astype(F32)
    m = r.astype(BF16)
    l = (r - m.astype(F32)).astype(BF16)
    return h, m, l


def _dot_f32(a, b3_bf16):
    return jnp.dot(jnp.concatenate(_split3(a), axis=1), b3_bf16, preferred_element_type=F32)


def _ssd_body(z_ref, xbc_ref, dt_ref, cw_ref, cb_ref, dtb_ref, a_ref, dx_ref, nw_ref, e_ref,
              o_ref, state_ref, xpad_ref, *, nheads, hd, ns, ngroups):
    Q = z_ref.shape[0]
    width = nheads * hd
    gw = width // ngroups
    halo = xpad_ref.shape[0] - Q

    @pl.when(pl.program_id(1) == 0)
    def _():
        state_ref[...] = jnp.zeros_like(state_ref)
        xpad_ref[0:halo, :] = jnp.zeros((halo, xpad_ref.shape[1]), F32)

    xpad_ref[halo:halo + Q, :] = xbc_ref[...]
    xp = xpad_ref[...]
    acc = cw_ref[0:1, :] * xp
    for k in range(1, SSD_CONV):
        acc = cw_ref[k:k + 1, :] * xp + pltpu.roll(acc, 1, axis=0)
    xpad_ref[0:halo, :] = xpad_ref[Q:Q + halo, :]
    xc = jax.nn.silu(acc[halo:, :] + cb_ref[...])
    xs = xc[:, 0:width]
    bm = xc[:, width:width + ngroups * ns].astype(BF16)
    cm = xc[:, width + ngroups * ns:].astype(BF16)

    lane = lax.broadcasted_iota(jnp.int32, (Q, LANE), 1)
    dt = jnp.where(lane < nheads, jax.nn.softplus(dt_ref[...] + dtb_ref[...]), 0.0)
    a = dt * a_ref[...]
    rowi = lax.broadcasted_iota(jnp.int32, (Q, Q), 0)
    coli = lax.broadcasted_iota(jnp.int32, (Q, Q), 1)
    tril = coli <= rowi
    acs = _dot_f32_lhs(tril.astype(BF16), a)
    acs_t = acs.T
    last = acs[Q - 1:Q, :]
    emat = e_ref[...]
    dt_x = _expand(dt, emat)
    dec_x = _expand(jnp.exp(acs), emat)
    dte_x = _expand(jnp.exp(last - acs), emat)
    dlast_x = dec_x[Q - 1:Q, :]

    xd = xs * dt_x
    xd_b = xd.astype(BF16)
    xw_b = (xd * dte_x).astype(BF16)
    st = state_ref[...]
    st_b = st.astype(BF16)

    y_parts = []
    new_state = []
    hlane = lax.broadcasted_iota(jnp.int32, (Q, LANE), 1) < hd
    heads_per_group = nheads // ngroups
    for g in range(ngroups):
        cg = cm[:, g * ns:(g + 1) * ns]
        bg = bm[:, g * ns:(g + 1) * ns]
        cb = lax.dot_general(cg, bg, (((1,), (1,)), ((), ())), preferred_element_type=F32)
        y_off = jnp.dot(cg, st_b[:, g * gw:(g + 1) * gw], preferred_element_type=F32)
        pair_out = []
        for pr in range(gw // LANE):
            xpair = xd_b[:, g * gw + pr * LANE: g * gw + (pr + 1) * LANE]
            res = []
            for hh in range(LANE // hd):
                h = g * heads_per_group + pr * (LANE // hd) + hh
                diff = acs[:, h:h + 1] - acs_t[h:h + 1, :]
                s_h = jnp.where(tril, cb * jnp.exp(jnp.minimum(diff, 0.0)), 0.0).astype(BF16)
                res.append(jnp.dot(s_h, xpair, preferred_element_type=F32))
            pair_out.append(jnp.where(hlane, res[0], res[1]))
        y_diag = jnp.concatenate(pair_out, axis=1)
        y_parts.append(y_diag + y_off * dec_x[:, g * gw:(g + 1) * gw])
        upd = lax.dot_general(bg, xw_b[:, g * gw:(g + 1) * gw], (((0,), (0,)), ((), ())),
                              preferred_element_type=F32)
        new_state.append(st[:, g * gw:(g + 1) * gw] * dlast_x[:, g * gw:(g + 1) * gw] + upd)
    state_ref[...] = jnp.concatenate(new_state, axis=1)

    y = jnp.concatenate(y_parts, axis=1) + xs * dx_ref[...]
    gt = y * jax.nn.silu(z_ref[...])
    ms = jnp.mean(gt * gt, axis=-1, keepdims=True)
    o_ref[...] = (gt * lax.rsqrt(ms + EPS) * nw_ref[...]).astype(BF16)


def _dot_f32_lhs(a_bf16, b):
    return jnp.dot(jnp.concatenate([a_bf16] * 3, axis=1), jnp.concatenate(_split3(b), axis=0),
                   preferred_element_type=F32)


def _expand(v, emat):
    return _dot_f32(v, emat)


def _ssd_mixer(proj, B, L, z_blk, xbc_blk, dt_blk, prm, emat, layer, Q):
    cw, cb, dtb, avec, dx, nw = prm
    width = nw.shape[-1]
    cdim = cw.shape[-1]
    nheads = width // SSD_HEAD_DIM
    nt = L // Q
    body = functools.partial(_ssd_body, nheads=nheads, hd=SSD_HEAD_DIM, ns=SSD_STATE, ngroups=SSD_GROUPS)
    full = lambda a: _layer_spec(a, layer)
    return pl.pallas_call(
        body,
        grid=(B, nt),
        in_specs=[pl.BlockSpec((Q, width), lambda b, t: (b * nt + t, z_blk)),
                  pl.BlockSpec((Q, cdim), lambda b, t: (b * nt + t, xbc_blk)),
                  pl.BlockSpec((Q, LANE), lambda b, t: (b * nt + t, dt_blk)),
                  full(cw), full(cb), full(dtb), full(avec), full(dx), full(nw),
                  pl.BlockSpec(emat.shape, lambda b, t: (0, 0))],
        out_specs=pl.BlockSpec((Q, width), lambda b, t: (b * nt + t, 0)),
        out_shape=jax.ShapeDtypeStruct((B * L, width), BF16),
        scratch_shapes=[pltpu.VMEM((SSD_STATE, width), F32),
                        pltpu.VMEM((Q + SUBLANE, cdim), F32)],
        compiler_params=_cparams(("parallel", "arbitrary")),
        name="ssd_mixer",
    )(proj, proj, proj, cw, cb, dtb, avec, dx, nw, emat)


def _ssd_params(conv_w, conv_b, dt_bias, A_log, D, norm_w):
    depth, nheads = dt_bias.shape
    width = nheads * SSD_HEAD_DIM
    pad = lambda v: jnp.pad(v, ((0, 0), (0, LANE - nheads))).reshape(depth, 1, LANE)
    return (conv_w, conv_b.reshape(depth, 1, -1), pad(dt_bias), pad(-jnp.exp(A_log)),
            jnp.repeat(D, SSD_HEAD_DIM, axis=1).reshape(depth, 1, width), norm_w.reshape(depth, 1, width))


def _head_expansion(nheads):
    e = np.zeros((LANE, nheads * SSD_HEAD_DIM), np.float32)
    e[np.repeat(np.arange(nheads), SSD_HEAD_DIM), np.arange(nheads * SSD_HEAD_DIM)] = 1.0
    return jnp.asarray(np.concatenate([e] * 3, axis=0), BF16)


def _float_key(x):
    b = lax.bitcast_convert_type(x, jnp.int32)
    return b ^ ((b >> 31) & 0x7FFFFFFF)


def _place(blk, lane0, width, dst):
    lane = lax.broadcasted_iota(jnp.int32, blk.shape, 1)
    shift = (dst - lane0) % LANE
    r = pltpu.roll(blk, shift, axis=1) if shift else blk
    lo = jnp.where((lane >= dst) & (lane < min(dst + width, LANE)), r, 0.0)
    hi = jnp.where(lane < dst + width - LANE, r, 0.0)
    return jnp.concatenate([lo, hi], axis=1)


def _head_offsets(offs, nheads, hd):
    return sorted({(offs[f] + h * hd) % LANE for f in ('q', 'qi') for h in range(nheads)})


def _lane_window(x, start, width):
    off = start % LANE
    base = start - off
    if off == 0:
        return x[:, start:start + width]
    lane = lax.broadcasted_iota(jnp.int32, (x.shape[0], LANE), 1)
    out = []
    for p in range(width // LANE):
        a = pltpu.roll(x[:, base + p * LANE:base + (p + 1) * LANE], LANE - off, axis=1)
        b = pltpu.roll(x[:, base + (p + 1) * LANE:base + (p + 2) * LANE], LANE - off, axis=1)
        out.append(jnp.where(lane < LANE - off, a, b))
    return jnp.concatenate(out, axis=1)


def _dsa_body(qx_ref, qxn_ref, kv_ref, ki_ref, band_ref, far_ref, o_ref,
              kpad_ref, kipad_ref, vt_ref, qop_ref, key2_ref, hi2_ref, lo2_ref, lom_ref, lg_ref, bound_ref,
              m_ref, acc_ref,
              *, nheads, hd, topk, offs):
    i = pl.program_id(1)
    cur = lax.rem(i, 2)
    key_ref, hi_ref, lo_ref = key2_ref.at[cur], hi2_ref.at[cur], lo2_ref.at[cur]
    TQ = qx_ref.shape[0]
    KB = TQ
    L = kv_ref.shape[0]
    nkb_total = L // KB
    NT = (((1,), (1,)), ((), ()))
    WIN = kpad_ref.shape[2]

    def head_window(field_lo, h):
        lo = field_lo + h * hd
        return (lo // LANE) * LANE, lo % LANE

    head_offs = _head_offsets(offs, nheads, hd)

    @pl.when(i == 0)
    def _():
        for kb in range(nkb_total):
            r = slice(kb * KB, (kb + 1) * KB)
            kgrp = (offs['k'] // LANE) * LANE
            kblk = kv_ref[r, kgrp:kgrp + LANE]
            kiblk = ki_ref[r, :]
            for n, o in enumerate(head_offs):
                kpad_ref[n, r, :] = _place(kblk, offs['k'] - kgrp, hd, o)[:, :WIN].astype(BF16)
                kipad_ref[n, r, :] = _place(kiblk, offs['ki'], hd, o)[:, :WIN].astype(BF16)
            vt_ref[0:hd, r] = kv_ref[r, :].T[offs['v']:offs['v'] + hd, :].astype(BF16)
            vt_ref[hd:, r] = jnp.ones((vt_ref.shape[0] - hd, KB), BF16)

    def stage_windows(slot, field_lo, scale, src_ref):
        bases = sorted({head_window(field_lo, h)[0] for h in range(nheads)})
        for n, base in enumerate(bases):
            qop_ref[slot, n] = (src_ref[:, base:base + WIN] * scale).astype(BF16)
        return [(bases.index(head_window(field_lo, h)[0]), head_offs.index(head_window(field_lo, h)[1]))
                for h in range(nheads)]

    qs = stage_windows(0, offs['q'], hd ** -0.5, qx_ref)

    def stage_indexer(src_ref):
        table = stage_windows(1, offs['qi'], 1.0, src_ref)
        wbase = (offs['w'] // LANE) * LANE
        wt = src_ref[:, wbase:wbase + LANE].T * ((nheads * IDX_DIM) ** -0.5)
        return table, [wt[offs['w'] - wbase + h:offs['w'] - wbase + h + 1, :] for h in range(nheads)]

    RB = LANE
    I16 = jnp.int16
    HALF = 1 << 15

    def rows(kb, r=0, n=KB):
        return pl.ds(pl.multiple_of(kb * KB + r, SUBLANE), n)

    PART = m_ref.shape[1]

    def idx_keys(kb, diagonal, slot, indexer):
        qis, w_rows = indexer
        for r in range(0, KB, RB):
            s = jnp.zeros((RB, TQ), F32)
            for h in range(nheads):
                win, ko = qis[h]
                lg = lax.dot_general(kipad_ref[ko, rows(kb, r, RB), :], qop_ref[1, win], NT,
                                     preferred_element_type=F32)
                s += jnp.maximum(lg, 0.0) * w_rows[h]
            key = _float_key(s)
            if diagonal:
                krow = lax.broadcasted_iota(jnp.int32, (RB, TQ), 0) + r
                qcol = lax.broadcasted_iota(jnp.int32, (RB, TQ), 1)
                key = jnp.where((krow // CHUNK) <= (qcol // CHUNK), key, INT_MIN)
            key2_ref[slot, rows(kb, r, RB), :] = key
            hi2_ref[slot, rows(kb, r, RB), :] = (key >> 16).astype(I16)
            lo2_ref[slot, rows(kb, r, RB), :] = ((key & (2 * HALF - 1)) - HALF).astype(I16)

    @pl.when(i == 0)
    def _():
        idx_keys(0, True, cur, stage_indexer(qx_ref))

    PACK = 2 * SUBLANE
    P16 = 4 * PACK

    def rows16(x):
        return x.reshape(x.shape[0] // PACK, PACK, TQ)

    def as16(v):
        return jnp.broadcast_to(v, (PACK, TQ)).astype(I16)

    def count16(src_ref, pred, flags=False):
        def body(kb, c):
            blk = rows16(src_ref[rows(kb), :])
            ind = jnp.where(pred(blk, kb), blk if flags else I16(1), I16(0)).reshape(KB // P16, P16, TQ)
            for n in range(KB // P16):
                c = c + ind[n]
            return c
        c = lax.fori_loop(0, i + 1, body, jnp.zeros((P16, TQ), I16))
        c = rows16(c)
        c = (c[0] + c[1]) + (c[2] + c[3])
        return jnp.sum(c.astype(F32), axis=0, keepdims=True)

    def bisect16(src_ref, target):
        def step(s, tu):
            cand_u = tu | jnp.left_shift(jnp.int32(1), 15 - s)
            cand = as16(cand_u - HALF)[None]
            cnt = count16(src_ref, lambda blk, kb: blk >= cand)
            return jnp.where(cnt >= target, cand_u, tu)
        return lax.fori_loop(0, 16, step, jnp.zeros((1, TQ), jnp.int32))

    thr_hi = bisect16(hi_ref, float(topk)) - HALF
    thr_hi16 = as16(thr_hi)[None]
    need_lo = topk - count16(hi_ref, lambda blk, kb: blk > thr_hi16)

    def mask_low_halves(kb, _):
        keep_lo = rows16(hi_ref[rows(kb), :]) == thr_hi16
        lom_ref[rows(kb), :] = jnp.where(keep_lo, rows16(lo_ref[rows(kb), :]), I16(-HALF)).reshape(KB, TQ)
        return 0

    lax.fori_loop(0, i + 1, mask_low_halves, 0)
    thr_lo = bisect16(lom_ref, need_lo)
    thr = thr_hi * (2 * HALF) + thr_lo

    thr_lo16 = as16(thr_lo - HALF)[None]
    cgt = (topk - need_lo) + count16(lom_ref, lambda blk, kb: blk > thr_lo16)
    ceq = count16(lom_ref, lambda blk, kb: blk == thr_lo16)
    need = topk - cgt
    nbits = int(L).bit_length()
    bound_ref[...] = jnp.full((1, TQ), 2 ** nbits - 1, jnp.int32)

    @pl.when(jnp.max(ceq - need) > 0.0)
    def _():
        def flag_ties(kb, _):
            tie = ((rows16(hi_ref[rows(kb), :]) == thr_hi16) & (rows16(lo_ref[rows(kb), :]) == thr_lo16))
            lom_ref[rows(kb), :] = jnp.where(tie, I16(1), I16(0)).reshape(KB, TQ)
            return 0

        lax.fori_loop(0, i + 1, flag_ties, 0)
        krow16 = rows16(lax.broadcasted_iota(jnp.int32, (KB, TQ), 0).astype(I16))

        def bisect_idx(step, bnd):
            bit = jnp.left_shift(jnp.int32(1), nbits - 1 - step)
            cand = bnd | bit
            cnt = count16(lom_ref, lambda blk, kb: krow16 < as16(cand - kb * KB)[None], flags=True)
            return jnp.where(cnt <= need, cand, bnd)

        bound_ref[...] = lax.fori_loop(0, nbits, bisect_idx, jnp.zeros((1, TQ), jnp.int32))

    bound = bound_ref[...]

    m_ref[...] = jnp.full(m_ref.shape, NEG, F32)

    def fold_max(x):
        return jnp.max(x.reshape(x.shape[0] // PART, PART, TQ), axis=0)

    def store_logits(kb, bias_of_head):
        for r in range(0, KB, RB):
            blk = key_ref[rows(kb, r, RB), :]
            pos = lax.broadcasted_iota(jnp.int32, (RB, TQ), 0) + (kb * KB + r)
            keep = ((blk > thr) | ((blk == thr) & (pos < bound))) & (blk != INT_MIN)
            mask = jnp.where(keep, 0.0, NEG)
            for h in range(nheads):
                win, ko = qs[h]
                lg = lax.dot_general(kpad_ref[ko, rows(kb, r, RB), :], qop_ref[0, win], NT,
                                     preferred_element_type=F32)
                lg = lg + bias_of_head(h, r) + mask
                lg_ref[h, rows(kb, r, RB), :] = lg
                m_ref[h] = jnp.maximum(m_ref[h], fold_max(lg))

    def far_logits(kb, _):
        store_logits(kb, lambda h, r: far_ref[h])
        return 0

    def far_logits2(j, _):
        store_logits(2 * j, lambda h, r: far_ref[h])
        store_logits(2 * j + 1, lambda h, r: far_ref[h])
        return 0

    nfar = jnp.maximum(i - 1, 0)
    lax.fori_loop(0, lax.shift_right_logical(nfar, 1), far_logits2, 0)

    @pl.when((nfar & 1) == 1)
    def _():
        far_logits(nfar - 1, 0)

    @pl.when(i >= 1)
    def _():
        store_logits(i - 1, lambda h, r: band_ref[h, r:r + RB, :])

    store_logits(i, lambda h, r: band_ref[h, KB + r:KB + r + RB, :])

    m_fin = [jnp.max(m_ref[h], axis=0, keepdims=True) for h in range(nheads)]
    acc_ref[...] = jnp.zeros(acc_ref.shape, F32)

    def accumulate(kb):
        vt = vt_ref[:, rows(kb)]
        for h in range(nheads):
            p = jnp.exp(lg_ref[h, rows(kb), :] - m_fin[h])
            acc_ref[h] += jnp.dot(vt, p.astype(BF16), preferred_element_type=F32)

    has_next = i + 1 < pl.num_programs(1)

    @pl.when(has_next)
    def _():
        indexer = stage_indexer(qxn_ref)

        def both(kb):
            accumulate(kb)
            idx_keys(kb, False, 1 - cur, indexer)

        def both2(j, _):
            both(2 * j)
            both(2 * j + 1)
            return 0

        lax.fori_loop(0, lax.shift_right_logical(i + 1, 1), both2, 0)

        @pl.when(((i + 1) & 1) == 1)
        def _():
            both(i)

        idx_keys(i + 1, True, 1 - cur, indexer)

    @pl.when(jnp.logical_not(has_next))
    def _():
        def only(kb, _):
            accumulate(kb)
            return 0

        lax.fori_loop(0, i + 1, only, 0)

    outs = [acc_ref[h, 0:hd, :] / acc_ref[h, hd:hd + 1, :] for h in range(nheads)]
    out = jnp.concatenate(outs, axis=0).T
    z = _lane_window(qx_ref, offs['z'], nheads * hd)
    o_ref[...] = (out * jax.nn.silu(z)).astype(BF16)


def _dsa_mixer(proj, B, L, src, band, far, nheads, TQ):
    width = nheads * ATT_HEAD_DIM
    nq = L // TQ
    hd = ATT_HEAD_DIM
    qx_w = 16 * LANE
    qx_blk = src['att_q'][0] // qx_w
    qx_lo = qx_blk * qx_w
    kv_w = 2 * LANE
    kv_blk = src['att_k'][0] // kv_w
    ki_blk = src['idx_k'][0] // LANE
    offs = {'q': src['att_q'][0] - qx_lo, 'qi': src['idx_q'][0] - qx_lo, 'w': src['idx_w'][0] - qx_lo,
            'z': src['att_z'][0] - qx_lo, 'k': src['att_k'][0] - kv_blk * kv_w,
            'v': src['att_v'][0] - kv_blk * kv_w, 'ki': src['idx_k'][0] - ki_blk * LANE}
    assert offs['z'] + width + LANE <= qx_w and qx_lo + qx_w <= proj.shape[1]
    assert offs['k'] % LANE + hd <= LANE and offs['ki'] + hd <= LANE
    assert offs['v'] % SUBLANE == 0 and offs['v'] + hd <= kv_w
    assert offs['w'] // LANE == (offs['w'] + nheads - 1) // LANE
    win_off = _head_offsets(offs, nheads, hd)
    win = LANE if max(win_off) + hd <= LANE else 2 * LANE
    win_base = max((sorted({(offs[f] + h * hd) // LANE for h in range(nheads)}) for f in ('q', 'qi')), key=len)
    body = functools.partial(_dsa_body, nheads=nheads, hd=hd, topk=min(TOPK, L // 4), offs=offs)
    full = lambda a: pl.BlockSpec(a.shape, lambda b, i: (0,) * a.ndim)
    return pl.pallas_call(
        body,
        grid=(B, nq),
        in_specs=[pl.BlockSpec((TQ, qx_w), lambda b, i: (b * nq + i, qx_blk)),
                  pl.BlockSpec((TQ, qx_w), lambda b, i: (b * nq + jnp.minimum(i + 1, nq - 1), qx_blk)),
                  pl.BlockSpec((L, kv_w), lambda b, i: (b, kv_blk)),
                  pl.BlockSpec((L, LANE), lambda b, i: (b, ki_blk)),
                  full(band), full(far)],
        out_specs=pl.BlockSpec((TQ, width), lambda b, i: (b * nq + i, 0)),
        out_shape=jax.ShapeDtypeStruct((B * L, width), BF16),
        scratch_shapes=[pltpu.VMEM((len(win_off), L, win), BF16),
                        pltpu.VMEM((len(win_off), L, win), BF16),
                        pltpu.VMEM((ATT_HEAD_DIM + 2 * SUBLANE, L), BF16),
                        pltpu.VMEM((2, len(win_base), TQ, win), BF16),
                        pltpu.VMEM((2, L, TQ), jnp.int32),
                        pltpu.VMEM((2, L, TQ), jnp.int16),
                        pltpu.VMEM((2, L, TQ), jnp.int16),
                        pltpu.VMEM((L, TQ), jnp.int16),
                        pltpu.VMEM((nheads, L, TQ), F32),
                        pltpu.VMEM((1, TQ), jnp.int32),
                        pltpu.VMEM((nheads, 4 * SUBLANE, TQ), F32),
                        pltpu.VMEM((nheads, ATT_HEAD_DIM + 2 * SUBLANE, TQ), F32)],
        compiler_params=_cparams(("parallel", "arbitrary")),
        name="dsa_mixer",
    )(proj, proj, proj, proj, band, far)


def _t5_bucket_static(rel):
    nb = N_BUCKETS // 2
    max_exact = nb // 2
    ret = np.where(rel > 0, nb, 0)
    n = np.abs(rel)
    nf = np.maximum(n, 1).astype(np.float64)
    large = max_exact + (np.log(nf / max_exact) / math.log(MAX_DISTANCE / max_exact)
                         * (nb - max_exact)).astype(np.int32)
    large = np.minimum(large, nb - 1)
    return ret + np.where(n < max_exact, n, large)


def _dsa_bias_tables(rel_bias, TQ):
    a = np.arange(2 * TQ)[:, None]
    j = np.arange(TQ)[None, :]
    band_idx = _t5_bucket_static(a - TQ - j)
    idx = jnp.asarray(band_idx, jnp.int32)[None]
    band = jnp.zeros((rel_bias.shape[1],) + band_idx.shape, F32)
    for bucket in np.unique(band_idx):
        band = jnp.where(idx == int(bucket), rel_bias[int(bucket)][:, None, None], band)
    far_bucket = int(_t5_bucket_static(np.array([-(TQ + 1)]))[0])
    assert far_bucket == int(_t5_bucket_static(np.array([-(10 ** 6)]))[0])
    far = jnp.broadcast_to(rel_bias[far_bucket][:, None, None], (rel_bias.shape[1], 1, TQ))
    return band, far


def _layout(d_model):
    s5w = d_model // 4
    ssdw = d_model // 2
    attw = d_model // 4
    cdim = ssdw + 2 * SSD_GROUPS * SSD_STATE
    nh_ssd = ssdw // SSD_HEAD_DIM
    nh_att = attw // ATT_HEAD_DIM
    splits = (s5w, s5w, ssdw, cdim, nh_ssd, attw, ATT_HEAD_DIM, ATT_HEAD_DIM,
              nh_att * IDX_DIM, IDX_DIM, nh_att, attw)
    names = ('s5_u', 's5_z', 'ssd_z', 'ssd_xbc', 'ssd_dt', 'att_q', 'att_k', 'att_v',
             'idx_q', 'idx_k', 'idx_w', 'att_z')
    src, o = {}, 0
    for nme, s in zip(names, splits):
        src[nme] = (o, s)
        o += s
    return src, o


def kernel(x, norm_w, w_in, s5_A_re, s5_A_im, s5_log_dt, s5_B_re, s5_B_im, s5_C_re, s5_C_im, s5_D, s5_glu_w, s5_glu_b, ssd_conv_w, ssd_conv_b, ssd_dt_bias, ssd_A_log, ssd_D, ssd_norm_w, rel_bias, w_out, final_norm_w):
    B, L, d = x.shape
    depth = w_in.shape[0]
    src, total = _layout(d)
    s5w, ssdw, attw = d // 4, d // 2, d // 4
    cdim = ssdw + 2 * SSD_GROUPS * SSD_STATE
    nh_att = attw // ATT_HEAD_DIM
    assert src['s5_u'][0] % s5w == 0 and src['s5_z'][0] % s5w == 0 and src['ssd_z'][0] % ssdw == 0
    assert src['ssd_xbc'][0] % cdim == 0 and src['ssd_dt'][0] % LANE == 0

    tn = 1024
    cuts, shift, padded = [(0, 0)], 0, {}
    for nme, (so, sz) in sorted(src.items(), key=lambda kv: kv[1][0]):
        gap = -(so + shift) % LANE if nme in ('att_q', 'att_k', 'idx_q', 'idx_k', 'att_z') else 0
        if gap:
            shift += gap
            cuts.append((so, shift))
        padded[nme] = (so + shift, sz)
    src = padded
    total_pad = -(-(total + shift) // tn) * tn
    segs = tuple((lo, cuts[n + 1][0] if n + 1 < len(cuts) else total, lo + sh) for n, (lo, sh) in enumerate(cuts))
    w_in_p = _relayout_w_in(jnp.swapaxes(w_in, 1, 2), segs, total_pad, min(256, d))

    TQ = 256
    Q = 128
    band, far = _dsa_bias_tables(rel_bias, TQ)
    tm_in = min(1024, B * L)
    tm_out = min(512, B * L)
    nchunk = 2
    seg = L // SUBLANE

    s5p = jax.vmap(lambda *a: _s5_params(*a, seg, nchunk))(
        s5_A_re, s5_A_im, s5_log_dt, s5_B_re, s5_B_im, s5_C_re, s5_C_im, s5_D, s5_glu_w, s5_glu_b)
    ssdp = _ssd_params(ssd_conv_w, ssd_conv_b, ssd_dt_bias, ssd_A_log, ssd_D, ssd_norm_w)
    emat = _head_expansion(ssdw // SSD_HEAD_DIM)
    norm_w3 = norm_w.reshape(depth, 1, d)

    x2 = x.reshape(B * L, d)
    for l in range(depth):
        proj = _inproj(x2, norm_w3, w_in_p, l, tm_in, tn)
        y_s5 = _s5_mixer(proj, B, L, src['s5_u'][0] // s5w, src['s5_z'][0] // s5w, s5w, s5p, l)
        y_ssd = _ssd_mixer(proj, B, L, src['ssd_z'][0] // ssdw, src['ssd_xbc'][0] // cdim,
                           src['ssd_dt'][0] // LANE, ssdp, emat, l, Q)
        y_att = _dsa_mixer(proj, B, L, src, band, far, nh_att, TQ)
        x2 = _outproj(y_s5, y_ssd, y_att, w_out, l, x2, final_norm_w.reshape(1, d), tm_out,
                      final=(l == depth - 1))
    return x2.reshape(B, L, d)
```

```python
import functools
import math

import numpy as np
import jax
import jax.numpy as jnp
from jax import lax
from jax.experimental import pallas as pl
from jax.experimental.pallas import tpu as pltpu

F32 = jnp.float32
BF16 = jnp.bfloat16

EPS = 1e-6
CHUNK = 64

SSD_HEAD_DIM = 64
SSD_STATE = 128
SSD_GROUPS = 4
SSD_CONV = 4
ATT_HEAD_DIM = 64
IDX_DIM = 64
TOPK = 256
N_BUCKETS = 32
MAX_DISTANCE = 128

LANE = 128
SUBLANE = 8
VMEM_LIMIT = 56 * 1024 * 1024

INT_MIN = -2 ** 31
NEG = -1e30


def _cparams(sem):
    return pltpu.CompilerParams(dimension_semantics=sem, vmem_limit_bytes=VMEM_LIMIT)


def _inproj_body(x_ref, nw_ref, w_ref, o_ref, h_ref):
    @pl.when(pl.program_id(1) == 0)
    def _():
        xf = x_ref[...]
        ms = jnp.mean(xf * xf, axis=-1, keepdims=True)
        h_ref[...] = (xf * lax.rsqrt(ms + EPS) * nw_ref[...]).astype(BF16)

    o_ref[...] = lax.dot_general(h_ref[...], w_ref[...], (((1,), (1,)), ((), ())),
                                 preferred_element_type=F32)


def _inproj(x2, nw, w, layer, tm, tn):
    m, d = x2.shape
    n = w.shape[1]
    return pl.pallas_call(
        _inproj_body,
        grid=(m // tm, n // tn),
        in_specs=[pl.BlockSpec((tm, d), lambda i, j: (i, 0)),
                  pl.BlockSpec((None, 1, d), lambda i, j: (layer, 0, 0)),
                  pl.BlockSpec((None, tn, d), lambda i, j: (layer, j, 0))],
        out_specs=pl.BlockSpec((tm, tn), lambda i, j: (i, j)),
        out_shape=jax.ShapeDtypeStruct((m, n), F32),
        scratch_shapes=[pltpu.VMEM((tm, d), BF16)],
        compiler_params=_cparams(("parallel", "arbitrary")),
        name="inproj",
    )(x2, nw, w)


def _relayout_body(w_ref, o_ref, *, segs):
    pack = 2 * SUBLANE
    pos = 0
    for src_lo, src_hi, dst_lo in segs:
        if dst_lo > pos:
            o_ref[pos:dst_lo, :] = jnp.zeros((dst_lo - pos, o_ref.shape[1]), BF16)
        val = w_ref[src_lo:src_hi, :]
        fill = -(src_hi - src_lo) % pack
        if fill:
            val = jnp.concatenate([val, jnp.zeros((fill, val.shape[1]), F32)], axis=0)
        o_ref[dst_lo:dst_lo + val.shape[0], :] = val.astype(BF16)
        pos = dst_lo + val.shape[0]
    if pos < o_ref.shape[0]:
        o_ref[pos:, :] = jnp.zeros((o_ref.shape[0] - pos, o_ref.shape[1]), BF16)


def _relayout_w_in(w_t, segs, total_pad, tk):
    depth, n, d = w_t.shape
    assert all(lo % SUBLANE == 0 and (hi - lo) % SUBLANE == 0 and dst % (2 * SUBLANE) == 0
               for lo, hi, dst in segs)
    return pl.pallas_call(
        functools.partial(_relayout_body, segs=segs),
        grid=(depth, d // tk),
        in_specs=[pl.BlockSpec((None, n, tk), lambda l, i: (l, 0, i))],
        out_specs=pl.BlockSpec((None, total_pad, tk), lambda l, i: (l, 0, i)),
        out_shape=jax.ShapeDtypeStruct((depth, total_pad, d), BF16),
        compiler_params=_cparams(("parallel", "parallel")),
        name="w_in_relayout",
    )(w_t)


def _outproj_body(ys5_ref, yssd_ref, yatt_ref, w_ref, x_ref, fnw_ref, o_ref, wb_ref, *, w5, wssd, final):
    @pl.when(pl.program_id(0) == 0)
    def _():
        step = 256
        for r in range(0, w_ref.shape[0], step):
            wb_ref[r:r + step, :] = w_ref[r:r + step, :].astype(BF16)

    acc = x_ref[...]
    acc += jnp.dot(ys5_ref[...], wb_ref[0:w5, :], preferred_element_type=F32)
    acc += jnp.dot(yssd_ref[...], wb_ref[w5:w5 + wssd, :], preferred_element_type=F32)
    acc += jnp.dot(yatt_ref[...], wb_ref[w5 + wssd:, :], preferred_element_type=F32)
    if final:
        ms = jnp.mean(acc * acc, axis=-1, keepdims=True)
        acc = acc * lax.rsqrt(ms + EPS) * fnw_ref[...]
    o_ref[...] = acc


def _outproj(ys5, yssd, yatt, w, layer, x2, fnw, tm, final):
    m, d = x2.shape
    w5, wssd, watt = ys5.shape[1], yssd.shape[1], yatt.shape[1]
    body = functools.partial(_outproj_body, w5=w5, wssd=wssd, final=final)
    return pl.pallas_call(
        body,
        grid=(m // tm,),
        in_specs=[pl.BlockSpec((tm, w5), lambda i: (i, 0)),
                  pl.BlockSpec((tm, wssd), lambda i: (i, 0)),
                  pl.BlockSpec((tm, watt), lambda i: (i, 0)),
                  pl.BlockSpec((None,) + w.shape[1:], lambda i: (layer, 0, 0), pipeline_mode=pl.Buffered(1)),
                  pl.BlockSpec((tm, d), lambda i: (i, 0)),
                  pl.BlockSpec((1, d), lambda i: (0, 0))],
        out_specs=pl.BlockSpec((tm, d), lambda i: (i, 0)),
        out_shape=jax.ShapeDtypeStruct((m, d), F32),
        scratch_shapes=[pltpu.VMEM(w.shape[1:], BF16)],
        compiler_params=_cparams(("arbitrary",)),
        name="outproj",
    )(ys5, yssd, yatt, w, x2, fnw)


def _shift_down_one(x):
    rolled = pltpu.roll(x, 1, axis=0)
    row = lax.broadcasted_iota(jnp.int32, x.shape, 0)
    return jnp.where(row == 0, 0.0, rolled)


def _s5_body(u_ref, z_ref, bblk_ref, cblk_ref, lam_ref, lamseg_ref, d_ref, gw_ref, gb_ref,
             o_ref, uperm_ref, xs_ref, yperm_ref, *, nchunk, cw, sw, rt):
    L = u_ref.shape[0]
    seg = L // SUBLANE

    for c in range(nchunk):
        for j in range(SUBLANE):
            for k in range(cw // LANE):
                lo = c * cw + k * LANE
                uperm_ref[k, pl.ds(j, seg, stride=SUBLANE), :] = u_ref[pl.ds(j * seg, seg), lo:lo + LANE]
        for r0 in range(0, L, rt):
            up = jnp.concatenate([uperm_ref[k, r0:r0 + rt, :] for k in range(cw // LANE)], axis=1)
            xs_ref[r0:r0 + rt, :] = jnp.dot(up.astype(BF16), bblk_ref[c], preferred_element_type=F32)
        lr = lam_ref[c, 0]
        li = lam_ref[c, 1]

        def scan_step(tau, carry):
            xr, xi = carry
            row = pl.multiple_of(tau * SUBLANE, SUBLANE)
            nxr = lr * xr - li * xi + xs_ref[pl.ds(row, SUBLANE), 0:sw]
            nxi = lr * xi + li * xr + xs_ref[pl.ds(row, SUBLANE), sw:2 * sw]
            xs_ref[pl.ds(row, SUBLANE), 0:sw] = nxr
            xs_ref[pl.ds(row, SUBLANE), sw:2 * sw] = nxi
            return nxr, nxi

        zero = jnp.zeros((SUBLANE, sw), F32)
        er, ei = lax.fori_loop(0, seg, scan_step, (zero, zero), unroll=8)

        sr = lamseg_ref[c, 0]
        si = lamseg_ref[c, 1]
        cr, ci = zero, zero
        for _ in range(SUBLANE - 1):
            tr = er + (sr * cr - si * ci)
            ti = ei + (sr * ci + si * cr)
            cr, ci = _shift_down_one(tr), _shift_down_one(ti)

        def fix_step(tau, carry):
            fr, fi = carry
            nfr = lr * fr - li * fi
            nfi = lr * fi + li * fr
            row = pl.multiple_of(tau * SUBLANE, SUBLANE)
            xs_ref[pl.ds(row, SUBLANE), 0:sw] += nfr
            xs_ref[pl.ds(row, SUBLANE), sw:2 * sw] += nfi
            return nfr, nfi

        lax.fori_loop(0, seg, fix_step, (cr, ci), unroll=8)

        for r0 in range(0, L, rt):
            yc = jnp.dot(xs_ref[r0:r0 + rt, :].astype(BF16), cblk_ref[c], preferred_element_type=F32)
            for k in range(cw // LANE):
                yperm_ref[c * (cw // LANE) + k, r0:r0 + rt, :] = yc[:, k * LANE:(k + 1) * LANE]

    for j in range(SUBLANE):
        y = jnp.concatenate([yperm_ref[k, pl.ds(j, seg, stride=SUBLANE), :]
                             for k in range(yperm_ref.shape[0])], axis=1)
        y = y + d_ref[...] * u_ref[j * seg:(j + 1) * seg, :]
        y = jax.nn.gelu(y)
        g = jnp.dot(y.astype(BF16), gw_ref[...], preferred_element_type=F32) + gb_ref[...]
        y = y * jax.nn.sigmoid(g)
        o_ref[j * seg:(j + 1) * seg, :] = (y * jax.nn.silu(z_ref[j * seg:(j + 1) * seg, :])).astype(BF16)


def _layer_spec(a, layer):
    return pl.BlockSpec((None,) + a.shape[1:], lambda *_: (layer,) + (0,) * (a.ndim - 1))


def _s5_mixer(proj, B, L, u_blk, z_blk, width, prm, layer):
    bblk, cblk, lam, lamseg, dvec, gw, gb = prm
    nchunk, cw, sw2 = bblk.shape[1:]
    sw = sw2 // 2
    rt = min(512, L)
    body = functools.partial(_s5_body, nchunk=nchunk, cw=cw, sw=sw, rt=rt)
    full = lambda a: _layer_spec(a, layer)
    return pl.pallas_call(
        body,
        grid=(B,),
        in_specs=[pl.BlockSpec((L, width), lambda b: (b, u_blk)),
                  pl.BlockSpec((L, width), lambda b: (b, z_blk)),
                  full(bblk), full(cblk), full(lam), full(lamseg), full(dvec), full(gw), full(gb)],
        out_specs=pl.BlockSpec((L, width), lambda b: (b, 0)),
        out_shape=jax.ShapeDtypeStruct((B * L, width), BF16),
        scratch_shapes=[pltpu.VMEM((cw // LANE, L, LANE), F32),
                        pltpu.VMEM((L, 2 * sw), F32),
                        pltpu.VMEM((width // LANE, L, LANE), F32)],
        compiler_params=_cparams(("parallel",)),
        name="s5_mixer",
    )(proj, proj, bblk, cblk, lam, lamseg, dvec, gw, gb)


def _cpow(re, im, n):
    rr, ri = jnp.ones_like(re), jnp.zeros_like(im)
    br, bi = re, im
    while n:
        if n & 1:
            rr, ri = rr * br - ri * bi, rr * bi + ri * br
        br, bi = br * br - bi * bi, 2.0 * br * bi
        n >>= 1
    return rr, ri


def _s5_params(A_re, A_im, log_dt, B_re, B_im, C_re, C_im, D, glu_w, glu_b, seg, nchunk):
    G, P, C = B_re.shape
    dt = jnp.exp(log_dt)[:, None]
    lre = jnp.minimum(A_re, -1e-4)
    lim = A_im
    mag = jnp.exp(lre * dt)
    lbr = mag * jnp.cos(lim * dt)
    lbi = mag * jnp.sin(lim * dt)
    nr, ni = lbr - 1.0, lbi
    den = lre * lre + lim * lim
    fr = (nr * lre + ni * lim) / den
    fi = (ni * lre - nr * lim) / den
    bbr = fr[..., None] * B_re - fi[..., None] * B_im
    bbi = fr[..., None] * B_im + fi[..., None] * B_re
    gc = G // nchunk
    eye = jnp.eye(gc, dtype=F32)

    def blockdiag_in(bb):
        t = jnp.transpose(bb, (0, 2, 1)).reshape(nchunk, gc, C, P)
        return jnp.einsum('ngcp,gh->ngchp', t, eye).reshape(nchunk, gc * C, gc * P)

    def blockdiag_out(cc):
        t = jnp.transpose(cc, (0, 2, 1)).reshape(nchunk, gc, P, C)
        return jnp.einsum('ngpc,gh->ngphc', t, eye).reshape(nchunk, gc * P, gc * C)

    bblk = jnp.concatenate([blockdiag_in(bbr), blockdiag_in(bbi)], axis=-1).astype(BF16)
    cblk = jnp.concatenate([blockdiag_out(C_re), blockdiag_out(-C_im)], axis=1).astype(BF16)

    def rows(v):
        return jnp.broadcast_to(v.reshape(nchunk, 1, gc * P), (nchunk, SUBLANE, gc * P))

    lam = jnp.stack([rows(lbr), rows(lbi)], axis=1)
    pr, pi = _cpow(lbr, lbi, seg)
    lamseg = jnp.stack([rows(pr), rows(pi)], axis=1)
    return (bblk, cblk, lam, lamseg, D.reshape(1, G * C), glu_w.astype(BF16), glu_b.reshape(1, -1))


def _split3(x):
    h = x.astype(BF16)
    r = x - h.astype(F32)
    m = r.astype(BF16)
    l = (r - m.astype(F32)).astype(BF16)
    return h, m, l


def _dot_f32(a, b3_bf16):
    return jnp.dot(jnp.concatenate(_split3(a), axis=1), b3_bf16, preferred_element_type=F32)


def _ssd_body(z_ref, xbc_ref, dt_ref, cw_ref, cb_ref, dtb_ref, a_ref, dx_ref, nw_ref, e_ref,
              o_ref, state_ref, xpad_ref, *, nheads, hd, ns, ngroups):
    Q = z_ref.shape[0]
    width = nheads * hd
    gw = width // ngroups
    halo = xpad_ref.shape[0] - Q

    @pl.when(pl.program_id(1) == 0)
    def _():
        state_ref[...] = jnp.zeros_like(state_ref)
        xpad_ref[0:halo, :] = jnp.zeros((halo, xpad_ref.shape[1]), F32)

    xpad_ref[halo:halo + Q, :] = xbc_ref[...]
    xp = xpad_ref[...]
    acc = cw_ref[0:1, :] * xp
    for k in range(1, SSD_CONV):
        acc = cw_ref[k:k + 1, :] * xp + pltpu.roll(acc, 1, axis=0)
    xpad_ref[0:halo, :] = xpad_ref[Q:Q + halo, :]
    xc = jax.nn.silu(acc[halo:, :] + cb_ref[...])
    xs = xc[:, 0:width]
    bm = xc[:, width:width + ngroups * ns].astype(BF16)
    cm = xc[:, width + ngroups * ns:].astype(BF16)

    lane = lax.broadcasted_iota(jnp.int32, (Q, LANE), 1)
    dt = jnp.where(lane < nheads, jax.nn.softplus(dt_ref[...] + dtb_ref[...]), 0.0)
    a = dt * a_ref[...]
    rowi = lax.broadcasted_iota(jnp.int32, (Q, Q), 0)
    coli = lax.broadcasted_iota(jnp.int32, (Q, Q), 1)
    tril = coli <= rowi
    acs = _dot_f32_lhs(tril.astype(BF16), a)
    acs_t = acs.T
    last = acs[Q - 1:Q, :]
    emat = e_ref[...]
    dt_x = _expand(dt, emat)
    dec_x = _expand(jnp.exp(acs), emat)
    dte_x = _expand(jnp.exp(last - acs), emat)
    dlast_x = dec_x[Q - 1:Q, :]

    xd = xs * dt_x
    xd_b = xd.astype(BF16)
    xw_b = (xd * dte_x).astype(BF16)
    st = state_ref[...]
    st_b = st.astype(BF16)

    y_parts = []
    new_state = []
    hlane = lax.broadcasted_iota(jnp.int32, (Q, LANE), 1) < hd
    heads_per_group = nheads // ngroups
    for g in range(ngroups):
        cg = cm[:, g * ns:(g + 1) * ns]
        bg = bm[:, g * ns:(g + 1) * ns]
        cb = lax.dot_general(cg, bg, (((1,), (1,)), ((), ())), preferred_element_type=F32)
        y_off = jnp.dot(cg, st_b[:, g * gw:(g + 1) * gw], preferred_element_type=F32)
        pair_out = []
        for pr in range(gw // LANE):
            xpair = xd_b[:, g * gw + pr * LANE: g * gw + (pr + 1) * LANE]
            res = []
            for hh in range(LANE // hd):
                h = g * heads_per_group + pr * (LANE // hd) + hh
                diff = acs[:, h:h + 1] - acs_t[h:h + 1, :]
                s_h = jnp.where(tril, cb * jnp.exp(jnp.minimum(diff, 0.0)), 0.0).astype(BF16)
                res.append(jnp.dot(s_h, xpair, preferred_element_type=F32))
            pair_out.append(jnp.where(hlane, res[0], res[1]))
        y_diag = jnp.concatenate(pair_out, axis=1)
        y_parts.append(y_diag + y_off * dec_x[:, g * gw:(g + 1) * gw])
        upd = lax.dot_general(bg, xw_b[:, g * gw:(g + 1) * gw], (((0,), (0,)), ((), ())),
                              preferred_element_type=F32)
        new_state.append(st[:, g * gw:(g + 1) * gw] * dlast_x[:, g * gw:(g + 1) * gw] + upd)
    state_ref[...] = jnp.concatenate(new_state, axis=1)

    y = jnp.concatenate(y_parts, axis=1) + xs * dx_ref[...]
    gt = y * jax.nn.silu(z_ref[...])
    ms = jnp.mean(gt * gt, axis=-1, keepdims=True)
    o_ref[...] = (gt * lax.rsqrt(ms + EPS) * nw_ref[...]).astype(BF16)


def _dot_f32_lhs(a_bf16, b):
    return jnp.dot(jnp.concatenate([a_bf16] * 3, axis=1), jnp.concatenate(_split3(b), axis=0),
                   preferred_element_type=F32)


def _expand(v, emat):
    return _dot_f32(v, emat)


def _ssd_mixer(proj, B, L, z_blk, xbc_blk, dt_blk, prm, emat, layer, Q):
    cw, cb, dtb, avec, dx, nw = prm
    width = nw.shape[-1]
    cdim = cw.shape[-1]
    nheads = width // SSD_HEAD_DIM
    nt = L // Q
    body = functools.partial(_ssd_body, nheads=nheads, hd=SSD_HEAD_DIM, ns=SSD_STATE, ngroups=SSD_GROUPS)
    full = lambda a: _layer_spec(a, layer)
    return pl.pallas_call(
        body,
        grid=(B, nt),
        in_specs=[pl.BlockSpec((Q, width), lambda b, t: (b * nt + t, z_blk)),
                  pl.BlockSpec((Q, cdim), lambda b, t: (b * nt + t, xbc_blk)),
                  pl.BlockSpec((Q, LANE), lambda b, t: (b * nt + t, dt_blk)),
                  full(cw), full(cb), full(dtb), full(avec), full(dx), full(nw),
                  pl.BlockSpec(emat.shape, lambda b, t: (0, 0))],
        out_specs=pl.BlockSpec((Q, width), lambda b, t: (b * nt + t, 0)),
        out_shape=jax.ShapeDtypeStruct((B * L, width), BF16),
        scratch_shapes=[pltpu.VMEM((SSD_STATE, width), F32),
                        pltpu.VMEM((Q + SUBLANE, cdim), F32)],
        compiler_params=_cparams(("parallel", "arbitrary")),
        name="ssd_mixer",
    )(proj, proj, proj, cw, cb, dtb, avec, dx, nw, emat)


def _ssd_params(conv_w, conv_b, dt_bias, A_log, D, norm_w):
    depth, nheads = dt_bias.shape
    width = nheads * SSD_HEAD_DIM
    pad = lambda v: jnp.pad(v, ((0, 0), (0, LANE - nheads))).reshape(depth, 1, LANE)
    return (conv_w, conv_b.reshape(depth, 1, -1), pad(dt_bias), pad(-jnp.exp(A_log)),
            jnp.repeat(D, SSD_HEAD_DIM, axis=1).reshape(depth, 1, width), norm_w.reshape(depth, 1, width))


def _head_expansion(nheads):
    e = np.zeros((LANE, nheads * SSD_HEAD_DIM), np.float32)
    e[np.repeat(np.arange(nheads), SSD_HEAD_DIM), np.arange(nheads * SSD_HEAD_DIM)] = 1.0
    return jnp.asarray(np.concatenate([e] * 3, axis=0), BF16)


def _float_key(x):
    b = lax.bitcast_convert_type(x, jnp.int32)
    return b ^ ((b >> 31) & 0x7FFFFFFF)


def _place(blk, lane0, width, dst):
    lane = lax.broadcasted_iota(jnp.int32, blk.shape, 1)
    shift = (dst - lane0) % LANE
    r = pltpu.roll(blk, shift, axis=1) if shift else blk
    lo = jnp.where((lane >= dst) & (lane < min(dst + width, LANE)), r, 0.0)
    hi = jnp.where(lane < dst + width - LANE, r, 0.0)
    return jnp.concatenate([lo, hi], axis=1)


def _head_offsets(offs, nheads, hd):
    return sorted({(offs[f] + h * hd) % LANE for f in ('q', 'qi') for h in range(nheads)})


def _lane_window(x, start, width):
    off = start % LANE
    base = start - off
    if off == 0:
        return x[:, start:start + width]
    lane = lax.broadcasted_iota(jnp.int32, (x.shape[0], LANE), 1)
    out = []
    for p in range(width // LANE):
        a = pltpu.roll(x[:, base + p * LANE:base + (p + 1) * LANE], LANE - off, axis=1)
        b = pltpu.roll(x[:, base + (p + 1) * LANE:base + (p + 2) * LANE], LANE - off, axis=1)
        out.append(jnp.where(lane < LANE - off, a, b))
    return jnp.concatenate(out, axis=1)


def _dsa_body(qx_ref, qxn_ref, kv_ref, ki_ref, band_ref, far_ref, o_ref,
              kpad_ref, kipad_ref, vt_ref, qop_ref, key2_ref, hi2_ref, lo2_ref, lom_ref, lg_ref, bound_ref,
              m_ref, acc_ref,
              *, nheads, hd, topk, offs):
    i = pl.program_id(1)
    cur = lax.rem(i, 2)
    key_ref, hi_ref, lo_ref = key2_ref.at[cur], hi2_ref.at[cur], lo2_ref.at[cur]
    TQ = qx_ref.shape[0]
    KB = TQ
    L = kv_ref.shape[0]
    nkb_total = L // KB
    NT = (((1,), (1,)), ((), ()))
    WIN = kpad_ref.shape[2]

    def head_window(field_lo, h):
        lo = field_lo + h * hd
        return (lo // LANE) * LANE, lo % LANE

    head_offs = _head_offsets(offs, nheads, hd)

    @pl.when(i == 0)
    def _():
        for kb in range(nkb_total):
            r = slice(kb * KB, (kb + 1) * KB)
            kgrp = (offs['k'] // LANE) * LANE
            kblk = kv_ref[r, kgrp:kgrp + LANE]
            kiblk = ki_ref[r, :]
            for n, o in enumerate(head_offs):
                kpad_ref[n, r, :] = _place(kblk, offs['k'] - kgrp, hd, o)[:, :WIN].astype(BF16)
                kipad_ref[n, r, :] = _place(kiblk, offs['ki'], hd, o)[:, :WIN].astype(BF16)
            vt_ref[0:hd, r] = kv_ref[r, :].T[offs['v']:offs['v'] + hd, :].astype(BF16)
            vt_ref[hd:, r] = jnp.ones((vt_ref.shape[0] - hd, KB), BF16)

    def stage_windows(slot, field_lo, scale, src_ref):
        bases = sorted({head_window(field_lo, h)[0] for h in range(nheads)})
        for n, base in enumerate(bases):
            qop_ref[slot, n] = (src_ref[:, base:base + WIN] * scale).astype(BF16)
        return [(bases.index(head_window(field_lo, h)[0]), head_offs.index(head_window(field_lo, h)[1]))
                for h in range(nheads)]

    qs = stage_windows(0, offs['q'], hd ** -0.5, qx_ref)

    def stage_indexer(src_ref):
        table = stage_windows(1, offs['qi'], 1.0, src_ref)
        wbase = (offs['w'] // LANE) * LANE
        wt = src_ref[:, wbase:wbase + LANE].T * ((nheads * IDX_DIM) ** -0.5)
        return table, [wt[offs['w'] - wbase + h:offs['w'] - wbase + h + 1, :] for h in range(nheads)]

    RB = LANE
    I16 = jnp.int16
    HALF = 1 << 15

    def rows(kb, r=0, n=KB):
        return pl.ds(pl.multiple_of(kb * KB + r, SUBLANE), n)

    PART = m_ref.shape[1]

    def fold_sum(x):
        return jnp.sum(x.reshape(x.shape[0] // PART, PART, TQ), axis=0)

    def idx_keys(kb, diagonal, slot, indexer):
        qis, w_rows = indexer
        for r in range(0, KB, RB):
            s = jnp.zeros((RB, TQ), F32)
            for h in range(nheads):
                win, ko = qis[h]
                lg = lax.dot_general(kipad_ref[ko, rows(kb, r, RB), :], qop_ref[1, win], NT,
                                     preferred_element_type=F32)
                s += jnp.maximum(lg, 0.0) * w_rows[h]
            key = _float_key(s)
            if diagonal:
                krow = lax.broadcasted_iota(jnp.int32, (RB, TQ), 0) + r
                qcol = lax.broadcasted_iota(jnp.int32, (RB, TQ), 1)
                key = jnp.where((krow // CHUNK) <= (qcol // CHUNK), key, INT_MIN)
            key2_ref[slot, rows(kb, r, RB), :] = key
            hi2_ref[slot, rows(kb, r, RB), :] = (key >> 16).astype(I16)
            lo2_ref[slot, rows(kb, r, RB), :] = ((key & (2 * HALF - 1)) - HALF).astype(I16)

    @pl.when(i == 0)
    def _():
        idx_keys(0, True, cur, stage_indexer(qx_ref))

    def count(preds):
        def body(kb, cs):
            blk = key_ref[rows(kb), :]
            return tuple(c + fold_sum(jnp.where(p(blk, kb), 1.0, 0.0)) for c, p in zip(cs, preds))
        z = jnp.zeros((PART, TQ), F32)
        cs = lax.fori_loop(0, i + 1, body, (z,) * len(preds))
        return [jnp.sum(c, axis=0, keepdims=True) for c in cs]

    PACK = 2 * SUBLANE
    P16 = 4 * PACK

    def rows16(x):
        return x.reshape(x.shape[0] // PACK, PACK, TQ)

    def as16(v):
        return jnp.broadcast_to(v, (PACK, TQ)).astype(I16)

    def count16(src_ref, pred, flags=False):
        def body(kb, c):
            blk = rows16(src_ref[rows(kb), :])
            ind = jnp.where(pred(blk, kb), blk if flags else I16(1), I16(0)).reshape(KB // P16, P16, TQ)
            for n in range(KB // P16):
                c = c + ind[n]
            return c
        c = lax.fori_loop(0, i + 1, body, jnp.zeros((P16, TQ), I16))
        c = rows16(c)
        c = (c[0] + c[1]) + (c[2] + c[3])
        return jnp.sum(c.astype(F32), axis=0, keepdims=True)

    def bisect16(src_ref, target):
        def step(s, tu):
            cand_u = tu | jnp.left_shift(jnp.int32(1), 15 - s)
            cand = as16(cand_u - HALF)[None]
            cnt = count16(src_ref, lambda blk, kb: blk >= cand)
            return jnp.where(cnt >= target, cand_u, tu)
        return lax.fori_loop(0, 16, step, jnp.zeros((1, TQ), jnp.int32))

    thr_hi = bisect16(hi_ref, float(topk)) - HALF
    thr_hi16 = as16(thr_hi)[None]
    need_lo = topk - count16(hi_ref, lambda blk, kb: blk > thr_hi16)

    def mask_low_halves(kb, _):
        keep_lo = rows16(hi_ref[rows(kb), :]) == thr_hi16
        lom_ref[rows(kb), :] = jnp.where(keep_lo, rows16(lo_ref[rows(kb), :]), I16(-HALF)).reshape(KB, TQ)
        return 0

    lax.fori_loop(0, i + 1, mask_low_halves, 0)
    thr_lo = bisect16(lom_ref, need_lo)
    thr = thr_hi * (2 * HALF) + thr_lo

    cgt, ceq = count([lambda blk, kb: blk > thr, lambda blk, kb: blk == thr])
    need = topk - cgt
    nbits = int(L).bit_length()
    bound_ref[...] = jnp.full((1, TQ), 2 ** nbits - 1, jnp.int32)

    @pl.when(jnp.max(ceq - need) > 0.0)
    def _():
        thr_lo16 = as16(thr_lo - HALF)[None]

        def flag_ties(kb, _):
            tie = ((rows16(hi_ref[rows(kb), :]) == thr_hi16) & (rows16(lo_ref[rows(kb), :]) == thr_lo16))
            lom_ref[rows(kb), :] = jnp.where(tie, I16(1), I16(0)).reshape(KB, TQ)
            return 0

        lax.fori_loop(0, i + 1, flag_ties, 0)
        krow16 = rows16(lax.broadcasted_iota(jnp.int32, (KB, TQ), 0).astype(I16))

        def bisect_idx(step, bnd):
            bit = jnp.left_shift(jnp.int32(1), nbits - 1 - step)
            cand = bnd | bit
            cnt = count16(lom_ref, lambda blk, kb: krow16 < as16(cand - kb * KB)[None], flags=True)
            return jnp.where(cnt <= need, cand, bnd)

        bound_ref[...] = lax.fori_loop(0, nbits, bisect_idx, jnp.zeros((1, TQ), jnp.int32))

    bound = bound_ref[...]

    m_ref[...] = jnp.full(m_ref.shape, NEG, F32)

    def fold_max(x):
        return jnp.max(x.reshape(x.shape[0] // PART, PART, TQ), axis=0)

    def store_logits(kb, bias_of_head):
        for r in range(0, KB, RB):
            blk = key_ref[rows(kb, r, RB), :]
            pos = lax.broadcasted_iota(jnp.int32, (RB, TQ), 0) + (kb * KB + r)
            keep = ((blk > thr) | ((blk == thr) & (pos < bound))) & (blk != INT_MIN)
            mask = jnp.where(keep, 0.0, NEG)
            for h in range(nheads):
                win, ko = qs[h]
                lg = lax.dot_general(kpad_ref[ko, rows(kb, r, RB), :], qop_ref[0, win], NT,
                                     preferred_element_type=F32)
                lg = lg + bias_of_head(h, r) + mask
                lg_ref[h, rows(kb, r, RB), :] = lg
                m_ref[h] = jnp.maximum(m_ref[h], fold_max(lg))

    def far_logits(kb, _):
        store_logits(kb, lambda h, r: far_ref[h])
        return 0

    def far_logits2(j, _):
        store_logits(2 * j, lambda h, r: far_ref[h])
        store_logits(2 * j + 1, lambda h, r: far_ref[h])
        return 0

    nfar = jnp.maximum(i - 1, 0)
    lax.fori_loop(0, lax.shift_right_logical(nfar, 1), far_logits2, 0)

    @pl.when((nfar & 1) == 1)
    def _():
        far_logits(nfar - 1, 0)

    @pl.when(i >= 1)
    def _():
        store_logits(i - 1, lambda h, r: band_ref[h, r:r + RB, :])

    store_logits(i, lambda h, r: band_ref[h, KB + r:KB + r + RB, :])

    m_fin = [jnp.max(m_ref[h], axis=0, keepdims=True) for h in range(nheads)]
    acc_ref[...] = jnp.zeros(acc_ref.shape, F32)

    def accumulate(kb):
        vt = vt_ref[:, rows(kb)]
        for h in range(nheads):
            p = jnp.exp(lg_ref[h, rows(kb), :] - m_fin[h])
            acc_ref[h] += jnp.dot(vt, p.astype(BF16), preferred_element_type=F32)

    has_next = i + 1 < pl.num_programs(1)

    @pl.when(has_next)
    def _():
        indexer = stage_indexer(qxn_ref)

        def both(kb, _):
            accumulate(kb)
            idx_keys(kb, False, 1 - cur, indexer)
            return 0

        lax.fori_loop(0, i + 1, both, 0)
        idx_keys(i + 1, True, 1 - cur, indexer)

    @pl.when(jnp.logical_not(has_next))
    def _():
        def only(kb, _):
            accumulate(kb)
            return 0

        lax.fori_loop(0, i + 1, only, 0)

    outs = [acc_ref[h, 0:hd, :] / acc_ref[h, hd:hd + 1, :] for h in range(nheads)]
    out = jnp.concatenate(outs, axis=0).T
    z = _lane_window(qx_ref, offs['z'], nheads * hd)
    o_ref[...] = (out * jax.nn.silu(z)).astype(BF16)


def _dsa_mixer(proj, B, L, src, band, far, nheads, TQ):
    width = nheads * ATT_HEAD_DIM
    nq = L // TQ
    hd = ATT_HEAD_DIM
    qx_w = 16 * LANE
    qx_blk = src['att_q'][0] // qx_w
    qx_lo = qx_blk * qx_w
    kv_w = 2 * LANE
    kv_blk = src['att_k'][0] // kv_w
    ki_blk = src['idx_k'][0] // LANE
    offs = {'q': src['att_q'][0] - qx_lo, 'qi': src['idx_q'][0] - qx_lo, 'w': src['idx_w'][0] - qx_lo,
            'z': src['att_z'][0] - qx_lo, 'k': src['att_k'][0] - kv_blk * kv_w,
            'v': src['att_v'][0] - kv_blk * kv_w, 'ki': src['idx_k'][0] - ki_blk * LANE}
    assert offs['z'] + width + LANE <= qx_w and qx_lo + qx_w <= proj.shape[1]
    assert offs['k'] % LANE + hd <= LANE and offs['ki'] + hd <= LANE
    assert offs['v'] % SUBLANE == 0 and offs['v'] + hd <= kv_w
    assert offs['w'] // LANE == (offs['w'] + nheads - 1) // LANE
    win_off = _head_offsets(offs, nheads, hd)
    win = LANE if max(win_off) + hd <= LANE else 2 * LANE
    win_base = max((sorted({(offs[f] + h * hd) // LANE for h in range(nheads)}) for f in ('q', 'qi')), key=len)
    body = functools.partial(_dsa_body, nheads=nheads, hd=hd, topk=min(TOPK, L // 4), offs=offs)
    full = lambda a: pl.BlockSpec(a.shape, lambda b, i: (0,) * a.ndim)
    return pl.pallas_call(
        body,
        grid=(B, nq),
        in_specs=[pl.BlockSpec((TQ, qx_w), lambda b, i: (b * nq + i, qx_blk)),
                  pl.BlockSpec((TQ, qx_w), lambda b, i: (b * nq + jnp.minimum(i + 1, nq - 1), qx_blk)),
                  pl.BlockSpec((L, kv_w), lambda b, i: (b, kv_blk)),
                  pl.BlockSpec((L, LANE), lambda b, i: (b, ki_blk)),
                  full(band), full(far)],
        out_specs=pl.BlockSpec((TQ, width), lambda b, i: (b * nq + i, 0)),
        out_shape=jax.ShapeDtypeStruct((B * L, width), BF16),
        scratch_shapes=[pltpu.VMEM((len(win_off), L, win), BF16),
                        pltpu.VMEM((len(win_off), L, win), BF16),
                        pltpu.VMEM((ATT_HEAD_DIM + 2 * SUBLANE, L), BF16),
                        pltpu.VMEM((2, len(win_base), TQ, win), BF16),
                        pltpu.VMEM((2, L, TQ), jnp.int32),
                        pltpu.VMEM((2, L, TQ), jnp.int16),
                        pltpu.VMEM((2, L, TQ), jnp.int16),
                        pltpu.VMEM((L, TQ), jnp.int16),
                        pltpu.VMEM((nheads, L, TQ), F32),
                        pltpu.VMEM((1, TQ), jnp.int32),
                        pltpu.VMEM((nheads, 4 * SUBLANE, TQ), F32),
                        pltpu.VMEM((nheads, ATT_HEAD_DIM + 2 * SUBLANE, TQ), F32)],
        compiler_params=_cparams(("parallel", "arbitrary")),
        name="dsa_mixer",
    )(proj, proj, proj, proj, band, far)


def _t5_bucket_static(rel):
    nb = N_BUCKETS // 2
    max_exact = nb // 2
    ret = np.where(rel > 0, nb, 0)
    n = np.abs(rel)
    nf = np.maximum(n, 1).astype(np.float64)
    large = max_exact + (np.log(nf / max_exact) / math.log(MAX_DISTANCE / max_exact)
                         * (nb - max_exact)).astype(np.int32)
    large = np.minimum(large, nb - 1)
    return ret + np.where(n < max_exact, n, large)


def _dsa_bias_tables(rel_bias, TQ):
    a = np.arange(2 * TQ)[:, None]
    j = np.arange(TQ)[None, :]
    band_idx = _t5_bucket_static(a - TQ - j)
    idx = jnp.asarray(band_idx, jnp.int32)[None]
    band = jnp.zeros((rel_bias.shape[1],) + band_idx.shape, F32)
    for bucket in np.unique(band_idx):
        band = jnp.where(idx == int(bucket), rel_bias[int(bucket)][:, None, None], band)
    far_bucket = int(_t5_bucket_static(np.array([-(TQ + 1)]))[0])
    assert far_bucket == int(_t5_bucket_static(np.array([-(10 ** 6)]))[0])
    far = jnp.broadcast_to(rel_bias[far_bucket][:, None, None], (rel_bias.shape[1], 1, TQ))
    return band, far


def _layout(d_model):
    s5w = d_model // 4
    ssdw = d_model // 2
    attw = d_model // 4
    cdim = ssdw + 2 * SSD_GROUPS * SSD_STATE
    nh_ssd = ssdw // SSD_HEAD_DIM
    nh_att = attw // ATT_HEAD_DIM
    splits = (s5w, s5w, ssdw, cdim, nh_ssd, attw, ATT_HEAD_DIM, ATT_HEAD_DIM,
              nh_att * IDX_DIM, IDX_DIM, nh_att, attw)
    names = ('s5_u', 's5_z', 'ssd_z', 'ssd_xbc', 'ssd_dt', 'att_q', 'att_k', 'att_v',
             'idx_q', 'idx_k', 'idx_w', 'att_z')
    src, o = {}, 0
    for nme, s in zip(names, splits):
        src[nme] = (o, s)
        o += s
    return src, o


def kernel(x, norm_w, w_in, s5_A_re, s5_A_im, s5_log_dt, s5_B_re, s5_B_im, s5_C_re, s5_C_im, s5_D, s5_glu_w, s5_glu_b, ssd_conv_w, ssd_conv_b, ssd_dt_bias, ssd_A_log, ssd_D, ssd_norm_w, rel_bias, w_out, final_norm_w):
    B, L, d = x.shape
    depth = w_in.shape[0]
    src, total = _layout(d)
    s5w, ssdw, attw = d // 4, d // 2, d // 4
    cdim = ssdw + 2 * SSD_GROUPS * SSD_STATE
    nh_att = attw // ATT_HEAD_DIM
    assert src['s5_u'][0] % s5w == 0 and src['s5_z'][0] % s5w == 0 and src['ssd_z'][0] % ssdw == 0
    assert src['ssd_xbc'][0] % cdim == 0 and src['ssd_dt'][0] % LANE == 0

    tn = 1024
    cuts, shift, padded = [(0, 0)], 0, {}
    for nme, (so, sz) in sorted(src.items(), key=lambda kv: kv[1][0]):
        gap = -(so + shift) % LANE if nme in ('att_q', 'att_k', 'idx_q', 'idx_k', 'att_z') else 0
        if gap:
            shift += gap
            cuts.append((so, shift))
        padded[nme] = (so + shift, sz)
    src = padded
    total_pad = -(-(total + shift) // tn) * tn
    segs = tuple((lo, cuts[n + 1][0] if n + 1 < len(cuts) else total, lo + sh) for n, (lo, sh) in enumerate(cuts))
    w_in_p = _relayout_w_in(jnp.swapaxes(w_in, 1, 2), segs, total_pad, min(256, d))

    TQ = 256
    Q = 256
    band, far = _dsa_bias_tables(rel_bias, TQ)
    tm_in = min(1024, B * L)
    tm_out = min(512, B * L)
    nchunk = 2
    seg = L // SUBLANE

    s5p = jax.vmap(lambda *a: _s5_params(*a, seg, nchunk))(
        s5_A_re, s5_A_im, s5_log_dt, s5_B_re, s5_B_im, s5_C_re, s5_C_im, s5_D, s5_glu_w, s5_glu_b)
    ssdp = _ssd_params(ssd_conv_w, ssd_conv_b, ssd_dt_bias, ssd_A_log, ssd_D, ssd_norm_w)
    emat = _head_expansion(ssdw // SSD_HEAD_DIM)
    norm_w3 = norm_w.reshape(depth, 1, d)

    x2 = x.reshape(B * L, d)
    for l in range(depth):
        proj = _inproj(x2, norm_w3, w_in_p, l, tm_in, tn)
        y_s5 = _s5_mixer(proj, B, L, src['s5_u'][0] // s5w, src['s5_z'][0] // s5w, s5w, s5p, l)
        y_ssd = _ssd_mixer(proj, B, L, src['ssd_z'][0] // ssdw, src['ssd_xbc'][0] // cdim,
                           src['ssd_dt'][0] // LANE, ssdp, emat, l, Q)
        y_att = _dsa_mixer(proj, B, L, src, band, far, nh_att, TQ)
        x2 = _outproj(y_s5, y_ssd, y_att, w_out, l, x2, final_norm_w.reshape(1, d), tm_out,
                      final=(l == depth - 1))
    return x2.reshape(B, L, d)
```

```python
import functools
import math

import numpy as np
import jax
import jax.numpy as jnp
from jax import lax
from jax.experimental import pallas as pl
from jax.experimental.pallas import tpu as pltpu

F32 = jnp.float32
BF16 = jnp.bfloat16

EPS = 1e-6
CHUNK = 64

SSD_HEAD_DIM = 64
SSD_STATE = 128
SSD_GROUPS = 4
SSD_CONV = 4
ATT_HEAD_DIM = 64
IDX_DIM = 64
TOPK = 256
N_BUCKETS = 32
MAX_DISTANCE = 128

LANE = 128
SUBLANE = 8
VMEM_LIMIT = 56 * 1024 * 1024

INT_MIN = -2 ** 31
NEG = -1e30


def _cparams(sem):
    return pltpu.CompilerParams(dimension_semantics=sem, vmem_limit_bytes=VMEM_LIMIT)


def _inproj_body(x_ref, nw_ref, w_ref, o_ref, h_ref):
    @pl.when(pl.program_id(1) == 0)
    def _():
        xf = x_ref[...]
        ms = jnp.mean(xf * xf, axis=-1, keepdims=True)
        h_ref[...] = (xf * lax.rsqrt(ms + EPS) * nw_ref[...]).astype(BF16)

    o_ref[...] = lax.dot_general(h_ref[...], w_ref[...], (((1,), (1,)), ((), ())),
                                 preferred_element_type=F32)


def _inproj(x2, nw, w, layer, tm, tn):
    m, d = x2.shape
    n = w.shape[1]
    return pl.pallas_call(
        _inproj_body,
        grid=(m // tm, n // tn),
        in_specs=[pl.BlockSpec((tm, d), lambda i, j: (i, 0)),
                  pl.BlockSpec((None, 1, d), lambda i, j: (layer, 0, 0)),
                  pl.BlockSpec((None, tn, d), lambda i, j: (layer, j, 0))],
        out_specs=pl.BlockSpec((tm, tn), lambda i, j: (i, j)),
        out_shape=jax.ShapeDtypeStruct((m, n), F32),
        scratch_shapes=[pltpu.VMEM((tm, d), BF16)],
        compiler_params=_cparams(("parallel", "arbitrary")),
        name="inproj",
    )(x2, nw, w)


def _relayout_body(w_ref, o_ref, *, segs):
    pack = 2 * SUBLANE
    pos = 0
    for src_lo, src_hi, dst_lo in segs:
        if dst_lo > pos:
            o_ref[pos:dst_lo, :] = jnp.zeros((dst_lo - pos, o_ref.shape[1]), BF16)
        val = w_ref[src_lo:src_hi, :]
        fill = -(src_hi - src_lo) % pack
        if fill:
            val = jnp.concatenate([val, jnp.zeros((fill, val.shape[1]), F32)], axis=0)
        o_ref[dst_lo:dst_lo + val.shape[0], :] = val.astype(BF16)
        pos = dst_lo + val.shape[0]
    if pos < o_ref.shape[0]:
        o_ref[pos:, :] = jnp.zeros((o_ref.shape[0] - pos, o_ref.shape[1]), BF16)


def _relayout_w_in(w_t, segs, total_pad, tk):
    depth, n, d = w_t.shape
    assert all(lo % SUBLANE == 0 and (hi - lo) % SUBLANE == 0 and dst % (2 * SUBLANE) == 0
               for lo, hi, dst in segs)
    return pl.pallas_call(
        functools.partial(_relayout_body, segs=segs),
        grid=(depth, d // tk),
        in_specs=[pl.BlockSpec((None, n, tk), lambda l, i: (l, 0, i))],
        out_specs=pl.BlockSpec((None, total_pad, tk), lambda l, i: (l, 0, i)),
        out_shape=jax.ShapeDtypeStruct((depth, total_pad, d), BF16),
        compiler_params=_cparams(("parallel", "parallel")),
        name="w_in_relayout",
    )(w_t)


def _outproj_body(ys5_ref, yssd_ref, yatt_ref, w_ref, x_ref, fnw_ref, o_ref, wb_ref, *, w5, wssd, final):
    @pl.when(pl.program_id(0) == 0)
    def _():
        step = 256
        for r in range(0, w_ref.shape[0], step):
            wb_ref[r:r + step, :] = w_ref[r:r + step, :].astype(BF16)

    acc = x_ref[...]
    acc += jnp.dot(ys5_ref[...], wb_ref[0:w5, :], preferred_element_type=F32)
    acc += jnp.dot(yssd_ref[...], wb_ref[w5:w5 + wssd, :], preferred_element_type=F32)
    acc += jnp.dot(yatt_ref[...], wb_ref[w5 + wssd:, :], preferred_element_type=F32)
    if final:
        ms = jnp.mean(acc * acc, axis=-1, keepdims=True)
        acc = acc * lax.rsqrt(ms + EPS) * fnw_ref[...]
    o_ref[...] = acc


def _outproj(ys5, yssd, yatt, w, layer, x2, fnw, tm, final):
    m, d = x2.shape
    w5, wssd, watt = ys5.shape[1], yssd.shape[1], yatt.shape[1]
    body = functools.partial(_outproj_body, w5=w5, wssd=wssd, final=final)
    return pl.pallas_call(
        body,
        grid=(m // tm,),
        in_specs=[pl.BlockSpec((tm, w5), lambda i: (i, 0)),
                  pl.BlockSpec((tm, wssd), lambda i: (i, 0)),
                  pl.BlockSpec((tm, watt), lambda i: (i, 0)),
                  pl.BlockSpec((None,) + w.shape[1:], lambda i: (layer, 0, 0), pipeline_mode=pl.Buffered(1)),
                  pl.BlockSpec((tm, d), lambda i: (i, 0)),
                  pl.BlockSpec((1, d), lambda i: (0, 0))],
        out_specs=pl.BlockSpec((tm, d), lambda i: (i, 0)),
        out_shape=jax.ShapeDtypeStruct((m, d), F32),
        scratch_shapes=[pltpu.VMEM(w.shape[1:], BF16)],
        compiler_params=_cparams(("arbitrary",)),
        name="outproj",
    )(ys5, yssd, yatt, w, x2, fnw)


def _shift_down_one(x):
    rolled = pltpu.roll(x, 1, axis=0)
    row = lax.broadcasted_iota(jnp.int32, x.shape, 0)
    return jnp.where(row == 0, 0.0, rolled)


def _s5_body(u_ref, z_ref, bblk_ref, cblk_ref, lam_ref, lamseg_ref, d_ref, gw_ref, gb_ref,
             o_ref, uperm_ref, xs_ref, yperm_ref, *, nchunk, cw, sw, rt):
    L = u_ref.shape[0]
    seg = L // SUBLANE

    for c in range(nchunk):
        for j in range(SUBLANE):
            for k in range(cw // LANE):
                lo = c * cw + k * LANE
                uperm_ref[k, pl.ds(j, seg, stride=SUBLANE), :] = u_ref[pl.ds(j * seg, seg), lo:lo + LANE]
        for r0 in range(0, L, rt):
            up = jnp.concatenate([uperm_ref[k, r0:r0 + rt, :] for k in range(cw // LANE)], axis=1)
            xs_ref[r0:r0 + rt, :] = jnp.dot(up.astype(BF16), bblk_ref[c], preferred_element_type=F32)
        lr = lam_ref[c, 0]
        li = lam_ref[c, 1]

        def scan_step(tau, carry):
            xr, xi = carry
            row = pl.multiple_of(tau * SUBLANE, SUBLANE)
            nxr = lr * xr - li * xi + xs_ref[pl.ds(row, SUBLANE), 0:sw]
            nxi = lr * xi + li * xr + xs_ref[pl.ds(row, SUBLANE), sw:2 * sw]
            xs_ref[pl.ds(row, SUBLANE), 0:sw] = nxr
            xs_ref[pl.ds(row, SUBLANE), sw:2 * sw] = nxi
            return nxr, nxi

        zero = jnp.zeros((SUBLANE, sw), F32)
        er, ei = lax.fori_loop(0, seg, scan_step, (zero, zero), unroll=8)

        sr = lamseg_ref[c, 0]
        si = lamseg_ref[c, 1]
        cr, ci = zero, zero
        for _ in range(SUBLANE - 1):
            tr = er + (sr * cr - si * ci)
            ti = ei + (sr * ci + si * cr)
            cr, ci = _shift_down_one(tr), _shift_down_one(ti)

        def fix_step(tau, carry):
            fr, fi = carry
            nfr = lr * fr - li * fi
            nfi = lr * fi + li * fr
            row = pl.multiple_of(tau * SUBLANE, SUBLANE)
            xs_ref[pl.ds(row, SUBLANE), 0:sw] += nfr
            xs_ref[pl.ds(row, SUBLANE), sw:2 * sw] += nfi
            return nfr, nfi

        lax.fori_loop(0, seg, fix_step, (cr, ci), unroll=8)

        for r0 in range(0, L, rt):
            yc = jnp.dot(xs_ref[r0:r0 + rt, :].astype(BF16), cblk_ref[c], preferred_element_type=F32)
            for k in range(cw // LANE):
                yperm_ref[c * (cw // LANE) + k, r0:r0 + rt, :] = yc[:, k * LANE:(k + 1) * LANE]

    for j in range(SUBLANE):
        y = jnp.concatenate([yperm_ref[k, pl.ds(j, seg, stride=SUBLANE), :]
                             for k in range(yperm_ref.shape[0])], axis=1)
        y = y + d_ref[...] * u_ref[j * seg:(j + 1) * seg, :]
        y = jax.nn.gelu(y)
        g = jnp.dot(y.astype(BF16), gw_ref[...], preferred_element_type=F32) + gb_ref[...]
        y = y * jax.nn.sigmoid(g)
        o_ref[j * seg:(j + 1) * seg, :] = (y * jax.nn.silu(z_ref[j * seg:(j + 1) * seg, :])).astype(BF16)


def _layer_spec(a, layer):
    return pl.BlockSpec((None,) + a.shape[1:], lambda *_: (layer,) + (0,) * (a.ndim - 1))


def _s5_mixer(proj, B, L, u_blk, z_blk, width, prm, layer):
    bblk, cblk, lam, lamseg, dvec, gw, gb = prm
    nchunk, cw, sw2 = bblk.shape[1:]
    sw = sw2 // 2
    rt = min(512, L)
    body = functools.partial(_s5_body, nchunk=nchunk, cw=cw, sw=sw, rt=rt)
    full = lambda a: _layer_spec(a, layer)
    return pl.pallas_call(
        body,
        grid=(B,),
        in_specs=[pl.BlockSpec((L, width), lambda b: (b, u_blk)),
                  pl.BlockSpec((L, width), lambda b: (b, z_blk)),
                  full(bblk), full(cblk), full(lam), full(lamseg), full(dvec), full(gw), full(gb)],
        out_specs=pl.BlockSpec((L, width), lambda b: (b, 0)),
        out_shape=jax.ShapeDtypeStruct((B * L, width), BF16),
        scratch_shapes=[pltpu.VMEM((cw // LANE, L, LANE), F32),
                        pltpu.VMEM((L, 2 * sw), F32),
                        pltpu.VMEM((width // LANE, L, LANE), F32)],
        compiler_params=_cparams(("parallel",)),
        name="s5_mixer",
    )(proj, proj, bblk, cblk, lam, lamseg, dvec, gw, gb)


def _cpow(re, im, n):
    rr, ri = jnp.ones_like(re), jnp.zeros_like(im)
    br, bi = re, im
    while n:
        if n & 1:
            rr, ri = rr * br - ri * bi, rr * bi + ri * br
        br, bi = br * br - bi * bi, 2.0 * br * bi
        n >>= 1
    return rr, ri


def _s5_params(A_re, A_im, log_dt, B_re, B_im, C_re, C_im, D, glu_w, glu_b, seg, nchunk):
    G, P, C = B_re.shape
    dt = jnp.exp(log_dt)[:, None]
    lre = jnp.minimum(A_re, -1e-4)
    lim = A_im
    mag = jnp.exp(lre * dt)
    lbr = mag * jnp.cos(lim * dt)
    lbi = mag * jnp.sin(lim * dt)
    nr, ni = lbr - 1.0, lbi
    den = lre * lre + lim * lim
    fr = (nr * lre + ni * lim) / den
    fi = (ni * lre - nr * lim) / den
    bbr = fr[..., None] * B_re - fi[..., None] * B_im
    bbi = fr[..., None] * B_im + fi[..., None] * B_re
    gc = G // nchunk
    eye = jnp.eye(gc, dtype=F32)

    def blockdiag_in(bb):
        t = jnp.transpose(bb, (0, 2, 1)).reshape(nchunk, gc, C, P)
        return jnp.einsum('ngcp,gh->ngchp', t, eye).reshape(nchunk, gc * C, gc * P)

    def blockdiag_out(cc):
        t = jnp.transpose(cc, (0, 2, 1)).reshape(nchunk, gc, P, C)
        return jnp.einsum('ngpc,gh->ngphc', t, eye).reshape(nchunk, gc * P, gc * C)

    bblk = jnp.concatenate([blockdiag_in(bbr), blockdiag_in(bbi)], axis=-1).astype(BF16)
    cblk = jnp.concatenate([blockdiag_out(C_re), blockdiag_out(-C_im)], axis=1).astype(BF16)

    def rows(v):
        return jnp.broadcast_to(v.reshape(nchunk, 1, gc * P), (nchunk, SUBLANE, gc * P))

    lam = jnp.stack([rows(lbr), rows(lbi)], axis=1)
    pr, pi = _cpow(lbr, lbi, seg)
    lamseg = jnp.stack([rows(pr), rows(pi)], axis=1)
    return (bblk, cblk, lam, lamseg, D.reshape(1, G * C), glu_w.astype(BF16), glu_b.reshape(1, -1))


def _split3(x):
    h = x.astype(BF16)
    r = x - h.astype(F32)
    m = r.astype(BF16)
    l = (r - m.astype(F32)).astype(BF16)
    return h, m, l


def _dot_f32(a, b3_bf16):
    return jnp.dot(jnp.concatenate(_split3(a), axis=1), b3_bf16, preferred_element_type=F32)


def _ssd_body(z_ref, xbc_ref, dt_ref, cw_ref, cb_ref, dtb_ref, a_ref, dx_ref, nw_ref, e_ref,
              o_ref, state_ref, xpad_ref, *, nheads, hd, ns, ngroups):
    Q = z_ref.shape[0]
    width = nheads * hd
    gw = width // ngroups
    halo = xpad_ref.shape[0] - Q

    @pl.when(pl.program_id(1) == 0)
    def _():
        state_ref[...] = jnp.zeros_like(state_ref)
        xpad_ref[0:halo, :] = jnp.zeros((halo, xpad_ref.shape[1]), F32)

    xpad_ref[halo:halo + Q, :] = xbc_ref[...]
    xp = xpad_ref[...]
    acc = cw_ref[0:1, :] * xp
    for k in range(1, SSD_CONV):
        acc = cw_ref[k:k + 1, :] * xp + pltpu.roll(acc, 1, axis=0)
    xpad_ref[0:halo, :] = xpad_ref[Q:Q + halo, :]
    xc = jax.nn.silu(acc[halo:, :] + cb_ref[...])
    xs = xc[:, 0:width]
    bm = xc[:, width:width + ngroups * ns].astype(BF16)
    cm = xc[:, width + ngroups * ns:].astype(BF16)

    lane = lax.broadcasted_iota(jnp.int32, (Q, LANE), 1)
    dt = jnp.where(lane < nheads, jax.nn.softplus(dt_ref[...] + dtb_ref[...]), 0.0)
    a = dt * a_ref[...]
    rowi = lax.broadcasted_iota(jnp.int32, (Q, Q), 0)
    coli = lax.broadcasted_iota(jnp.int32, (Q, Q), 1)
    tril = coli <= rowi
    acs = _dot_f32_lhs(tril.astype(BF16), a)
    acs_t = acs.T
    last = acs[Q - 1:Q, :]
    emat = e_ref[...]
    dt_x = _expand(dt, emat)
    dec_x = _expand(jnp.exp(acs), emat)
    dte_x = _expand(jnp.exp(last - acs), emat)
    dlast_x = dec_x[Q - 1:Q, :]

    xd = xs * dt_x
    xd_b = xd.astype(BF16)
    xw_b = (xd * dte_x).astype(BF16)
    st = state_ref[...]
    st_b = st.astype(BF16)

    y_parts = []
    new_state = []
    hlane = lax.broadcasted_iota(jnp.int32, (Q, LANE), 1) < hd
    heads_per_group = nheads // ngroups
    for g in range(ngroups):
        cg = cm[:, g * ns:(g + 1) * ns]
        bg = bm[:, g * ns:(g + 1) * ns]
        cb = lax.dot_general(cg, bg, (((1,), (1,)), ((), ())), preferred_element_type=F32)
        y_off = jnp.dot(cg, st_b[:, g * gw:(g + 1) * gw], preferred_element_type=F32)
        pair_out = []
        for pr in range(gw // LANE):
            xpair = xd_b[:, g * gw + pr * LANE: g * gw + (pr + 1) * LANE]
            res = []
            for hh in range(LANE // hd):
                h = g * heads_per_group + pr * (LANE // hd) + hh
                diff = acs[:, h:h + 1] - acs_t[h:h + 1, :]
                s_h = jnp.where(tril, cb * jnp.exp(jnp.minimum(diff, 0.0)), 0.0).astype(BF16)
                res.append(jnp.dot(s_h, xpair, preferred_element_type=F32))
            pair_out.append(jnp.where(hlane, res[0], res[1]))
        y_diag = jnp.concatenate(pair_out, axis=1)
        y_parts.append(y_diag + y_off * dec_x[:, g * gw:(g + 1) * gw])
        upd = lax.dot_general(bg, xw_b[:, g * gw:(g + 1) * gw], (((0,), (0,)), ((), ())),
                              preferred_element_type=F32)
        new_state.append(st[:, g * gw:(g + 1) * gw] * dlast_x[:, g * gw:(g + 1) * gw] + upd)
    state_ref[...] = jnp.concatenate(new_state, axis=1)

    y = jnp.concatenate(y_parts, axis=1) + xs * dx_ref[...]
    gt = y * jax.nn.silu(z_ref[...])
    ms = jnp.mean(gt * gt, axis=-1, keepdims=True)
    o_ref[...] = (gt * lax.rsqrt(ms + EPS) * nw_ref[...]).astype(BF16)


def _dot_f32_lhs(a_bf16, b):
    return jnp.dot(jnp.concatenate([a_bf16] * 3, axis=1), jnp.concatenate(_split3(b), axis=0),
                   preferred_element_type=F32)


def _expand(v, emat):
    return _dot_f32(v, emat)


def _ssd_mixer(proj, B, L, z_blk, xbc_blk, dt_blk, prm, emat, layer, Q):
    cw, cb, dtb, avec, dx, nw = prm
    width = nw.shape[-1]
    cdim = cw.shape[-1]
    nheads = width // SSD_HEAD_DIM
    nt = L // Q
    body = functools.partial(_ssd_body, nheads=nheads, hd=SSD_HEAD_DIM, ns=SSD_STATE, ngroups=SSD_GROUPS)
    full = lambda a: _layer_spec(a, layer)
    return pl.pallas_call(
        body,
        grid=(B, nt),
        in_specs=[pl.BlockSpec((Q, width), lambda b, t: (b * nt + t, z_blk)),
                  pl.BlockSpec((Q, cdim), lambda b, t: (b * nt + t, xbc_blk)),
                  pl.BlockSpec((Q, LANE), lambda b, t: (b * nt + t, dt_blk)),
                  full(cw), full(cb), full(dtb), full(avec), full(dx), full(nw),
                  pl.BlockSpec(emat.shape, lambda b, t: (0, 0))],
        out_specs=pl.BlockSpec((Q, width), lambda b, t: (b * nt + t, 0)),
        out_shape=jax.ShapeDtypeStruct((B * L, width), BF16),
        scratch_shapes=[pltpu.VMEM((SSD_STATE, width), F32),
                        pltpu.VMEM((Q + SUBLANE, cdim), F32)],
        compiler_params=_cparams(("parallel", "arbitrary")),
        name="ssd_mixer",
    )(proj, proj, proj, cw, cb, dtb, avec, dx, nw, emat)


def _ssd_params(conv_w, conv_b, dt_bias, A_log, D, norm_w):
    depth, nheads = dt_bias.shape
    width = nheads * SSD_HEAD_DIM
    pad = lambda v: jnp.pad(v, ((0, 0), (0, LANE - nheads))).reshape(depth, 1, LANE)
    return (conv_w, conv_b.reshape(depth, 1, -1), pad(dt_bias), pad(-jnp.exp(A_log)),
            jnp.repeat(D, SSD_HEAD_DIM, axis=1).reshape(depth, 1, width), norm_w.reshape(depth, 1, width))


def _head_expansion(nheads):
    e = np.zeros((LANE, nheads * SSD_HEAD_DIM), np.float32)
    e[np.repeat(np.arange(nheads), SSD_HEAD_DIM), np.arange(nheads * SSD_HEAD_DIM)] = 1.0
    return jnp.asarray(np.concatenate([e] * 3, axis=0), BF16)


def _float_key(x):
    b = lax.bitcast_convert_type(x, jnp.int32)
    return b ^ ((b >> 31) & 0x7FFFFFFF)


def _place(blk, lane0, width, dst):
    lane = lax.broadcasted_iota(jnp.int32, blk.shape, 1)
    shift = (dst - lane0) % LANE
    r = pltpu.roll(blk, shift, axis=1) if shift else blk
    lo = jnp.where((lane >= dst) & (lane < min(dst + width, LANE)), r, 0.0)
    hi = jnp.where(lane < dst + width - LANE, r, 0.0)
    return jnp.concatenate([lo, hi], axis=1)


def _head_offsets(offs, nheads, hd):
    return sorted({(offs[f] + h * hd) % LANE for f in ('q', 'qi') for h in range(nheads)})


def _lane_window(x, start, width):
    off = start % LANE
    base = start - off
    if off == 0:
        return x[:, start:start + width]
    lane = lax.broadcasted_iota(jnp.int32, (x.shape[0], LANE), 1)
    out = []
    for p in range(width // LANE):
        a = pltpu.roll(x[:, base + p * LANE:base + (p + 1) * LANE], LANE - off, axis=1)
        b = pltpu.roll(x[:, base + (p + 1) * LANE:base + (p + 2) * LANE], LANE - off, axis=1)
        out.append(jnp.where(lane < LANE - off, a, b))
    return jnp.concatenate(out, axis=1)


def _dsa_body(qx_ref, qxn_ref, kv_ref, ki_ref, band_ref, far_ref, o_ref,
              kpad_ref, kipad_ref, vt_ref, qop_ref, key2_ref, hi2_ref, lo2_ref, lom_ref, lg_ref, bound_ref,
              m_ref, acc_ref,
              *, nheads, hd, topk, offs):
    i = pl.program_id(1)
    cur = lax.rem(i, 2)
    key_ref, hi_ref, lo_ref = key2_ref.at[cur], hi2_ref.at[cur], lo2_ref.at[cur]
    TQ = qx_ref.shape[0]
    KB = TQ
    L = kv_ref.shape[0]
    nkb_total = L // KB
    NT = (((1,), (1,)), ((), ()))
    WIN = kpad_ref.shape[2]

    def head_window(field_lo, h):
        lo = field_lo + h * hd
        return (lo // LANE) * LANE, lo % LANE

    head_offs = _head_offsets(offs, nheads, hd)

    @pl.when(i == 0)
    def _():
        for kb in range(nkb_total):
            r = slice(kb * KB, (kb + 1) * KB)
            kgrp = (offs['k'] // LANE) * LANE
            kblk = kv_ref[r, kgrp:kgrp + LANE]
            kiblk = ki_ref[r, :]
            for n, o in enumerate(head_offs):
                kpad_ref[n, r, :] = _place(kblk, offs['k'] - kgrp, hd, o)[:, :WIN].astype(BF16)
                kipad_ref[n, r, :] = _place(kiblk, offs['ki'], hd, o)[:, :WIN].astype(BF16)
            vt_ref[0:hd, r] = kv_ref[r, :].T[offs['v']:offs['v'] + hd, :].astype(BF16)
            vt_ref[hd:, r] = jnp.ones((vt_ref.shape[0] - hd, KB), BF16)

    def stage_windows(slot, field_lo, scale, src_ref):
        bases = sorted({head_window(field_lo, h)[0] for h in range(nheads)})
        for n, base in enumerate(bases):
            qop_ref[slot, n] = (src_ref[:, base:base + WIN] * scale).astype(BF16)
        return [(bases.index(head_window(field_lo, h)[0]), head_offs.index(head_window(field_lo, h)[1]))
                for h in range(nheads)]

    qs = stage_windows(0, offs['q'], hd ** -0.5, qx_ref)

    def stage_indexer(src_ref):
        table = stage_windows(1, offs['qi'], 1.0, src_ref)
        wbase = (offs['w'] // LANE) * LANE
        wt = src_ref[:, wbase:wbase + LANE].T * ((nheads * IDX_DIM) ** -0.5)
        return table, [wt[offs['w'] - wbase + h:offs['w'] - wbase + h + 1, :] for h in range(nheads)]

    RB = LANE
    I16 = jnp.int16
    HALF = 1 << 15

    def rows(kb, r=0, n=KB):
        return pl.ds(pl.multiple_of(kb * KB + r, SUBLANE), n)

    PART = m_ref.shape[1]

    def fold_sum(x):
        return jnp.sum(x.reshape(x.shape[0] // PART, PART, TQ), axis=0)

    def idx_keys(kb, diagonal, slot, indexer):
        qis, w_rows = indexer
        for r in range(0, KB, RB):
            s = jnp.zeros((RB, TQ), F32)
            for h in range(nheads):
                win, ko = qis[h]
                lg = lax.dot_general(kipad_ref[ko, rows(kb, r, RB), :], qop_ref[1, win], NT,
                                     preferred_element_type=F32)
                s += jnp.maximum(lg, 0.0) * w_rows[h]
            key = _float_key(s)
            if diagonal:
                krow = lax.broadcasted_iota(jnp.int32, (RB, TQ), 0) + r
                qcol = lax.broadcasted_iota(jnp.int32, (RB, TQ), 1)
                key = jnp.where((krow // CHUNK) <= (qcol // CHUNK), key, INT_MIN)
            key2_ref[slot, rows(kb, r, RB), :] = key
            hi2_ref[slot, rows(kb, r, RB), :] = (key >> 16).astype(I16)
            lo2_ref[slot, rows(kb, r, RB), :] = ((key & (2 * HALF - 1)) - HALF).astype(I16)

    @pl.when(i == 0)
    def _():
        idx_keys(0, True, cur, stage_indexer(qx_ref))

    def count(preds):
        def body(kb, cs):
            blk = key_ref[rows(kb), :]
            return tuple(c + fold_sum(jnp.where(p(blk, kb), 1.0, 0.0)) for c, p in zip(cs, preds))
        z = jnp.zeros((PART, TQ), F32)
        cs = lax.fori_loop(0, i + 1, body, (z,) * len(preds))
        return [jnp.sum(c, axis=0, keepdims=True) for c in cs]

    PACK = 2 * SUBLANE
    P16 = 4 * PACK

    def rows16(x):
        return x.reshape(x.shape[0] // PACK, PACK, TQ)

    def as16(v):
        return jnp.broadcast_to(v, (PACK, TQ)).astype(I16)

    def count16(src_ref, pred, flags=False):
        def body(kb, c):
            blk = rows16(src_ref[rows(kb), :])
            ind = jnp.where(pred(blk, kb), blk if flags else I16(1), I16(0)).reshape(KB // P16, P16, TQ)
            for n in range(KB // P16):
                c = c + ind[n]
            return c
        c = lax.fori_loop(0, i + 1, body, jnp.zeros((P16, TQ), I16))
        c = rows16(c)
        c = (c[0] + c[1]) + (c[2] + c[3])
        return jnp.sum(c.astype(F32), axis=0, keepdims=True)

    def bisect16(src_ref, target):
        def step(s, tu):
            cand_u = tu | jnp.left_shift(jnp.int32(1), 15 - s)
            cand = as16(cand_u - HALF)[None]
            cnt = count16(src_ref, lambda blk, kb: blk >= cand)
            return jnp.where(cnt >= target, cand_u, tu)
        return lax.fori_loop(0, 16, step, jnp.zeros((1, TQ), jnp.int32))

    thr_hi = bisect16(hi_ref, float(topk)) - HALF
    thr_hi16 = as16(thr_hi)[None]
    need_lo = topk - count16(hi_ref, lambda blk, kb: blk > thr_hi16)

    def mask_low_halves(kb, _):
        keep_lo = rows16(hi_ref[rows(kb), :]) == thr_hi16
        lom_ref[rows(kb), :] = jnp.where(keep_lo, rows16(lo_ref[rows(kb), :]), I16(-HALF)).reshape(KB, TQ)
        return 0

    lax.fori_loop(0, i + 1, mask_low_halves, 0)
    thr_lo = bisect16(lom_ref, need_lo)
    thr = thr_hi * (2 * HALF) + thr_lo

    cgt, ceq = count([lambda blk, kb: blk > thr, lambda blk, kb: blk == thr])
    need = topk - cgt
    nbits = int(L).bit_length()
    bound_ref[...] = jnp.full((1, TQ), 2 ** nbits - 1, jnp.int32)

    @pl.when(jnp.max(ceq - need) > 0.0)
    def _():
        thr_lo16 = as16(thr_lo - HALF)[None]

        def flag_ties(kb, _):
            tie = ((rows16(hi_ref[rows(kb), :]) == thr_hi16) & (rows16(lo_ref[rows(kb), :]) == thr_lo16))
            lom_ref[rows(kb), :] = jnp.where(tie, I16(1), I16(0)).reshape(KB, TQ)
            return 0

        lax.fori_loop(0, i + 1, flag_ties, 0)
        krow16 = rows16(lax.broadcasted_iota(jnp.int32, (KB, TQ), 0).astype(I16))

        def bisect_idx(step, bnd):
            bit = jnp.left_shift(jnp.int32(1), nbits - 1 - step)
            cand = bnd | bit
            cnt = count16(lom_ref, lambda blk, kb: krow16 < as16(cand - kb * KB)[None], flags=True)
            return jnp.where(cnt <= need, cand, bnd)

        bound_ref[...] = lax.fori_loop(0, nbits, bisect_idx, jnp.zeros((1, TQ), jnp.int32))

    bound = bound_ref[...]

    m_ref[...] = jnp.full(m_ref.shape, NEG, F32)

    def fold_max(x):
        return jnp.max(x.reshape(x.shape[0] // PART, PART, TQ), axis=0)

    def store_logits(kb, bias_of_head):
        for r in range(0, KB, RB):
            blk = key_ref[rows(kb, r, RB), :]
            pos = lax.broadcasted_iota(jnp.int32, (RB, TQ), 0) + (kb * KB + r)
            keep = ((blk > thr) | ((blk == thr) & (pos < bound))) & (blk != INT_MIN)
            mask = jnp.where(keep, 0.0, NEG)
            for h in range(nheads):
                win, ko = qs[h]
                lg = lax.dot_general(kpad_ref[ko, rows(kb, r, RB), :], qop_ref[0, win], NT,
                                     preferred_element_type=F32)
                lg = lg + bias_of_head(h, r) + mask
                lg_ref[h, rows(kb, r, RB), :] = lg
                m_ref[h] = jnp.maximum(m_ref[h], fold_max(lg))

    def far_logits(kb, _):
        store_logits(kb, lambda h, r: far_ref[h])
        return 0

    def far_logits2(j, _):
        store_logits(2 * j, lambda h, r: far_ref[h])
        store_logits(2 * j + 1, lambda h, r: far_ref[h])
        return 0

    nfar = jnp.maximum(i - 1, 0)
    lax.fori_loop(0, lax.shift_right_logical(nfar, 1), far_logits2, 0)

    @pl.when((nfar & 1) == 1)
    def _():
        far_logits(nfar - 1, 0)

    @pl.when(i >= 1)
    def _():
        store_logits(i - 1, lambda h, r: band_ref[h, r:r + RB, :])

    store_logits(i, lambda h, r: band_ref[h, KB + r:KB + r + RB, :])

    m_fin = [jnp.max(m_ref[h], axis=0, keepdims=True) for h in range(nheads)]
    acc_ref[...] = jnp.zeros(acc_ref.shape, F32)

    def accumulate(kb):
        vt = vt_ref[:, rows(kb)]
        for h in range(nheads):
            p = jnp.exp(lg_ref[h, rows(kb), :] - m_fin[h])
            acc_ref[h] += jnp.dot(vt, p.astype(BF16), preferred_element_type=F32)

    has_next = i + 1 < pl.num_programs(1)

    @pl.when(has_next)
    def _():
        indexer = stage_indexer(qxn_ref)

        def both(kb, _):
            accumulate(kb)
            idx_keys(kb, False, 1 - cur, indexer)
            return 0

        lax.fori_loop(0, i + 1, both, 0)
        idx_keys(i + 1, True, 1 - cur, indexer)

    @pl.when(jnp.logical_not(has_next))
    def _():
        def only(kb, _):
            accumulate(kb)
            return 0

        lax.fori_loop(0, i + 1, only, 0)

    outs = [acc_ref[h, 0:hd, :] / acc_ref[h, hd:hd + 1, :] for h in range(nheads)]
    out = jnp.concatenate(outs, axis=0).T
    z = _lane_window(qx_ref, offs['z'], nheads * hd)
    o_ref[...] = (out * jax.nn.silu(z)).astype(BF16)


def _dsa_mixer(proj, B, L, src, band, far, nheads, TQ):
    width = nheads * ATT_HEAD_DIM
    nq = L // TQ
    hd = ATT_HEAD_DIM
    qx_w = 16 * LANE
    qx_blk = src['att_q'][0] // qx_w
    qx_lo = qx_blk * qx_w
    kv_w = 2 * LANE
    kv_blk = src['att_k'][0] // kv_w
    ki_blk = src['idx_k'][0] // LANE
    offs = {'q': src['att_q'][0] - qx_lo, 'qi': src['idx_q'][0] - qx_lo, 'w': src['idx_w'][0] - qx_lo,
            'z': src['att_z'][0] - qx_lo, 'k': src['att_k'][0] - kv_blk * kv_w,
            'v': src['att_v'][0] - kv_blk * kv_w, 'ki': src['idx_k'][0] - ki_blk * LANE}
    assert offs['z'] + width + LANE <= qx_w and qx_lo + qx_w <= proj.shape[1]
    assert offs['k'] % LANE + hd <= LANE and offs['ki'] + hd <= LANE
    assert offs['v'] % SUBLANE == 0 and offs['v'] + hd <= kv_w
    assert offs['w'] // LANE == (offs['w'] + nheads - 1) // LANE
    win_off = _head_offsets(offs, nheads, hd)
    win = LANE if max(win_off) + hd <= LANE else 2 * LANE
    win_base = max((sorted({(offs[f] + h * hd) // LANE for h in range(nheads)}) for f in ('q', 'qi')), key=len)
    body = functools.partial(_dsa_body, nheads=nheads, hd=hd, topk=min(TOPK, L // 4), offs=offs)
    full = lambda a: pl.BlockSpec(a.shape, lambda b, i: (0,) * a.ndim)
    return pl.pallas_call(
        body,
        grid=(B, nq),
        in_specs=[pl.BlockSpec((TQ, qx_w), lambda b, i: (b * nq + i, qx_blk)),
                  pl.BlockSpec((TQ, qx_w), lambda b, i: (b * nq + jnp.minimum(i + 1, nq - 1), qx_blk)),
                  pl.BlockSpec((L, kv_w), lambda b, i: (b, kv_blk)),
                  pl.BlockSpec((L, LANE), lambda b, i: (b, ki_blk)),
                  full(band), full(far)],
        out_specs=pl.BlockSpec((TQ, width), lambda b, i: (b * nq + i, 0)),
        out_shape=jax.ShapeDtypeStruct((B * L, width), BF16),
        scratch_shapes=[pltpu.VMEM((len(win_off), L, win), BF16),
                        pltpu.VMEM((len(win_off), L, win), BF16),
                        pltpu.VMEM((ATT_HEAD_DIM + 2 * SUBLANE, L), BF16),
                        pltpu.VMEM((2, len(win_base), TQ, win), BF16),
                        pltpu.VMEM((2, L, TQ), jnp.int32),
                        pltpu.VMEM((2, L, TQ), jnp.int16),
                        pltpu.VMEM((2, L, TQ), jnp.int16),
                        pltpu.VMEM((L, TQ), jnp.int16),
                        pltpu.VMEM((nheads, L, TQ), F32),
                        pltpu.VMEM((1, TQ), jnp.int32),
                        pltpu.VMEM((nheads, 4 * SUBLANE, TQ), F32),
                        pltpu.VMEM((nheads, ATT_HEAD_DIM + 2 * SUBLANE, TQ), F32)],
        compiler_params=_cparams(("parallel", "arbitrary")),
        name="dsa_mixer",
    )(proj, proj, proj, proj, band, far)


def _t5_bucket_static(rel):
    nb = N_BUCKETS // 2
    max_exact = nb // 2
    ret = np.where(rel > 0, nb, 0)
    n = np.abs(rel)
    nf = np.maximum(n, 1).astype(np.float64)
    large = max_exact + (np.log(nf / max_exact) / math.log(MAX_DISTANCE / max_exact)
                         * (nb - max_exact)).astype(np.int32)
    large = np.minimum(large, nb - 1)
    return ret + np.where(n < max_exact, n, large)


def _dsa_bias_tables(rel_bias, TQ):
    a = np.arange(2 * TQ)[:, None]
    j = np.arange(TQ)[None, :]
    band_idx = _t5_bucket_static(a - TQ - j)
    idx = jnp.asarray(band_idx, jnp.int32)[None]
    band = jnp.zeros((rel_bias.shape[1],) + band_idx.shape, F32)
    for bucket in np.unique(band_idx):
        band = jnp.where(idx == int(bucket), rel_bias[int(bucket)][:, None, None], band)
    far_bucket = int(_t5_bucket_static(np.array([-(TQ + 1)]))[0])
    assert far_bucket == int(_t5_bucket_static(np.array([-(10 ** 6)]))[0])
    far = jnp.broadcast_to(rel_bias[far_bucket][:, None, None], (rel_bias.shape[1], 1, TQ))
    return band, far


def _layout(d_model):
    s5w = d_model // 4
    ssdw = d_model // 2
    attw = d_model // 4
    cdim = ssdw + 2 * SSD_GROUPS * SSD_STATE
    nh_ssd = ssdw // SSD_HEAD_DIM
    nh_att = attw // ATT_HEAD_DIM
    splits = (s5w, s5w, ssdw, cdim, nh_ssd, attw, ATT_HEAD_DIM, ATT_HEAD_DIM,
              nh_att * IDX_DIM, IDX_DIM, nh_att, attw)
    names = ('s5_u', 's5_z', 'ssd_z', 'ssd_xbc', 'ssd_dt', 'att_q', 'att_k', 'att_v',
             'idx_q', 'idx_k', 'idx_w', 'att_z')
    src, o = {}, 0
    for nme, s in zip(names, splits):
        src[nme] = (o, s)
        o += s
    return src, o


def kernel(x, norm_w, w_in, s5_A_re, s5_A_im, s5_log_dt, s5_B_re, s5_B_im, s5_C_re, s5_C_im, s5_D, s5_glu_w, s5_glu_b, ssd_conv_w, ssd_conv_b, ssd_dt_bias, ssd_A_log, ssd_D, ssd_norm_w, rel_bias, w_out, final_norm_w):
    B, L, d = x.shape
    depth = w_in.shape[0]
    src, total = _layout(d)
    s5w, ssdw, attw = d // 4, d // 2, d // 4
    cdim = ssdw + 2 * SSD_GROUPS * SSD_STATE
    nh_att = attw // ATT_HEAD_DIM
    assert src['s5_u'][0] % s5w == 0 and src['s5_z'][0] % s5w == 0 and src['ssd_z'][0] % ssdw == 0
    assert src['ssd_xbc'][0] % cdim == 0 and src['ssd_dt'][0] % LANE == 0

    tn = 1536
    cuts, shift, padded = [(0, 0)], 0, {}
    for nme, (so, sz) in sorted(src.items(), key=lambda kv: kv[1][0]):
        gap = -(so + shift) % LANE if nme in ('att_q', 'att_k', 'idx_q', 'idx_k', 'att_z') else 0
        if gap:
            shift += gap
            cuts.append((so, shift))
        padded[nme] = (so + shift, sz)
    src = padded
    total_pad = -(-(total + shift) // tn) * tn
    segs = tuple((lo, cuts[n + 1][0] if n + 1 < len(cuts) else total, lo + sh) for n, (lo, sh) in enumerate(cuts))
    w_in_p = _relayout_w_in(jnp.swapaxes(w_in, 1, 2), segs, total_pad, min(256, d))

    TQ = 256
    Q = 256
    band, far = _dsa_bias_tables(rel_bias, TQ)
    tm_in = min(1024, B * L)
    tm_out = min(512, B * L)
    nchunk = 2
    seg = L // SUBLANE

    s5p = jax.vmap(lambda *a: _s5_params(*a, seg, nchunk))(
        s5_A_re, s5_A_im, s5_log_dt, s5_B_re, s5_B_im, s5_C_re, s5_C_im, s5_D, s5_glu_w, s5_glu_b)
    ssdp = _ssd_params(ssd_conv_w, ssd_conv_b, ssd_dt_bias, ssd_A_log, ssd_D, ssd_norm_w)
    emat = _head_expansion(ssdw // SSD_HEAD_DIM)
    norm_w3 = norm_w.reshape(depth, 1, d)

    x2 = x.reshape(B * L, d)
    for l in range(depth):
        proj = _inproj(x2, norm_w3, w_in_p, l, tm_in, tn)
        y_s5 = _s5_mixer(proj, B, L, src['s5_u'][0] // s5w, src['s5_z'][0] // s5w, s5w, s5p, l)
        y_ssd = _ssd_mixer(proj, B, L, src['ssd_z'][0] // ssdw, src['ssd_xbc'][0] // cdim,
                           src['ssd_dt'][0] // LANE, ssdp, emat, l, Q)
        y_att = _dsa_mixer(proj, B, L, src, band, far, nh_att, TQ)
        x2 = _outproj(y_s5, y_ssd, y_att, w_out, l, x2, final_norm_w.reshape(1, d), tm_out,
                      final=(l == depth - 1))
    return x2.reshape(B, L, d)
```

```python
import functools
import math

import numpy as np
import jax
import jax.numpy as jnp
from jax import lax
from jax.experimental import pallas as pl
from jax.experimental.pallas import tpu as pltpu

F32 = jnp.float32
BF16 = jnp.bfloat16

EPS = 1e-6
CHUNK = 64

SSD_HEAD_DIM = 64
SSD_STATE = 128
SSD_GROUPS = 4
SSD_CONV = 4
ATT_HEAD_DIM = 64
IDX_DIM = 64
TOPK = 256
N_BUCKETS = 32
MAX_DISTANCE = 128

LANE = 128
SUBLANE = 8
VMEM_LIMIT = 56 * 1024 * 1024

INT_MIN = -2 ** 31
NEG = -1e30


def _cparams(sem):
    return pltpu.CompilerParams(dimension_semantics=sem, vmem_limit_bytes=VMEM_LIMIT)


def _inproj_body(x_ref, nw_ref, w_ref, o_ref, h_ref):
    @pl.when(pl.program_id(1) == 0)
    def _():
        xf = x_ref[...]
        ms = jnp.mean(xf * xf, axis=-1, keepdims=True)
        h_ref[...] = (xf * lax.rsqrt(ms + EPS) * nw_ref[...]).astype(BF16)

    o_ref[...] = lax.dot_general(h_ref[...], w_ref[...], (((1,), (1,)), ((), ())),
                                 preferred_element_type=F32)


def _inproj_prenormed_body(h_ref, w_ref, o_ref):
    o_ref[...] = lax.dot_general(h_ref[...], w_ref[...], (((1,), (1,)), ((), ())),
                                 preferred_element_type=F32)


def _inproj_prenormed(h, w, layer, tm, tn):
    m, d = h.shape
    n = w.shape[1]
    return pl.pallas_call(
        _inproj_prenormed_body,
        grid=(m // tm, n // tn),
        in_specs=[pl.BlockSpec((tm, d), lambda i, j: (i, 0)),
                  pl.BlockSpec((None, tn, d), lambda i, j: (layer, j, 0))],
        out_specs=pl.BlockSpec((tm, tn), lambda i, j: (i, j)),
        out_shape=jax.ShapeDtypeStruct((m, n), F32),
        compiler_params=_cparams(("parallel", "parallel")),
        name="inproj_prenormed",
    )(h, w)


def _inproj(x2, nw, w, layer, tm, tn):
    m, d = x2.shape
    n = w.shape[1]
    return pl.pallas_call(
        _inproj_body,
        grid=(m // tm, n // tn),
        in_specs=[pl.BlockSpec((tm, d), lambda i, j: (i, 0)),
                  pl.BlockSpec((None, 1, d), lambda i, j: (layer, 0, 0)),
                  pl.BlockSpec((None, tn, d), lambda i, j: (layer, j, 0))],
        out_specs=pl.BlockSpec((tm, tn), lambda i, j: (i, j)),
        out_shape=jax.ShapeDtypeStruct((m, n), F32),
        scratch_shapes=[pltpu.VMEM((tm, d), BF16)],
        compiler_params=_cparams(("parallel", "arbitrary")),
        name="inproj",
    )(x2, nw, w)


def _relayout_body(w_ref, o_ref, *, segs):
    pack = 2 * SUBLANE
    pos = 0
    for src_lo, src_hi, dst_lo in segs:
        if dst_lo > pos:
            o_ref[pos:dst_lo, :] = jnp.zeros((dst_lo - pos, o_ref.shape[1]), BF16)
        val = w_ref[src_lo:src_hi, :]
        fill = -(src_hi - src_lo) % pack
        if fill:
            val = jnp.concatenate([val, jnp.zeros((fill, val.shape[1]), F32)], axis=0)
        o_ref[dst_lo:dst_lo + val.shape[0], :] = val.astype(BF16)
        pos = dst_lo + val.shape[0]
    if pos < o_ref.shape[0]:
        o_ref[pos:, :] = jnp.zeros((o_ref.shape[0] - pos, o_ref.shape[1]), BF16)


def _relayout_w_in(w_t, segs, total_pad, tk):
    depth, n, d = w_t.shape
    assert all(lo % SUBLANE == 0 and (hi - lo) % SUBLANE == 0 and dst % (2 * SUBLANE) == 0
               for lo, hi, dst in segs)
    return pl.pallas_call(
        functools.partial(_relayout_body, segs=segs),
        grid=(depth, d // tk),
        in_specs=[pl.BlockSpec((None, n, tk), lambda l, i: (l, 0, i))],
        out_specs=pl.BlockSpec((None, total_pad, tk), lambda l, i: (l, 0, i)),
        out_shape=jax.ShapeDtypeStruct((depth, total_pad, d), BF16),
        compiler_params=_cparams(("parallel", "parallel")),
        name="w_in_relayout",
    )(w_t)


def _outproj_body(ys5_ref, yssd_ref, yatt_ref, w_ref, x_ref, fnw_ref, o_ref, *rest, w5, wssd, final):
    h_ref, wb_ref = (None, rest[0]) if final else rest
    @pl.when(pl.program_id(0) == 0)
    def _():
        step = 256
        for r in range(0, w_ref.shape[0], step):
            wb_ref[r:r + step, :] = w_ref[r:r + step, :].astype(BF16)

    acc = x_ref[...]
    acc += jnp.dot(ys5_ref[...], wb_ref[0:w5, :], preferred_element_type=F32)
    acc += jnp.dot(yssd_ref[...], wb_ref[w5:w5 + wssd, :], preferred_element_type=F32)
    acc += jnp.dot(yatt_ref[...], wb_ref[w5 + wssd:, :], preferred_element_type=F32)
    ms = jnp.mean(acc * acc, axis=-1, keepdims=True)
    normed = acc * lax.rsqrt(ms + EPS) * fnw_ref[...]
    if final:
        o_ref[...] = normed
    else:
        o_ref[...] = acc
        h_ref[...] = normed.astype(BF16)


def _outproj(ys5, yssd, yatt, w, layer, x2, fnw, tm, final):
    m, d = x2.shape
    w5, wssd, watt = ys5.shape[1], yssd.shape[1], yatt.shape[1]
    body = functools.partial(_outproj_body, w5=w5, wssd=wssd, final=final)
    row_spec = pl.BlockSpec((tm, d), lambda i: (i, 0))
    if final:
        out_specs, out_shape = row_spec, jax.ShapeDtypeStruct((m, d), F32)
    else:
        out_specs = (row_spec, row_spec)
        out_shape = (jax.ShapeDtypeStruct((m, d), F32), jax.ShapeDtypeStruct((m, d), BF16))
    return pl.pallas_call(
        body,
        grid=(m // tm,),
        in_specs=[pl.BlockSpec((tm, w5), lambda i: (i, 0)),
                  pl.BlockSpec((tm, wssd), lambda i: (i, 0)),
                  pl.BlockSpec((tm, watt), lambda i: (i, 0)),
                  pl.BlockSpec((None,) + w.shape[1:], lambda i: (layer, 0, 0), pipeline_mode=pl.Buffered(1)),
                  pl.BlockSpec((tm, d), lambda i: (i, 0)),
                  pl.BlockSpec((1, d), lambda i: (0, 0))],
        out_specs=out_specs,
        out_shape=out_shape,
        scratch_shapes=[pltpu.VMEM(w.shape[1:], BF16)],
        compiler_params=_cparams(("arbitrary",)),
        name="outproj",
    )(ys5, yssd, yatt, w, x2, fnw)


def _shift_down_one(x):
    rolled = pltpu.roll(x, 1, axis=0)
    row = lax.broadcasted_iota(jnp.int32, x.shape, 0)
    return jnp.where(row == 0, 0.0, rolled)


def _s5_body(u_ref, z_ref, bblk_ref, cblk_ref, lam_ref, lamseg_ref, d_ref, gw_ref, gb_ref,
             o_ref, uperm_ref, xs_ref, yperm_ref, *, nchunk, cw, sw, rt):
    L = u_ref.shape[0]
    seg = L // SUBLANE

    for c in range(nchunk):
        for j in range(SUBLANE):
            for k in range(cw // LANE):
                lo = c * cw + k * LANE
                uperm_ref[k, pl.ds(j, seg, stride=SUBLANE), :] = u_ref[pl.ds(j * seg, seg), lo:lo + LANE]
        for r0 in range(0, L, rt):
            up = jnp.concatenate([uperm_ref[k, r0:r0 + rt, :] for k in range(cw // LANE)], axis=1)
            xs_ref[r0:r0 + rt, :] = jnp.dot(up.astype(BF16), bblk_ref[c], preferred_element_type=F32)
        lr = lam_ref[c, 0]
        li = lam_ref[c, 1]

        def scan_step(tau, carry):
            xr, xi = carry
            row = pl.multiple_of(tau * SUBLANE, SUBLANE)
            nxr = lr * xr - li * xi + xs_ref[pl.ds(row, SUBLANE), 0:sw]
            nxi = lr * xi + li * xr + xs_ref[pl.ds(row, SUBLANE), sw:2 * sw]
            xs_ref[pl.ds(row, SUBLANE), 0:sw] = nxr
            xs_ref[pl.ds(row, SUBLANE), sw:2 * sw] = nxi
            return nxr, nxi

        zero = jnp.zeros((SUBLANE, sw), F32)
        er, ei = lax.fori_loop(0, seg, scan_step, (zero, zero), unroll=8)

        sr = lamseg_ref[c, 0]
        si = lamseg_ref[c, 1]
        cr, ci = zero, zero
        for _ in range(SUBLANE - 1):
            tr = er + (sr * cr - si * ci)
            ti = ei + (sr * ci + si * cr)
            cr, ci = _shift_down_one(tr), _shift_down_one(ti)

        def fix_step(tau, carry):
            fr, fi = carry
            nfr = lr * fr - li * fi
            nfi = lr * fi + li * fr
            row = pl.multiple_of(tau * SUBLANE, SUBLANE)
            xs_ref[pl.ds(row, SUBLANE), 0:sw] += nfr
            xs_ref[pl.ds(row, SUBLANE), sw:2 * sw] += nfi
            return nfr, nfi

        lax.fori_loop(0, seg, fix_step, (cr, ci), unroll=8)

        for r0 in range(0, L, rt):
            yc = jnp.dot(xs_ref[r0:r0 + rt, :].astype(BF16), cblk_ref[c], preferred_element_type=F32)
            for k in range(cw // LANE):
                yperm_ref[c * (cw // LANE) + k, r0:r0 + rt, :] = yc[:, k * LANE:(k + 1) * LANE]

    for j in range(SUBLANE):
        y = jnp.concatenate([yperm_ref[k, pl.ds(j, seg, stride=SUBLANE), :]
                             for k in range(yperm_ref.shape[0])], axis=1)
        y = y + d_ref[...] * u_ref[j * seg:(j + 1) * seg, :]
        y = jax.nn.gelu(y)
        g = jnp.dot(y.astype(BF16), gw_ref[...], preferred_element_type=F32) + gb_ref[...]
        y = y * jax.nn.sigmoid(g)
        o_ref[j * seg:(j + 1) * seg, :] = (y * jax.nn.silu(z_ref[j * seg:(j + 1) * seg, :])).astype(BF16)


def _layer_spec(a, layer):
    return pl.BlockSpec((None,) + a.shape[1:], lambda *_: (layer,) + (0,) * (a.ndim - 1))


def _s5_mixer(proj, B, L, u_blk, z_blk, width, prm, layer):
    bblk, cblk, lam, lamseg, dvec, gw, gb = prm
    nchunk, cw, sw2 = bblk.shape[1:]
    sw = sw2 // 2
    rt = min(512, L)
    body = functools.partial(_s5_body, nchunk=nchunk, cw=cw, sw=sw, rt=rt)
    full = lambda a: _layer_spec(a, layer)
    return pl.pallas_call(
        body,
        grid=(B,),
        in_specs=[pl.BlockSpec((L, width), lambda b: (b, u_blk)),
                  pl.BlockSpec((L, width), lambda b: (b, z_blk)),
                  full(bblk), full(cblk), full(lam), full(lamseg), full(dvec), full(gw), full(gb)],
        out_specs=pl.BlockSpec((L, width), lambda b: (b, 0)),
        out_shape=jax.ShapeDtypeStruct((B * L, width), BF16),
        scratch_shapes=[pltpu.VMEM((cw // LANE, L, LANE), F32),
                        pltpu.VMEM((L, 2 * sw), F32),
                        pltpu.VMEM((width // LANE, L, LANE), F32)],
        compiler_params=_cparams(("parallel",)),
        name="s5_mixer",
    )(proj, proj, bblk, cblk, lam, lamseg, dvec, gw, gb)


def _cpow(re, im, n):
    rr, ri = jnp.ones_like(re), jnp.zeros_like(im)
    br, bi = re, im
    while n:
        if n & 1:
            rr, ri = rr * br - ri * bi, rr * bi + ri * br
        br, bi = br * br - bi * bi, 2.0 * br * bi
        n >>= 1
    return rr, ri


def _s5_params(A_re, A_im, log_dt, B_re, B_im, C_re, C_im, D, glu_w, glu_b, seg, nchunk):
    G, P, C = B_re.shape
    dt = jnp.exp(log_dt)[:, None]
    lre = jnp.minimum(A_re, -1e-4)
    lim = A_im
    mag = jnp.exp(lre * dt)
    lbr = mag * jnp.cos(lim * dt)
    lbi = mag * jnp.sin(lim * dt)
    nr, ni = lbr - 1.0, lbi
    den = lre * lre + lim * lim
    fr = (nr * lre + ni * lim) / den
    fi = (ni * lre - nr * lim) / den
    bbr = fr[..., None] * B_re - fi[..., None] * B_im
    bbi = fr[..., None] * B_im + fi[..., None] * B_re
    gc = G // nchunk
    eye = jnp.eye(gc, dtype=F32)

    def blockdiag_in(bb):
        t = jnp.transpose(bb, (0, 2, 1)).reshape(nchunk, gc, C, P)
        return jnp.einsum('ngcp,gh->ngchp', t, eye).reshape(nchunk, gc * C, gc * P)

    def blockdiag_out(cc):
        t = jnp.transpose(cc, (0, 2, 1)).reshape(nchunk, gc, P, C)
        return jnp.einsum('ngpc,gh->ngphc', t, eye).reshape(nchunk, gc * P, gc * C)

    bblk = jnp.concatenate([blockdiag_in(bbr), blockdiag_in(bbi)], axis=-1).astype(BF16)
    cblk = jnp.concatenate([blockdiag_out(C_re), blockdiag_out(-C_im)], axis=1).astype(BF16)

    def rows(v):
        return jnp.broadcast_to(v.reshape(nchunk, 1, gc * P), (nchunk, SUBLANE, gc * P))

    lam = jnp.stack([rows(lbr), rows(lbi)], axis=1)
    pr, pi = _cpow(lbr, lbi, seg)
    lamseg = jnp.stack([rows(pr), rows(pi)], axis=1)
    return (bblk, cblk, lam, lamseg, D.reshape(1, G * C), glu_w.astype(BF16), glu_b.reshape(1, -1))


def _split3(x):
    h = x.astype(BF16)
    r = x - h.astype(F32)
    m = r.astype(BF16)
    l = (r - m.astype(F32)).astype(BF16)
    return h, m, l


def _dot_f32(a, b3_bf16):
    return jnp.dot(jnp.concatenate(_split3(a), axis=1), b3_bf16, preferred_element_type=F32)


def _ssd_body(z_ref, xbc_ref, dt_ref, cw_ref, cb_ref, dtb_ref, a_ref, dx_ref, nw_ref, e_ref,
              o_ref, state_ref, xpad_ref, *, nheads, hd, ns, ngroups):
    Q = z_ref.shape[0]
    width = nheads * hd
    gw = width // ngroups
    halo = xpad_ref.shape[0] - Q

    @pl.when(pl.program_id(1) == 0)
    def _():
        state_ref[...] = jnp.zeros_like(state_ref)
        xpad_ref[0:halo, :] = jnp.zeros((halo, xpad_ref.shape[1]), F32)

    xpad_ref[halo:halo + Q, :] = xbc_ref[...]
    xp = xpad_ref[...]
    acc = cw_ref[0:1, :] * xp
    for k in range(1, SSD_CONV):
        acc = cw_ref[k:k + 1, :] * xp + pltpu.roll(acc, 1, axis=0)
    xpad_ref[0:halo, :] = xpad_ref[Q:Q + halo, :]
    xc = jax.nn.silu(acc[halo:, :] + cb_ref[...])
    xs = xc[:, 0:width]
    bm = xc[:, width:width + ngroups * ns].astype(BF16)
    cm = xc[:, width + ngroups * ns:].astype(BF16)

    lane = lax.broadcasted_iota(jnp.int32, (Q, LANE), 1)
    dt = jnp.where(lane < nheads, jax.nn.softplus(dt_ref[...] + dtb_ref[...]), 0.0)
    a = dt * a_ref[...]
    rowi = lax.broadcasted_iota(jnp.int32, (Q, Q), 0)
    coli = lax.broadcasted_iota(jnp.int32, (Q, Q), 1)
    tril = coli <= rowi
    acs = _dot_f32_lhs(tril.astype(BF16), a)
    acs_t = acs.T
    last = acs[Q - 1:Q, :]
    emat = e_ref[...]
    dt_x = _expand(dt, emat)
    dec_x = _expand(jnp.exp(acs), emat)
    dte_x = _expand(jnp.exp(last - acs), emat)
    dlast_x = dec_x[Q - 1:Q, :]

    xd = xs * dt_x
    xd_b = xd.astype(BF16)
    xw_b = (xd * dte_x).astype(BF16)
    st = state_ref[...]
    st_b = st.astype(BF16)

    y_parts = []
    new_state = []
    hlane = lax.broadcasted_iota(jnp.int32, (Q, LANE), 1) < hd
    heads_per_group = nheads // ngroups
    for g in range(ngroups):
        cg = cm[:, g * ns:(g + 1) * ns]
        bg = bm[:, g * ns:(g + 1) * ns]
        cb = lax.dot_general(cg, bg, (((1,), (1,)), ((), ())), preferred_element_type=F32)
        y_off = jnp.dot(cg, st_b[:, g * gw:(g + 1) * gw], preferred_element_type=F32)
        pair_out = []
        for pr in range(gw // LANE):
            xpair = xd_b[:, g * gw + pr * LANE: g * gw + (pr + 1) * LANE]
            res = []
            for hh in range(LANE // hd):
                h = g * heads_per_group + pr * (LANE // hd) + hh
                diff = acs[:, h:h + 1] - acs_t[h:h + 1, :]
                s_h = jnp.where(tril, cb * jnp.exp(jnp.minimum(diff, 0.0)), 0.0).astype(BF16)
                res.append(jnp.dot(s_h, xpair, preferred_element_type=F32))
            pair_out.append(jnp.where(hlane, res[0], res[1]))
        y_diag = jnp.concatenate(pair_out, axis=1)
        y_parts.append(y_diag + y_off * dec_x[:, g * gw:(g + 1) * gw])
        upd = lax.dot_general(bg, xw_b[:, g * gw:(g + 1) * gw], (((0,), (0,)), ((), ())),
                              preferred_element_type=F32)
        new_state.append(st[:, g * gw:(g + 1) * gw] * dlast_x[:, g * gw:(g + 1) * gw] + upd)
    state_ref[...] = jnp.concatenate(new_state, axis=1)

    y = jnp.concatenate(y_parts, axis=1) + xs * dx_ref[...]
    gt = y * jax.nn.silu(z_ref[...])
    ms = jnp.mean(gt * gt, axis=-1, keepdims=True)
    o_ref[...] = (gt * lax.rsqrt(ms + EPS) * nw_ref[...]).astype(BF16)


def _dot_f32_lhs(a_bf16, b):
    return jnp.dot(jnp.concatenate([a_bf16] * 3, axis=1), jnp.concatenate(_split3(b), axis=0),
                   preferred_element_type=F32)


def _expand(v, emat):
    return _dot_f32(v, emat)


def _ssd_mixer(proj, B, L, z_blk, xbc_blk, dt_blk, prm, emat, layer, Q):
    cw, cb, dtb, avec, dx, nw = prm
    width = nw.shape[-1]
    cdim = cw.shape[-1]
    nheads = width // SSD_HEAD_DIM
    nt = L // Q
    body = functools.partial(_ssd_body, nheads=nheads, hd=SSD_HEAD_DIM, ns=SSD_STATE, ngroups=SSD_GROUPS)
    full = lambda a: _layer_spec(a, layer)
    return pl.pallas_call(
        body,
        grid=(B, nt),
        in_specs=[pl.BlockSpec((Q, width), lambda b, t: (b * nt + t, z_blk)),
                  pl.BlockSpec((Q, cdim), lambda b, t: (b * nt + t, xbc_blk)),
                  pl.BlockSpec((Q, LANE), lambda b, t: (b * nt + t, dt_blk)),
                  full(cw), full(cb), full(dtb), full(avec), full(dx), full(nw),
                  pl.BlockSpec(emat.shape, lambda b, t: (0, 0))],
        out_specs=pl.BlockSpec((Q, width), lambda b, t: (b * nt + t, 0)),
        out_shape=jax.ShapeDtypeStruct((B * L, width), BF16),
        scratch_shapes=[pltpu.VMEM((SSD_STATE, width), F32),
                        pltpu.VMEM((Q + SUBLANE, cdim), F32)],
        compiler_params=_cparams(("parallel", "arbitrary")),
        name="ssd_mixer",
    )(proj, proj, proj, cw, cb, dtb, avec, dx, nw, emat)


def _ssd_params(conv_w, conv_b, dt_bias, A_log, D, norm_w):
    depth, nheads = dt_bias.shape
    width = nheads * SSD_HEAD_DIM
    pad = lambda v: jnp.pad(v, ((0, 0), (0, LANE - nheads))).reshape(depth, 1, LANE)
    return (conv_w, conv_b.reshape(depth, 1, -1), pad(dt_bias), pad(-jnp.exp(A_log)),
            jnp.repeat(D, SSD_HEAD_DIM, axis=1).reshape(depth, 1, width), norm_w.reshape(depth, 1, width))


def _head_expansion(nheads):
    e = np.zeros((LANE, nheads * SSD_HEAD_DIM), np.float32)
    e[np.repeat(np.arange(nheads), SSD_HEAD_DIM), np.arange(nheads * SSD_HEAD_DIM)] = 1.0
    return jnp.asarray(np.concatenate([e] * 3, axis=0), BF16)


def _float_key(x):
    b = lax.bitcast_convert_type(x, jnp.int32)
    return b ^ ((b >> 31) & 0x7FFFFFFF)


def _place(blk, lane0, width, dst):
    lane = lax.broadcasted_iota(jnp.int32, blk.shape, 1)
    shift = (dst - lane0) % LANE
    r = pltpu.roll(blk, shift, axis=1) if shift else blk
    lo = jnp.where((lane >= dst) & (lane < min(dst + width, LANE)), r, 0.0)
    hi = jnp.where(lane < dst + width - LANE, r, 0.0)
    return jnp.concatenate([lo, hi], axis=1)


def _head_offsets(offs, nheads, hd):
    return sorted({(offs[f] + h * hd) % LANE for f in ('q', 'qi') for h in range(nheads)})


def _lane_window(x, start, width):
    off = start % LANE
    base = start - off
    if off == 0:
        return x[:, start:start + width]
    lane = lax.broadcasted_iota(jnp.int32, (x.shape[0], LANE), 1)
    out = []
    for p in range(width // LANE):
        a = pltpu.roll(x[:, base + p * LANE:base + (p + 1) * LANE], LANE - off, axis=1)
        b = pltpu.roll(x[:, base + (p + 1) * LANE:base + (p + 2) * LANE], LANE - off, axis=1)
        out.append(jnp.where(lane < LANE - off, a, b))
    return jnp.concatenate(out, axis=1)


def _dsa_body(qx_ref, qxn_ref, kv_ref, ki_ref, band_ref, far_ref, o_ref,
              kpad_ref, kipad_ref, vt_ref, qop_ref, key2_ref, hi2_ref, lo2_ref, lom_ref, lg_ref, bound_ref,
              m_ref, acc_ref,
              *, nheads, hd, topk, offs):
    i = pl.program_id(1)
    cur = lax.rem(i, 2)
    key_ref, hi_ref, lo_ref = key2_ref.at[cur], hi2_ref.at[cur], lo2_ref.at[cur]
    TQ = qx_ref.shape[0]
    KB = TQ
    L = kv_ref.shape[0]
    nkb_total = L // KB
    NT = (((1,), (1,)), ((), ()))
    WIN = kpad_ref.shape[2]

    def head_window(field_lo, h):
        lo = field_lo + h * hd
        return (lo // LANE) * LANE, lo % LANE

    head_offs = _head_offsets(offs, nheads, hd)

    @pl.when(i == 0)
    def _():
        for kb in range(nkb_total):
            r = slice(kb * KB, (kb + 1) * KB)
            kgrp = (offs['k'] // LANE) * LANE
            kblk = kv_ref[r, kgrp:kgrp + LANE]
            kiblk = ki_ref[r, :]
            for n, o in enumerate(head_offs):
                kpad_ref[n, r, :] = _place(kblk, offs['k'] - kgrp, hd, o)[:, :WIN].astype(BF16)
                kipad_ref[n, r, :] = _place(kiblk, offs['ki'], hd, o)[:, :WIN].astype(BF16)
            vt_ref[0:hd, r] = kv_ref[r, :].T[offs['v']:offs['v'] + hd, :].astype(BF16)
            vt_ref[hd:, r] = jnp.ones((vt_ref.shape[0] - hd, KB), BF16)

    def stage_windows(slot, field_lo, scale, src_ref):
        bases = sorted({head_window(field_lo, h)[0] for h in range(nheads)})
        for n, base in enumerate(bases):
            qop_ref[slot, n] = (src_ref[:, base:base + WIN] * scale).astype(BF16)
        return [(bases.index(head_window(field_lo, h)[0]), head_offs.index(head_window(field_lo, h)[1]))
                for h in range(nheads)]

    qs = stage_windows(0, offs['q'], hd ** -0.5, qx_ref)

    def stage_indexer(src_ref):
        table = stage_windows(1, offs['qi'], 1.0, src_ref)
        wbase = (offs['w'] // LANE) * LANE
        wt = src_ref[:, wbase:wbase + LANE].T * ((nheads * IDX_DIM) ** -0.5)
        return table, [wt[offs['w'] - wbase + h:offs['w'] - wbase + h + 1, :] for h in range(nheads)]

    RB = LANE
    I16 = jnp.int16
    HALF = 1 << 15

    def rows(kb, r=0, n=KB):
        return pl.ds(pl.multiple_of(kb * KB + r, SUBLANE), n)

    PART = m_ref.shape[1]

    def fold_sum(x):
        return jnp.sum(x.reshape(x.shape[0] // PART, PART, TQ), axis=0)

    def idx_keys(kb, diagonal, slot, indexer):
        qis, w_rows = indexer
        for r in range(0, KB, RB):
            s = jnp.zeros((RB, TQ), F32)
            for h in range(nheads):
                win, ko = qis[h]
                lg = lax.dot_general(kipad_ref[ko, rows(kb, r, RB), :], qop_ref[1, win], NT,
                                     preferred_element_type=F32)
                s += jnp.maximum(lg, 0.0) * w_rows[h]
            key = _float_key(s)
            if diagonal:
                krow = lax.broadcasted_iota(jnp.int32, (RB, TQ), 0) + r
                qcol = lax.broadcasted_iota(jnp.int32, (RB, TQ), 1)
                key = jnp.where((krow // CHUNK) <= (qcol // CHUNK), key, INT_MIN)
            key2_ref[slot, rows(kb, r, RB), :] = key
            hi2_ref[slot, rows(kb, r, RB), :] = (key >> 16).astype(I16)
            lo2_ref[slot, rows(kb, r, RB), :] = ((key & (2 * HALF - 1)) - HALF).astype(I16)

    @pl.when(i == 0)
    def _():
        idx_keys(0, True, cur, stage_indexer(qx_ref))

    def count(preds):
        def body(kb, cs):
            blk = key_ref[rows(kb), :]
            return tuple(c + fold_sum(jnp.where(p(blk, kb), 1.0, 0.0)) for c, p in zip(cs, preds))
        z = jnp.zeros((PART, TQ), F32)
        cs = lax.fori_loop(0, i + 1, body, (z,) * len(preds))
        return [jnp.sum(c, axis=0, keepdims=True) for c in cs]

    PACK = 2 * SUBLANE
    P16 = 4 * PACK

    def rows16(x):
        return x.reshape(x.shape[0] // PACK, PACK, TQ)

    def as16(v):
        return jnp.broadcast_to(v, (PACK, TQ)).astype(I16)

    def count16(src_ref, pred, flags=False):
        def body(kb, c):
            blk = rows16(src_ref[rows(kb), :])
            ind = jnp.where(pred(blk, kb), blk if flags else I16(1), I16(0)).reshape(KB // P16, P16, TQ)
            for n in range(KB // P16):
                c = c + ind[n]
            return c
        c = lax.fori_loop(0, i + 1, body, jnp.zeros((P16, TQ), I16))
        c = rows16(c)
        c = (c[0] + c[1]) + (c[2] + c[3])
        return jnp.sum(c.astype(F32), axis=0, keepdims=True)

    def bisect16(src_ref, target):
        def step(s, tu):
            cand_u = tu | jnp.left_shift(jnp.int32(1), 15 - s)
            cand = as16(cand_u - HALF)[None]
            cnt = count16(src_ref, lambda blk, kb: blk >= cand)
            return jnp.where(cnt >= target, cand_u, tu)
        return lax.fori_loop(0, 16, step, jnp.zeros((1, TQ), jnp.int32))

    thr_hi = bisect16(hi_ref, float(topk)) - HALF
    thr_hi16 = as16(thr_hi)[None]
    need_lo = topk - count16(hi_ref, lambda blk, kb: blk > thr_hi16)

    def mask_low_halves(kb, _):
        keep_lo = rows16(hi_ref[rows(kb), :]) == thr_hi16
        lom_ref[rows(kb), :] = jnp.where(keep_lo, rows16(lo_ref[rows(kb), :]), I16(-HALF)).reshape(KB, TQ)
        return 0

    lax.fori_loop(0, i + 1, mask_low_halves, 0)
    thr_lo = bisect16(lom_ref, need_lo)
    thr = thr_hi * (2 * HALF) + thr_lo

    cgt, ceq = count([lambda blk, kb: blk > thr, lambda blk, kb: blk == thr])
    need = topk - cgt
    nbits = int(L).bit_length()
    bound_ref[...] = jnp.full((1, TQ), 2 ** nbits - 1, jnp.int32)

    @pl.when(jnp.max(ceq - need) > 0.0)
    def _():
        thr_lo16 = as16(thr_lo - HALF)[None]

        def flag_ties(kb, _):
            tie = ((rows16(hi_ref[rows(kb), :]) == thr_hi16) & (rows16(lo_ref[rows(kb), :]) == thr_lo16))
            lom_ref[rows(kb), :] = jnp.where(tie, I16(1), I16(0)).reshape(KB, TQ)
            return 0

        lax.fori_loop(0, i + 1, flag_ties, 0)
        krow16 = rows16(lax.broadcasted_iota(jnp.int32, (KB, TQ), 0).astype(I16))

        def bisect_idx(step, bnd):
            bit = jnp.left_shift(jnp.int32(1), nbits - 1 - step)
            cand = bnd | bit
            cnt = count16(lom_ref, lambda blk, kb: krow16 < as16(cand - kb * KB)[None], flags=True)
            return jnp.where(cnt <= need, cand, bnd)

        bound_ref[...] = lax.fori_loop(0, nbits, bisect_idx, jnp.zeros((1, TQ), jnp.int32))

    bound = bound_ref[...]

    m_ref[...] = jnp.full(m_ref.shape, NEG, F32)

    def fold_max(x):
        return jnp.max(x.reshape(x.shape[0] // PART, PART, TQ), axis=0)

    def store_logits(kb, bias_of_head):
        for r in range(0, KB, RB):
            blk = key_ref[rows(kb, r, RB), :]
            pos = lax.broadcasted_iota(jnp.int32, (RB, TQ), 0) + (kb * KB + r)
            keep = ((blk > thr) | ((blk == thr) & (pos < bound))) & (blk != INT_MIN)
            mask = jnp.where(keep, 0.0, NEG)
            for h in range(nheads):
                win, ko = qs[h]
                lg = lax.dot_general(kpad_ref[ko, rows(kb, r, RB), :], qop_ref[0, win], NT,
                                     preferred_element_type=F32)
                lg = lg + bias_of_head(h, r) + mask
                lg_ref[h, rows(kb, r, RB), :] = lg
                m_ref[h] = jnp.maximum(m_ref[h], fold_max(lg))

    def far_logits(kb, _):
        store_logits(kb, lambda h, r: far_ref[h])
        return 0

    def far_logits2(j, _):
        store_logits(2 * j, lambda h, r: far_ref[h])
        store_logits(2 * j + 1, lambda h, r: far_ref[h])
        return 0

    nfar = jnp.maximum(i - 1, 0)
    lax.fori_loop(0, lax.shift_right_logical(nfar, 1), far_logits2, 0)

    @pl.when((nfar & 1) == 1)
    def _():
        far_logits(nfar - 1, 0)

    @pl.when(i >= 1)
    def _():
        store_logits(i - 1, lambda h, r: band_ref[h, r:r + RB, :])

    store_logits(i, lambda h, r: band_ref[h, KB + r:KB + r + RB, :])

    m_fin = [jnp.max(m_ref[h], axis=0, keepdims=True) for h in range(nheads)]
    acc_ref[...] = jnp.zeros(acc_ref.shape, F32)

    def accumulate(kb):
        vt = vt_ref[:, rows(kb)]
        for h in range(nheads):
            p = jnp.exp(lg_ref[h, rows(kb), :] - m_fin[h])
            acc_ref[h] += jnp.dot(vt, p.astype(BF16), preferred_element_type=F32)

    has_next = i + 1 < pl.num_programs(1)

    @pl.when(has_next)
    def _():
        indexer = stage_indexer(qxn_ref)

        def both(kb, _):
            accumulate(kb)
            idx_keys(kb, False, 1 - cur, indexer)
            return 0

        lax.fori_loop(0, i + 1, both, 0)
        idx_keys(i + 1, True, 1 - cur, indexer)

    @pl.when(jnp.logical_not(has_next))
    def _():
        def only(kb, _):
            accumulate(kb)
            return 0

        lax.fori_loop(0, i + 1, only, 0)

    outs = [acc_ref[h, 0:hd, :] / acc_ref[h, hd:hd + 1, :] for h in range(nheads)]
    out = jnp.concatenate(outs, axis=0).T
    z = _lane_window(qx_ref, offs['z'], nheads * hd)
    o_ref[...] = (out * jax.nn.silu(z)).astype(BF16)


def _dsa_mixer(proj, B, L, src, band, far, nheads, TQ):
    width = nheads * ATT_HEAD_DIM
    nq = L // TQ
    hd = ATT_HEAD_DIM
    qx_w = 16 * LANE
    qx_blk = src['att_q'][0] // qx_w
    qx_lo = qx_blk * qx_w
    kv_w = 2 * LANE
    kv_blk = src['att_k'][0] // kv_w
    ki_blk = src['idx_k'][0] // LANE
    offs = {'q': src['att_q'][0] - qx_lo, 'qi': src['idx_q'][0] - qx_lo, 'w': src['idx_w'][0] - qx_lo,
            'z': src['att_z'][0] - qx_lo, 'k': src['att_k'][0] - kv_blk * kv_w,
            'v': src['att_v'][0] - kv_blk * kv_w, 'ki': src['idx_k'][0] - ki_blk * LANE}
    assert offs['z'] + width + LANE <= qx_w and qx_lo + qx_w <= proj.shape[1]
    assert offs['k'] % LANE + hd <= LANE and offs['ki'] + hd <= LANE
    assert offs['v'] % SUBLANE == 0 and offs['v'] + hd <= kv_w
    assert offs['w'] // LANE == (offs['w'] + nheads - 1) // LANE
    win_off = _head_offsets(offs, nheads, hd)
    win = LANE if max(win_off) + hd <= LANE else 2 * LANE
    win_base = max((sorted({(offs[f] + h * hd) // LANE for h in range(nheads)}) for f in ('q', 'qi')), key=len)
    body = functools.partial(_dsa_body, nheads=nheads, hd=hd, topk=min(TOPK, L // 4), offs=offs)
    full = lambda a: pl.BlockSpec(a.shape, lambda b, i: (0,) * a.ndim)
    return pl.pallas_call(
        body,
        grid=(B, nq),
        in_specs=[pl.BlockSpec((TQ, qx_w), lambda b, i: (b * nq + i, qx_blk)),
                  pl.BlockSpec((TQ, qx_w), lambda b, i: (b * nq + jnp.minimum(i + 1, nq - 1), qx_blk)),
                  pl.BlockSpec((L, kv_w), lambda b, i: (b, kv_blk)),
                  pl.BlockSpec((L, LANE), lambda b, i: (b, ki_blk)),
                  full(band), full(far)],
        out_specs=pl.BlockSpec((TQ, width), lambda b, i: (b * nq + i, 0)),
        out_shape=jax.ShapeDtypeStruct((B * L, width), BF16),
        scratch_shapes=[pltpu.VMEM((len(win_off), L, win), BF16),
                        pltpu.VMEM((len(win_off), L, win), BF16),
                        pltpu.VMEM((ATT_HEAD_DIM + 2 * SUBLANE, L), BF16),
                        pltpu.VMEM((2, len(win_base), TQ, win), BF16),
                        pltpu.VMEM((2, L, TQ), jnp.int32),
                        pltpu.VMEM((2, L, TQ), jnp.int16),
                        pltpu.VMEM((2, L, TQ), jnp.int16),
                        pltpu.VMEM((L, TQ), jnp.int16),
                        pltpu.VMEM((nheads, L, TQ), F32),
                        pltpu.VMEM((1, TQ), jnp.int32),
                        pltpu.VMEM((nheads, 4 * SUBLANE, TQ), F32),
                        pltpu.VMEM((nheads, ATT_HEAD_DIM + 2 * SUBLANE, TQ), F32)],
        compiler_params=_cparams(("parallel", "arbitrary")),
        name="dsa_mixer",
    )(proj, proj, proj, proj, band, far)


def _t5_bucket_static(rel):
    nb = N_BUCKETS // 2
    max_exact = nb // 2
    ret = np.where(rel > 0, nb, 0)
    n = np.abs(rel)
    nf = np.maximum(n, 1).astype(np.float64)
    large = max_exact + (np.log(nf / max_exact) / math.log(MAX_DISTANCE / max_exact)
                         * (nb - max_exact)).astype(np.int32)
    large = np.minimum(large, nb - 1)
    return ret + np.where(n < max_exact, n, large)


def _dsa_bias_tables(rel_bias, TQ):
    a = np.arange(2 * TQ)[:, None]
    j = np.arange(TQ)[None, :]
    band_idx = _t5_bucket_static(a - TQ - j)
    idx = jnp.asarray(band_idx, jnp.int32)[None]
    band = jnp.zeros((rel_bias.shape[1],) + band_idx.shape, F32)
    for bucket in np.unique(band_idx):
        band = jnp.where(idx == int(bucket), rel_bias[int(bucket)][:, None, None], band)
    far_bucket = int(_t5_bucket_static(np.array([-(TQ + 1)]))[0])
    assert far_bucket == int(_t5_bucket_static(np.array([-(10 ** 6)]))[0])
    far = jnp.broadcast_to(rel_bias[far_bucket][:, None, None], (rel_bias.shape[1], 1, TQ))
    return band, far


def _layout(d_model):
    s5w = d_model // 4
    ssdw = d_model // 2
    attw = d_model // 4
    cdim = ssdw + 2 * SSD_GROUPS * SSD_STATE
    nh_ssd = ssdw // SSD_HEAD_DIM
    nh_att = attw // ATT_HEAD_DIM
    splits = (s5w, s5w, ssdw, cdim, nh_ssd, attw, ATT_HEAD_DIM, ATT_HEAD_DIM,
              nh_att * IDX_DIM, IDX_DIM, nh_att, attw)
    names = ('s5_u', 's5_z', 'ssd_z', 'ssd_xbc', 'ssd_dt', 'att_q', 'att_k', 'att_v',
             'idx_q', 'idx_k', 'idx_w', 'att_z')
    src, o = {}, 0
    for nme, s in zip(names, splits):
        src[nme] = (o, s)
        o += s
    return src, o


def kernel(x, norm_w, w_in, s5_A_re, s5_A_im, s5_log_dt, s5_B_re, s5_B_im, s5_C_re, s5_C_im, s5_D, s5_glu_w, s5_glu_b, ssd_conv_w, ssd_conv_b, ssd_dt_bias, ssd_A_log, ssd_D, ssd_norm_w, rel_bias, w_out, final_norm_w):
    B, L, d = x.shape
    depth = w_in.shape[0]
    src, total = _layout(d)
    s5w, ssdw, attw = d // 4, d // 2, d // 4
    cdim = ssdw + 2 * SSD_GROUPS * SSD_STATE
    nh_att = attw // ATT_HEAD_DIM
    assert src['s5_u'][0] % s5w == 0 and src['s5_z'][0] % s5w == 0 and src['ssd_z'][0] % ssdw == 0
    assert src['ssd_xbc'][0] % cdim == 0 and src['ssd_dt'][0] % LANE == 0

    tn = 1536
    cuts, shift, padded = [(0, 0)], 0, {}
    for nme, (so, sz) in sorted(src.items(), key=lambda kv: kv[1][0]):
        gap = -(so + shift) % LANE if nme in ('att_q', 'att_k', 'idx_q', 'idx_k', 'att_z') else 0
        if gap:
            shift += gap
            cuts.append((so, shift))
        padded[nme] = (so + shift, sz)
    src = padded
    total_pad = -(-(total + shift) // tn) * tn
    segs = tuple((lo, cuts[n + 1][0] if n + 1 < len(cuts) else total, lo + sh) for n, (lo, sh) in enumerate(cuts))
    w_in_p = _relayout_w_in(jnp.swapaxes(w_in, 1, 2), segs, total_pad, min(256, d))

    TQ = 256
    Q = 256
    band, far = _dsa_bias_tables(rel_bias, TQ)
    tm_in = min(1024, B * L)
    tm_out = min(512, B * L)
    nchunk = 2
    seg = L // SUBLANE

    s5p = jax.vmap(lambda *a: _s5_params(*a, seg, nchunk))(
        s5_A_re, s5_A_im, s5_log_dt, s5_B_re, s5_B_im, s5_C_re, s5_C_im, s5_D, s5_glu_w, s5_glu_b)
    ssdp = _ssd_params(ssd_conv_w, ssd_conv_b, ssd_dt_bias, ssd_A_log, ssd_D, ssd_norm_w)
    emat = _head_expansion(ssdw // SSD_HEAD_DIM)
    norm_w3 = norm_w.reshape(depth, 1, d)

    x2 = x.reshape(B * L, d)
    h = None
    for l in range(depth):
        proj = (_inproj(x2, norm_w3, w_in_p, l, tm_in, tn) if h is None
                else _inproj_prenormed(h, w_in_p, l, tm_in, tn))
        y_s5 = _s5_mixer(proj, B, L, src['s5_u'][0] // s5w, src['s5_z'][0] // s5w, s5w, s5p, l)
        y_ssd = _ssd_mixer(proj, B, L, src['ssd_z'][0] // ssdw, src['ssd_xbc'][0] // cdim,
                           src['ssd_dt'][0] // LANE, ssdp, emat, l, Q)
        y_att = _dsa_mixer(proj, B, L, src, band, far, nh_att, TQ)
        if l == depth - 1:
            x2 = _outproj(y_s5, y_ssd, y_att, w_out, l, x2, final_norm_w.reshape(1, d), tm_out, final=True)
        else:
            x2, h = _outproj(y_s5, y_ssd, y_att, w_out, l, x2, norm_w[l + 1].reshape(1, d), tm_out, final=False)
    return x2.reshape(B, L, d)
```
